```python
import math
import jax
import jax.numpy as jnp
from jax import lax
import numpy as np

D_MODEL = 2048
BATCH = 4
SEQ = 2048
DEPTH = 1
DEC_BATCH = 32
DEC_SEQ = 1
PAST_LEN = 8192
PAGE_SIZE = 128

A_HEADS = 8
A_DH = 128
A_W = A_HEADS * A_DH
MOBA_BLOCK = 256
MOBA_TOPK = 3
MOBA_Q_CHUNK = 32
T5_BUCKETS = 32
T5_MAX_DIST = 128
R_HEADS = 4
R_DK = 256
R_DV = 256
R_KW = R_HEADS * R_DK
R_VW = R_HEADS * R_DV
RET_CHUNK = 128
RET_ROPE_BASE = 10000.0
M_HEADS = 4
M_DH = 256
M_W = M_HEADS * M_DH
N_MEM = 256
D_FF = 5632
EPS = 1e-6
NEG_INF = -1e30
IN_SECTIONS = (A_W, A_W, A_W, R_KW, R_KW, R_VW, R_VW, M_W)
SPLIT_POINTS = tuple(int(s) for s in np.cumsum(IN_SECTIONS)[:-1])
IN_COLS = int(sum(IN_SECTIONS))

kernel_name = "hybrid_moba_retention_memory_decoder_step"


def rmsnorm(x, g):
    xf = x.astype(jnp.float32)
    y = xf * lax.rsqrt(jnp.mean(xf * xf, axis=-1, keepdims=True) + EPS)
    return (y * g.astype(jnp.float32)).astype(x.dtype)


def head_rmsnorm(x):
    return x * lax.rsqrt(jnp.mean(x * x, axis=-1, keepdims=True) + EPS)


def ffn_half(x, g, w1, w3, w2):
    h = rmsnorm(x, g)
    return x + 0.5 * ((jax.nn.silu(h @ w1) * (h @ w3)) @ w2)


def t5_bucket(n):
    n = jnp.maximum(n, 0)
    max_exact = T5_BUCKETS // 2
    nf = jnp.maximum(n, 1).astype(jnp.float32)
    large = max_exact + (jnp.log(nf / max_exact) / math.log(T5_MAX_DIST / max_exact)
                         * (T5_BUCKETS - max_exact)).astype(jnp.int32)
    return jnp.where(n < max_exact, n, jnp.minimum(large, T5_BUCKETS - 1))


def rotate(x, pos):
    d = x.shape[-1]
    half = d // 2
    inv = 1.0 / (RET_ROPE_BASE ** jnp.linspace(0.0, 1.0, half, dtype=jnp.float32))
    ang = pos.astype(jnp.float32)[:, None] * inv[None, :]
    cos = jnp.cos(ang)[None, :, None, :]
    sin = jnp.sin(ang)[None, :, None, :]
    xf = x.astype(jnp.float32)
    x1, x2 = xf[..., :half], xf[..., half:]
    return jnp.concatenate([x1 * cos - x2 * sin, x1 * sin + x2 * cos], axis=-1)


def moba_attend(q, k, v, q_pos, t5_table):
    B, Sq, H, dh = q.shape
    nb = k.shape[1] // MOBA_BLOCK
    kbh = k.reshape(B, nb, MOBA_BLOCK, H, dh).transpose(0, 3, 1, 2, 4)
    vbh = v.reshape(B, nb, MOBA_BLOCK, H, dh).transpose(0, 3, 1, 2, 4)
    kmean = jnp.mean(kbh.astype(jnp.float32), axis=3)
    n_sel = min(MOBA_TOPK, nb)
    qc = math.gcd(Sq, MOBA_Q_CHUNK)
    n_chunks = Sq // qc
    bias_tab = t5_table.astype(jnp.float32).T
    scale = dh ** -0.5
    b_ix = jnp.arange(B)[:, None, None, None]
    h_ix = jnp.arange(H)[None, :, None, None]
    blk_ar = jnp.arange(MOBA_BLOCK, dtype=jnp.int32)
    blk_ids = jnp.arange(nb, dtype=jnp.int32)

    def one_chunk(args):
        qb, pos = args
        qf = qb.astype(jnp.float32)
        cur = pos // MOBA_BLOCK
        gate = jnp.einsum('bqhd,bhnd->bhqn', qf, kmean)
        past = blk_ids[None, :] < cur[:, None]
        gate = jnp.where(past[None, None], gate, NEG_INF)
        _, sel = lax.top_k(gate, n_sel)
        sel_ok = sel < cur[None, None, :, None]
        ks = kbh[b_ix, h_ix, sel].astype(jnp.float32)
        vs = vbh[b_ix, h_ix, sel].astype(jnp.float32)
        s_sel = jnp.einsum('bqhd,bhqsjd->bhqsj', qf, ks) * scale
        dist_sel = pos[None, None, :, None, None] - (sel[..., None] * MOBA_BLOCK + blk_ar)
        s_sel = s_sel + bias_tab[h_ix[..., None], t5_bucket(dist_sel)]
        s_sel = jnp.where(sel_ok[..., None], s_sel, NEG_INF)
        own = pos[0] // MOBA_BLOCK
        k_own = lax.dynamic_index_in_dim(kbh, own, axis=2, keepdims=False).astype(jnp.float32)
        v_own = lax.dynamic_index_in_dim(vbh, own, axis=2, keepdims=False).astype(jnp.float32)
        dist_own = pos[:, None] - (own * MOBA_BLOCK + blk_ar)[None, :]
        s_own = jnp.einsum('bqhd,bhjd->bhqj', qf, k_own) * scale + bias_tab[:, t5_bucket(dist_own)][None]
        s_own = jnp.where(dist_own[None, None] >= 0, s_own, NEG_INF)
        logits = jnp.concatenate([s_sel.reshape(B, H, qc, n_sel * MOBA_BLOCK), s_own], axis=-1)
        p = jax.nn.softmax(logits, axis=-1)
        p_sel = p[..., :n_sel * MOBA_BLOCK].reshape(B, H, qc, n_sel, MOBA_BLOCK)
        p_own = p[..., n_sel * MOBA_BLOCK:]
        o = (jnp.einsum('bhqsj,bhqsjd->bqhd', p_sel, vs)
             + jnp.einsum('bhqj,bhjd->bqhd', p_own, v_own))
        return o.astype(q.dtype)

    qs = q.reshape(B, n_chunks, qc, H, dh).transpose(1, 0, 2, 3, 4)
    ps = q_pos.reshape(n_chunks, qc)
    out = lax.map(one_chunk, (qs, ps))
    return out.transpose(1, 0, 2, 3, 4).reshape(B, Sq, H, dh)


def retention_scan(q, k, v, s0):
    B, S, H, dk = q.shape
    dv = v.shape[-1]
    c = math.gcd(S, RET_CHUNK)
    n = S // c
    log_g = jnp.log(1.0 - jnp.exp2(-5.0 - jnp.arange(H, dtype=jnp.float32)))
    i = jnp.arange(c, dtype=jnp.float32)
    diff = i[:, None] - i[None, :]
    dmask = jnp.where(diff[None] >= 0, jnp.exp(jnp.maximum(diff, 0.0)[None] * log_g[:, None, None]), 0.0)
    q_dec = jnp.exp((i[:, None] + 1.0) * log_g[None, :])
    k_dec = jnp.exp((c - 1.0 - i)[:, None] * log_g[None, :])
    c_dec = jnp.exp(c * log_g)

    def to_chunks(t):
        return t.reshape(B, n, c, H, t.shape[-1]).transpose(1, 0, 2, 3, 4)

    def step(s, xs):
        qc, kc, vc = xs
        att = jnp.einsum('bihd,bjhd->bhij', qc, kc) * dmask[None]
        o = (jnp.einsum('bhij,bjhe->bihe', att, vc)
             + jnp.einsum('bihd,bhde->bihe', qc * q_dec[None, :, :, None], s))
        s = s * c_dec[None, :, None, None] + jnp.einsum('bjhd,bjhe->bhde', kc * k_dec[None, :, :, None], vc)
        return s, o

    s, o = lax.scan(step, s0, (to_chunks(q), to_chunks(k), to_chunks(v)))
    return o.transpose(1, 0, 2, 3, 4).reshape(B, S, H, dv), s


def mem_attend(q, mk, mv):
    s = jnp.einsum('bqhd,bmhd->bhqm', q.astype(jnp.float32), mk.astype(jnp.float32)) * (M_DH ** -0.5)
    p = jax.nn.softmax(s, axis=-1)
    return jnp.einsum('bhqm,bmhd->bqhd', p, mv.astype(jnp.float32))


def mem_kv(mem, g, w_mem_kv):
    B = mem.shape[0]
    kv = rmsnorm(mem, g) @ w_mem_kv
    mk, mv = jnp.split(kv, 2, axis=-1)
    return mk.reshape(B, N_MEM, M_HEADS, M_DH), mv.reshape(B, N_MEM, M_HEADS, M_DH)


def decoder_layer(x, pos, mk, mv, k_past, v_past, ret_s0, t5_table, lw):
    B, S, _ = x.shape
    x = ffn_half(x, lw['ffn1_norm'], lw['ffn1_w1'], lw['ffn1_w3'], lw['ffn1_w2'])
    h = rmsnorm(x, lw['mix_norm'])
    z = h @ lw['w_in']
    qa, ka, va, qr, kr, vr, gr, qm = jnp.split(z, SPLIT_POINTS, axis=-1)
    qa = qa.reshape(B, S, A_HEADS, A_DH)
    ka = ka.reshape(B, S, A_HEADS, A_DH)
    va = va.reshape(B, S, A_HEADS, A_DH)
    if k_past is None:
        k_cat, v_cat = ka, va
    else:
        k_cat = jnp.concatenate([k_past.astype(ka.dtype), ka], axis=1)
        v_cat = jnp.concatenate([v_past.astype(va.dtype), va], axis=1)
    pad = (-k_cat.shape[1]) % MOBA_BLOCK
    k_cat = jnp.pad(k_cat, ((0, 0), (0, pad), (0, 0), (0, 0)))
    v_cat = jnp.pad(v_cat, ((0, 0), (0, pad), (0, 0), (0, 0)))
    oa = moba_attend(qa, k_cat, v_cat, pos, t5_table)
    qr = rotate(qr.reshape(B, S, R_HEADS, R_DK), pos)
    kr = rotate(kr.reshape(B, S, R_HEADS, R_DK), pos) * (R_DK ** -0.5)
    vr = vr.reshape(B, S, R_HEADS, R_DV).astype(jnp.float32)
    o_r, s_new = retention_scan(qr, kr, vr, ret_s0.astype(jnp.float32))
    o_r = head_rmsnorm(o_r) * jax.nn.silu(gr.astype(jnp.float32)).reshape(B, S, R_HEADS, R_DV)
    om = mem_attend(qm.reshape(B, S, M_HEADS, M_DH), mk, mv)
    gates = jax.nn.sigmoid((h @ lw['w_gate'] + lw['b_gate']).astype(jnp.float32))
    g_a, g_r, g_m = jnp.split(gates, 3, axis=-1)
    merged = (g_a * (oa.reshape(B, S, A_W) @ lw['w_br_moba']).astype(jnp.float32)
              + g_r * (o_r.reshape(B, S, R_VW).astype(x.dtype) @ lw['w_br_ret']).astype(jnp.float32)
              + g_m * (om.reshape(B, S, M_W).astype(x.dtype) @ lw['w_br_mem']).astype(jnp.float32))
    x = x + merged.astype(x.dtype) @ lw['w_out']
    x = ffn_half(x, lw['ffn2_norm'], lw['ffn2_w1'], lw['ffn2_w3'], lw['ffn2_w2'])
    return x, ka, va, s_new


def setup_inputs(seed: int = 0) -> dict:
    key = jax.random.key(seed)
    ks = iter(jax.random.split(key, 48))
    n_pages = PAST_LEN // PAGE_SIZE
    used = DEC_BATCH * n_pages
    n_pool = used + used // 4

    def nrm(shape, scale):
        return jax.random.normal(next(ks), shape, jnp.float32) * scale

    def gain(shape):
        return 1.0 + nrm(shape, 0.05)

    page_table = jax.random.permutation(next(ks), n_pool)[:used].reshape(DEC_BATCH, n_pages).astype(jnp.int32)
    return {
        "x_prompt": nrm((BATCH, SEQ, D_MODEL), 1.0),
        "x_sample": nrm((DEC_BATCH, DEC_SEQ, D_MODEL), 1.0),
        "mem_prompt": nrm((BATCH, N_MEM, D_MODEL), 1.0),
        "cache_k": nrm((DEPTH, n_pool, PAGE_SIZE, A_HEADS, A_DH), 1.0),
        "cache_v": nrm((DEPTH, n_pool, PAGE_SIZE, A_HEADS, A_DH), 1.0),
        "cache_mem_k": nrm((DEPTH, DEC_BATCH, N_MEM, M_HEADS, M_DH), 1.0),
        "cache_mem_v": nrm((DEPTH, DEC_BATCH, N_MEM, M_HEADS, M_DH), 1.0),
        "state_ret": nrm((DEPTH, DEC_BATCH, R_HEADS, R_DK, R_DV), 0.1),
        "page_table": page_table,
        "t5_table": nrm((T5_BUCKETS, A_HEADS), 0.5),
        "ffn1_norm": gain((DEPTH, D_MODEL)),
        "ffn1_w1": nrm((DEPTH, D_MODEL, D_FF), D_MODEL ** -0.5),
        "ffn1_w3": nrm((DEPTH, D_MODEL, D_FF), D_MODEL ** -0.5),
        "ffn1_w2": nrm((DEPTH, D_FF, D_MODEL), D_FF ** -0.5),
        "mix_norm": gain((DEPTH, D_MODEL)),
        "mem_norm": gain((DEPTH, D_MODEL)),
        "w_in": nrm((DEPTH, D_MODEL, IN_COLS), D_MODEL ** -0.5),
        "w_mem_kv": nrm((DEPTH, D_MODEL, 2 * M_W), D_MODEL ** -0.5),
        "w_gate": nrm((DEPTH, D_MODEL, 3 * D_MODEL), D_MODEL ** -0.5),
        "b_gate": nrm((DEPTH, 3 * D_MODEL), 0.1),
        "w_br_moba": nrm((DEPTH, A_W, D_MODEL), A_W ** -0.5),
        "w_br_ret": nrm((DEPTH, R_VW, D_MODEL), R_VW ** -0.5),
        "w_br_mem": nrm((DEPTH, M_W, D_MODEL), M_W ** -0.5),
        "w_out": nrm((DEPTH, D_MODEL, D_MODEL), D_MODEL ** -0.5),
        "ffn2_norm": gain((DEPTH, D_MODEL)),
        "ffn2_w1": nrm((DEPTH, D_MODEL, D_FF), D_MODEL ** -0.5),
        "ffn2_w3": nrm((DEPTH, D_MODEL, D_FF), D_MODEL ** -0.5),
        "ffn2_w2": nrm((DEPTH, D_FF, D_MODEL), D_FF ** -0.5),
        "final_norm": gain((D_MODEL,)),
    }


def reference(x_prompt, x_sample, mem_prompt, cache_k, cache_v, cache_mem_k, cache_mem_v, state_ret,
              page_table, t5_table, ffn1_norm, ffn1_w1, ffn1_w3, ffn1_w2, mix_norm, mem_norm, w_in,
              w_mem_kv, w_gate, b_gate, w_br_moba, w_br_ret, w_br_mem, w_out, ffn2_norm, ffn2_w1,
              ffn2_w3, ffn2_w2, final_norm):
    B, S, _ = x_prompt.shape
    DB, DS, _ = x_sample.shape
    past_len = page_table.shape[1] * PAGE_SIZE
    pos_p = jnp.arange(S, dtype=jnp.int32)
    pos_s = past_len + jnp.arange(DS, dtype=jnp.int32)
    xp, xs = x_prompt, x_sample
    kp_l, vp_l, mkp_l, mvp_l, sp_l, ks_l, vs_l, ss_l = [], [], [], [], [], [], [], []
    for l in range(DEPTH):
        lw = dict(ffn1_norm=ffn1_norm[l], ffn1_w1=ffn1_w1[l], ffn1_w3=ffn1_w3[l], ffn1_w2=ffn1_w2[l],
                  mix_norm=mix_norm[l], w_in=w_in[l], w_gate=w_gate[l], b_gate=b_gate[l],
                  w_br_moba=w_br_moba[l], w_br_ret=w_br_ret[l], w_br_mem=w_br_mem[l], w_out=w_out[l],
                  ffn2_norm=ffn2_norm[l], ffn2_w1=ffn2_w1[l], ffn2_w3=ffn2_w3[l], ffn2_w2=ffn2_w2[l])
        mk_p, mv_p = mem_kv(mem_prompt, mem_norm[l], w_mem_kv[l])
        s0_p = jnp.zeros((B, R_HEADS, R_DK, R_DV), jnp.float32)
        xp, k_new_p, v_new_p, s_new_p = decoder_layer(xp, pos_p, mk_p, mv_p, None, None, s0_p, t5_table, lw)
        k_past = cache_k[l][page_table].reshape(DB, past_len, A_HEADS, A_DH)
        v_past = cache_v[l][page_table].reshape(DB, past_len, A_HEADS, A_DH)
        xs, k_new_s, v_new_s, s_new_s = decoder_layer(xs, pos_s, cache_mem_k[l], cache_mem_v[l], k_past, v_past,
                                                      state_ret[l], t5_table, lw)
        kp_l.append(k_new_p)
        vp_l.append(v_new_p)
        mkp_l.append(mk_p)
        mvp_l.append(mv_p)
        sp_l.append(s_new_p)
        ks_l.append(k_new_s)
        vs_l.append(v_new_s)
        ss_l.append(s_new_s)
    y_prompt = rmsnorm(xp, final_norm)
    y_sample = rmsnorm(xs, final_norm)
    return (y_prompt, y_sample, jnp.stack(kp_l), jnp.stack(vp_l), jnp.stack(mkp_l), jnp.stack(mvp_l),
            jnp.stack(sp_l), jnp.stack(ks_l), jnp.stack(vs_l), jnp.stack(ss_l))
```

```python
import functools
import math

import jax
import jax.numpy as jnp
import numpy as np
from jax import lax
from jax.experimental import pallas as pl
from jax.experimental.pallas import tpu as pltpu

F32 = jnp.float32
BF16 = jnp.bfloat16
I32 = jnp.int32

A_HEADS, A_DH = 8, 128
MOBA_BLOCK, MOBA_TOPK = 256, 3
T5_BUCKETS, T5_MAX_DIST = 32, 128
R_HEADS, R_DK, R_DV = 4, 256, 256
RET_CHUNK = 128
RET_ROPE_BASE = 10000.0
M_HEADS, M_DH = 4, 256
PAGE_SIZE = 128
EPS = 1e-6
NEG_INF = -1e30
SEC = 1024
(SEC_QA, SEC_KA, SEC_VA, SEC_QR, SEC_KR, SEC_VR, SEC_GR, SEC_QM) = range(8)

V7X_VMEM_LIMIT_BYTES = 56 * 1024 * 1024

NT_DIMS = (((1,), (1,)), ((), ()))
TN_DIMS = (((0,), (0,)), ((), ()))


def _params(*sem):
    return pltpu.CompilerParams(dimension_semantics=sem, vmem_limit_bytes=V7X_VMEM_LIMIT_BYTES)


def _tile(n, pref):
    t = min(n, pref)
    while n % t:
        t -= 1
    return t


def _rms(x, g):
    return x * lax.rsqrt(jnp.mean(x * x, axis=-1, keepdims=True) + EPS) * g


def _silu(x):
    return x * jax.nn.sigmoid(x)


def _dot(a, b):
    return jnp.dot(a, b, preferred_element_type=F32)


def _ffn_body(x_ref, g_ref, w1_ref, w3_ref, w2_ref, gp_ref, *refs, emit_x):
    if emit_x:
        xo_ref, ho_ref, h_sc, acc_sc = refs
    else:
        ho_ref, h_sc, acc_sc = refs
    f = pl.program_id(1)

    @pl.when(f == 0)
    def _():
        h_sc[...] = _rms(x_ref[...], g_ref[...]).astype(BF16)
        acc_sc[...] = jnp.zeros_like(acc_sc)

    h = h_sc[...]
    t = _silu(_dot(h, w1_ref[...])) * _dot(h, w3_ref[...])
    acc_sc[...] += _dot(t.astype(BF16), w2_ref[...])

    @pl.when(f == pl.num_programs(1) - 1)
    def _():
        xo = x_ref[...] + 0.5 * acc_sc[...]
        if emit_x:
            xo_ref[...] = xo
        ho_ref[...] = _rms(xo, gp_ref[...]).astype(ho_ref.dtype)


def _ffn_half(x, g, w1, w3, w2, g_post, *, emit_x, post_dtype):
    rows, d = x.shape
    ff = w1.shape[1]
    tm, tf = _tile(rows, 512), _tile(ff, 512)
    row_spec = pl.BlockSpec((tm, d), lambda i, f: (i, 0))
    vec_spec = pl.BlockSpec((1, d), lambda i, f: (0, 0))
    out_shape = [jax.ShapeDtypeStruct((rows, d), post_dtype)]
    out_specs = [row_spec]
    if emit_x:
        out_shape = [jax.ShapeDtypeStruct((rows, d), F32)] + out_shape
        out_specs = [row_spec] + out_specs
    return pl.pallas_call(
        functools.partial(_ffn_body, emit_x=emit_x),
        grid=(rows // tm, ff // tf),
        in_specs=[row_spec, vec_spec,
                  pl.BlockSpec((d, tf), lambda i, f: (0, f)),
                  pl.BlockSpec((d, tf), lambda i, f: (0, f)),
                  pl.BlockSpec((tf, d), lambda i, f: (f, 0)),
                  vec_spec],
        out_specs=out_specs,
        out_shape=out_shape,
        scratch_shapes=[pltpu.VMEM((tm, d), BF16), pltpu.VMEM((tm, d), F32)],
        compiler_params=_params("parallel", "arbitrary"),
        name="ffn_half",
    )(x, g.reshape(1, d), w1, w3, w2, g_post.reshape(1, d))


def _norm_body(x_ref, g_ref, o_ref):
    o_ref[...] = _rms(x_ref[...], g_ref[...]).astype(o_ref.dtype)


def _norm_rows(x, g, out_dtype):
    rows, d = x.shape
    tm = _tile(rows, 512)
    return pl.pallas_call(
        _norm_body,
        grid=(rows // tm,),
        in_specs=[pl.BlockSpec((tm, d), lambda i: (i, 0)), pl.BlockSpec((1, d), lambda i: (0, 0))],
        out_specs=pl.BlockSpec((tm, d), lambda i: (i, 0)),
        out_shape=jax.ShapeDtypeStruct((rows, d), out_dtype),
        compiler_params=_params("parallel"),
        name="rmsnorm_rows",
    )(x, g.reshape(1, d))


def _mm_body(a_ref, w_ref, o_ref):
    o_ref[...] = _dot(a_ref[...], w_ref[...]).astype(o_ref.dtype)


def _mm_res_body(a_ref, w_ref, r_ref, o_ref):
    o_ref[...] = (r_ref[...] + _dot(a_ref[...], w_ref[...])).astype(o_ref.dtype)


def _matmul_sections(a, w):
    rows, k = a.shape
    n_sec = w.shape[1] // SEC
    tm = _tile(rows, 1024)
    return pl.pallas_call(
        _mm_body,
        grid=(rows // tm, n_sec),
        in_specs=[pl.BlockSpec((tm, k), lambda i, j: (i, 0)),
                  pl.BlockSpec((k, SEC), lambda i, j: (0, j))],
        out_specs=pl.BlockSpec((None, tm, SEC), lambda i, j: (j, i, 0)),
        out_shape=jax.ShapeDtypeStruct((n_sec, rows, SEC), F32),
        compiler_params=_params("parallel", "arbitrary"),
        name="in_proj",
    )(a, w)


def _matmul(a, w, *, col_block=0, n_cols=None, residual=None, out_dtype=F32, name="matmul"):
    rows, k = a.shape
    n_cols = w.shape[1] if n_cols is None else n_cols
    tm, tn = _tile(rows, 1024), _tile(n_cols, 1024)
    in_specs = [pl.BlockSpec((tm, k), lambda i, j: (i, 0)),
                pl.BlockSpec((k, tn), lambda i, j: (0, col_block + j))]
    out_spec = pl.BlockSpec((tm, tn), lambda i, j: (i, j))
    args = [a, w]
    body = _mm_body
    if residual is not None:
        in_specs.append(out_spec)
        args.append(residual)
        body = _mm_res_body
    return pl.pallas_call(
        body,
        grid=(rows // tm, n_cols // tn),
        in_specs=in_specs,
        out_specs=out_spec,
        out_shape=jax.ShapeDtypeStruct((rows, n_cols), out_dtype),
        compiler_params=_params("parallel", "arbitrary"),
        name=name,
    )(*args)


def _t5_bucket(n):
    n = jnp.maximum(n, 0)
    max_exact = T5_BUCKETS // 2
    nf = jnp.maximum(n, 1).astype(F32)
    large = max_exact + (jnp.log(nf / max_exact) / math.log(T5_MAX_DIST / max_exact)
                         * (T5_BUCKETS - max_exact)).astype(I32)
    return jnp.where(n < max_exact, n, jnp.minimum(large, T5_BUCKETS - 1))


def _moba_prompt_body(t5_ref, q_ref, k_ref, v_ref, o_ref, kb_sc, vb_sc, kmean_sc, bias_sc, madd_sc, *, nb):
    blk = MOBA_BLOCK
    h, b, c = pl.program_id(0), pl.program_id(1), pl.program_id(2)
    scale = A_DH ** -0.5
    ii = lax.broadcasted_iota(I32, (blk, blk), 0)
    jj = lax.broadcasted_iota(I32, (blk, blk), 1)

    @pl.when((b == 0) & (c == 0))
    def _():
        for t in range(2):
            bucket = _t5_bucket(t * blk + ii - jj)
            tile = jnp.zeros((blk, blk), F32)
            for bkt in range(T5_BUCKETS):
                tile = jnp.where(bucket == bkt, t5_ref[bkt, h], tile)
            bias_sc[t] = tile

    @pl.when(c == 0)
    def _():
        kmean_sc[...] = jnp.zeros_like(kmean_sc)
        for n in range(nb):
            kf = k_ref[n * blk:(n + 1) * blk, :]
            kmean_sc[n:n + 1, :] = jnp.mean(kf, axis=0, keepdims=True)
            kb_sc[n * blk:(n + 1) * blk, :] = kf.astype(BF16)
        vb_sc[...] = v_ref[...].astype(BF16)

    q = q_ref[...]
    gate = lax.dot_general(q, kmean_sc[...], NT_DIMS, precision=lax.Precision.HIGHEST,
                           preferred_element_type=F32)
    n_iota = lax.broadcasted_iota(I32, gate.shape, 1)
    rank = jnp.zeros(gate.shape, F32)
    for m in range(nb):
        gm = gate[:, m:m + 1]
        beats = jnp.where(gm > gate, 1.0, jnp.where(gm == gate, jnp.where(n_iota > m, 1.0, 0.0), 0.0))
        rank = rank + jnp.where(m < c, beats, 0.0)
    sel = jnp.where(n_iota < c, jnp.where(rank < MOBA_TOPK, 0.0, NEG_INF), NEG_INF)
    for n in range(nb):
        madd_sc[n] = jnp.broadcast_to(sel[:, n:n + 1], (blk, 128))

    qb = q.astype(BF16)

    def scores(kblk):
        return lax.dot_general(qb, kblk, NT_DIMS, preferred_element_type=F32) * scale

    def rows(n):
        return pl.ds(pl.multiple_of(n * blk, blk), blk)

    s = scores(kb_sc[rows(c), :]) + bias_sc[0]
    s = jnp.where(ii >= jj, s, NEG_INF)
    m0 = jnp.max(s, axis=1, keepdims=True)
    p = jnp.exp(s - m0)
    l0 = jnp.sum(p, axis=1, keepdims=True)
    acc0 = _dot(p.astype(BF16), vb_sc[rows(c), :])
    far_bias = t5_ref[T5_BUCKETS - 1, h]

    def body(n, carry):
        m, l, acc = carry
        bias = jnp.where(n == c - 1, bias_sc[1], far_bias)
        ma = madd_sc[n]
        s = scores(kb_sc[rows(n), :]) + bias + jnp.concatenate([ma, ma], axis=1)
        m_new = jnp.maximum(m, jnp.max(s, axis=1, keepdims=True))
        alpha = jnp.exp(m - m_new)
        p = jnp.exp(s - m_new)
        l = alpha * l + jnp.sum(p, axis=1, keepdims=True)
        acc = alpha * acc + _dot(p.astype(BF16), vb_sc[rows(n), :])
        return m_new, l, acc

    _, l, acc = lax.fori_loop(0, c, body, (m0, l0, acc0))
    o_ref[...] = (acc / l).astype(o_ref.dtype)


def _moba_prompt(z3, t5_table, batch, seq):
    blk = MOBA_BLOCK
    nb = seq // blk
    nb_pad = -(-nb // 8) * 8
    return pl.pallas_call(
        functools.partial(_moba_prompt_body, nb=nb),
        grid=(A_HEADS, batch, nb),
        in_specs=[pl.BlockSpec(memory_space=pltpu.SMEM),
                  pl.BlockSpec((None, blk, A_DH), lambda h, b, c: (SEC_QA, b * nb + c, h)),
                  pl.BlockSpec((None, seq, A_DH), lambda h, b, c: (SEC_KA, b, h)),
                  pl.BlockSpec((None, seq, A_DH), lambda h, b, c: (SEC_VA, b, h))],
        out_specs=pl.BlockSpec((blk, A_DH), lambda h, b, c: (b * nb + c, h)),
        out_shape=jax.ShapeDtypeStruct((batch * seq, A_HEADS * A_DH), BF16),
        scratch_shapes=[pltpu.VMEM((seq, A_DH), BF16), pltpu.VMEM((seq, A_DH), BF16),
                        pltpu.VMEM((nb_pad, A_DH), F32), pltpu.VMEM((2, blk, blk), F32),
                        pltpu.VMEM((nb, blk, 128), F32)],
        compiler_params=_params("arbitrary", "arbitrary", "arbitrary"),
        name="moba_prompt",
    )(t5_table, z3, z3, z3)


def _rope_body(inv_ref, cos_ref, sin_ref, *, pos0):
    pos = pos0 + lax.broadcasted_iota(I32, cos_ref.shape, 0)
    ang = pos.astype(F32) * inv_ref[...]
    cos_ref[...] = jnp.cos(ang)
    sin_ref[...] = jnp.sin(ang)


def _rope_tables(pos0, n_pos):
    half = R_DK // 2
    inv = 1.0 / (RET_ROPE_BASE ** jnp.linspace(0.0, 1.0, half, dtype=F32))
    rows = -(-n_pos // 8) * 8
    return pl.pallas_call(
        functools.partial(_rope_body, pos0=pos0),
        out_shape=[jax.ShapeDtypeStruct((rows, half), F32)] * 2,
        name="rope_tables",
    )(inv.reshape(1, half))


def _rotate(x, cos, sin):
    half = x.shape[-1] // 2
    x1, x2 = x[:, :half], x[:, half:]
    return jnp.concatenate([x1 * cos - x2 * sin, x1 * sin + x2 * cos], axis=1)


def _log_decay(h, shape):
    hf = jnp.full(shape, h, I32).astype(F32)
    return jnp.log(1.0 - jnp.exp2(-5.0 - hf))


def _ret_prompt_body(q_ref, k_ref, v_ref, g_ref, cos_ref, sin_ref, o_ref, so_ref, s_sc, *, c):
    h, n = pl.program_id(1), pl.program_id(2)

    @pl.when(n == 0)
    def _():
        s_sc[...] = jnp.zeros_like(s_sc)

    lg = _log_decay(h, (c, c))
    i = lax.broadcasted_iota(I32, (c, c), 0).astype(F32)
    j = lax.broadcasted_iota(I32, (c, c), 1).astype(F32)
    diff = i - j
    dmask = jnp.where(diff >= 0, jnp.exp(jnp.maximum(diff, 0.0) * lg), 0.0)
    i_col = lax.broadcasted_iota(I32, (c, 1), 0).astype(F32)
    lg_col = _log_decay(h, (c, 1))
    q_dec = jnp.exp((i_col + 1.0) * lg_col)
    k_dec = jnp.exp((c - 1.0 - i_col) * lg_col)
    c_dec = jnp.exp(c * _log_decay(h, (1, R_DV)))

    cos, sin = cos_ref[...], sin_ref[...]
    qr = _rotate(q_ref[...], cos, sin)
    kr = _rotate(k_ref[...], cos, sin) * (R_DK ** -0.5)
    vb = v_ref[...].astype(BF16)
    s = s_sc[...]
    att = lax.dot_general(qr.astype(BF16), kr.astype(BF16), NT_DIMS, preferred_element_type=F32) * dmask
    o = _dot(att.astype(BF16), vb) + _dot((qr * q_dec).astype(BF16), s.astype(BF16))
    s_new = s * c_dec + lax.dot_general((kr * k_dec).astype(BF16), vb, TN_DIMS, preferred_element_type=F32)
    s_sc[...] = s_new
    on = o * lax.rsqrt(jnp.mean(o * o, axis=-1, keepdims=True) + EPS)
    o_ref[...] = (on * _silu(g_ref[...])).astype(o_ref.dtype)

    @pl.when(n == pl.num_programs(2) - 1)
    def _():
        so_ref[...] = s_new


def _ret_prompt(z3, cos, sin, batch, seq):
    c = math.gcd(seq, RET_CHUNK)
    nc = seq // c

    def sec(s):
        return pl.BlockSpec((None, c, R_DK), lambda b, h, n: (s, b * nc + n, h))

    tab = pl.BlockSpec((c, R_DK // 2), lambda b, h, n: (n, 0))
    return pl.pallas_call(
        functools.partial(_ret_prompt_body, c=c),
        grid=(batch, R_HEADS, nc),
        in_specs=[sec(SEC_QR), sec(SEC_KR), sec(SEC_VR), sec(SEC_GR), tab, tab],
        out_specs=[pl.BlockSpec((c, R_DV), lambda b, h, n: (b * nc + n, h)),
                   pl.BlockSpec((None, None, R_DK, R_DV), lambda b, h, n: (b, h, 0, 0))],
        out_shape=[jax.ShapeDtypeStruct((batch * seq, R_HEADS * R_DV), BF16),
                   jax.ShapeDtypeStruct((batch, R_HEADS, R_DK, R_DV), F32)],
        scratch_shapes=[pltpu.VMEM((R_DK, R_DV), F32)],
        compiler_params=_params("parallel", "parallel", "arbitrary"),
        name="retention_prompt",
    )(z3, z3, z3, z3, cos, sin)


def _ret_step_body(q_ref, k_ref, v_ref, g_ref, cos_ref, sin_ref, s_ref, o_ref, so_ref):
    cos, sin = cos_ref[0:1, :], sin_ref[0:1, :]
    row0 = lax.broadcasted_iota(I32, (8, R_DK), 0) == 0
    hi = lax.Precision.HIGHEST
    for h in range(R_HEADS):
        decay = jnp.exp(_log_decay(h, (1, R_DV)))
        qr = _rotate(q_ref[h:h + 1, :], cos, sin)
        kr = _rotate(k_ref[h:h + 1, :], cos, sin) * (R_DK ** -0.5)
        v = v_ref[h:h + 1, :]
        s = s_ref[h]
        q8 = jnp.broadcast_to(qr * decay, (8, R_DK))
        qs = lax.dot_general(q8, s, (((1,), (0,)), ((), ())), precision=hi, preferred_element_type=F32)[0:1, :]
        o = jnp.sum(qr * kr, axis=-1, keepdims=True) * v + qs
        k8 = jnp.where(row0, jnp.broadcast_to(kr, (8, R_DK)), 0.0)
        v8 = jnp.broadcast_to(v, (8, R_DV))
        so_ref[h] = s * decay + lax.dot_general(k8, v8, TN_DIMS, precision=hi, preferred_element_type=F32)
        on = o * lax.rsqrt(jnp.mean(o * o, axis=-1, keepdims=True) + EPS)
        o_ref[h:h + 1, :] = (on * _silu(g_ref[h:h + 1, :])).astype(o_ref.dtype)


def _ret_step(z3, cos, sin, state):
    batch = z3.shape[1]
    z4 = z3.reshape(8, batch, R_HEADS, R_DK)

    def sec(s):
        return pl.BlockSpec((None, None, R_HEADS, R_DK), lambda b: (s, b, 0, 0))

    tab = pl.BlockSpec((8, R_DK // 2), lambda b: (0, 0))
    st = pl.BlockSpec((None, R_HEADS, R_DK, R_DV), lambda b: (b, 0, 0, 0))
    return pl.pallas_call(
        _ret_step_body,
        grid=(batch,),
        in_specs=[sec(SEC_QR), sec(SEC_KR), sec(SEC_VR), sec(SEC_GR), tab, tab, st],
        out_specs=[pl.BlockSpec((None, R_HEADS, R_DV), lambda b: (b, 0, 0)), st],
        out_shape=[jax.ShapeDtypeStruct((batch, R_HEADS, R_DV), F32),
                   jax.ShapeDtypeStruct(state.shape, F32)],
        compiler_params=_params("parallel"),
        name="retention_step",
    )(z4, z4, z4, z4, cos, sin, state)


def _mem_body(q_ref, k_ref, v_ref, o_ref):
    s = lax.dot_general(q_ref[...].astype(BF16), k_ref[...].astype(BF16), NT_DIMS,
                        preferred_element_type=F32) * (M_DH ** -0.5)
    m = jnp.max(s, axis=-1, keepdims=True)
    p = jnp.exp(s - m)
    l = jnp.sum(p, axis=-1, keepdims=True)
    o_ref[...] = (_dot(p.astype(BF16), v_ref[...].astype(BF16)) / l).astype(o_ref.dtype)


def _mem_attend(z3, mk, mv, batch, seq):
    n_mem = mk.shape[0] // batch
    ts = _tile(seq, 512)
    nt = seq // ts
    kv = pl.BlockSpec((n_mem, M_DH), lambda b, t, h: (b, h))
    return pl.pallas_call(
        _mem_body,
        grid=(batch, nt, M_HEADS),
        in_specs=[pl.BlockSpec((None, ts, M_DH), lambda b, t, h: (SEC_QM, b * nt + t, h)), kv, kv],
        out_specs=pl.BlockSpec((ts, M_DH), lambda b, t, h: (b * nt + t, h)),
        out_shape=jax.ShapeDtypeStruct((batch * seq, M_HEADS * M_DH), BF16),
        compiler_params=_params("parallel", "parallel", "arbitrary"),
        name="mem_attend",
    )(z3, mk, mv)


def _merge_body(h_ref, oa_ref, or_ref, om_ref, wga_ref, wgr_ref, wgm_ref, bga_ref, bgr_ref, bgm_ref,
                wba_ref, wbr_ref, wbm_ref, o_ref):
    h = h_ref[...]

    def branch(x_ref, wg_ref, bg_ref, wb_ref):
        return jax.nn.sigmoid(_dot(h, wg_ref[...]) + bg_ref[...]) * _dot(x_ref[...], wb_ref[...])

    merged = (branch(oa_ref, wga_ref, bga_ref, wba_ref) + branch(or_ref, wgr_ref, bgr_ref, wbr_ref)
              + branch(om_ref, wgm_ref, bgm_ref, wbm_ref))
    o_ref[...] = merged.astype(o_ref.dtype)


def _merge(h, oa, o_r, om, w_gate, b_gate, w_br_moba, w_br_ret, w_br_mem):
    rows, d = h.shape
    tm, tn = _tile(rows, 1024), _tile(d, 512)
    nj = d // tn
    b_gate = b_gate.reshape(1, 3 * d)

    def row(width):
        return pl.BlockSpec((tm, width), lambda i, j: (i, 0))

    def gate_w(g):
        return pl.BlockSpec((d, tn), lambda i, j: (0, g * nj + j))

    def gate_b(g):
        return pl.BlockSpec((1, tn), lambda i, j: (0, g * nj + j))

    def br_w(w):
        return pl.BlockSpec((w.shape[0], tn), lambda i, j: (0, j))

    return pl.pallas_call(
        _merge_body,
        grid=(rows // tm, nj),
        in_specs=[row(d), row(oa.shape[1]), row(o_r.shape[1]), row(om.shape[1]),
                  gate_w(0), gate_w(1), gate_w(2), gate_b(0), gate_b(1), gate_b(2),
                  br_w(w_br_moba), br_w(w_br_ret), br_w(w_br_mem)],
        out_specs=pl.BlockSpec((tm, tn), lambda i, j: (i, j)),
        out_shape=jax.ShapeDtypeStruct((rows, d), BF16),
        compiler_params=_params("parallel", "arbitrary"),
        name="gated_merge",
    )(h, oa, o_r, om, w_gate, w_gate, w_gate, b_gate, b_gate, b_gate, w_br_moba, w_br_ret, w_br_mem)


def _moba_step_body(pt_ref, t5t_ref, q_ref, kn_ref, vn_ref, k0_ref, k1_ref, v0_ref, v1_ref, o_ref,
                    g_sc, m_sc, l_sc, o_sc, *, nb_past, past_len):
    del pt_ref
    blk = MOBA_BLOCK
    n = pl.program_id(1)
    scale = A_DH ** -0.5
    q = q_ref[...]
    tab = t5t_ref[...]
    lane = lax.broadcasted_iota(I32, (A_HEADS, 128), 1)

    @pl.when(n == 0)
    def _():
        g_sc[...] = jnp.zeros_like(g_sc)
        l_sc[...] = jnp.zeros_like(l_sc)
        m_sc[...] = jnp.full_like(m_sc, NEG_INF)

    def far_bias():
        return [jnp.broadcast_to(tab[:, T5_BUCKETS - 1:T5_BUCKETS][None], (PAGE_SIZE, A_HEADS, 1))] * 2

    def near_bias():
        out = []
        for t in range(2):
            j = lax.broadcasted_iota(I32, (PAGE_SIZE, 1, 1), 0)
            bucket = _t5_bucket(past_len - (n * blk + t * PAGE_SIZE + j))
            bias = jnp.zeros((PAGE_SIZE, A_HEADS, 1), F32)
            for bkt in range(T5_BUCKETS):
                bias = jnp.where(bucket == bkt, tab[:, bkt:bkt + 1][None], bias)
            out.append(bias)
        return out

    bias = lax.cond(n * blk + blk - 1 + T5_MAX_DIST <= past_len, far_bias, near_bias)

    ks = [k0_ref[...], k1_ref[...]]
    vs = [v0_ref[...], v1_ref[...]]
    s = [jnp.sum(ks[t] * q[None], axis=2, keepdims=True) * scale + bias[t] for t in range(2)]
    mb = jnp.maximum(jnp.max(s[0], axis=0), jnp.max(s[1], axis=0))
    p = [jnp.exp(s[t] - mb[None]) for t in range(2)]
    lb = jnp.sum(p[0], axis=0) + jnp.sum(p[1], axis=0)
    ob = jnp.sum(p[0] * vs[0], axis=0) + jnp.sum(p[1] * vs[1], axis=0)
    ksum = jnp.sum(ks[0], axis=0) + jnp.sum(ks[1], axis=0)
    gate = jnp.sum(q * ksum, axis=1, keepdims=True) / blk
    here = lane == n
    g_sc[...] = jnp.where(here, gate, g_sc[...])
    m_sc[...] = jnp.where(here, mb, m_sc[...])
    l_sc[...] = jnp.where(here, lb, l_sc[...])
    o_sc[n] = ob

    @pl.when(n == nb_past - 1)
    def _():
        gw = jnp.where(lane < nb_past, g_sc[...], NEG_INF)
        sel = jnp.zeros(gw.shape, F32)
        for _ in range(min(MOBA_TOPK, nb_past + 1)):
            mx = jnp.max(gw, axis=1, keepdims=True)
            idx = jnp.min(jnp.where(gw == mx, lane, 128), axis=1, keepdims=True)
            pick = lane == idx
            sel = jnp.where(pick, jnp.where(lane < nb_past, 1.0, 0.0), sel)
            gw = jnp.where(pick, -jnp.inf, gw)
        s_own = jnp.sum(q * kn_ref[...], axis=1, keepdims=True) * scale + tab[:, 0:1]
        m_all = jnp.maximum(jnp.max(jnp.where(sel > 0, m_sc[...], NEG_INF), axis=1, keepdims=True), s_own)
        w = jnp.where(sel > 0, jnp.exp(jnp.minimum(m_sc[...] - m_all, 0.0)), 0.0)
        w_own = jnp.exp(s_own - m_all)
        l_all = jnp.sum(w * l_sc[...], axis=1, keepdims=True) + w_own
        o_all = w_own * vn_ref[...]
        for nn in range(nb_past):
            o_all = o_all + w[:, nn:nn + 1] * o_sc[nn]
        o_ref[...] = (o_all / l_all).astype(o_ref.dtype)


def _moba_step(z3, cache_k, cache_v, page_table, t5_table):
    batch, n_pages = page_table.shape
    past_len = n_pages * PAGE_SIZE
    assert past_len % MOBA_BLOCK == 0 and MOBA_BLOCK == 2 * PAGE_SIZE
    nb_past = past_len // MOBA_BLOCK
    assert 0 < nb_past <= 128
    z4 = z3.reshape(8, batch, A_HEADS, A_DH)

    def sec(s):
        return pl.BlockSpec((None, None, A_HEADS, A_DH), lambda b, n, pt: (s, b, 0, 0))

    def page(t):
        return pl.BlockSpec((None, PAGE_SIZE, A_HEADS, A_DH), lambda b, n, pt: (pt[b, 2 * n + t], 0, 0, 0))

    grid_spec = pltpu.PrefetchScalarGridSpec(
        num_scalar_prefetch=1,
        grid=(batch, nb_past),
        in_specs=[pl.BlockSpec((A_HEADS, T5_BUCKETS), lambda b, n, pt: (0, 0)),
                  sec(SEC_QA), sec(SEC_KA), sec(SEC_VA), page(0), page(1), page(0), page(1)],
        out_specs=pl.BlockSpec((None, A_HEADS, A_DH), lambda b, n, pt: (b, 0, 0)),
        scratch_shapes=[pltpu.VMEM((A_HEADS, 128), F32)] * 3 + [pltpu.VMEM((nb_past, A_HEADS, A_DH), F32)],
    )
    return pl.pallas_call(
        functools.partial(_moba_step_body, nb_past=nb_past, past_len=past_len),
        grid_spec=grid_spec,
        out_shape=jax.ShapeDtypeStruct((batch, A_HEADS, A_DH), F32),
        compiler_params=_params("parallel", "arbitrary"),
        name="moba_step",
    )(page_table, t5_table.T, z4, z4, z4, cache_k, cache_k, cache_v, cache_v)


def _mixer_tail(x1, h2, oa, o_r, om, lw):
    merged = _merge(h2, oa, o_r, om, lw["w_gate"], lw["b_gate"], lw["w_br_moba"], lw["w_br_ret"], lw["w_br_mem"])
    return _matmul(merged, lw["w_out"], residual=x1, name="out_proj")


def kernel(x_prompt, x_sample, mem_prompt, cache_k, cache_v, cache_mem_k, cache_mem_v, state_ret, page_table, t5_table, ffn1_norm, ffn1_w1, ffn1_w3, ffn1_w2, mix_norm, mem_norm, w_in, w_mem_kv, w_gate, b_gate, w_br_moba, w_br_ret, w_br_mem, w_out, ffn2_norm, ffn2_w1, ffn2_w3, ffn2_w2, final_norm):
    batch, seq, d = x_prompt.shape
    dec_batch, dec_seq, _ = x_sample.shape
    assert dec_seq == 1
    depth = w_in.shape[0]
    n_mem = mem_prompt.shape[1]
    past_len = page_table.shape[1] * PAGE_SIZE
    cos_p, sin_p = _rope_tables(0, seq)
    cos_s, sin_s = _rope_tables(past_len, dec_seq)

    xp = x_prompt.reshape(batch * seq, d)
    xs = x_sample.reshape(dec_batch, d)
    outs = [[] for _ in range(8)]
    for l in range(depth):
        last = l == depth - 1
        lw = dict(w_gate=w_gate[l].astype(BF16), b_gate=b_gate[l], w_br_moba=w_br_moba[l].astype(BF16),
                  w_br_ret=w_br_ret[l].astype(BF16), w_br_mem=w_br_mem[l].astype(BF16),
                  w_out=w_out[l].astype(BF16))
        f1 = (ffn1_norm[l], ffn1_w1[l].astype(BF16), ffn1_w3[l].astype(BF16), ffn1_w2[l].astype(BF16), mix_norm[l])
        f2 = (ffn2_norm[l], ffn2_w1[l].astype(BF16), ffn2_w3[l].astype(BF16), ffn2_w2[l].astype(BF16))
        g_next = final_norm if last else ffn1_norm[l + 1]
        w_in_l = w_in[l].astype(BF16)
        w_mem_l = w_mem_kv[l].astype(BF16)

        mem_h = _norm_rows(mem_prompt.reshape(batch * n_mem, d), mem_norm[l], BF16)
        mk_p = _matmul(mem_h, w_mem_l, col_block=0, n_cols=M_HEADS * M_DH, name="mem_k")
        mv_p = _matmul(mem_h, w_mem_l, col_block=1, n_cols=M_HEADS * M_DH, name="mem_v")
        x1, h2 = _ffn_half(xp, *f1, emit_x=True, post_dtype=BF16)
        z3 = _matmul_sections(h2, w_in_l)
        oa = _moba_prompt(z3, t5_table, batch, seq)
        o_r, s_new_p = _ret_prompt(z3, cos_p, sin_p, batch, seq)
        om = _mem_attend(z3, mk_p, mv_p, batch, seq)
        x2 = _mixer_tail(x1, h2, oa, o_r, om, lw)
        if last:
            (xp,) = _ffn_half(x2, *f2, g_next, emit_x=False, post_dtype=F32)
        else:
            xp, _ = _ffn_half(x2, *f2, g_next, emit_x=True, post_dtype=F32)

        x1s, h2s = _ffn_half(xs, *f1, emit_x=True, post_dtype=BF16)
        z3s = _matmul_sections(h2s, w_in_l)
        oa_s = _moba_step(z3s, cache_k[l], cache_v[l], page_table, t5_table)
        o_rs, s_new_s = _ret_step(z3s, cos_s, sin_s, state_ret[l])
        z3s_pad = jnp.broadcast_to(z3s[:, :, None, :], (8, dec_batch, 8, SEC)).reshape(8, dec_batch * 8, SEC)
        om_s = _mem_attend(z3s_pad, cache_mem_k[l].reshape(dec_batch * n_mem, M_HEADS * M_DH),
                           cache_mem_v[l].reshape(dec_batch * n_mem, M_HEADS * M_DH), dec_batch, 8)
        om_s = om_s.reshape(dec_batch, 8, M_HEADS * M_DH)[:, 0, :]
        x2s = _mixer_tail(x1s, h2s, oa_s.reshape(dec_batch, -1).astype(BF16),
                          o_rs.reshape(dec_batch, -1).astype(BF16), om_s, lw)
        if last:
            (xs,) = _ffn_half(x2s, *f2, g_next, emit_x=False, post_dtype=F32)
        else:
            xs, _ = _ffn_half(x2s, *f2, g_next, emit_x=True, post_dtype=F32)

        new = (z3[SEC_KA].reshape(batch, seq, A_HEADS, A_DH), z3[SEC_VA].reshape(batch, seq, A_HEADS, A_DH),
               mk_p.reshape(batch, n_mem, M_HEADS, M_DH), mv_p.reshape(batch, n_mem, M_HEADS, M_DH), s_new_p,
               z3s[SEC_KA].reshape(dec_batch, dec_seq, A_HEADS, A_DH),
               z3s[SEC_VA].reshape(dec_batch, dec_seq, A_HEADS, A_DH), s_new_s)
        for acc, val in zip(outs, new):
            acc.append(val)

    y_prompt = xp.reshape(batch, seq, d)
    y_sample = xs.reshape(dec_batch, dec_seq, d)
    return (y_prompt, y_sample) + tuple(jnp.stack(o) for o in outs)
```

```python
import functools
import math

import jax
import jax.numpy as jnp
import numpy as np
from jax import lax
from jax.experimental import pallas as pl
from jax.experimental.pallas import tpu as pltpu

F32 = jnp.float32
BF16 = jnp.bfloat16
I32 = jnp.int32

A_HEADS, A_DH = 8, 128
MOBA_BLOCK, MOBA_TOPK = 256, 3
T5_BUCKETS, T5_MAX_DIST = 32, 128
R_HEADS, R_DK, R_DV = 4, 256, 256
RET_CHUNK = 128
RET_ROPE_BASE = 10000.0
M_HEADS, M_DH = 4, 256
PAGE_SIZE = 128
EPS = 1e-6
NEG_INF = -1e30
SEC = 1024
(SEC_QA, SEC_KA, SEC_VA, SEC_QR, SEC_KR, SEC_VR, SEC_GR, SEC_QM) = range(8)

V7X_VMEM_LIMIT_BYTES = 56 * 1024 * 1024

NT_DIMS = (((1,), (1,)), ((), ()))
TN_DIMS = (((0,), (0,)), ((), ()))


def _params(*sem):
    return pltpu.CompilerParams(dimension_semantics=sem, vmem_limit_bytes=V7X_VMEM_LIMIT_BYTES)


def _tile(n, pref):
    t = min(n, pref)
    while n % t:
        t -= 1
    return t


def _rms(x, g):
    return x * lax.rsqrt(jnp.mean(x * x, axis=-1, keepdims=True) + EPS) * g


def _silu(x):
    return x * jax.nn.sigmoid(x)


def _dot(a, b):
    return jnp.dot(a, b, preferred_element_type=F32)


def _bf16(w):
    return w if w.dtype == BF16 else w.astype(BF16)


def _ffn_body(x_ref, g_ref, w1_ref, w3_ref, w2_ref, gp_ref, *refs, emit_x, emit_w):
    refs = list(refs)
    xo_ref = refs.pop(0) if emit_x else None
    ho_ref = refs.pop(0)
    wb_refs = [refs.pop(0) for _ in range(3)] if emit_w else []
    h_sc, acc_sc = refs
    f = pl.program_id(1)

    @pl.when(f == 0)
    def _():
        h_sc[...] = _rms(x_ref[...], g_ref[...]).astype(BF16)
        acc_sc[...] = jnp.zeros_like(acc_sc)

    w1, w3, w2 = _bf16(w1_ref[...]), _bf16(w3_ref[...]), _bf16(w2_ref[...])
    for wb_ref, w in zip(wb_refs, (w1, w3, w2)):
        wb_ref[...] = w
    h = h_sc[...]
    t = _silu(_dot(h, w1)) * _dot(h, w3)
    acc_sc[...] += _dot(t.astype(BF16), w2)

    @pl.when(f == pl.num_programs(1) - 1)
    def _():
        xo = x_ref[...] + 0.5 * acc_sc[...]
        if emit_x:
            xo_ref[...] = xo
        ho_ref[...] = _rms(xo, gp_ref[...]).astype(ho_ref.dtype)


def _ffn_half(x, g, w1, w3, w2, g_post, *, emit_x, post_dtype, emit_w=False):
    rows, d = x.shape
    ff = w1.shape[1]
    tm, tf = _tile(rows, 512), _tile(ff, 512)
    assert not emit_w or rows == tm
    row_spec = pl.BlockSpec((tm, d), lambda i, f: (i, 0))
    vec_spec = pl.BlockSpec((1, d), lambda i, f: (0, 0))
    w13_spec = pl.BlockSpec((d, tf), lambda i, f: (0, f))
    w2_spec = pl.BlockSpec((tf, d), lambda i, f: (f, 0))
    out_shape = [jax.ShapeDtypeStruct((rows, d), post_dtype)]
    out_specs = [row_spec]
    if emit_x:
        out_shape = [jax.ShapeDtypeStruct((rows, d), F32)] + out_shape
        out_specs = [row_spec] + out_specs
    if emit_w:
        out_shape += [jax.ShapeDtypeStruct(w.shape, BF16) for w in (w1, w3, w2)]
        out_specs += [w13_spec, w13_spec, w2_spec]
    return pl.pallas_call(
        functools.partial(_ffn_body, emit_x=emit_x, emit_w=emit_w),
        grid=(rows // tm, ff // tf),
        in_specs=[row_spec, vec_spec, w13_spec, w13_spec, w2_spec, vec_spec],
        out_specs=out_specs,
        out_shape=out_shape,
        scratch_shapes=[pltpu.VMEM((tm, d), BF16), pltpu.VMEM((tm, d), F32)],
        compiler_params=_params("parallel", "arbitrary"),
        name="ffn_half",
    )(x, g.reshape(1, d), w1, w3, w2, g_post.reshape(1, d))


def _norm_body(x_ref, g_ref, o_ref):
    o_ref[...] = _rms(x_ref[...], g_ref[...]).astype(o_ref.dtype)


def _norm_rows(x, g, out_dtype):
    rows, d = x.shape
    tm = _tile(rows, 512)
    return pl.pallas_call(
        _norm_body,
        grid=(rows // tm,),
        in_specs=[pl.BlockSpec((tm, d), lambda i: (i, 0)), pl.BlockSpec((1, d), lambda i: (0, 0))],
        out_specs=pl.BlockSpec((tm, d), lambda i: (i, 0)),
        out_shape=jax.ShapeDtypeStruct((rows, d), out_dtype),
        compiler_params=_params("parallel"),
        name="rmsnorm_rows",
    )(x, g.reshape(1, d))


def _mm_body(a_ref, w_ref, *refs, has_residual, emit_w):
    refs = list(refs)
    r_ref = refs.pop(0) if has_residual else None
    o_ref = refs.pop(0)
    w = _bf16(w_ref[...])
    if emit_w:
        refs.pop(0)[...] = w
    acc = _dot(a_ref[...], w)
    if has_residual:
        acc = r_ref[...] + acc
    o_ref[...] = acc.astype(o_ref.dtype)


def _matmul_sections(a, w, *, emit_w=False):
    rows, k = a.shape
    n_sec = w.shape[1] // SEC
    tm = _tile(rows, 1024)
    assert not emit_w or rows == tm
    w_spec = pl.BlockSpec((k, SEC), lambda i, j: (0, j))
    out_specs = [pl.BlockSpec((None, tm, SEC), lambda i, j: (j, i, 0))]
    out_shape = [jax.ShapeDtypeStruct((n_sec, rows, SEC), F32)]
    if emit_w:
        out_specs.append(w_spec)
        out_shape.append(jax.ShapeDtypeStruct(w.shape, BF16))
    out = pl.pallas_call(
        functools.partial(_mm_body, has_residual=False, emit_w=emit_w),
        grid=(rows // tm, n_sec),
        in_specs=[pl.BlockSpec((tm, k), lambda i, j: (i, 0)), w_spec],
        out_specs=out_specs,
        out_shape=out_shape,
        compiler_params=_params("parallel", "arbitrary"),
        name="in_proj",
    )(a, w)
    return out if emit_w else out[0]


def _matmul(a, w, *, col_block=0, n_cols=None, residual=None, out_dtype=F32, emit_w=False, name="matmul"):
    rows, k = a.shape
    n_cols = w.shape[1] if n_cols is None else n_cols
    tm, tn = _tile(rows, 1024), _tile(n_cols, 1024)
    assert not emit_w or (rows == tm and n_cols == w.shape[1] and col_block == 0)
    w_spec = pl.BlockSpec((k, tn), lambda i, j: (0, col_block + j))
    in_specs = [pl.BlockSpec((tm, k), lambda i, j: (i, 0)), w_spec]
    out_spec = pl.BlockSpec((tm, tn), lambda i, j: (i, j))
    args = [a, w]
    if residual is not None:
        in_specs.append(out_spec)
        args.append(residual)
    out_specs = [out_spec]
    out_shape = [jax.ShapeDtypeStruct((rows, n_cols), out_dtype)]
    if emit_w:
        out_specs.append(w_spec)
        out_shape.append(jax.ShapeDtypeStruct(w.shape, BF16))
    out = pl.pallas_call(
        functools.partial(_mm_body, has_residual=residual is not None, emit_w=emit_w),
        grid=(rows // tm, n_cols // tn),
        in_specs=in_specs,
        out_specs=out_specs,
        out_shape=out_shape,
        compiler_params=_params("parallel", "arbitrary"),
        name=name,
    )(*args)
    return out if emit_w else out[0]


def _t5_bucket(n):
    n = jnp.maximum(n, 0)
    max_exact = T5_BUCKETS // 2
    nf = jnp.maximum(n, 1).astype(F32)
    large = max_exact + (jnp.log(nf / max_exact) / math.log(T5_MAX_DIST / max_exact)
                         * (T5_BUCKETS - max_exact)).astype(I32)
    return jnp.where(n < max_exact, n, jnp.minimum(large, T5_BUCKETS - 1))


def _moba_prompt_body(t5_ref, q_ref, k_ref, v_ref, o_ref, kb_sc, vb_sc, kmean_sc, bias_sc, *, nb):
    blk = MOBA_BLOCK
    h, b, c = pl.program_id(0), pl.program_id(1), pl.program_id(2)
    scale = A_DH ** -0.5
    ii = lax.broadcasted_iota(I32, (blk, blk), 0)
    jj = lax.broadcasted_iota(I32, (blk, blk), 1)

    @pl.when((b == 0) & (c == 0))
    def _():
        for t in range(2):
            bucket = _t5_bucket(t * blk + ii - jj)
            tile = jnp.zeros((blk, blk), F32)
            for bkt in range(T5_BUCKETS):
                tile = jnp.where(bucket == bkt, t5_ref[bkt, h], tile)
            bias_sc[t] = tile

    @pl.when(c == 0)
    def _():
        kmean_sc[...] = jnp.zeros_like(kmean_sc)
        for n in range(nb):
            kf = k_ref[n * blk:(n + 1) * blk, :]
            kmean_sc[n:n + 1, :] = jnp.mean(kf, axis=0, keepdims=True)
            kb_sc[n * blk:(n + 1) * blk, :] = kf.astype(BF16)
        vb_sc[...] = v_ref[...].astype(BF16)

    q = q_ref[...]
    gate = lax.dot_general(q, kmean_sc[...], NT_DIMS, precision=lax.Precision.HIGHEST,
                           preferred_element_type=F32)
    n_iota = lax.broadcasted_iota(I32, gate.shape, 1)
    rank = jnp.zeros(gate.shape, F32)
    for m in range(nb):
        gm = gate[:, m:m + 1]
        beats = jnp.where(gm > gate, 1.0, jnp.where(gm == gate, jnp.where(n_iota > m, 1.0, 0.0), 0.0))
        rank = rank + jnp.where(m < c, beats, 0.0)
    sel = jnp.where(n_iota < c, jnp.where(rank < MOBA_TOPK, 0.0, NEG_INF), NEG_INF)
    causal = jnp.where(ii >= jj, 0.0, NEG_INF)
    far_bias = t5_ref[T5_BUCKETS - 1, h]
    qb = q.astype(BF16)

    def attend(n_blocks):
        s_all = lax.dot_general(qb, kb_sc[0:n_blocks * blk, :], NT_DIMS, preferred_element_type=F32) * scale
        tiles = []
        for n in range(n_blocks):
            bias = jnp.where(n == c, bias_sc[0], jnp.where(n == c - 1, bias_sc[1], far_bias))
            mask = jnp.where(n == c, causal, sel[:, n:n + 1])
            tiles.append(s_all[:, n * blk:(n + 1) * blk] + bias + mask)
        m = jnp.max(functools.reduce(jnp.maximum, tiles), axis=1, keepdims=True)
        ps = [jnp.exp(t - m) for t in tiles]
        l = jnp.sum(functools.reduce(jnp.add, ps), axis=1, keepdims=True)
        p_all = jnp.concatenate([p.astype(BF16) for p in ps], axis=1)
        o_ref[...] = (_dot(p_all, vb_sc[0:n_blocks * blk, :]) / l).astype(o_ref.dtype)

    step = 2 if nb % 2 == 0 else 1
    for hi in range(step, nb + 1, step):
        pl.when((c >= hi - step) & (c < hi))(functools.partial(attend, hi))


def _moba_prompt(z3, t5_table, batch, seq):
    blk = MOBA_BLOCK
    nb = seq // blk
    nb_pad = -(-nb // 8) * 8
    return pl.pallas_call(
        functools.partial(_moba_prompt_body, nb=nb),
        grid=(A_HEADS, batch, nb),
        in_specs=[pl.BlockSpec(memory_space=pltpu.SMEM),
                  pl.BlockSpec((None, blk, A_DH), lambda h, b, c: (SEC_QA, b * nb + c, h)),
                  pl.BlockSpec((None, seq, A_DH), lambda h, b, c: (SEC_KA, b, h)),
                  pl.BlockSpec((None, seq, A_DH), lambda h, b, c: (SEC_VA, b, h))],
        out_specs=pl.BlockSpec((blk, A_DH), lambda h, b, c: (b * nb + c, h)),
        out_shape=jax.ShapeDtypeStruct((batch * seq, A_HEADS * A_DH), BF16),
        scratch_shapes=[pltpu.VMEM((seq, A_DH), BF16), pltpu.VMEM((seq, A_DH), BF16),
                        pltpu.VMEM((nb_pad, A_DH), F32), pltpu.VMEM((2, blk, blk), F32)],
        compiler_params=_params("arbitrary", "arbitrary", "arbitrary"),
        name="moba_prompt",
    )(t5_table, z3, z3, z3)


def _rope_body(inv_ref, cos_ref, sin_ref, *, pos0):
    pos = pos0 + lax.broadcasted_iota(I32, cos_ref.shape, 0)
    ang = pos.astype(F32) * inv_ref[...]
    cos_ref[...] = jnp.cos(ang)
    sin_ref[...] = jnp.sin(ang)


def _rope_tables(pos0, n_pos):
    half = R_DK // 2
    inv = 1.0 / (RET_ROPE_BASE ** jnp.linspace(0.0, 1.0, half, dtype=F32))
    rows = -(-n_pos // 8) * 8
    return pl.pallas_call(
        functools.partial(_rope_body, pos0=pos0),
        out_shape=[jax.ShapeDtypeStruct((rows, half), F32)] * 2,
        name="rope_tables",
    )(inv.reshape(1, half))


def _rotate(x, cos, sin):
    half = x.shape[-1] // 2
    x1, x2 = x[:, :half], x[:, half:]
    return jnp.concatenate([x1 * cos - x2 * sin, x1 * sin + x2 * cos], axis=1)


def _log_decay(h, shape):
    hf = jnp.full(shape, h, I32).astype(F32)
    return jnp.log(1.0 - jnp.exp2(-5.0 - hf))


def _ret_prompt_body(q_ref, k_ref, v_ref, g_ref, cos_ref, sin_ref, o_ref, so_ref, s_sc, *, c):
    n = pl.program_id(1)

    @pl.when(n == 0)
    def _():
        s_sc[...] = jnp.zeros_like(s_sc)

    i = lax.broadcasted_iota(I32, (c, c), 0).astype(F32)
    j = lax.broadcasted_iota(I32, (c, c), 1).astype(F32)
    diff = i - j
    i_col = lax.broadcasted_iota(I32, (c, 1), 0).astype(F32)
    cos, sin = cos_ref[...], sin_ref[...]

    for h in range(R_HEADS):
        dk, dv = slice(h * R_DK, (h + 1) * R_DK), slice(h * R_DV, (h + 1) * R_DV)
        dmask = jnp.where(diff >= 0, jnp.exp(jnp.maximum(diff, 0.0) * _log_decay(h, (c, c))), 0.0)
        lg_col = _log_decay(h, (c, 1))
        q_dec = jnp.exp((i_col + 1.0) * lg_col)
        k_dec = jnp.exp((c - 1.0 - i_col) * lg_col)
        c_dec = jnp.exp(c * _log_decay(h, (1, R_DV)))
        qr = _rotate(q_ref[:, dk], cos, sin)
        kr = _rotate(k_ref[:, dk], cos, sin) * (R_DK ** -0.5)
        vb = v_ref[:, dv].astype(BF16)
        s = s_sc[h]
        att = lax.dot_general(qr.astype(BF16), kr.astype(BF16), NT_DIMS, preferred_element_type=F32) * dmask
        o = _dot(att.astype(BF16), vb) + _dot((qr * q_dec).astype(BF16), s.astype(BF16))
        s_sc[h] = s * c_dec + lax.dot_general((kr * k_dec).astype(BF16), vb, TN_DIMS, preferred_element_type=F32)
        on = o * lax.rsqrt(jnp.mean(o * o, axis=-1, keepdims=True) + EPS)
        o_ref[:, dv] = (on * _silu(g_ref[:, dv])).astype(o_ref.dtype)

    @pl.when(n == pl.num_programs(1) - 1)
    def _():
        so_ref[...] = s_sc[...]


def _ret_prompt(z3, cos, sin, batch, seq):
    c = math.gcd(seq, RET_CHUNK)
    nc = seq // c

    def sec(s):
        return pl.BlockSpec((None, c, SEC), lambda b, n: (s, b * nc + n, 0))

    tab = pl.BlockSpec((c, R_DK // 2), lambda b, n: (n, 0))
    return pl.pallas_call(
        functools.partial(_ret_prompt_body, c=c),
        grid=(batch, nc),
        in_specs=[sec(SEC_QR), sec(SEC_KR), sec(SEC_VR), sec(SEC_GR), tab, tab],
        out_specs=[pl.BlockSpec((c, R_HEADS * R_DV), lambda b, n: (b * nc + n, 0)),
                   pl.BlockSpec((None, R_HEADS, R_DK, R_DV), lambda b, n: (b, 0, 0, 0))],
        out_shape=[jax.ShapeDtypeStruct((batch * seq, R_HEADS * R_DV), BF16),
                   jax.ShapeDtypeStruct((batch, R_HEADS, R_DK, R_DV), F32)],
        scratch_shapes=[pltpu.VMEM((R_HEADS, R_DK, R_DV), F32)],
        compiler_params=_params("parallel", "arbitrary"),
        name="retention_prompt",
    )(z3, z3, z3, z3, cos, sin)


def _ret_step_body(q_ref, k_ref, v_ref, g_ref, cos_ref, sin_ref, s_ref, o_ref, so_ref):
    cos, sin = cos_ref[0:1, :], sin_ref[0:1, :]
    row0 = lax.broadcasted_iota(I32, (8, R_DK), 0) == 0
    hi = lax.Precision.HIGHEST
    for h in range(R_HEADS):
        decay = jnp.exp(_log_decay(h, (1, R_DV)))
        qr = _rotate(q_ref[h:h + 1, :], cos, sin)
        kr = _rotate(k_ref[h:h + 1, :], cos, sin) * (R_DK ** -0.5)
        v = v_ref[h:h + 1, :]
        s = s_ref[h]
        q8 = jnp.broadcast_to(qr * decay, (8, R_DK))
        qs = lax.dot_general(q8, s, (((1,), (0,)), ((), ())), precision=hi, preferred_element_type=F32)[0:1, :]
        o = jnp.sum(qr * kr, axis=-1, keepdims=True) * v + qs
        k8 = jnp.where(row0, jnp.broadcast_to(kr, (8, R_DK)), 0.0)
        v8 = jnp.broadcast_to(v, (8, R_DV))
        so_ref[h] = s * decay + lax.dot_general(k8, v8, TN_DIMS, precision=hi, preferred_element_type=F32)
        on = o * lax.rsqrt(jnp.mean(o * o, axis=-1, keepdims=True) + EPS)
        o_ref[h:h + 1, :] = (on * _silu(g_ref[h:h + 1, :])).astype(o_ref.dtype)


def _ret_step(z3, cos, sin, state):
    batch = z3.shape[1]
    z4 = z3.reshape(8, batch, R_HEADS, R_DK)

    def sec(s):
        return pl.BlockSpec((None, None, R_HEADS, R_DK), lambda b: (s, b, 0, 0))

    tab = pl.BlockSpec((8, R_DK // 2), lambda b: (0, 0))
    st = pl.BlockSpec((None, R_HEADS, R_DK, R_DV), lambda b: (b, 0, 0, 0))
    return pl.pallas_call(
        _ret_step_body,
        grid=(batch,),
        in_specs=[sec(SEC_QR), sec(SEC_KR), sec(SEC_VR), sec(SEC_GR), tab, tab, st],
        out_specs=[pl.BlockSpec((None, R_HEADS, R_DV), lambda b: (b, 0, 0)), st],
        out_shape=[jax.ShapeDtypeStruct((batch, R_HEADS, R_DV), F32),
                   jax.ShapeDtypeStruct(state.shape, F32)],
        compiler_params=_params("parallel"),
        name="retention_step",
    )(z4, z4, z4, z4, cos, sin, state)


def _mem_body(q_ref, k_ref, v_ref, o_ref):
    s = lax.dot_general(q_ref[...].astype(BF16), k_ref[...].astype(BF16), NT_DIMS,
                        preferred_element_type=F32) * (M_DH ** -0.5)
    m = jnp.max(s, axis=-1, keepdims=True)
    p = jnp.exp(s - m)
    l = jnp.sum(p, axis=-1, keepdims=True)
    o_ref[...] = (_dot(p.astype(BF16), v_ref[...].astype(BF16)) / l).astype(o_ref.dtype)


def _mem_attend(z3, mk, mv, batch, seq):
    n_mem = mk.shape[0] // batch
    ts = _tile(seq, 512)
    nt = seq // ts
    kv = pl.BlockSpec((n_mem, M_DH), lambda b, t, h: (b, h))
    return pl.pallas_call(
        _mem_body,
        grid=(batch, nt, M_HEADS),
        in_specs=[pl.BlockSpec((None, ts, M_DH), lambda b, t, h: (SEC_QM, b * nt + t, h)), kv, kv],
        out_specs=pl.BlockSpec((ts, M_DH), lambda b, t, h: (b * nt + t, h)),
        out_shape=jax.ShapeDtypeStruct((batch * seq, M_HEADS * M_DH), BF16),
        compiler_params=_params("parallel", "parallel", "arbitrary"),
        name="mem_attend",
    )(z3, mk, mv)


def _merge_body(h_ref, *refs, emit_w):
    x_refs, wg_refs, bg_refs, wb_refs = refs[0:3], refs[3:6], refs[6:9], refs[9:12]
    o_ref = refs[12]
    wgo_refs, wbo_refs = (refs[13:16], refs[16:19]) if emit_w else ((), ())
    h = h_ref[...]
    merged = None
    for g in range(3):
        wg, wb = _bf16(wg_refs[g][...]), _bf16(wb_refs[g][...])
        if emit_w:
            wgo_refs[g][...] = wg
            wbo_refs[g][...] = wb
        term = jax.nn.sigmoid(_dot(h, wg) + bg_refs[g][...]) * _dot(x_refs[g][...], wb)
        merged = term if merged is None else merged + term
    o_ref[...] = merged.astype(o_ref.dtype)


def _merge(h, branches, gate_ws, b_gate, branch_ws, *, emit_w=False):
    rows, d = h.shape
    tm, tn = _tile(rows, 1024), _tile(d, 256 if emit_w else 512)
    assert not emit_w or rows == tm
    nj = d // tn
    b_gate = b_gate.reshape(1, 3 * d)

    def row(width):
        return pl.BlockSpec((tm, width), lambda i, j: (i, 0))

    def cols(n_rows, first_col=0):
        return pl.BlockSpec((n_rows, tn), lambda i, j: (0, first_col // tn + j))

    out_specs = [pl.BlockSpec((tm, tn), lambda i, j: (i, j))]
    out_shape = [jax.ShapeDtypeStruct((rows, d), BF16)]
    if emit_w:
        out_specs += [cols(d)] * 3 + [cols(w.shape[0]) for w in branch_ws]
        out_shape += [jax.ShapeDtypeStruct((d, d), BF16)] * 3
        out_shape += [jax.ShapeDtypeStruct(w.shape, BF16) for w in branch_ws]
    out = pl.pallas_call(
        functools.partial(_merge_body, emit_w=emit_w),
        grid=(rows // tm, nj),
        in_specs=([row(d)] + [row(x.shape[1]) for x in branches]
                  + [cols(d, first) for _, first in gate_ws]
                  + [cols(1, g * d) for g in range(3)]
                  + [cols(w.shape[0]) for w in branch_ws]),
        out_specs=out_specs,
        out_shape=out_shape,
        compiler_params=_params("parallel", "arbitrary"),
        name="gated_merge",
    )(h, *branches, *[w for w, _ in gate_ws], b_gate, b_gate, b_gate, *branch_ws)
    if emit_w:
        return out[0], [(w, 0) for w in out[1:4]], out[4:7]
    return out[0]


def _moba_step_body(pt_ref, t5t_ref, q_ref, kn_ref, vn_ref, k0_ref, k1_ref, v0_ref, v1_ref, o_ref,
                    g_sc, m_sc, l_sc, o_sc, *, nb_past, past_len):
    del pt_ref
    blk = MOBA_BLOCK
    rows = PAGE_SIZE * A_HEADS
    n = pl.program_id(1)
    scale = A_DH ** -0.5
    q = q_ref[...]
    tab = t5t_ref[...]
    lane = lax.broadcasted_iota(I32, (A_HEADS, 128), 1)
    head = lax.broadcasted_iota(I32, (A_HEADS, rows), 0)
    col = lax.broadcasted_iota(I32, (A_HEADS, rows), 1)
    own_head = jnp.bitwise_and(col, A_HEADS - 1) == head
    pos_in_page = jnp.right_shift(col, int(math.log2(A_HEADS)))

    @pl.when(n == 0)
    def _():
        g_sc[...] = jnp.zeros_like(g_sc)
        l_sc[...] = jnp.zeros_like(l_sc)
        m_sc[...] = jnp.full_like(m_sc, NEG_INF)

    def far_bias():
        bias = jnp.where(own_head, tab[:, T5_BUCKETS - 1:T5_BUCKETS], NEG_INF)
        return [bias, bias]

    def near_bias():
        out = []
        for t in range(2):
            bucket = _t5_bucket(past_len - (n * blk + t * PAGE_SIZE + pos_in_page))
            bias = jnp.zeros((A_HEADS, rows), F32)
            for bkt in range(T5_BUCKETS):
                bias = jnp.where(bucket == bkt, tab[:, bkt:bkt + 1], bias)
            out.append(jnp.where(own_head, bias, NEG_INF))
        return out

    bias = lax.cond(n * blk + blk - 1 + T5_MAX_DIST <= past_len, far_bias, near_bias)

    qb = q.astype(BF16)
    k_refs, v_refs = (k0_ref, k1_ref), (v0_ref, v1_ref)
    s = [lax.dot_general(qb, k_refs[t][...].reshape(rows, A_DH).astype(BF16), NT_DIMS,
                         preferred_element_type=F32) * scale + bias[t] for t in range(2)]
    mb = jnp.max(jnp.maximum(s[0], s[1]), axis=1, keepdims=True)
    p = [jnp.exp(s[t] - mb) for t in range(2)]
    lb = jnp.sum(p[0] + p[1], axis=1, keepdims=True)
    ob = (_dot(p[0].astype(BF16), v_refs[0][...].reshape(rows, A_DH).astype(BF16))
          + _dot(p[1].astype(BF16), v_refs[1][...].reshape(rows, A_DH).astype(BF16)))
    ksum = jnp.sum(k0_ref[...], axis=0) + jnp.sum(k1_ref[...], axis=0)
    gate = jnp.sum(q * ksum, axis=1, keepdims=True) / blk
    here = lane == n
    g_sc[...] = jnp.where(here, gate, g_sc[...])
    m_sc[...] = jnp.where(here, mb, m_sc[...])
    l_sc[...] = jnp.where(here, lb, l_sc[...])
    o_sc[n] = ob

    @pl.when(n == nb_past - 1)
    def _():
        gw = jnp.where(lane < nb_past, g_sc[...], NEG_INF)
        sel = jnp.zeros(gw.shape, F32)
        for _ in range(min(MOBA_TOPK, nb_past + 1)):
            mx = jnp.max(gw, axis=1, keepdims=True)
            idx = jnp.min(jnp.where(gw == mx, lane, 128), axis=1, keepdims=True)
            pick = lane == idx
            sel = jnp.where(pick, jnp.where(lane < nb_past, 1.0, 0.0), sel)
            gw = jnp.where(pick, -jnp.inf, gw)
        s_own = jnp.sum(q * kn_ref[...], axis=1, keepdims=True) * scale + tab[:, 0:1]
        m_all = jnp.maximum(jnp.max(jnp.where(sel > 0, m_sc[...], NEG_INF), axis=1, keepdims=True), s_own)
        w = jnp.where(sel > 0, jnp.exp(jnp.minimum(m_sc[...] - m_all, 0.0)), 0.0)
        w_own = jnp.exp(s_own - m_all)
        l_all = jnp.sum(w * l_sc[...], axis=1, keepdims=True) + w_own
        o_all = w_own * vn_ref[...]
        for nn in range(nb_past):
            o_all = o_all + w[:, nn:nn + 1] * o_sc[nn]
        o_ref[...] = (o_all / l_all).astype(o_ref.dtype)


def _moba_step(z3, cache_k, cache_v, page_table, t5_table):
    batch, n_pages = page_table.shape
    past_len = n_pages * PAGE_SIZE
    assert past_len % MOBA_BLOCK == 0 and MOBA_BLOCK == 2 * PAGE_SIZE
    nb_past = past_len // MOBA_BLOCK
    assert 0 < nb_past <= 128
    z4 = z3.reshape(8, batch, A_HEADS, A_DH)

    def sec(s):
        return pl.BlockSpec((None, None, A_HEADS, A_DH), lambda b, n, pt: (s, b, 0, 0))

    def page(t):
        return pl.BlockSpec((None, PAGE_SIZE, A_HEADS, A_DH), lambda b, n, pt: (pt[b, 2 * n + t], 0, 0, 0))

    grid_spec = pltpu.PrefetchScalarGridSpec(
        num_scalar_prefetch=1,
        grid=(batch, nb_past),
        in_specs=[pl.BlockSpec((A_HEADS, T5_BUCKETS), lambda b, n, pt: (0, 0)),
                  sec(SEC_QA), sec(SEC_KA), sec(SEC_VA), page(0), page(1), page(0), page(1)],
        out_specs=pl.BlockSpec((None, A_HEADS, A_DH), lambda b, n, pt: (b, 0, 0)),
        scratch_shapes=[pltpu.VMEM((A_HEADS, 128), F32)] * 3 + [pltpu.VMEM((nb_past, A_HEADS, A_DH), F32)],
    )
    return pl.pallas_call(
        functools.partial(_moba_step_body, nb_past=nb_past, past_len=past_len),
        grid_spec=grid_spec,
        out_shape=jax.ShapeDtypeStruct((batch, A_HEADS, A_DH), F32),
        compiler_params=_params("parallel", "arbitrary"),
        name="moba_step",
    )(page_table, t5_table.T, z4, z4, z4, cache_k, cache_k, cache_v, cache_v)


def kernel(x_prompt, x_sample, mem_prompt, cache_k, cache_v, cache_mem_k, cache_mem_v, state_ret, page_table, t5_table, ffn1_norm, ffn1_w1, ffn1_w3, ffn1_w2, mix_norm, mem_norm, w_in, w_mem_kv, w_gate, b_gate, w_br_moba, w_br_ret, w_br_mem, w_out, ffn2_norm, ffn2_w1, ffn2_w3, ffn2_w2, final_norm):
    batch, seq, d = x_prompt.shape
    dec_batch, dec_seq, _ = x_sample.shape
    assert dec_seq == 1
    depth = w_in.shape[0]
    n_mem = mem_prompt.shape[1]
    past_len = page_table.shape[1] * PAGE_SIZE
    cos_p, sin_p = _rope_tables(0, seq)
    cos_s, sin_s = _rope_tables(past_len, dec_seq)

    xp = x_prompt.reshape(batch * seq, d)
    xs = x_sample.reshape(dec_batch, d)
    outs = [[] for _ in range(8)]
    for l in range(depth):
        last = l == depth - 1
        g_next = final_norm if last else ffn1_norm[l + 1]
        branch_ws = [w_br_moba[l], w_br_ret[l], w_br_mem[l]]
        gate_ws = [(w_gate[l], g * d) for g in range(3)]

        x1s, h2s, *f1_b = _ffn_half(xs, ffn1_norm[l], ffn1_w1[l], ffn1_w3[l], ffn1_w2[l], mix_norm[l],
                                    emit_x=True, post_dtype=BF16, emit_w=True)
        x1, h2 = _ffn_half(xp, ffn1_norm[l], *f1_b, mix_norm[l], emit_x=True, post_dtype=BF16)
        z3s, w_in_b = _matmul_sections(h2s, w_in[l], emit_w=True)
        z3 = _matmul_sections(h2, w_in_b)

        oa_s = _moba_step(z3s, cache_k[l], cache_v[l], page_table, t5_table)
        o_rs, s_new_s = _ret_step(z3s, cos_s, sin_s, state_ret[l])
        z3s_pad = jnp.broadcast_to(z3s[:, :, None, :], (8, dec_batch, 8, SEC)).reshape(8, dec_batch * 8, SEC)
        om_s = _mem_attend(z3s_pad, cache_mem_k[l].reshape(dec_batch * n_mem, M_HEADS * M_DH),
                           cache_mem_v[l].reshape(dec_batch * n_mem, M_HEADS * M_DH), dec_batch, 8)
        om_s = om_s.reshape(dec_batch, 8, M_HEADS * M_DH)[:, 0, :]
        branches_s = [oa_s.reshape(dec_batch, -1).astype(BF16), o_rs.reshape(dec_batch, -1).astype(BF16), om_s]

        mem_h = _norm_rows(mem_prompt.reshape(batch * n_mem, d), mem_norm[l], BF16)
        mk_p = _matmul(mem_h, w_mem_kv[l], col_block=0, n_cols=M_HEADS * M_DH, name="mem_k")
        mv_p = _matmul(mem_h, w_mem_kv[l], col_block=1, n_cols=M_HEADS * M_DH, name="mem_v")
        oa = _moba_prompt(z3, t5_table, batch, seq)
        o_r, s_new_p = _ret_prompt(z3, cos_p, sin_p, batch, seq)
        om = _mem_attend(z3, mk_p, mv_p, batch, seq)

        merged_s, gate_b, branch_b = _merge(h2s, branches_s, gate_ws, b_gate[l], branch_ws, emit_w=True)
        merged = _merge(h2, [oa, o_r, om], gate_b, b_gate[l], branch_b)
        x2s, w_out_b = _matmul(merged_s, w_out[l], residual=x1s, emit_w=True, name="out_proj")
        x2 = _matmul(merged, w_out_b, residual=x1, name="out_proj")
        xs, *rest = _ffn_half(x2s, ffn2_norm[l], ffn2_w1[l], ffn2_w3[l], ffn2_w2[l], g_next,
                              emit_x=not last, post_dtype=F32, emit_w=True)
        f2_b = rest[-3:]
        xp = _ffn_half(x2, ffn2_norm[l], *f2_b, g_next, emit_x=not last, post_dtype=F32)[0]

        new = (z3[SEC_KA].reshape(batch, seq, A_HEADS, A_DH), z3[SEC_VA].reshape(batch, seq, A_HEADS, A_DH),
               mk_p.reshape(batch, n_mem, M_HEADS, M_DH), mv_p.reshape(batch, n_mem, M_HEADS, M_DH), s_new_p,
               z3s[SEC_KA].reshape(dec_batch, dec_seq, A_HEADS, A_DH),
               z3s[SEC_VA].reshape(dec_batch, dec_seq, A_HEADS, A_DH), s_new_s)
        for acc, val in zip(outs, new):
            acc.append(val)

    y_prompt = xp.reshape(batch, seq, d)
    y_sample = xs.reshape(dec_batch, dec_seq, d)
    return (y_prompt, y_sample) + tuple(jnp.stack(o) for o in outs)
```

```python
import functools
import math

import jax
import jax.numpy as jnp
import numpy as np
from jax import lax
from jax.experimental import pallas as pl
from jax.experimental.pallas import tpu as pltpu

F32 = jnp.float32
BF16 = jnp.bfloat16
I32 = jnp.int32

A_HEADS, A_DH = 8, 128
MOBA_BLOCK, MOBA_TOPK = 256, 3
T5_BUCKETS, T5_MAX_DIST = 32, 128
R_HEADS, R_DK, R_DV = 4, 256, 256
RET_CHUNK = 128
RET_ROPE_BASE = 10000.0
M_HEADS, M_DH = 4, 256
PAGE_SIZE = 128
EPS = 1e-6
NEG_INF = -1e30
SEC = 1024
(SEC_QA, SEC_KA, SEC_VA, SEC_QR, SEC_KR, SEC_VR, SEC_GR, SEC_QM) = range(8)

V7X_VMEM_LIMIT_BYTES = 56 * 1024 * 1024

NT_DIMS = (((1,), (1,)), ((), ()))
TN_DIMS = (((0,), (0,)), ((), ()))


def _params(*sem):
    return pltpu.CompilerParams(dimension_semantics=sem, vmem_limit_bytes=V7X_VMEM_LIMIT_BYTES)


def _tile(n, pref):
    t = min(n, pref)
    while n % t:
        t -= 1
    return t


def _rms(x, g):
    return x * lax.rsqrt(jnp.mean(x * x, axis=-1, keepdims=True) + EPS) * g


def _silu(x):
    return x * jax.nn.sigmoid(x)


def _dot(a, b):
    return jnp.dot(a, b, preferred_element_type=F32)


def _bf16(w):
    return w if w.dtype == BF16 else w.astype(BF16)


def _ffn_body(x_ref, g_ref, w1_ref, w3_ref, w2_ref, gp_ref, *refs, emit_x, emit_w):
    refs = list(refs)
    xo_ref = refs.pop(0) if emit_x else None
    ho_ref = refs.pop(0)
    wb_refs = [refs.pop(0) for _ in range(3)] if emit_w else []
    h_sc, acc_sc = refs
    f = pl.program_id(1)

    @pl.when(f == 0)
    def _():
        h_sc[...] = _rms(x_ref[...], g_ref[...]).astype(BF16)
        acc_sc[...] = jnp.zeros_like(acc_sc)

    w1, w3, w2 = _bf16(w1_ref[...]), _bf16(w3_ref[...]), _bf16(w2_ref[...])
    for wb_ref, w in zip(wb_refs, (w1, w3, w2)):
        wb_ref[...] = w
    h = h_sc[...]
    t = _silu(_dot(h, w1)) * _dot(h, w3)
    acc_sc[...] += _dot(t.astype(BF16), w2)

    @pl.when(f == pl.num_programs(1) - 1)
    def _():
        xo = x_ref[...] + 0.5 * acc_sc[...]
        if emit_x:
            xo_ref[...] = xo
        ho_ref[...] = _rms(xo, gp_ref[...]).astype(ho_ref.dtype)


def _ffn_half(x, g, w1, w3, w2, g_post, *, emit_x, post_dtype, emit_w=False):
    rows, d = x.shape
    ff = w1.shape[1]
    tm, tf = _tile(rows, 512), _tile(ff, 512)
    assert not emit_w or rows == tm
    row_spec = pl.BlockSpec((tm, d), lambda i, f: (i, 0))
    vec_spec = pl.BlockSpec((1, d), lambda i, f: (0, 0))
    w13_spec = pl.BlockSpec((d, tf), lambda i, f: (0, f))
    w2_spec = pl.BlockSpec((tf, d), lambda i, f: (f, 0))
    out_shape = [jax.ShapeDtypeStruct((rows, d), post_dtype)]
    out_specs = [row_spec]
    if emit_x:
        out_shape = [jax.ShapeDtypeStruct((rows, d), F32)] + out_shape
        out_specs = [row_spec] + out_specs
    if emit_w:
        out_shape += [jax.ShapeDtypeStruct(w.shape, BF16) for w in (w1, w3, w2)]
        out_specs += [w13_spec, w13_spec, w2_spec]
    return pl.pallas_call(
        functools.partial(_ffn_body, emit_x=emit_x, emit_w=emit_w),
        grid=(rows // tm, ff // tf),
        in_specs=[row_spec, vec_spec, w13_spec, w13_spec, w2_spec, vec_spec],
        out_specs=out_specs,
        out_shape=out_shape,
        scratch_shapes=[pltpu.VMEM((tm, d), BF16), pltpu.VMEM((tm, d), F32)],
        compiler_params=_params("parallel", "arbitrary"),
        name="ffn_half",
    )(x, g.reshape(1, d), w1, w3, w2, g_post.reshape(1, d))


def _norm_body(x_ref, g_ref, o_ref):
    o_ref[...] = _rms(x_ref[...], g_ref[...]).astype(o_ref.dtype)


def _norm_rows(x, g, out_dtype):
    rows, d = x.shape
    tm = _tile(rows, 512)
    return pl.pallas_call(
        _norm_body,
        grid=(rows // tm,),
        in_specs=[pl.BlockSpec((tm, d), lambda i: (i, 0)), pl.BlockSpec((1, d), lambda i: (0, 0))],
        out_specs=pl.BlockSpec((tm, d), lambda i: (i, 0)),
        out_shape=jax.ShapeDtypeStruct((rows, d), out_dtype),
        compiler_params=_params("parallel"),
        name="rmsnorm_rows",
    )(x, g.reshape(1, d))


def _mm_body(a_ref, w_ref, *refs, has_residual, emit_w):
    refs = list(refs)
    r_ref = refs.pop(0) if has_residual else None
    o_ref = refs.pop(0)
    w = _bf16(w_ref[...])
    if emit_w:
        refs.pop(0)[...] = w
    acc = _dot(a_ref[...], w)
    if has_residual:
        acc = r_ref[...] + acc
    o_ref[...] = acc.astype(o_ref.dtype)


def _matmul_sections(a, w, *, emit_w=False):
    rows, k = a.shape
    n_sec = w.shape[1] // SEC
    tm = _tile(rows, 1024)
    assert not emit_w or rows == tm
    w_spec = pl.BlockSpec((k, SEC), lambda i, j: (0, j))
    out_specs = [pl.BlockSpec((None, tm, SEC), lambda i, j: (j, i, 0))]
    out_shape = [jax.ShapeDtypeStruct((n_sec, rows, SEC), F32)]
    if emit_w:
        out_specs.append(w_spec)
        out_shape.append(jax.ShapeDtypeStruct(w.shape, BF16))
    out = pl.pallas_call(
        functools.partial(_mm_body, has_residual=False, emit_w=emit_w),
        grid=(rows // tm, n_sec),
        in_specs=[pl.BlockSpec((tm, k), lambda i, j: (i, 0)), w_spec],
        out_specs=out_specs,
        out_shape=out_shape,
        compiler_params=_params("parallel", "arbitrary"),
        name="in_proj",
    )(a, w)
    return out if emit_w else out[0]


def _matmul(a, w, *, col_block=0, n_cols=None, residual=None, out_dtype=F32, emit_w=False, name="matmul"):
    rows, k = a.shape
    n_cols = w.shape[1] if n_cols is None else n_cols
    tm, tn = _tile(rows, 1024), _tile(n_cols, 1024)
    assert not emit_w or (rows == tm and n_cols == w.shape[1] and col_block == 0)
    w_spec = pl.BlockSpec((k, tn), lambda i, j: (0, col_block + j))
    in_specs = [pl.BlockSpec((tm, k), lambda i, j: (i, 0)), w_spec]
    out_spec = pl.BlockSpec((tm, tn), lambda i, j: (i, j))
    args = [a, w]
    if residual is not None:
        in_specs.append(out_spec)
        args.append(residual)
    out_specs = [out_spec]
    out_shape = [jax.ShapeDtypeStruct((rows, n_cols), out_dtype)]
    if emit_w:
        out_specs.append(w_spec)
        out_shape.append(jax.ShapeDtypeStruct(w.shape, BF16))
    out = pl.pallas_call(
        functools.partial(_mm_body, has_residual=residual is not None, emit_w=emit_w),
        grid=(rows // tm, n_cols // tn),
        in_specs=in_specs,
        out_specs=out_specs,
        out_shape=out_shape,
        compiler_params=_params("parallel", "arbitrary"),
        name=name,
    )(*args)
    return out if emit_w else out[0]


def _t5_bucket(n):
    n = jnp.maximum(n, 0)
    max_exact = T5_BUCKETS // 2
    nf = jnp.maximum(n, 1).astype(F32)
    large = max_exact + (jnp.log(nf / max_exact) / math.log(T5_MAX_DIST / max_exact)
                         * (T5_BUCKETS - max_exact)).astype(I32)
    return jnp.where(n < max_exact, n, jnp.minimum(large, T5_BUCKETS - 1))


def _moba_prompt_body(t5_ref, q_ref, k_ref, v_ref, o_ref, kb_sc, vb_sc, kmean_sc, bias_sc, *, nb):
    blk = MOBA_BLOCK
    h, b, c = pl.program_id(0), pl.program_id(1), pl.program_id(2)
    scale = A_DH ** -0.5
    ii = lax.broadcasted_iota(I32, (blk, blk), 0)
    jj = lax.broadcasted_iota(I32, (blk, blk), 1)

    @pl.when((b == 0) & (c == 0))
    def _():
        for t in range(2):
            bucket = _t5_bucket(t * blk + ii - jj)
            tile = jnp.zeros((blk, blk), F32)
            for bkt in range(T5_BUCKETS):
                tile = jnp.where(bucket == bkt, t5_ref[bkt, h], tile)
            bias_sc[t] = jnp.where(ii >= jj, tile, NEG_INF) if t == 0 else tile

    @pl.when(c == 0)
    def _():
        kmean_sc[...] = jnp.zeros_like(kmean_sc)
        for n in range(nb):
            kf = k_ref[n * blk:(n + 1) * blk, :]
            kmean_sc[n:n + 1, :] = jnp.mean(kf, axis=0, keepdims=True)
            kb_sc[n * blk:(n + 1) * blk, :] = kf.astype(BF16)
        vb_sc[...] = v_ref[...].astype(BF16)

    q = q_ref[...]
    far_bias = t5_ref[T5_BUCKETS - 1, h]
    qb = q.astype(BF16)

    def attend(cc):
        sel = None
        if cc > MOBA_TOPK:
            gate = lax.dot_general(q, kmean_sc[...], NT_DIMS, precision=lax.Precision.HIGHEST,
                                   preferred_element_type=F32)
            n_iota = lax.broadcasted_iota(I32, gate.shape, 1)
            rank = jnp.zeros(gate.shape, F32)
            for m in range(cc):
                gm = gate[:, m:m + 1]
                rank = rank + jnp.where(gm > gate, 1.0, jnp.where(gm == gate, jnp.where(n_iota > m, 1.0, 0.0), 0.0))
            sel = jnp.where(rank < MOBA_TOPK, 0.0, NEG_INF)
        s_all = lax.dot_general(qb, kb_sc[0:(cc + 1) * blk, :], NT_DIMS, preferred_element_type=F32) * scale
        tiles = []
        for n in range(cc + 1):
            t = s_all[:, n * blk:(n + 1) * blk]
            if n == cc:
                t = t + bias_sc[0]
            elif n == cc - 1:
                t = t + bias_sc[1]
                if sel is not None:
                    t = t + sel[:, n:n + 1]
            else:
                t = t + (far_bias if sel is None else sel[:, n:n + 1] + far_bias)
            tiles.append(t)
        m = jnp.max(functools.reduce(jnp.maximum, tiles), axis=1, keepdims=True)
        ps = [jnp.exp(t - m) for t in tiles]
        l = jnp.sum(functools.reduce(jnp.add, ps), axis=1, keepdims=True)
        p_all = jnp.concatenate([p.astype(BF16) for p in ps], axis=1)
        o_ref[...] = (_dot(p_all, vb_sc[0:(cc + 1) * blk, :]) / l).astype(o_ref.dtype)

    for cc in range(nb):
        pl.when(c == cc)(functools.partial(attend, cc))


def _moba_prompt(z3, t5_table, batch, seq):
    blk = MOBA_BLOCK
    nb = seq // blk
    nb_pad = -(-nb // 8) * 8
    return pl.pallas_call(
        functools.partial(_moba_prompt_body, nb=nb),
        grid=(A_HEADS, batch, nb),
        in_specs=[pl.BlockSpec(memory_space=pltpu.SMEM),
                  pl.BlockSpec((None, blk, A_DH), lambda h, b, c: (SEC_QA, b * nb + c, h)),
                  pl.BlockSpec((None, seq, A_DH), lambda h, b, c: (SEC_KA, b, h)),
                  pl.BlockSpec((None, seq, A_DH), lambda h, b, c: (SEC_VA, b, h))],
        out_specs=pl.BlockSpec((blk, A_DH), lambda h, b, c: (b * nb + c, h)),
        out_shape=jax.ShapeDtypeStruct((batch * seq, A_HEADS * A_DH), BF16),
        scratch_shapes=[pltpu.VMEM((seq, A_DH), BF16), pltpu.VMEM((seq, A_DH), BF16),
                        pltpu.VMEM((nb_pad, A_DH), F32), pltpu.VMEM((2, blk, blk), F32)],
        compiler_params=_params("arbitrary", "arbitrary", "arbitrary"),
        name="moba_prompt",
    )(t5_table, z3, z3, z3)


def _rope_body(inv_ref, cos_ref, sin_ref, *, pos0):
    pos = pos0 + lax.broadcasted_iota(I32, cos_ref.shape, 0)
    ang = pos.astype(F32) * inv_ref[...]
    cos_ref[...] = jnp.cos(ang)
    sin_ref[...] = jnp.sin(ang)


def _rope_tables(pos0, n_pos):
    half = R_DK // 2
    inv = 1.0 / (RET_ROPE_BASE ** jnp.linspace(0.0, 1.0, half, dtype=F32))
    rows = -(-n_pos // 8) * 8
    return pl.pallas_call(
        functools.partial(_rope_body, pos0=pos0),
        out_shape=[jax.ShapeDtypeStruct((rows, half), F32)] * 2,
        name="rope_tables",
    )(inv.reshape(1, half))


def _rotate(x, cos, sin):
    half = x.shape[-1] // 2
    x1, x2 = x[:, :half], x[:, half:]
    return jnp.concatenate([x1 * cos - x2 * sin, x1 * sin + x2 * cos], axis=1)


def _log_decay(h, shape):
    hf = jnp.full(shape, h, I32).astype(F32)
    return jnp.log(1.0 - jnp.exp2(-5.0 - hf))


def _ret_prompt_body(q_ref, k_ref, v_ref, g_ref, cos_ref, sin_ref, o_ref, so_ref, s_sc, *, c):
    n = pl.program_id(1)

    @pl.when(n == 0)
    def _():
        s_sc[...] = jnp.zeros_like(s_sc)

    i = lax.broadcasted_iota(I32, (c, c), 0).astype(F32)
    j = lax.broadcasted_iota(I32, (c, c), 1).astype(F32)
    diff = i - j
    i_col = lax.broadcasted_iota(I32, (c, 1), 0).astype(F32)
    cos, sin = cos_ref[...], sin_ref[...]

    for h in range(R_HEADS):
        dk, dv = slice(h * R_DK, (h + 1) * R_DK), slice(h * R_DV, (h + 1) * R_DV)
        dmask = jnp.where(diff >= 0, jnp.exp(jnp.maximum(diff, 0.0) * _log_decay(h, (c, c))), 0.0)
        lg_col = _log_decay(h, (c, 1))
        q_dec = jnp.exp((i_col + 1.0) * lg_col)
        k_dec = jnp.exp((c - 1.0 - i_col) * lg_col)
        c_dec = jnp.exp(c * _log_decay(h, (1, R_DV)))
        qr = _rotate(q_ref[:, dk], cos, sin)
        kr = _rotate(k_ref[:, dk], cos, sin) * (R_DK ** -0.5)
        vb = v_ref[:, dv].astype(BF16)
        s = s_sc[h]
        att = lax.dot_general(qr.astype(BF16), kr.astype(BF16), NT_DIMS, preferred_element_type=F32) * dmask
        o = _dot(att.astype(BF16), vb) + _dot((qr * q_dec).astype(BF16), s.astype(BF16))
        s_sc[h] = s * c_dec + lax.dot_general((kr * k_dec).astype(BF16), vb, TN_DIMS, preferred_element_type=F32)
        on = o * lax.rsqrt(jnp.mean(o * o, axis=-1, keepdims=True) + EPS)
        o_ref[:, dv] = (on * _silu(g_ref[:, dv])).astype(o_ref.dtype)

    @pl.when(n == pl.num_programs(1) - 1)
    def _():
        so_ref[...] = s_sc[...]


def _ret_prompt(z3, cos, sin, batch, seq):
    c = math.gcd(seq, RET_CHUNK)
    nc = seq // c

    def sec(s):
        return pl.BlockSpec((None, c, SEC), lambda b, n: (s, b * nc + n, 0))

    tab = pl.BlockSpec((c, R_DK // 2), lambda b, n: (n, 0))
    return pl.pallas_call(
        functools.partial(_ret_prompt_body, c=c),
        grid=(batch, nc),
        in_specs=[sec(SEC_QR), sec(SEC_KR), sec(SEC_VR), sec(SEC_GR), tab, tab],
        out_specs=[pl.BlockSpec((c, R_HEADS * R_DV), lambda b, n: (b * nc + n, 0)),
                   pl.BlockSpec((None, R_HEADS, R_DK, R_DV), lambda b, n: (b, 0, 0, 0))],
        out_shape=[jax.ShapeDtypeStruct((batch * seq, R_HEADS * R_DV), BF16),
                   jax.ShapeDtypeStruct((batch, R_HEADS, R_DK, R_DV), F32)],
        scratch_shapes=[pltpu.VMEM((R_HEADS, R_DK, R_DV), F32)],
        compiler_params=_params("parallel", "arbitrary"),
        name="retention_prompt",
    )(z3, z3, z3, z3, cos, sin)


def _ret_step_body(q_ref, k_ref, v_ref, g_ref, cos_ref, sin_ref, s_ref, o_ref, so_ref):
    cos, sin = cos_ref[0:1, :], sin_ref[0:1, :]
    row0 = lax.broadcasted_iota(I32, (8, R_DK), 0) == 0
    hi = lax.Precision.HIGHEST
    for h in range(R_HEADS):
        decay = jnp.exp(_log_decay(h, (1, R_DV)))
        qr = _rotate(q_ref[h:h + 1, :], cos, sin)
        kr = _rotate(k_ref[h:h + 1, :], cos, sin) * (R_DK ** -0.5)
        v = v_ref[h:h + 1, :]
        s = s_ref[h]
        q8 = jnp.broadcast_to(qr * decay, (8, R_DK))
        qs = lax.dot_general(q8, s, (((1,), (0,)), ((), ())), precision=hi, preferred_element_type=F32)[0:1, :]
        o = jnp.sum(qr * kr, axis=-1, keepdims=True) * v + qs
        k8 = jnp.where(row0, jnp.broadcast_to(kr, (8, R_DK)), 0.0)
        v8 = jnp.broadcast_to(v, (8, R_DV))
        so_ref[h] = s * decay + lax.dot_general(k8, v8, TN_DIMS, precision=hi, preferred_element_type=F32)
        on = o * lax.rsqrt(jnp.mean(o * o, axis=-1, keepdims=True) + EPS)
        o_ref[h:h + 1, :] = (on * _silu(g_ref[h:h + 1, :])).astype(o_ref.dtype)


def _ret_step(z3, cos, sin, state):
    batch = z3.shape[1]
    z4 = z3.reshape(8, batch, R_HEADS, R_DK)

    def sec(s):
        return pl.BlockSpec((None, None, R_HEADS, R_DK), lambda b: (s, b, 0, 0))

    tab = pl.BlockSpec((8, R_DK // 2), lambda b: (0, 0))
    st = pl.BlockSpec((None, R_HEADS, R_DK, R_DV), lambda b: (b, 0, 0, 0))
    return pl.pallas_call(
        _ret_step_body,
        grid=(batch,),
        in_specs=[sec(SEC_QR), sec(SEC_KR), sec(SEC_VR), sec(SEC_GR), tab, tab, st],
        out_specs=[pl.BlockSpec((None, R_HEADS, R_DV), lambda b: (b, 0, 0)), st],
        out_shape=[jax.ShapeDtypeStruct((batch, R_HEADS, R_DV), F32),
                   jax.ShapeDtypeStruct(state.shape, F32)],
        compiler_params=_params("parallel"),
        name="retention_step",
    )(z4, z4, z4, z4, cos, sin, state)


def _mem_body(q_ref, k_ref, v_ref, o_ref):
    s = lax.dot_general(q_ref[...].astype(BF16), k_ref[...].astype(BF16), NT_DIMS,
                        preferred_element_type=F32) * (M_DH ** -0.5)
    m = jnp.max(s, axis=-1, keepdims=True)
    p = jnp.exp(s - m)
    l = jnp.sum(p, axis=-1, keepdims=True)
    o_ref[...] = (_dot(p.astype(BF16), v_ref[...].astype(BF16)) / l).astype(o_ref.dtype)


def _mem_attend(z3, mk, mv, batch, seq):
    n_mem = mk.shape[0] // batch
    ts = _tile(seq, 512)
    nt = seq // ts
    kv = pl.BlockSpec((n_mem, M_DH), lambda b, t, h: (b, h))
    return pl.pallas_call(
        _mem_body,
        grid=(batch, nt, M_HEADS),
        in_specs=[pl.BlockSpec((None, ts, M_DH), lambda b, t, h: (SEC_QM, b * nt + t, h)), kv, kv],
        out_specs=pl.BlockSpec((ts, M_DH), lambda b, t, h: (b * nt + t, h)),
        out_shape=jax.ShapeDtypeStruct((batch * seq, M_HEADS * M_DH), BF16),
        compiler_params=_params("parallel", "parallel", "arbitrary"),
        name="mem_attend",
    )(z3, mk, mv)


def _merge_body(h_ref, *refs, emit_w):
    x_refs, wg_refs, bg_refs, wb_refs = refs[0:3], refs[3:6], refs[6:9], refs[9:12]
    o_ref = refs[12]
    wgo_refs, wbo_refs = (refs[13:16], refs[16:19]) if emit_w else ((), ())
    h = h_ref[...]
    merged = None
    for g in range(3):
        wg, wb = _bf16(wg_refs[g][...]), _bf16(wb_refs[g][...])
        if emit_w:
            wgo_refs[g][...] = wg
            wbo_refs[g][...] = wb
        term = jax.nn.sigmoid(_dot(h, wg) + bg_refs[g][...]) * _dot(x_refs[g][...], wb)
        merged = term if merged is None else merged + term
    o_ref[...] = merged.astype(o_ref.dtype)


def _merge(h, branches, gate_ws, b_gate, branch_ws, *, emit_w=False):
    rows, d = h.shape
    tm, tn = _tile(rows, 1024), _tile(d, 256 if emit_w else 512)
    assert not emit_w or rows == tm
    nj = d // tn
    b_gate = b_gate.reshape(1, 3 * d)

    def row(width):
        return pl.BlockSpec((tm, width), lambda i, j: (i, 0))

    def cols(n_rows, first_col=0):
        return pl.BlockSpec((n_rows, tn), lambda i, j: (0, first_col // tn + j))

    out_specs = [pl.BlockSpec((tm, tn), lambda i, j: (i, j))]
    out_shape = [jax.ShapeDtypeStruct((rows, d), BF16)]
    if emit_w:
        out_specs += [cols(d)] * 3 + [cols(w.shape[0]) for w in branch_ws]
        out_shape += [jax.ShapeDtypeStruct((d, d), BF16)] * 3
        out_shape += [jax.ShapeDtypeStruct(w.shape, BF16) for w in branch_ws]
    out = pl.pallas_call(
        functools.partial(_merge_body, emit_w=emit_w),
        grid=(rows // tm, nj),
        in_specs=([row(d)] + [row(x.shape[1]) for x in branches]
                  + [cols(d, first) for _, first in gate_ws]
                  + [cols(1, g * d) for g in range(3)]
                  + [cols(w.shape[0]) for w in branch_ws]),
        out_specs=out_specs,
        out_shape=out_shape,
        compiler_params=_params("parallel", "arbitrary"),
        name="gated_merge",
    )(h, *branches, *[w for w, _ in gate_ws], b_gate, b_gate, b_gate, *branch_ws)
    if emit_w:
        return out[0], [(w, 0) for w in out[1:4]], out[4:7]
    return out[0]


def _moba_step_body(pt_ref, t5t_ref, q_ref, kn_ref, vn_ref, *refs, nb_past, past_len, bps, near_blocks):
    del pt_ref
    ppb = MOBA_BLOCK // PAGE_SIZE
    k_refs, v_refs = refs[:bps * ppb], refs[bps * ppb:2 * bps * ppb]
    o_ref, g_sc, m_sc, l_sc, o_sc, near_sc = refs[2 * bps * ppb:]
    blk = MOBA_BLOCK
    rows = PAGE_SIZE * A_HEADS
    n = pl.program_id(1)
    scale = A_DH ** -0.5
    q = q_ref[...]
    tab = t5t_ref[...]
    lane = lax.broadcasted_iota(I32, (A_HEADS, 128), 1)
    head = lax.broadcasted_iota(I32, (A_HEADS, rows), 0)
    col = lax.broadcasted_iota(I32, (A_HEADS, rows), 1)
    own_head = jnp.bitwise_and(col, A_HEADS - 1) == head
    far_bias = jnp.where(own_head, tab[:, T5_BUCKETS - 1:T5_BUCKETS], NEG_INF)

    @pl.when(n == 0)
    def _():
        g_sc[...] = jnp.zeros_like(g_sc)
        l_sc[...] = jnp.zeros_like(l_sc)
        m_sc[...] = jnp.full_like(m_sc, NEG_INF)
        pos_in_page = jnp.right_shift(col, int(math.log2(A_HEADS)))
        for i, nb in enumerate(near_blocks):
            for t in range(ppb):
                bucket = _t5_bucket(past_len - (nb * blk + t * PAGE_SIZE + pos_in_page))
                bias = jnp.zeros((A_HEADS, rows), F32)
                for bkt in range(T5_BUCKETS):
                    bias = jnp.where(bucket == bkt, tab[:, bkt:bkt + 1], bias)
                near_sc[i * ppb + t] = jnp.where(own_head, bias, NEG_INF)

    qb = q.astype(BF16)
    g_all, m_all, l_all = g_sc[...], m_sc[...], l_sc[...]
    for i in range(bps):
        nblk = n * bps + i
        kp = [k_refs[i * ppb + t][...] for t in range(ppb)]
        s = []
        for t in range(ppb):
            bias = far_bias
            for j, nb in enumerate(near_blocks):
                bias = jnp.where(nblk == nb, near_sc[j * ppb + t], bias)
            s.append(lax.dot_general(qb, kp[t].reshape(rows, A_DH).astype(BF16), NT_DIMS,
                                     preferred_element_type=F32) * scale + bias)
        mb = jnp.max(functools.reduce(jnp.maximum, s), axis=1, keepdims=True)
        p = [jnp.exp(x - mb) for x in s]
        lb = jnp.sum(functools.reduce(jnp.add, p), axis=1, keepdims=True)
        ob = functools.reduce(jnp.add, [
            _dot(p[t].astype(BF16), v_refs[i * ppb + t][...].reshape(rows, A_DH).astype(BF16))
            for t in range(ppb)])
        ksum = functools.reduce(jnp.add, [jnp.sum(x, axis=0) for x in kp])
        gate = jnp.sum(q * ksum, axis=1, keepdims=True) / blk
        here = lane == nblk
        g_all = jnp.where(here, gate, g_all)
        m_all = jnp.where(here, mb, m_all)
        l_all = jnp.where(here, lb, l_all)
        o_sc[nblk] = ob
    g_sc[...] = g_all
    m_sc[...] = m_all
    l_sc[...] = l_all

    @pl.when(n == pl.num_programs(1) - 1)
    def _():
        gw = jnp.where(lane < nb_past, g_sc[...], NEG_INF)
        sel = jnp.zeros(gw.shape, F32)
        for _ in range(min(MOBA_TOPK, nb_past + 1)):
            mx = jnp.max(gw, axis=1, keepdims=True)
            idx = jnp.min(jnp.where(gw == mx, lane, 128), axis=1, keepdims=True)
            pick = lane == idx
            sel = jnp.where(pick, jnp.where(lane < nb_past, 1.0, 0.0), sel)
            gw = jnp.where(pick, -jnp.inf, gw)
        s_own = jnp.sum(q * kn_ref[...], axis=1, keepdims=True) * scale + tab[:, 0:1]
        m_all = jnp.maximum(jnp.max(jnp.where(sel > 0, m_sc[...], NEG_INF), axis=1, keepdims=True), s_own)
        w = jnp.where(sel > 0, jnp.exp(jnp.minimum(m_sc[...] - m_all, 0.0)), 0.0)
        w_own = jnp.exp(s_own - m_all)
        l_all = jnp.sum(w * l_sc[...], axis=1, keepdims=True) + w_own
        o_all = w_own * vn_ref[...]
        for nn in range(nb_past):
            o_all = o_all + w[:, nn:nn + 1] * o_sc[nn]
        o_ref[...] = (o_all / l_all).astype(o_ref.dtype)


def _moba_step(z3, cache_k, cache_v, page_table, t5_table):
    batch, n_pages = page_table.shape
    past_len = n_pages * PAGE_SIZE
    assert past_len % MOBA_BLOCK == 0 and MOBA_BLOCK == 2 * PAGE_SIZE
    nb_past = past_len // MOBA_BLOCK
    assert 0 < nb_past <= 128
    z4 = z3.reshape(8, batch, A_HEADS, A_DH)
    ppb = MOBA_BLOCK // PAGE_SIZE
    bps = _tile(nb_past, 4)
    near_blocks = tuple(nb for nb in range(nb_past) if nb * MOBA_BLOCK + MOBA_BLOCK - 1 + T5_MAX_DIST > past_len)

    def sec(s):
        return pl.BlockSpec((None, None, A_HEADS, A_DH), lambda b, n, pt: (s, b, 0, 0))

    def page(t):
        return pl.BlockSpec((None, PAGE_SIZE, A_HEADS, A_DH), lambda b, n, pt: (pt[b, n * bps * ppb + t], 0, 0, 0))

    pages = [page(t) for t in range(bps * ppb)]
    grid_spec = pltpu.PrefetchScalarGridSpec(
        num_scalar_prefetch=1,
        grid=(batch, nb_past // bps),
        in_specs=[pl.BlockSpec((A_HEADS, T5_BUCKETS), lambda b, n, pt: (0, 0)),
                  sec(SEC_QA), sec(SEC_KA), sec(SEC_VA)] + pages + pages,
        out_specs=pl.BlockSpec((None, A_HEADS, A_DH), lambda b, n, pt: (b, 0, 0)),
        scratch_shapes=[pltpu.VMEM((A_HEADS, 128), F32)] * 3
        + [pltpu.VMEM((nb_past, A_HEADS, A_DH), F32),
           pltpu.VMEM((max(1, len(near_blocks)) * ppb, A_HEADS, PAGE_SIZE * A_HEADS), F32)],
    )
    return pl.pallas_call(
        functools.partial(_moba_step_body, nb_past=nb_past, past_len=past_len, bps=bps, near_blocks=near_blocks),
        grid_spec=grid_spec,
        out_shape=jax.ShapeDtypeStruct((batch, A_HEADS, A_DH), F32),
        compiler_params=_params("parallel", "arbitrary"),
        name="moba_step",
    )(page_table, t5_table.T, z4, z4, z4, *([cache_k] * (bps * ppb)), *([cache_v] * (bps * ppb)))


def kernel(x_prompt, x_sample, mem_prompt, cache_k, cache_v, cache_mem_k, cache_mem_v, state_ret, page_table, t5_table, ffn1_norm, ffn1_w1, ffn1_w3, ffn1_w2, mix_norm, mem_norm, w_in, w_mem_kv, w_gate, b_gate, w_br_moba, w_br_ret, w_br_mem, w_out, ffn2_norm, ffn2_w1, ffn2_w3, ffn2_w2, final_norm):
    batch, seq, d = x_prompt.shape
    dec_batch, dec_seq, _ = x_sample.shape
    assert dec_seq == 1
    depth = w_in.shape[0]
    n_mem = mem_prompt.shape[1]
    past_len = page_table.shape[1] * PAGE_SIZE
    cos_p, sin_p = _rope_tables(0, seq)
    cos_s, sin_s = _rope_tables(past_len, dec_seq)

    xp = x_prompt.reshape(batch * seq, d)
    xs = x_sample.reshape(dec_batch, d)
    outs = [[] for _ in range(8)]
    for l in range(depth):
        last = l == depth - 1
        g_next = final_norm if last else ffn1_norm[l + 1]
        branch_ws = [w_br_moba[l], w_br_ret[l], w_br_mem[l]]
        gate_ws = [(w_gate[l], g * d) for g in range(3)]

        x1s, h2s, *f1_b = _ffn_half(xs, ffn1_norm[l], ffn1_w1[l], ffn1_w3[l], ffn1_w2[l], mix_norm[l],
                                    emit_x=True, post_dtype=BF16, emit_w=True)
        x1, h2 = _ffn_half(xp, ffn1_norm[l], *f1_b, mix_norm[l], emit_x=True, post_dtype=BF16)
        z3s, w_in_b = _matmul_sections(h2s, w_in[l], emit_w=True)
        z3 = _matmul_sections(h2, w_in_b)

        oa_s = _moba_step(z3s, cache_k[l], cache_v[l], page_table, t5_table)
        o_rs, s_new_s = _ret_step(z3s, cos_s, sin_s, state_ret[l])
        z3s_pad = jnp.broadcast_to(z3s[:, :, None, :], (8, dec_batch, 8, SEC)).reshape(8, dec_batch * 8, SEC)
        om_s = _mem_attend(z3s_pad, cache_mem_k[l].reshape(dec_batch * n_mem, M_HEADS * M_DH),
                           cache_mem_v[l].reshape(dec_batch * n_mem, M_HEADS * M_DH), dec_batch, 8)
        om_s = om_s.reshape(dec_batch, 8, M_HEADS * M_DH)[:, 0, :]
        branches_s = [oa_s.reshape(dec_batch, -1).astype(BF16), o_rs.reshape(dec_batch, -1).astype(BF16), om_s]

        mem_h = _norm_rows(mem_prompt.reshape(batch * n_mem, d), mem_norm[l], BF16)
        mk_p = _matmul(mem_h, w_mem_kv[l], col_block=0, n_cols=M_HEADS * M_DH, name="mem_k")
        mv_p = _matmul(mem_h, w_mem_kv[l], col_block=1, n_cols=M_HEADS * M_DH, name="mem_v")
        oa = _moba_prompt(z3, t5_table, batch, seq)
        o_r, s_new_p = _ret_prompt(z3, cos_p, sin_p, batch, seq)
        om = _mem_attend(z3, mk_p, mv_p, batch, seq)

        merged_s, gate_b, branch_b = _merge(h2s, branches_s, gate_ws, b_gate[l], branch_ws, emit_w=True)
        merged = _merge(h2, [oa, o_r, om], gate_b, b_gate[l], branch_b)
        x2s, w_out_b = _matmul(merged_s, w_out[l], residual=x1s, emit_w=True, name="out_proj")
        x2 = _matmul(merged, w_out_b, residual=x1, name="out_proj")
        xs, *rest = _ffn_half(x2s, ffn2_norm[l], ffn2_w1[l], ffn2_w3[l], ffn2_w2[l], g_next,
                              emit_x=not last, post_dtype=F32, emit_w=True)
        f2_b = rest[-3:]
        xp = _ffn_half(x2, ffn2_norm[l], *f2_b, g_next, emit_x=not last, post_dtype=F32)[0]

        new = (z3[SEC_KA].reshape(batch, seq, A_HEADS, A_DH), z3[SEC_VA].reshape(batch, seq, A_HEADS, A_DH),
               mk_p.reshape(batch, n_mem, M_HEADS, M_DH), mv_p.reshape(batch, n_mem, M_HEADS, M_DH), s_new_p,
               z3s[SEC_KA].reshape(dec_batch, dec_seq, A_HEADS, A_DH),
               z3s[SEC_VA].reshape(dec_batch, dec_seq, A_HEADS, A_DH), s_new_s)
        for acc, val in zip(outs, new):
            acc.append(val)

    y_prompt = xp.reshape(batch, seq, d)
    y_sample = xs.reshape(dec_batch, dec_seq, d)
    return (y_prompt, y_sample) + tuple(jnp.stack(o) for o in outs)
```

```python
import functools
import math

import jax
import jax.numpy as jnp
import numpy as np
from jax import lax
from jax.experimental import pallas as pl
from jax.experimental.pallas import tpu as pltpu

F32 = jnp.float32
BF16 = jnp.bfloat16
I32 = jnp.int32

A_HEADS, A_DH = 8, 128
MOBA_BLOCK, MOBA_TOPK = 256, 3
T5_BUCKETS, T5_MAX_DIST = 32, 128
R_HEADS, R_DK, R_DV = 4, 256, 256
RET_CHUNK = 128
RET_ROPE_BASE = 10000.0
M_HEADS, M_DH = 4, 256
PAGE_SIZE = 128
EPS = 1e-6
NEG_INF = -1e30
SEC = 1024
W_IN_KA, W_IN_VA = 1, 2
(SEC_QA, SEC_QR, SEC_KR, SEC_VR, SEC_GR, SEC_QM) = range(6)

V7X_VMEM_LIMIT_BYTES = 56 * 1024 * 1024

NT_DIMS = (((1,), (1,)), ((), ()))
TN_DIMS = (((0,), (0,)), ((), ()))


def _params(*sem):
    return pltpu.CompilerParams(dimension_semantics=sem, vmem_limit_bytes=V7X_VMEM_LIMIT_BYTES)


def _tile(n, pref):
    t = min(n, pref)
    while n % t:
        t -= 1
    return t


def _rms(x, g):
    return x * lax.rsqrt(jnp.mean(x * x, axis=-1, keepdims=True) + EPS) * g


def _silu(x):
    return x * jax.nn.sigmoid(x)


def _dot(a, b):
    return jnp.dot(a, b, preferred_element_type=F32)


def _bf16(w):
    return w if w.dtype == BF16 else w.astype(BF16)


def _ffn_body(x_ref, g_ref, w1_ref, w3_ref, w2_ref, gp_ref, *refs, emit_x, emit_w):
    refs = list(refs)
    xo_ref = refs.pop(0) if emit_x else None
    ho_ref = refs.pop(0)
    wb_refs = [refs.pop(0) for _ in range(3)] if emit_w else []
    h_sc, acc_sc = refs
    f = pl.program_id(1)

    @pl.when(f == 0)
    def _():
        h_sc[...] = _rms(x_ref[...], g_ref[...]).astype(BF16)
        acc_sc[...] = jnp.zeros_like(acc_sc)

    w1, w3, w2 = _bf16(w1_ref[...]), _bf16(w3_ref[...]), _bf16(w2_ref[...])
    for wb_ref, w in zip(wb_refs, (w1, w3, w2)):
        wb_ref[...] = w
    h = h_sc[...]
    t = _silu(_dot(h, w1)) * _dot(h, w3)
    acc_sc[...] += _dot(t.astype(BF16), w2)

    @pl.when(f == pl.num_programs(1) - 1)
    def _():
        xo = x_ref[...] + 0.5 * acc_sc[...]
        if emit_x:
            xo_ref[...] = xo
        ho_ref[...] = _rms(xo, gp_ref[...]).astype(ho_ref.dtype)


def _ffn_half(x, g, w1, w3, w2, g_post, *, emit_x, post_dtype, emit_w=False):
    rows, d = x.shape
    ff = w1.shape[1]
    tm, tf = _tile(rows, 512), _tile(ff, 512)
    assert not emit_w or rows == tm
    row_spec = pl.BlockSpec((tm, d), lambda i, f: (i, 0))
    vec_spec = pl.BlockSpec((1, d), lambda i, f: (0, 0))
    w13_spec = pl.BlockSpec((d, tf), lambda i, f: (0, f))
    w2_spec = pl.BlockSpec((tf, d), lambda i, f: (f, 0))
    out_shape = [jax.ShapeDtypeStruct((rows, d), post_dtype)]
    out_specs = [row_spec]
    if emit_x:
        out_shape = [jax.ShapeDtypeStruct((rows, d), F32)] + out_shape
        out_specs = [row_spec] + out_specs
    if emit_w:
        out_shape += [jax.ShapeDtypeStruct(w.shape, BF16) for w in (w1, w3, w2)]
        out_specs += [w13_spec, w13_spec, w2_spec]
    return pl.pallas_call(
        functools.partial(_ffn_body, emit_x=emit_x, emit_w=emit_w),
        grid=(rows // tm, ff // tf),
        in_specs=[row_spec, vec_spec, w13_spec, w13_spec, w2_spec, vec_spec],
        out_specs=out_specs,
        out_shape=out_shape,
        scratch_shapes=[pltpu.VMEM((tm, d), BF16), pltpu.VMEM((tm, d), F32)],
        compiler_params=_params("parallel", "arbitrary"),
        name="ffn_half",
    )(x, g.reshape(1, d), w1, w3, w2, g_post.reshape(1, d))


def _norm_body(x_ref, g_ref, o_ref):
    o_ref[...] = _rms(x_ref[...], g_ref[...]).astype(o_ref.dtype)


def _norm_rows(x, g, out_dtype):
    rows, d = x.shape
    tm = _tile(rows, 512)
    return pl.pallas_call(
        _norm_body,
        grid=(rows // tm,),
        in_specs=[pl.BlockSpec((tm, d), lambda i: (i, 0)), pl.BlockSpec((1, d), lambda i: (0, 0))],
        out_specs=pl.BlockSpec((tm, d), lambda i: (i, 0)),
        out_shape=jax.ShapeDtypeStruct((rows, d), out_dtype),
        compiler_params=_params("parallel"),
        name="rmsnorm_rows",
    )(x, g.reshape(1, d))


def _mm_body(a_ref, w_ref, *refs, has_residual, emit_w):
    refs = list(refs)
    r_ref = refs.pop(0) if has_residual else None
    o_ref = refs.pop(0)
    w = _bf16(w_ref[...])
    if emit_w:
        refs.pop(0)[...] = w
    acc = _dot(a_ref[...], w)
    if has_residual:
        acc = r_ref[...] + acc
    o_ref[...] = acc.astype(o_ref.dtype)


def _in_proj_body(a_ref, w_ref, z_ref, k_ref, v_ref, *wb_refs):
    j = pl.program_id(1)
    w = _bf16(w_ref[...])
    for wb_ref in wb_refs:
        wb_ref[...] = w
    acc = _dot(a_ref[...], w)

    @pl.when(j == W_IN_KA)
    def _():
        k_ref[...] = acc

    @pl.when(j == W_IN_VA)
    def _():
        v_ref[...] = acc

    @pl.when((j != W_IN_KA) & (j != W_IN_VA))
    def _():
        z_ref[...] = acc


def _in_proj(a, w, *, emit_w=False):
    rows, k = a.shape
    n_sec = w.shape[1] // SEC
    assert (W_IN_KA, W_IN_VA) == (1, 2) and n_sec == 8
    tm = _tile(rows, 1024)
    assert not emit_w or rows == tm
    w_spec = pl.BlockSpec((k, SEC), lambda i, j: (0, j))
    kv_spec = pl.BlockSpec((tm, SEC), lambda i, j: (i, 0))
    out_specs = [pl.BlockSpec((None, tm, SEC), lambda i, j: (jnp.maximum(j - 2, 0), i, 0)), kv_spec, kv_spec]
    out_shape = [jax.ShapeDtypeStruct((n_sec - 2, rows, SEC), F32)] + [jax.ShapeDtypeStruct((rows, SEC), F32)] * 2
    if emit_w:
        out_specs.append(w_spec)
        out_shape.append(jax.ShapeDtypeStruct(w.shape, BF16))
    return pl.pallas_call(
        _in_proj_body,
        grid=(rows // tm, n_sec),
        in_specs=[pl.BlockSpec((tm, k), lambda i, j: (i, 0)), w_spec],
        out_specs=out_specs,
        out_shape=out_shape,
        compiler_params=_params("parallel", "arbitrary"),
        name="in_proj",
    )(a, w)


def _matmul(a, w, *, col_block=0, n_cols=None, residual=None, out_dtype=F32, emit_w=False, name="matmul"):
    rows, k = a.shape
    n_cols = w.shape[1] if n_cols is None else n_cols
    tm, tn = _tile(rows, 1024), _tile(n_cols, 1024)
    assert not emit_w or (rows == tm and n_cols == w.shape[1] and col_block == 0)
    w_spec = pl.BlockSpec((k, tn), lambda i, j: (0, col_block + j))
    in_specs = [pl.BlockSpec((tm, k), lambda i, j: (i, 0)), w_spec]
    out_spec = pl.BlockSpec((tm, tn), lambda i, j: (i, j))
    args = [a, w]
    if residual is not None:
        in_specs.append(out_spec)
        args.append(residual)
    out_specs = [out_spec]
    out_shape = [jax.ShapeDtypeStruct((rows, n_cols), out_dtype)]
    if emit_w:
        out_specs.append(w_spec)
        out_shape.append(jax.ShapeDtypeStruct(w.shape, BF16))
    out = pl.pallas_call(
        functools.partial(_mm_body, has_residual=residual is not None, emit_w=emit_w),
        grid=(rows // tm, n_cols // tn),
        in_specs=in_specs,
        out_specs=out_specs,
        out_shape=out_shape,
        compiler_params=_params("parallel", "arbitrary"),
        name=name,
    )(*args)
    return out if emit_w else out[0]


def _t5_bucket(n):
    n = jnp.maximum(n, 0)
    max_exact = T5_BUCKETS // 2
    nf = jnp.maximum(n, 1).astype(F32)
    large = max_exact + (jnp.log(nf / max_exact) / math.log(T5_MAX_DIST / max_exact)
                         * (T5_BUCKETS - max_exact)).astype(I32)
    return jnp.where(n < max_exact, n, jnp.minimum(large, T5_BUCKETS - 1))


def _moba_prompt_body(t5_ref, q_ref, k_ref, v_ref, o_ref, kb_sc, vb_sc, kdiff_sc, bias_sc, *, nb):
    blk = MOBA_BLOCK
    nbp = 8
    assert nb <= nbp
    h, b, c = pl.program_id(0), pl.program_id(1), pl.program_id(2)
    scale = A_DH ** -0.5
    ii = lax.broadcasted_iota(I32, (blk, blk), 0)
    jj = lax.broadcasted_iota(I32, (blk, blk), 1)

    @pl.when((b == 0) & (c == 0))
    def _():
        for t in range(2):
            bucket = _t5_bucket(t * blk + ii - jj)
            tile = jnp.zeros((blk, blk), F32)
            for bkt in range(T5_BUCKETS):
                tile = jnp.where(bucket == bkt, t5_ref[bkt, h], tile)
            bias_sc[t] = jnp.where(ii >= jj, tile, NEG_INF) if t == 0 else tile

    @pl.when(c == 0)
    def _():
        lane = lax.broadcasted_iota(I32, (blk, 128), 1)
        km = jnp.concatenate(
            [jnp.mean(k_ref[n * blk:(n + 1) * blk, :], axis=0, keepdims=True) for n in range(nb)]
            + [jnp.zeros((1, A_DH), F32)] * (nbp - nb), axis=0)
        for n in range(nb):
            kdiff_sc[n * nbp:(n + 1) * nbp, :] = km[n:n + 1, :] - km
            rows = slice(n * blk, (n + 1) * blk)
            kb_sc[rows, 0:A_DH] = k_ref[rows, :].astype(BF16)
            kb_sc[rows, A_DH:2 * A_DH] = jnp.where(lane == n, 1.0, 0.0).astype(BF16)
        vb_sc[...] = v_ref[...].astype(BF16)

    q = q_ref[...]
    far_bias = t5_ref[T5_BUCKETS - 1, h]
    qs = (q * scale).astype(BF16)

    def attend(cc):
        n_keys = (cc + 1) * blk
        if cc > MOBA_TOPK:
            diff = lax.dot_general(q, kdiff_sc[0:cc * nbp, :], NT_DIMS, precision=lax.Precision.HIGHEST,
                                   preferred_element_type=F32)
            pair = lax.broadcasted_iota(I32, diff.shape, 1)
            m_of, n_of = jnp.right_shift(pair, 3), jnp.bitwise_and(pair, nbp - 1)
            beats = jnp.where(diff > 0, 1.0, jnp.where(diff == 0, jnp.where(n_of > m_of, 1.0, 0.0), 0.0))
            fold = jnp.where(jnp.bitwise_and(lax.broadcasted_iota(I32, (cc * nbp, 128), 0), nbp - 1)
                             == lax.broadcasted_iota(I32, (cc * nbp, 128), 1), 1.0, 0.0)
            rank = _dot(beats.astype(BF16), fold.astype(BF16))
            lane = lax.broadcasted_iota(I32, rank.shape, 1)
            sel = jnp.where(lane < cc, jnp.where(rank < MOBA_TOPK, 0.0, NEG_INF), 0.0)
            q_sel = jnp.concatenate([qs, sel.astype(BF16)], axis=1)
            s_all = lax.dot_general(q_sel, kb_sc[0:n_keys, :], NT_DIMS, preferred_element_type=F32)
        else:
            s_all = lax.dot_general(qs, kb_sc[0:n_keys, 0:A_DH], NT_DIMS, preferred_element_type=F32)
        tiles = []
        for n in range(cc + 1):
            t = s_all[:, n * blk:(n + 1) * blk]
            tiles.append(t + (bias_sc[0] if n == cc else bias_sc[1] if n == cc - 1 else far_bias))
        m = jnp.max(functools.reduce(jnp.maximum, tiles), axis=1, keepdims=True)
        ps = [jnp.exp(t - m) for t in tiles]
        l = jnp.sum(functools.reduce(jnp.add, ps), axis=1, keepdims=True)
        p_all = jnp.concatenate([p.astype(BF16) for p in ps], axis=1)
        o_ref[...] = (_dot(p_all, vb_sc[0:(cc + 1) * blk, :]) / l).astype(o_ref.dtype)

    for cc in range(nb):
        pl.when(c == cc)(functools.partial(attend, cc))


def _moba_prompt(z3, k, v, t5_table, batch, seq):
    blk = MOBA_BLOCK
    nb = seq // blk
    kv = pl.BlockSpec((seq, A_DH), lambda h, b, c: (b, h))
    return pl.pallas_call(
        functools.partial(_moba_prompt_body, nb=nb),
        grid=(A_HEADS, batch, nb),
        in_specs=[pl.BlockSpec(memory_space=pltpu.SMEM),
                  pl.BlockSpec((None, blk, A_DH), lambda h, b, c: (SEC_QA, b * nb + c, h)), kv, kv],
        out_specs=pl.BlockSpec((blk, A_DH), lambda h, b, c: (b * nb + c, h)),
        out_shape=jax.ShapeDtypeStruct((batch * seq, A_HEADS * A_DH), BF16),
        scratch_shapes=[pltpu.VMEM((seq, 2 * A_DH), BF16), pltpu.VMEM((seq, A_DH), BF16),
                        pltpu.VMEM((64, A_DH), F32), pltpu.VMEM((2, blk, blk), F32)],
        compiler_params=_params("arbitrary", "arbitrary", "arbitrary"),
        name="moba_prompt",
    )(t5_table, z3, k, v)


def _rope_body(inv_ref, cos_ref, sin_ref, *, pos0):
    pos = pos0 + lax.broadcasted_iota(I32, cos_ref.shape, 0)
    ang = pos.astype(F32) * inv_ref[...]
    cos_ref[...] = jnp.cos(ang)
    sin_ref[...] = jnp.sin(ang)


def _rope_tables(pos0, n_pos):
    half = R_DK // 2
    inv = 1.0 / (RET_ROPE_BASE ** jnp.linspace(0.0, 1.0, half, dtype=F32))
    rows = -(-n_pos // 8) * 8
    return pl.pallas_call(
        functools.partial(_rope_body, pos0=pos0),
        out_shape=[jax.ShapeDtypeStruct((rows, half), F32)] * 2,
        name="rope_tables",
    )(inv.reshape(1, half))


def _rotate(x, cos, sin):
    half = x.shape[-1] // 2
    x1, x2 = x[:, :half], x[:, half:]
    return jnp.concatenate([x1 * cos - x2 * sin, x1 * sin + x2 * cos], axis=1)


def _log_decay(h, shape):
    hf = jnp.full(shape, h, I32).astype(F32)
    return jnp.log(1.0 - jnp.exp2(-5.0 - hf))


def _ret_prompt_body(q_ref, k_ref, v_ref, g_ref, cos_ref, sin_ref, o_ref, so_ref, s_sc, *, c):
    n = pl.program_id(1)

    @pl.when(n == 0)
    def _():
        s_sc[...] = jnp.zeros_like(s_sc)

    i = lax.broadcasted_iota(I32, (c, c), 0).astype(F32)
    j = lax.broadcasted_iota(I32, (c, c), 1).astype(F32)
    diff = i - j
    i_col = lax.broadcasted_iota(I32, (c, 1), 0).astype(F32)
    cos, sin = cos_ref[...], sin_ref[...]

    for h in range(R_HEADS):
        dk, dv = slice(h * R_DK, (h + 1) * R_DK), slice(h * R_DV, (h + 1) * R_DV)
        dmask = jnp.where(diff >= 0, jnp.exp(jnp.maximum(diff, 0.0) * _log_decay(h, (c, c))), 0.0)
        lg_col = _log_decay(h, (c, 1))
        q_dec = jnp.exp((i_col + 1.0) * lg_col)
        k_dec = jnp.exp((c - 1.0 - i_col) * lg_col)
        c_dec = jnp.exp(c * _log_decay(h, (1, R_DV)))
        qr = _rotate(q_ref[:, dk], cos, sin)
        kr = _rotate(k_ref[:, dk], cos, sin) * (R_DK ** -0.5)
        vb = v_ref[:, dv].astype(BF16)
        s = s_sc[h]
        att = lax.dot_general(qr.astype(BF16), kr.astype(BF16), NT_DIMS, preferred_element_type=F32) * dmask
        o = _dot(att.astype(BF16), vb) + _dot((qr * q_dec).astype(BF16), s.astype(BF16))
        s_sc[h] = s * c_dec + lax.dot_general((kr * k_dec).astype(BF16), vb, TN_DIMS, preferred_element_type=F32)
        on = o * lax.rsqrt(jnp.mean(o * o, axis=-1, keepdims=True) + EPS)
        o_ref[:, dv] = (on * _silu(g_ref[:, dv])).astype(o_ref.dtype)

    @pl.when(n == pl.num_programs(1) - 1)
    def _():
        so_ref[...] = s_sc[...]


def _ret_prompt(z3, cos, sin, batch, seq):
    c = math.gcd(seq, RET_CHUNK)
    nc = seq // c

    def sec(s):
        return pl.BlockSpec((None, c, SEC), lambda b, n: (s, b * nc + n, 0))

    tab = pl.BlockSpec((c, R_DK // 2), lambda b, n: (n, 0))
    return pl.pallas_call(
        functools.partial(_ret_prompt_body, c=c),
        grid=(batch, nc),
        in_specs=[sec(SEC_QR), sec(SEC_KR), sec(SEC_VR), sec(SEC_GR), tab, tab],
        out_specs=[pl.BlockSpec((c, R_HEADS * R_DV), lambda b, n: (b * nc + n, 0)),
                   pl.BlockSpec((None, R_HEADS, R_DK, R_DV), lambda b, n: (b, 0, 0, 0))],
        out_shape=[jax.ShapeDtypeStruct((batch * seq, R_HEADS * R_DV), BF16),
                   jax.ShapeDtypeStruct((batch, R_HEADS, R_DK, R_DV), F32)],
        scratch_shapes=[pltpu.VMEM((R_HEADS, R_DK, R_DV), F32)],
        compiler_params=_params("parallel", "arbitrary"),
        name="retention_prompt",
    )(z3, z3, z3, z3, cos, sin)


def _ret_step_body(q_ref, k_ref, v_ref, g_ref, cos_ref, sin_ref, s_ref, o_ref, so_ref):
    cos, sin = cos_ref[0:1, :], sin_ref[0:1, :]
    row0 = lax.broadcasted_iota(I32, (8, R_DK), 0) == 0
    hi = lax.Precision.HIGHEST
    for h in range(R_HEADS):
        decay = jnp.exp(_log_decay(h, (1, R_DV)))
        qr = _rotate(q_ref[h:h + 1, :], cos, sin)
        kr = _rotate(k_ref[h:h + 1, :], cos, sin) * (R_DK ** -0.5)
        v = v_ref[h:h + 1, :]
        s = s_ref[h]
        q8 = jnp.broadcast_to(qr * decay, (8, R_DK))
        qs = lax.dot_general(q8, s, (((1,), (0,)), ((), ())), precision=hi, preferred_element_type=F32)[0:1, :]
        o = jnp.sum(qr * kr, axis=-1, keepdims=True) * v + qs
        k8 = jnp.where(row0, jnp.broadcast_to(kr, (8, R_DK)), 0.0)
        v8 = jnp.broadcast_to(v, (8, R_DV))
        so_ref[h] = s * decay + lax.dot_general(k8, v8, TN_DIMS, precision=hi, preferred_element_type=F32)
        on = o * lax.rsqrt(jnp.mean(o * o, axis=-1, keepdims=True) + EPS)
        o_ref[h:h + 1, :] = (on * _silu(g_ref[h:h + 1, :])).astype(o_ref.dtype)


def _ret_step(z3, cos, sin, state):
    batch = z3.shape[1]
    z4 = z3.reshape(z3.shape[0], batch, R_HEADS, R_DK)

    def sec(s):
        return pl.BlockSpec((None, None, R_HEADS, R_DK), lambda b: (s, b, 0, 0))

    tab = pl.BlockSpec((8, R_DK // 2), lambda b: (0, 0))
    st = pl.BlockSpec((None, R_HEADS, R_DK, R_DV), lambda b: (b, 0, 0, 0))
    return pl.pallas_call(
        _ret_step_body,
        grid=(batch,),
        in_specs=[sec(SEC_QR), sec(SEC_KR), sec(SEC_VR), sec(SEC_GR), tab, tab, st],
        out_specs=[pl.BlockSpec((None, R_HEADS, R_DV), lambda b: (b, 0, 0)), st],
        out_shape=[jax.ShapeDtypeStruct((batch, R_HEADS, R_DV), F32),
                   jax.ShapeDtypeStruct(state.shape, F32)],
        compiler_params=_params("parallel"),
        name="retention_step",
    )(z4, z4, z4, z4, cos, sin, state)


def _mem_body(q_ref, k_ref, v_ref, o_ref):
    for h in range(M_HEADS):
        dh = slice(h * M_DH, (h + 1) * M_DH)
        s = lax.dot_general(q_ref[:, dh].astype(BF16), k_ref[:, dh].astype(BF16), NT_DIMS,
                            preferred_element_type=F32) * (M_DH ** -0.5)
        m = jnp.max(s, axis=-1, keepdims=True)
        p = jnp.exp(s - m)
        l = jnp.sum(p, axis=-1, keepdims=True)
        o_ref[:, dh] = (_dot(p.astype(BF16), v_ref[:, dh].astype(BF16)) / l).astype(o_ref.dtype)


def _mem_attend(z3, mk, mv, batch, seq):
    n_mem = mk.shape[0] // batch
    width = M_HEADS * M_DH
    ts = _tile(seq, 512)
    nt = seq // ts
    kv = pl.BlockSpec((n_mem, width), lambda b, t: (b, 0))
    return pl.pallas_call(
        _mem_body,
        grid=(batch, nt),
        in_specs=[pl.BlockSpec((None, ts, width), lambda b, t: (SEC_QM, b * nt + t, 0)), kv, kv],
        out_specs=pl.BlockSpec((ts, width), lambda b, t: (b * nt + t, 0)),
        out_shape=jax.ShapeDtypeStruct((batch * seq, width), BF16),
        compiler_params=_params("parallel", "arbitrary"),
        name="mem_attend",
    )(z3, mk, mv)


def _merge_body(h_ref, *refs, emit_w):
    x_refs, wg_refs, bg_refs, wb_refs = refs[0:3], refs[3:6], refs[6:9], refs[9:12]
    o_ref = refs[12]
    wgo_refs, wbo_refs = (refs[13:16], refs[16:19]) if emit_w else ((), ())
    h = h_ref[...]
    merged = None
    for g in range(3):
        wg, wb = _bf16(wg_refs[g][...]), _bf16(wb_refs[g][...])
        if emit_w:
            wgo_refs[g][...] = wg
            wbo_refs[g][...] = wb
        term = jax.nn.sigmoid(_dot(h, wg) + bg_refs[g][...]) * _dot(x_refs[g][...], wb)
        merged = term if merged is None else merged + term
    o_ref[...] = merged.astype(o_ref.dtype)


def _merge(h, branches, gate_ws, b_gate, branch_ws, *, emit_w=False):
    rows, d = h.shape
    tm, tn = _tile(rows, 1024), _tile(d, 256 if emit_w else 512)
    assert not emit_w or rows == tm
    nj = d // tn
    b_gate = b_gate.reshape(1, 3 * d)

    def row(width):
        return pl.BlockSpec((tm, width), lambda i, j: (i, 0))

    def cols(n_rows, first_col=0):
        return pl.BlockSpec((n_rows, tn), lambda i, j: (0, first_col // tn + j))

    out_specs = [pl.BlockSpec((tm, tn), lambda i, j: (i, j))]
    out_shape = [jax.ShapeDtypeStruct((rows, d), BF16)]
    if emit_w:
        out_specs += [cols(d)] * 3 + [cols(w.shape[0]) for w in branch_ws]
        out_shape += [jax.ShapeDtypeStruct((d, d), BF16)] * 3
        out_shape += [jax.ShapeDtypeStruct(w.shape, BF16) for w in branch_ws]
    out = pl.pallas_call(
        functools.partial(_merge_body, emit_w=emit_w),
        grid=(rows // tm, nj),
        in_specs=([row(d)] + [row(x.shape[1]) for x in branches]
                  + [cols(d, first) for _, first in gate_ws]
                  + [cols(1, g * d) for g in range(3)]
                  + [cols(w.shape[0]) for w in branch_ws]),
        out_specs=out_specs,
        out_shape=out_shape,
        compiler_params=_params("parallel", "arbitrary"),
        name="gated_merge",
    )(h, *branches, *[w for w, _ in gate_ws], b_gate, b_gate, b_gate, *branch_ws)
    if emit_w:
        return out[0], [(w, 0) for w in out[1:4]], out[4:7]
    return out[0]


PAGES_PER_BLOCK = MOBA_BLOCK // PAGE_SIZE


def _moba_gate_body(pt_ref, q_ref, *refs, nb_past, bps):
    del pt_ref
    k_refs = refs[:bps * PAGES_PER_BLOCK]
    sel_ref, g_sc = refs[bps * PAGES_PER_BLOCK:]
    n = pl.program_id(1)
    q = q_ref[...]
    lane = lax.broadcasted_iota(I32, (A_HEADS, 128), 1)

    @pl.when(n == 0)
    def _():
        g_sc[...] = jnp.zeros_like(g_sc)

    g_all = g_sc[...]
    for i in range(bps):
        pages = [k_refs[i * PAGES_PER_BLOCK + t][...] for t in range(PAGES_PER_BLOCK)]
        ksum = functools.reduce(jnp.add, [jnp.sum(x, axis=0) for x in pages])
        gate = jnp.sum(q * ksum, axis=1, keepdims=True) / MOBA_BLOCK
        g_all = jnp.where(lane == n * bps + i, gate, g_all)
    g_sc[...] = g_all

    @pl.when(n == pl.num_programs(1) - 1)
    def _():
        gw = jnp.where(lane < nb_past, g_sc[...], NEG_INF)
        sel = jnp.zeros(gw.shape, I32)
        for r in range(MOBA_TOPK):
            mx = jnp.max(gw, axis=1, keepdims=True)
            idx = jnp.min(jnp.where(gw == mx, lane, 128), axis=1, keepdims=True)
            sel = jnp.where(lane == r, idx, sel)
            gw = jnp.where(lane == idx, -jnp.inf, gw)
        sel_ref[...] = sel


def _moba_attend_body(pt_ref, sel_ref, t5_ref, q_ref, kn_ref, vn_ref, ck_ref, cv_ref, o_ref, kbuf, vbuf, sem,
                      *, past_len):
    b = pl.program_id(0)
    slot = lax.rem(b, 2)
    scale = A_DH ** -0.5
    n_tiles = MOBA_TOPK * PAGES_PER_BLOCK

    def tile_copies(bb, to_slot):
        copies = []
        for h in range(A_HEADS):
            for r in range(MOBA_TOPK):
                block = sel_ref[bb, h * MOBA_TOPK + r]
                for t in range(PAGES_PER_BLOCK):
                    page = pt_ref[bb, block * PAGES_PER_BLOCK + t]
                    j = r * PAGES_PER_BLOCK + t
                    copies.append(pltpu.make_async_copy(ck_ref.at[page, :, h, :], kbuf.at[to_slot, h, j],
                                                        sem.at[to_slot]))
                    copies.append(pltpu.make_async_copy(cv_ref.at[page, :, h, :], vbuf.at[to_slot, h, j],
                                                        sem.at[to_slot]))
        return copies

    @pl.when(b == 0)
    def _():
        for cp in tile_copies(0, 0):
            cp.start()

    @pl.when(b + 1 < pl.num_programs(0))
    def _():
        for cp in tile_copies(b + 1, 1 - slot):
            cp.start()

    for cp in tile_copies(b, slot):
        cp.wait()

    q, kn, vn = q_ref[...], kn_ref[...], vn_ref[...]
    lane = lax.broadcasted_iota(I32, (1, PAGE_SIZE), 1)
    for h in range(A_HEADS):
        qh = q[h:h + 1, :]
        q8 = jnp.broadcast_to(qh, (8, A_DH)).astype(BF16)
        s = []
        for r in range(MOBA_TOPK):
            block = sel_ref[b, h * MOBA_TOPK + r]
            for t in range(PAGES_PER_BLOCK):
                kt = kbuf[slot, h, r * PAGES_PER_BLOCK + t].astype(BF16)
                raw = lax.dot_general(q8, kt, NT_DIMS, preferred_element_type=F32)[0:1, :]
                bucket = _t5_bucket(past_len - (block * MOBA_BLOCK + t * PAGE_SIZE + lane))
                bias = jnp.zeros((1, PAGE_SIZE), F32)
                for bkt in range(T5_BUCKETS):
                    bias = jnp.where(bucket == bkt, t5_ref[bkt, h], bias)
                s.append(raw * scale + bias)
        s_own = jnp.sum(qh * kn[h:h + 1, :], axis=1, keepdims=True) * scale + t5_ref[0, h]
        m = jnp.maximum(jnp.max(functools.reduce(jnp.maximum, s), axis=1, keepdims=True), s_own)
        p = [jnp.exp(x - m) for x in s]
        p_own = jnp.exp(s_own - m)
        l = jnp.sum(functools.reduce(jnp.add, p), axis=1, keepdims=True) + p_own
        acc = p_own * vn[h:h + 1, :]
        for j in range(n_tiles):
            p8 = jnp.broadcast_to(p[j], (8, PAGE_SIZE)).astype(BF16)
            acc = acc + _dot(p8, vbuf[slot, h, j].astype(BF16))[0:1, :]
        o_ref[h:h + 1, :] = acc / l


def _moba_step(z3, k_new, v_new, cache_k, cache_v, page_table, t5_table):
    batch, n_pages = page_table.shape
    past_len = n_pages * PAGE_SIZE
    assert past_len % MOBA_BLOCK == 0 and MOBA_BLOCK % PAGE_SIZE == 0
    nb_past = past_len // MOBA_BLOCK
    assert MOBA_TOPK <= nb_past <= 128
    z4 = z3.reshape(z3.shape[0], batch, A_HEADS, A_DH)
    k_new = k_new.reshape(batch, A_HEADS, A_DH)
    v_new = v_new.reshape(batch, A_HEADS, A_DH)
    bps = _tile(nb_past, 8)

    def page(t):
        return pl.BlockSpec((None, PAGE_SIZE, A_HEADS, A_DH),
                            lambda b, n, pt: (pt[b, n * bps * PAGES_PER_BLOCK + t], 0, 0, 0))

    pages = [page(t) for t in range(bps * PAGES_PER_BLOCK)]
    sel = pl.pallas_call(
        functools.partial(_moba_gate_body, nb_past=nb_past, bps=bps),
        grid_spec=pltpu.PrefetchScalarGridSpec(
            num_scalar_prefetch=1,
            grid=(batch, nb_past // bps),
            in_specs=[pl.BlockSpec((None, None, A_HEADS, A_DH), lambda b, n, pt: (SEC_QA, b, 0, 0))] + pages,
            out_specs=pl.BlockSpec((None, A_HEADS, 128), lambda b, n, pt: (b, 0, 0)),
            scratch_shapes=[pltpu.VMEM((A_HEADS, 128), F32)],
        ),
        out_shape=jax.ShapeDtypeStruct((batch, A_HEADS, 128), I32),
        compiler_params=_params("parallel", "arbitrary"),
        name="moba_gate",
    )(page_table, z4, *([cache_k] * len(pages)))
    sel = sel[:, :, :MOBA_TOPK].reshape(batch, A_HEADS * MOBA_TOPK)

    new_tok = pl.BlockSpec((None, A_HEADS, A_DH), lambda b, pt, sl: (b, 0, 0))
    tiles = pltpu.VMEM((2, A_HEADS, MOBA_TOPK * PAGES_PER_BLOCK, PAGE_SIZE, A_DH), F32)
    return pl.pallas_call(
        functools.partial(_moba_attend_body, past_len=past_len),
        grid_spec=pltpu.PrefetchScalarGridSpec(
            num_scalar_prefetch=2,
            grid=(batch,),
            in_specs=[pl.BlockSpec(memory_space=pltpu.SMEM),
                      pl.BlockSpec((None, None, A_HEADS, A_DH), lambda b, pt, sl: (SEC_QA, b, 0, 0)),
                      new_tok, new_tok,
                      pl.BlockSpec(memory_space=pl.ANY), pl.BlockSpec(memory_space=pl.ANY)],
            out_specs=pl.BlockSpec((None, A_HEADS, A_DH), lambda b, pt, sl: (b, 0, 0)),
            scratch_shapes=[tiles, tiles, pltpu.SemaphoreType.DMA((2,))],
        ),
        out_shape=jax.ShapeDtypeStruct((batch, A_HEADS, A_DH), F32),
        compiler_params=_params("arbitrary"),
        name="moba_attend",
    )(page_table, sel, t5_table, z4, k_new, v_new, cache_k, cache_v)


def kernel(x_prompt, x_sample, mem_prompt, cache_k, cache_v, cache_mem_k, cache_mem_v, state_ret, page_table, t5_table, ffn1_norm, ffn1_w1, ffn1_w3, ffn1_w2, mix_norm, mem_norm, w_in, w_mem_kv, w_gate, b_gate, w_br_moba, w_br_ret, w_br_mem, w_out, ffn2_norm, ffn2_w1, ffn2_w3, ffn2_w2, final_norm):
    batch, seq, d = x_prompt.shape
    dec_batch, dec_seq, _ = x_sample.shape
    assert dec_seq == 1
    depth = w_in.shape[0]
    n_mem = mem_prompt.shape[1]
    past_len = page_table.shape[1] * PAGE_SIZE
    cos_p, sin_p = _rope_tables(0, seq)
    cos_s, sin_s = _rope_tables(past_len, dec_seq)

    xp = x_prompt.reshape(batch * seq, d)
    xs = x_sample.reshape(dec_batch, d)
    outs = [[] for _ in range(8)]
    for l in range(depth):
        last = l == depth - 1
        g_next = final_norm if last else ffn1_norm[l + 1]
        branch_ws = [w_br_moba[l], w_br_ret[l], w_br_mem[l]]
        gate_ws = [(w_gate[l], g * d) for g in range(3)]

        x1s, h2s, *f1_b = _ffn_half(xs, ffn1_norm[l], ffn1_w1[l], ffn1_w3[l], ffn1_w2[l], mix_norm[l],
                                    emit_x=True, post_dtype=BF16, emit_w=True)
        x1, h2 = _ffn_half(xp, ffn1_norm[l], *f1_b, mix_norm[l], emit_x=True, post_dtype=BF16)
        z3s, ka_s, va_s, w_in_b = _in_proj(h2s, w_in[l], emit_w=True)
        z3, ka, va = _in_proj(h2, w_in_b)

        oa_s = _moba_step(z3s, ka_s, va_s, cache_k[l], cache_v[l], page_table, t5_table)
        o_rs, s_new_s = _ret_step(z3s, cos_s, sin_s, state_ret[l])
        z3s_pad = jnp.broadcast_to(z3s[:, :, None, :], (z3s.shape[0], dec_batch, 8, SEC))
        z3s_pad = z3s_pad.reshape(z3s.shape[0], dec_batch * 8, SEC)
        om_s = _mem_attend(z3s_pad, cache_mem_k[l].reshape(dec_batch * n_mem, M_HEADS * M_DH),
                           cache_mem_v[l].reshape(dec_batch * n_mem, M_HEADS * M_DH), dec_batch, 8)
        om_s = om_s.reshape(dec_batch, 8, M_HEADS * M_DH)[:, 0, :]
        branches_s = [oa_s.reshape(dec_batch, -1).astype(BF16), o_rs.reshape(dec_batch, -1).astype(BF16), om_s]

        mem_h = _norm_rows(mem_prompt.reshape(batch * n_mem, d), mem_norm[l], BF16)
        mk_p = _matmul(mem_h, w_mem_kv[l], col_block=0, n_cols=M_HEADS * M_DH, name="mem_k")
        mv_p = _matmul(mem_h, w_mem_kv[l], col_block=1, n_cols=M_HEADS * M_DH, name="mem_v")
        oa = _moba_prompt(z3, ka, va, t5_table, batch, seq)
        o_r, s_new_p = _ret_prompt(z3, cos_p, sin_p, batch, seq)
        om = _mem_attend(z3, mk_p, mv_p, batch, seq)

        merged_s, gate_b, branch_b = _merge(h2s, branches_s, gate_ws, b_gate[l], branch_ws, emit_w=True)
        merged = _merge(h2, [oa, o_r, om], gate_b, b_gate[l], branch_b)
        x2s, w_out_b = _matmul(merged_s, w_out[l], residual=x1s, emit_w=True, name="out_proj")
        x2 = _matmul(merged, w_out_b, residual=x1, name="out_proj")
        xs, *rest = _ffn_half(x2s, ffn2_norm[l], ffn2_w1[l], ffn2_w3[l], ffn2_w2[l], g_next,
                              emit_x=not last, post_dtype=F32, emit_w=True)
        f2_b = rest[-3:]
        xp = _ffn_half(x2, ffn2_norm[l], *f2_b, g_next, emit_x=not last, post_dtype=F32)[0]

        new = (ka.reshape(batch, seq, A_HEADS, A_DH), va.reshape(batch, seq, A_HEADS, A_DH),
               mk_p.reshape(batch, n_mem, M_HEADS, M_DH), mv_p.reshape(batch, n_mem, M_HEADS, M_DH), s_new_p,
               ka_s.reshape(dec_batch, dec_seq, A_HEADS, A_DH), va_s.reshape(dec_batch, dec_seq, A_HEADS, A_DH),
               s_new_s)
        for acc, val in zip(outs, new):
            acc.append(val)

    y_prompt = xp.reshape(batch, seq, d)
    y_sample = xs.reshape(dec_batch, dec_seq, d)
    return (y_prompt, y_sample) + tuple(jnp.stack(o) for o in outs)
```

```python
import functools
import math

import jax
import jax.numpy as jnp
import numpy as np
from jax import lax
from jax.experimental import pallas as pl
from jax.experimental.pallas import tpu as pltpu

F32 = jnp.float32
BF16 = jnp.bfloat16
I32 = jnp.int32

A_HEADS, A_DH = 8, 128
MOBA_BLOCK, MOBA_TOPK = 256, 3
T5_BUCKETS, T5_MAX_DIST = 32, 128
R_HEADS, R_DK, R_DV = 4, 256, 256
RET_CHUNK = 128
RET_ROPE_BASE = 10000.0
M_HEADS, M_DH = 4, 256
PAGE_SIZE = 128
EPS = 1e-6
NEG_INF = -1e30
SEC = 1024
W_IN_KA, W_IN_VA = 1, 2
(SEC_QA, SEC_QR, SEC_KR, SEC_VR, SEC_GR, SEC_QM) = range(6)

V7X_VMEM_LIMIT_BYTES = 56 * 1024 * 1024

NT_DIMS = (((1,), (1,)), ((), ()))
TN_DIMS = (((0,), (0,)), ((), ()))


def _params(*sem):
    return pltpu.CompilerParams(dimension_semantics=sem, vmem_limit_bytes=V7X_VMEM_LIMIT_BYTES)


def _tile(n, pref):
    t = min(n, pref)
    while n % t:
        t -= 1
    return t


def _rms(x, g):
    return x * lax.rsqrt(jnp.mean(x * x, axis=-1, keepdims=True) + EPS) * g


def _silu(x):
    return x * jax.nn.sigmoid(x)


def _dot(a, b):
    return jnp.dot(a, b, preferred_element_type=F32)


def _bf16(w):
    return w if w.dtype == BF16 else w.astype(BF16)


def _ffn_body(x_ref, g_ref, w1_ref, w3_ref, w2_ref, gp_ref, *refs, emit_x, emit_w):
    refs = list(refs)
    xo_ref = refs.pop(0) if emit_x else None
    ho_ref = refs.pop(0)
    wb_refs = [refs.pop(0) for _ in range(3)] if emit_w else []
    h_sc, acc_sc = refs
    f = pl.program_id(1)

    @pl.when(f == 0)
    def _():
        h_sc[...] = _rms(x_ref[...], g_ref[...]).astype(BF16)
        acc_sc[...] = jnp.zeros_like(acc_sc)

    w1, w3, w2 = _bf16(w1_ref[...]), _bf16(w3_ref[...]), _bf16(w2_ref[...])
    for wb_ref, w in zip(wb_refs, (w1, w3, w2)):
        wb_ref[...] = w
    h = h_sc[...]
    t = _silu(_dot(h, w1)) * _dot(h, w3)
    acc_sc[...] += _dot(t.astype(BF16), w2)

    @pl.when(f == pl.num_programs(1) - 1)
    def _():
        xo = x_ref[...] + 0.5 * acc_sc[...]
        if emit_x:
            xo_ref[...] = xo
        ho_ref[...] = _rms(xo, gp_ref[...]).astype(ho_ref.dtype)


def _ffn_half(x, g, w1, w3, w2, g_post, *, emit_x, post_dtype, emit_w=False):
    rows, d = x.shape
    ff = w1.shape[1]
    tm, tf = _tile(rows, 512), _tile(ff, 512)
    assert not emit_w or rows == tm
    row_spec = pl.BlockSpec((tm, d), lambda i, f: (i, 0))
    vec_spec = pl.BlockSpec((1, d), lambda i, f: (0, 0))
    w13_spec = pl.BlockSpec((d, tf), lambda i, f: (0, f))
    w2_spec = pl.BlockSpec((tf, d), lambda i, f: (f, 0))
    out_shape = [jax.ShapeDtypeStruct((rows, d), post_dtype)]
    out_specs = [row_spec]
    if emit_x:
        out_shape = [jax.ShapeDtypeStruct((rows, d), F32)] + out_shape
        out_specs = [row_spec] + out_specs
    if emit_w:
        out_shape += [jax.ShapeDtypeStruct(w.shape, BF16) for w in (w1, w3, w2)]
        out_specs += [w13_spec, w13_spec, w2_spec]
    return pl.pallas_call(
        functools.partial(_ffn_body, emit_x=emit_x, emit_w=emit_w),
        grid=(rows // tm, ff // tf),
        in_specs=[row_spec, vec_spec, w13_spec, w13_spec, w2_spec, vec_spec],
        out_specs=out_specs,
        out_shape=out_shape,
        scratch_shapes=[pltpu.VMEM((tm, d), BF16), pltpu.VMEM((tm, d), F32)],
        compiler_params=_params("parallel", "arbitrary"),
        name="ffn_half",
    )(x, g.reshape(1, d), w1, w3, w2, g_post.reshape(1, d))


def _norm_body(x_ref, g_ref, o_ref):
    o_ref[...] = _rms(x_ref[...], g_ref[...]).astype(o_ref.dtype)


def _norm_rows(x, g, out_dtype):
    rows, d = x.shape
    tm = _tile(rows, 512)
    return pl.pallas_call(
        _norm_body,
        grid=(rows // tm,),
        in_specs=[pl.BlockSpec((tm, d), lambda i: (i, 0)), pl.BlockSpec((1, d), lambda i: (0, 0))],
        out_specs=pl.BlockSpec((tm, d), lambda i: (i, 0)),
        out_shape=jax.ShapeDtypeStruct((rows, d), out_dtype),
        compiler_params=_params("parallel"),
        name="rmsnorm_rows",
    )(x, g.reshape(1, d))


def _mm_body(a_ref, w_ref, *refs, has_residual, emit_w):
    refs = list(refs)
    r_ref = refs.pop(0) if has_residual else None
    o_ref = refs.pop(0)
    w = _bf16(w_ref[...])
    if emit_w:
        refs.pop(0)[...] = w
    acc = _dot(a_ref[...], w)
    if has_residual:
        acc = r_ref[...] + acc
    o_ref[...] = acc.astype(o_ref.dtype)


def _in_proj_body(a_ref, w_ref, z_ref, k_ref, v_ref, *wb_refs):
    j = pl.program_id(1)
    w = _bf16(w_ref[...])
    for wb_ref in wb_refs:
        wb_ref[...] = w
    acc = _dot(a_ref[...], w)

    @pl.when(j == W_IN_KA)
    def _():
        k_ref[...] = acc

    @pl.when(j == W_IN_VA)
    def _():
        v_ref[...] = acc

    @pl.when((j != W_IN_KA) & (j != W_IN_VA))
    def _():
        z_ref[...] = acc


def _in_proj(a, w, *, emit_w=False):
    rows, k = a.shape
    n_sec = w.shape[1] // SEC
    assert (W_IN_KA, W_IN_VA) == (1, 2) and n_sec == 8
    tm = _tile(rows, 1024)
    assert not emit_w or rows == tm
    w_spec = pl.BlockSpec((k, SEC), lambda i, j: (0, j))
    kv_spec = pl.BlockSpec((tm, SEC), lambda i, j: (i, 0))
    out_specs = [pl.BlockSpec((None, tm, SEC), lambda i, j: (jnp.maximum(j - 2, 0), i, 0)), kv_spec, kv_spec]
    out_shape = [jax.ShapeDtypeStruct((n_sec - 2, rows, SEC), F32)] + [jax.ShapeDtypeStruct((rows, SEC), F32)] * 2
    if emit_w:
        out_specs.append(w_spec)
        out_shape.append(jax.ShapeDtypeStruct(w.shape, BF16))
    return pl.pallas_call(
        _in_proj_body,
        grid=(rows // tm, n_sec),
        in_specs=[pl.BlockSpec((tm, k), lambda i, j: (i, 0)), w_spec],
        out_specs=out_specs,
        out_shape=out_shape,
        compiler_params=_params("parallel", "arbitrary"),
        name="in_proj",
    )(a, w)


def _matmul(a, w, *, col_block=0, n_cols=None, residual=None, out_dtype=F32, emit_w=False, name="matmul"):
    rows, k = a.shape
    n_cols = w.shape[1] if n_cols is None else n_cols
    tm, tn = _tile(rows, 1024), _tile(n_cols, 1024)
    assert not emit_w or (rows == tm and n_cols == w.shape[1] and col_block == 0)
    w_spec = pl.BlockSpec((k, tn), lambda i, j: (0, col_block + j))
    in_specs = [pl.BlockSpec((tm, k), lambda i, j: (i, 0)), w_spec]
    out_spec = pl.BlockSpec((tm, tn), lambda i, j: (i, j))
    args = [a, w]
    if residual is not None:
        in_specs.append(out_spec)
        args.append(residual)
    out_specs = [out_spec]
    out_shape = [jax.ShapeDtypeStruct((rows, n_cols), out_dtype)]
    if emit_w:
        out_specs.append(w_spec)
        out_shape.append(jax.ShapeDtypeStruct(w.shape, BF16))
    out = pl.pallas_call(
        functools.partial(_mm_body, has_residual=residual is not None, emit_w=emit_w),
        grid=(rows // tm, n_cols // tn),
        in_specs=in_specs,
        out_specs=out_specs,
        out_shape=out_shape,
        compiler_params=_params("parallel", "arbitrary"),
        name=name,
    )(*args)
    return out if emit_w else out[0]


def _t5_bucket(n):
    n = jnp.maximum(n, 0)
    max_exact = T5_BUCKETS // 2
    nf = jnp.maximum(n, 1).astype(F32)
    large = max_exact + (jnp.log(nf / max_exact) / math.log(T5_MAX_DIST / max_exact)
                         * (T5_BUCKETS - max_exact)).astype(I32)
    return jnp.where(n < max_exact, n, jnp.minimum(large, T5_BUCKETS - 1))


PAGES_PER_BLOCK = MOBA_BLOCK // PAGE_SIZE
SUBLANES = 8


def _block_gates(q, page_refs, first_block, gates):
    lane = lax.broadcasted_iota(I32, gates.shape, 1)
    for i in range(len(page_refs) // PAGES_PER_BLOCK):
        pages = [page_refs[i * PAGES_PER_BLOCK + t][...] for t in range(PAGES_PER_BLOCK)]
        ksum = functools.reduce(jnp.add, [jnp.sum(x, axis=0) for x in pages])
        gate = jnp.sum(q * ksum, axis=1, keepdims=True) / MOBA_BLOCK
        gates = jnp.where(lane == first_block + i, gate, gates)
    return gates


def _moba_prompt_body(*refs, nb, side_blocks):
    refs = list(refs)
    if side_blocks:
        refs.pop(0)
    t5_ref, q_ref, k_ref, v_ref = refs[:4]
    refs = refs[4:]
    if side_blocks:
        qside_ref, page_refs = refs[0], refs[1:1 + side_blocks * PAGES_PER_BLOCK]
        refs = refs[1 + side_blocks * PAGES_PER_BLOCK:]
        o_ref, gates_ref, kb_sc, vb_sc, kdiff_sc, bias_sc = refs
    else:
        o_ref, kb_sc, vb_sc, kdiff_sc, bias_sc = refs
    blk = MOBA_BLOCK
    nbp = SUBLANES
    assert nb <= nbp
    h, b, c = pl.program_id(0), pl.program_id(1), pl.program_id(2)
    scale = A_DH ** -0.5
    ii = lax.broadcasted_iota(I32, (blk, blk), 0)
    jj = lax.broadcasted_iota(I32, (blk, blk), 1)

    @pl.when((b == 0) & (c == 0))
    def _():
        for t in range(2):
            bucket = _t5_bucket(t * blk + ii - jj)
            tile = jnp.zeros((blk, blk), F32)
            for bkt in range(T5_BUCKETS):
                tile = jnp.where(bucket == bkt, t5_ref[bkt, h], tile)
            bias_sc[t] = jnp.where(ii >= jj, tile, NEG_INF) if t == 0 else tile

    @pl.when(c == 0)
    def _():
        km = jnp.concatenate(
            [jnp.mean(k_ref[n * blk:(n + 1) * blk, :], axis=0, keepdims=True) for n in range(nb)]
            + [jnp.zeros((1, A_DH), F32)] * (nbp - nb), axis=0)
        for n in range(nb):
            kdiff_sc[n * nbp:(n + 1) * nbp, :] = km[n:n + 1, :] - km
            kb_sc[n * blk:(n + 1) * blk, A_DH:2 * A_DH] = jnp.where(
                lax.broadcasted_iota(I32, (blk, 128), 1) == n, 1.0, 0.0).astype(BF16)
        kb_sc[:, 0:A_DH] = k_ref[...].astype(BF16)
        vb_sc[...] = v_ref[...].astype(BF16)
        if side_blocks:
            gates_ref[...] = jnp.zeros_like(gates_ref)

    q = q_ref[...]
    far_bias = t5_ref[T5_BUCKETS - 1, h]
    qs = (q * scale).astype(BF16)

    def attend(cc):
        if side_blocks:
            gates_ref[...] = _block_gates(qside_ref[...], page_refs, cc * side_blocks, gates_ref[...])
        n_keys = (cc + 1) * blk
        if cc <= MOBA_TOPK:
            s_all = lax.dot_general(qs, kb_sc[0:n_keys, 0:A_DH], NT_DIMS, preferred_element_type=F32)
        else:
            diff_t = lax.dot_general(kdiff_sc[0:cc * nbp, :], q, NT_DIMS, precision=lax.Precision.HIGHEST,
                                     preferred_element_type=F32)
            pair = lax.broadcasted_iota(I32, diff_t.shape, 0)
            m_of, n_of = jnp.right_shift(pair, int(math.log2(nbp))), jnp.bitwise_and(pair, nbp - 1)
            beats_t = jnp.where(diff_t > 0, 1.0, jnp.where(diff_t == 0, jnp.where(n_of > m_of, 1.0, 0.0), 0.0))
            fold = jnp.where(jnp.bitwise_and(lax.broadcasted_iota(I32, (cc * nbp, 128), 0), nbp - 1)
                             == lax.broadcasted_iota(I32, (cc * nbp, 128), 1), 1.0, 0.0)
            rank = lax.dot_general(beats_t.astype(BF16), fold.astype(BF16), TN_DIMS,
                                   preferred_element_type=F32)
            lane = lax.broadcasted_iota(I32, rank.shape, 1)
            sel = jnp.where(lane < cc, jnp.where(rank < MOBA_TOPK, 0.0, NEG_INF), 0.0)
            q_sel = jnp.concatenate([qs, sel.astype(BF16)], axis=1)
            s_all = lax.dot_general(q_sel, kb_sc[0:n_keys, :], NT_DIMS, preferred_element_type=F32)
        tiles = []
        for n in range(cc + 1):
            t = s_all[:, n * blk:(n + 1) * blk]
            tiles.append(t + (bias_sc[0] if n == cc else bias_sc[1] if n == cc - 1 else far_bias))
        m = jnp.max(functools.reduce(jnp.maximum, tiles), axis=1, keepdims=True)
        ps = [jnp.exp(t - m) for t in tiles]
        l = jnp.sum(functools.reduce(jnp.add, ps), axis=1, keepdims=True)
        p_all = jnp.concatenate([p.astype(BF16) for p in ps], axis=1)
        o_ref[...] = (_dot(p_all, vb_sc[0:(cc + 1) * blk, :]) / l).astype(o_ref.dtype)

    for cc in range(nb):
        pl.when(c == cc)(functools.partial(attend, cc))


def _moba_prompt(z3, k, v, t5_table, batch, seq, paged=None):
    blk = MOBA_BLOCK
    nb = seq // blk
    side_blocks = 0
    if paged is not None:
        page_table, q_side, cache_k = paged
        nb_past = page_table.shape[1] // PAGES_PER_BLOCK
        if page_table.shape[0] == A_HEADS * batch and nb_past % nb == 0:
            side_blocks = nb_past // nb

    in_specs = [pl.BlockSpec(memory_space=pltpu.SMEM),
                pl.BlockSpec((None, blk, A_DH), lambda h, b, c, *_: (SEC_QA, b * nb + c, h)),
                pl.BlockSpec((seq, A_DH), lambda h, b, c, *_: (b, h)),
                pl.BlockSpec((seq, A_DH), lambda h, b, c, *_: (b, h))]
    out_specs = [pl.BlockSpec((blk, A_DH), lambda h, b, c, *_: (b * nb + c, h))]
    out_shape = [jax.ShapeDtypeStruct((batch * seq, A_HEADS * A_DH), BF16)]
    args = [t5_table, z3, k, v]
    if side_blocks:
        def page(t):
            return pl.BlockSpec((None, PAGE_SIZE, A_HEADS, A_DH),
                                lambda h, b, c, pt: (pt[h * batch + b, c * side_blocks * PAGES_PER_BLOCK + t], 0, 0, 0))

        pages = [page(t) for t in range(side_blocks * PAGES_PER_BLOCK)]
        in_specs += [pl.BlockSpec((None, None, A_HEADS, A_DH), lambda h, b, c, pt: (SEC_QA, h * batch + b, 0, 0))]
        in_specs += pages
        out_specs.append(pl.BlockSpec((None, A_HEADS, 128), lambda h, b, c, pt: (h * batch + b, 0, 0)))
        out_shape.append(jax.ShapeDtypeStruct((A_HEADS * batch, A_HEADS, 128), F32))
        args = [page_table] + args + [q_side] + [cache_k] * len(pages)
    out = pl.pallas_call(
        functools.partial(_moba_prompt_body, nb=nb, side_blocks=side_blocks),
        grid_spec=pltpu.PrefetchScalarGridSpec(
            num_scalar_prefetch=1 if side_blocks else 0,
            grid=(A_HEADS, batch, nb),
            in_specs=in_specs,
            out_specs=out_specs,
            scratch_shapes=[pltpu.VMEM((seq, 2 * A_DH), BF16), pltpu.VMEM((seq, A_DH), BF16),
                            pltpu.VMEM((SUBLANES * SUBLANES, A_DH), F32), pltpu.VMEM((2, blk, blk), F32)],
        ),
        out_shape=out_shape,
        compiler_params=_params("arbitrary", "arbitrary", "arbitrary"),
        name="moba_prompt",
    )(*args)
    if paged is None:
        return out[0]
    return out[0], (out[1] if side_blocks else None)


def _rope_body(inv_ref, cos_ref, sin_ref, *, pos0):
    pos = pos0 + lax.broadcasted_iota(I32, cos_ref.shape, 0)
    ang = pos.astype(F32) * inv_ref[...]
    cos_ref[...] = jnp.cos(ang)
    sin_ref[...] = jnp.sin(ang)


def _rope_tables(pos0, n_pos):
    half = R_DK // 2
    inv = 1.0 / (RET_ROPE_BASE ** jnp.linspace(0.0, 1.0, half, dtype=F32))
    rows = -(-n_pos // 8) * 8
    return pl.pallas_call(
        functools.partial(_rope_body, pos0=pos0),
        out_shape=[jax.ShapeDtypeStruct((rows, half), F32)] * 2,
        name="rope_tables",
    )(inv.reshape(1, half))


def _rotate(x, cos, sin):
    half = x.shape[-1] // 2
    x1, x2 = x[:, :half], x[:, half:]
    return jnp.concatenate([x1 * cos - x2 * sin, x1 * sin + x2 * cos], axis=1)


def _log_decay(h, shape):
    hf = jnp.full(shape, h, I32).astype(F32)
    return jnp.log(1.0 - jnp.exp2(-5.0 - hf))


def _ret_prompt_body(q_ref, k_ref, v_ref, g_ref, cos_ref, sin_ref, o_ref, so_ref, s_sc, *, c):
    n = pl.program_id(1)

    @pl.when(n == 0)
    def _():
        s_sc[...] = jnp.zeros_like(s_sc)

    i = lax.broadcasted_iota(I32, (c, c), 0).astype(F32)
    j = lax.broadcasted_iota(I32, (c, c), 1).astype(F32)
    diff = i - j
    i_col = lax.broadcasted_iota(I32, (c, 1), 0).astype(F32)
    cos, sin = cos_ref[...], sin_ref[...]

    for h in range(R_HEADS):
        dk, dv = slice(h * R_DK, (h + 1) * R_DK), slice(h * R_DV, (h + 1) * R_DV)
        dmask = jnp.where(diff >= 0, jnp.exp(jnp.maximum(diff, 0.0) * _log_decay(h, (c, c))), 0.0)
        lg_col = _log_decay(h, (c, 1))
        q_dec = jnp.exp((i_col + 1.0) * lg_col)
        k_dec = jnp.exp((c - 1.0 - i_col) * lg_col)
        c_dec = jnp.exp(c * _log_decay(h, (1, R_DV)))
        qr = _rotate(q_ref[:, dk], cos, sin)
        kr = _rotate(k_ref[:, dk], cos, sin) * (R_DK ** -0.5)
        vb = v_ref[:, dv].astype(BF16)
        s = s_sc[h]
        att = lax.dot_general(qr.astype(BF16), kr.astype(BF16), NT_DIMS, preferred_element_type=F32) * dmask
        o = _dot(att.astype(BF16), vb) + _dot((qr * q_dec).astype(BF16), s.astype(BF16))
        s_sc[h] = s * c_dec + lax.dot_general((kr * k_dec).astype(BF16), vb, TN_DIMS, preferred_element_type=F32)
        on = o * lax.rsqrt(jnp.mean(o * o, axis=-1, keepdims=True) + EPS)
        o_ref[:, dv] = (on * _silu(g_ref[:, dv])).astype(o_ref.dtype)

    @pl.when(n == pl.num_programs(1) - 1)
    def _():
        so_ref[...] = s_sc[...]


def _ret_prompt(z3, cos, sin, batch, seq):
    c = math.gcd(seq, RET_CHUNK)
    nc = seq // c

    def sec(s):
        return pl.BlockSpec((None, c, SEC), lambda b, n: (s, b * nc + n, 0))

    tab = pl.BlockSpec((c, R_DK // 2), lambda b, n: (n, 0))
    return pl.pallas_call(
        functools.partial(_ret_prompt_body, c=c),
        grid=(batch, nc),
        in_specs=[sec(SEC_QR), sec(SEC_KR), sec(SEC_VR), sec(SEC_GR), tab, tab],
        out_specs=[pl.BlockSpec((c, R_HEADS * R_DV), lambda b, n: (b * nc + n, 0)),
                   pl.BlockSpec((None, R_HEADS, R_DK, R_DV), lambda b, n: (b, 0, 0, 0))],
        out_shape=[jax.ShapeDtypeStruct((batch * seq, R_HEADS * R_DV), BF16),
                   jax.ShapeDtypeStruct((batch, R_HEADS, R_DK, R_DV), F32)],
        scratch_shapes=[pltpu.VMEM((R_HEADS, R_DK, R_DV), F32)],
        compiler_params=_params("parallel", "arbitrary"),
        name="retention_prompt",
    )(z3, z3, z3, z3, cos, sin)


def _ret_step_body(q_ref, k_ref, v_ref, g_ref, cos_ref, sin_ref, s_ref, o_ref, so_ref):
    cos, sin = cos_ref[0:1, :], sin_ref[0:1, :]
    row0 = lax.broadcasted_iota(I32, (8, R_DK), 0) == 0
    hi = lax.Precision.HIGHEST
    for h in range(R_HEADS):
        decay = jnp.exp(_log_decay(h, (1, R_DV)))
        qr = _rotate(q_ref[h:h + 1, :], cos, sin)
        kr = _rotate(k_ref[h:h + 1, :], cos, sin) * (R_DK ** -0.5)
        v = v_ref[h:h + 1, :]
        s = s_ref[h]
        q8 = jnp.broadcast_to(qr * decay, (8, R_DK))
        qs = lax.dot_general(q8, s, (((1,), (0,)), ((), ())), precision=hi, preferred_element_type=F32)[0:1, :]
        o = jnp.sum(qr * kr, axis=-1, keepdims=True) * v + qs
        k8 = jnp.where(row0, jnp.broadcast_to(kr, (8, R_DK)), 0.0)
        v8 = jnp.broadcast_to(v, (8, R_DV))
        so_ref[h] = s * decay + lax.dot_general(k8, v8, TN_DIMS, precision=hi, preferred_element_type=F32)
        on = o * lax.rsqrt(jnp.mean(o * o, axis=-1, keepdims=True) + EPS)
        o_ref[h:h + 1, :] = (on * _silu(g_ref[h:h + 1, :])).astype(o_ref.dtype)


def _ret_step(z3, cos, sin, state):
    batch = z3.shape[1]
    z4 = z3.reshape(z3.shape[0], batch, R_HEADS, R_DK)

    def sec(s):
        return pl.BlockSpec((None, None, R_HEADS, R_DK), lambda b: (s, b, 0, 0))

    tab = pl.BlockSpec((8, R_DK // 2), lambda b: (0, 0))
    st = pl.BlockSpec((None, R_HEADS, R_DK, R_DV), lambda b: (b, 0, 0, 0))
    return pl.pallas_call(
        _ret_step_body,
        grid=(batch,),
        in_specs=[sec(SEC_QR), sec(SEC_KR), sec(SEC_VR), sec(SEC_GR), tab, tab, st],
        out_specs=[pl.BlockSpec((None, R_HEADS, R_DV), lambda b: (b, 0, 0)), st],
        out_shape=[jax.ShapeDtypeStruct((batch, R_HEADS, R_DV), F32),
                   jax.ShapeDtypeStruct(state.shape, F32)],
        compiler_params=_params("parallel"),
        name="retention_step",
    )(z4, z4, z4, z4, cos, sin, state)


def _mem_body(q_ref, k_ref, v_ref, o_ref):
    for h in range(M_HEADS):
        dh = slice(h * M_DH, (h + 1) * M_DH)
        s = lax.dot_general(q_ref[:, dh].astype(BF16), k_ref[:, dh].astype(BF16), NT_DIMS,
                            preferred_element_type=F32) * (M_DH ** -0.5)
        m = jnp.max(s, axis=-1, keepdims=True)
        p = jnp.exp(s - m)
        l = jnp.sum(p, axis=-1, keepdims=True)
        o_ref[:, dh] = (_dot(p.astype(BF16), v_ref[:, dh].astype(BF16)) / l).astype(o_ref.dtype)


def _mem_attend(z3, mk, mv, batch, seq):
    n_mem = mk.shape[0] // batch
    width = M_HEADS * M_DH
    ts = _tile(seq, 512)
    nt = seq // ts
    kv = pl.BlockSpec((n_mem, width), lambda b, t: (b, 0))
    return pl.pallas_call(
        _mem_body,
        grid=(batch, nt),
        in_specs=[pl.BlockSpec((None, ts, width), lambda b, t: (SEC_QM, b * nt + t, 0)), kv, kv],
        out_specs=pl.BlockSpec((ts, width), lambda b, t: (b * nt + t, 0)),
        out_shape=jax.ShapeDtypeStruct((batch * seq, width), BF16),
        compiler_params=_params("parallel", "arbitrary"),
        name="mem_attend",
    )(z3, mk, mv)


def _merge_body(h_ref, *refs, emit_w):
    x_refs, wg_refs, bg_refs, wb_refs = refs[0:3], refs[3:6], refs[6:9], refs[9:12]
    o_ref = refs[12]
    wgo_refs, wbo_refs = (refs[13:16], refs[16:19]) if emit_w else ((), ())
    h = h_ref[...]
    merged = None
    for g in range(3):
        wg, wb = _bf16(wg_refs[g][...]), _bf16(wb_refs[g][...])
        if emit_w:
            wgo_refs[g][...] = wg
            wbo_refs[g][...] = wb
        term = jax.nn.sigmoid(_dot(h, wg) + bg_refs[g][...]) * _dot(x_refs[g][...], wb)
        merged = term if merged is None else merged + term
    o_ref[...] = merged.astype(o_ref.dtype)


def _merge(h, branches, gate_ws, b_gate, branch_ws, *, emit_w=False):
    rows, d = h.shape
    tm, tn = _tile(rows, 1024), _tile(d, 256 if emit_w else 512)
    assert not emit_w or rows == tm
    nj = d // tn
    b_gate = b_gate.reshape(1, 3 * d)

    def row(width):
        return pl.BlockSpec((tm, width), lambda i, j: (i, 0))

    def cols(n_rows, first_col=0):
        return pl.BlockSpec((n_rows, tn), lambda i, j: (0, first_col // tn + j))

    out_specs = [pl.BlockSpec((tm, tn), lambda i, j: (i, j))]
    out_shape = [jax.ShapeDtypeStruct((rows, d), BF16)]
    if emit_w:
        out_specs += [cols(d)] * 3 + [cols(w.shape[0]) for w in branch_ws]
        out_shape += [jax.ShapeDtypeStruct((d, d), BF16)] * 3
        out_shape += [jax.ShapeDtypeStruct(w.shape, BF16) for w in branch_ws]
    out = pl.pallas_call(
        functools.partial(_merge_body, emit_w=emit_w),
        grid=(rows // tm, nj),
        in_specs=([row(d)] + [row(x.shape[1]) for x in branches]
                  + [cols(d, first) for _, first in gate_ws]
                  + [cols(1, g * d) for g in range(3)]
                  + [cols(w.shape[0]) for w in branch_ws]),
        out_specs=out_specs,
        out_shape=out_shape,
        compiler_params=_params("parallel", "arbitrary"),
        name="gated_merge",
    )(h, *branches, *[w for w, _ in gate_ws], b_gate, b_gate, b_gate, *branch_ws)
    if emit_w:
        return out[0], [(w, 0) for w in out[1:4]], out[4:7]
    return out[0]


def _moba_gate_body(pt_ref, q_ref, *refs, bps):
    del pt_ref
    k_refs, gates_ref = refs[:-1], refs[-1]
    n = pl.program_id(1)

    @pl.when(n == 0)
    def _():
        gates_ref[...] = jnp.zeros_like(gates_ref)

    gates_ref[...] = _block_gates(q_ref[...], k_refs, n * bps, gates_ref[...])


def _moba_select_body(g_ref, sel_ref, *, nb_past):
    lane = lax.broadcasted_iota(I32, g_ref.shape, 1)
    gw = jnp.where(lane < nb_past, g_ref[...], NEG_INF)
    sel = jnp.zeros(gw.shape, I32)
    for r in range(MOBA_TOPK):
        mx = jnp.max(gw, axis=1, keepdims=True)
        idx = jnp.min(jnp.where(gw == mx, lane, 128), axis=1, keepdims=True)
        sel = jnp.where(lane == r, idx, sel)
        gw = jnp.where(lane == idx, -jnp.inf, gw)
    sel_ref[...] = sel


def _moba_attend_body(pt_ref, sel_ref, t5_ref, q_ref, kn_ref, vn_ref, ck_ref, cv_ref, o_ref, kbuf, vbuf, sem,
                      *, past_len):
    b = pl.program_id(0)
    slot = lax.rem(b, 2)
    scale = A_DH ** -0.5
    n_tiles = MOBA_TOPK * PAGES_PER_BLOCK

    def tile_copies(bb, to_slot):
        copies = []
        for h in range(A_HEADS):
            for r in range(MOBA_TOPK):
                block = sel_ref[bb, h * MOBA_TOPK + r]
                for t in range(PAGES_PER_BLOCK):
                    page = pt_ref[bb, block * PAGES_PER_BLOCK + t]
                    j = r * PAGES_PER_BLOCK + t
                    copies.append(pltpu.make_async_copy(ck_ref.at[page, :, h, :], kbuf.at[to_slot, h, j],
                                                        sem.at[to_slot]))
                    copies.append(pltpu.make_async_copy(cv_ref.at[page, :, h, :], vbuf.at[to_slot, h, j],
                                                        sem.at[to_slot]))
        return copies

    @pl.when(b == 0)
    def _():
        for cp in tile_copies(0, 0):
            cp.start()

    @pl.when(b + 1 < pl.num_programs(0))
    def _():
        for cp in tile_copies(b + 1, 1 - slot):
            cp.start()

    for cp in tile_copies(b, slot):
        cp.wait()

    q, kn, vn = q_ref[...], kn_ref[...], vn_ref[...]
    lane = lax.broadcasted_iota(I32, (1, PAGE_SIZE), 1)
    for h in range(A_HEADS):
        qh = q[h:h + 1, :]
        q8 = jnp.broadcast_to(qh, (8, A_DH)).astype(BF16)
        s = []
        for r in range(MOBA_TOPK):
            block = sel_ref[b, h * MOBA_TOPK + r]
            for t in range(PAGES_PER_BLOCK):
                kt = kbuf[slot, h, r * PAGES_PER_BLOCK + t].astype(BF16)
                raw = lax.dot_general(q8, kt, NT_DIMS, preferred_element_type=F32)[0:1, :]
                bucket = _t5_bucket(past_len - (block * MOBA_BLOCK + t * PAGE_SIZE + lane))
                bias = jnp.zeros((1, PAGE_SIZE), F32)
                for bkt in range(T5_BUCKETS):
                    bias = jnp.where(bucket == bkt, t5_ref[bkt, h], bias)
                s.append(raw * scale + bias)
        s_own = jnp.sum(qh * kn[h:h + 1, :], axis=1, keepdims=True) * scale + t5_ref[0, h]
        m = jnp.maximum(jnp.max(functools.reduce(jnp.maximum, s), axis=1, keepdims=True), s_own)
        p = [jnp.exp(x - m) for x in s]
        p_own = jnp.exp(s_own - m)
        l = jnp.sum(functools.reduce(jnp.add, p), axis=1, keepdims=True) + p_own
        acc = p_own * vn[h:h + 1, :]
        for j in range(n_tiles):
            p8 = jnp.broadcast_to(p[j], (8, PAGE_SIZE)).astype(BF16)
            acc = acc + _dot(p8, vbuf[slot, h, j].astype(BF16))[0:1, :]
        o_ref[h:h + 1, :] = acc / l


def _moba_gates(q4, cache_k, page_table):
    batch, n_pages = page_table.shape
    nb_past = n_pages // PAGES_PER_BLOCK
    bps = _tile(nb_past, 8)

    def page(t):
        return pl.BlockSpec((None, PAGE_SIZE, A_HEADS, A_DH),
                            lambda b, n, pt: (pt[b, n * bps * PAGES_PER_BLOCK + t], 0, 0, 0))

    pages = [page(t) for t in range(bps * PAGES_PER_BLOCK)]
    return pl.pallas_call(
        functools.partial(_moba_gate_body, bps=bps),
        grid_spec=pltpu.PrefetchScalarGridSpec(
            num_scalar_prefetch=1,
            grid=(batch, nb_past // bps),
            in_specs=[pl.BlockSpec((None, None, A_HEADS, A_DH), lambda b, n, pt: (SEC_QA, b, 0, 0))] + pages,
            out_specs=pl.BlockSpec((None, A_HEADS, 128), lambda b, n, pt: (b, 0, 0)),
        ),
        out_shape=jax.ShapeDtypeStruct((batch, A_HEADS, 128), F32),
        compiler_params=_params("parallel", "arbitrary"),
        name="moba_gate",
    )(page_table, q4, *([cache_k] * len(pages)))


def _moba_step(z4, k_new, v_new, cache_k, cache_v, page_table, t5_table, gates=None):
    batch, n_pages = page_table.shape
    past_len = n_pages * PAGE_SIZE
    assert past_len % MOBA_BLOCK == 0 and MOBA_BLOCK % PAGE_SIZE == 0
    nb_past = past_len // MOBA_BLOCK
    assert MOBA_TOPK <= nb_past <= 128
    k_new = k_new.reshape(batch, A_HEADS, A_DH)
    v_new = v_new.reshape(batch, A_HEADS, A_DH)
    if gates is None:
        gates = _moba_gates(z4, cache_k, page_table)
    sel = pl.pallas_call(
        functools.partial(_moba_select_body, nb_past=nb_past),
        out_shape=jax.ShapeDtypeStruct((batch * A_HEADS, 128), I32),
        name="moba_select",
    )(gates.reshape(batch * A_HEADS, 128))
    sel = sel[:, :MOBA_TOPK].reshape(batch, A_HEADS * MOBA_TOPK)

    new_tok = pl.BlockSpec((None, A_HEADS, A_DH), lambda b, pt, sl: (b, 0, 0))
    tiles = pltpu.VMEM((2, A_HEADS, MOBA_TOPK * PAGES_PER_BLOCK, PAGE_SIZE, A_DH), F32)
    return pl.pallas_call(
        functools.partial(_moba_attend_body, past_len=past_len),
        grid_spec=pltpu.PrefetchScalarGridSpec(
            num_scalar_prefetch=2,
            grid=(batch,),
            in_specs=[pl.BlockSpec(memory_space=pltpu.SMEM),
                      pl.BlockSpec((None, None, A_HEADS, A_DH), lambda b, pt, sl: (SEC_QA, b, 0, 0)),
                      new_tok, new_tok,
                      pl.BlockSpec(memory_space=pl.ANY), pl.BlockSpec(memory_space=pl.ANY)],
            out_specs=pl.BlockSpec((None, A_HEADS, A_DH), lambda b, pt, sl: (b, 0, 0)),
            scratch_shapes=[tiles, tiles, pltpu.SemaphoreType.DMA((2,))],
        ),
        out_shape=jax.ShapeDtypeStruct((batch, A_HEADS, A_DH), F32),
        compiler_params=_params("arbitrary"),
        name="moba_attend",
    )(page_table, sel, t5_table, z4, k_new, v_new, cache_k, cache_v)


def kernel(x_prompt, x_sample, mem_prompt, cache_k, cache_v, cache_mem_k, cache_mem_v, state_ret, page_table, t5_table, ffn1_norm, ffn1_w1, ffn1_w3, ffn1_w2, mix_norm, mem_norm, w_in, w_mem_kv, w_gate, b_gate, w_br_moba, w_br_ret, w_br_mem, w_out, ffn2_norm, ffn2_w1, ffn2_w3, ffn2_w2, final_norm):
    batch, seq, d = x_prompt.shape
    dec_batch, dec_seq, _ = x_sample.shape
    assert dec_seq == 1
    depth = w_in.shape[0]
    n_mem = mem_prompt.shape[1]
    past_len = page_table.shape[1] * PAGE_SIZE
    cos_p, sin_p = _rope_tables(0, seq)
    cos_s, sin_s = _rope_tables(past_len, dec_seq)

    xp = x_prompt.reshape(batch * seq, d)
    xs = x_sample.reshape(dec_batch, d)
    outs = [[] for _ in range(8)]
    for l in range(depth):
        last = l == depth - 1
        g_next = final_norm if last else ffn1_norm[l + 1]
        branch_ws = [w_br_moba[l], w_br_ret[l], w_br_mem[l]]
        gate_ws = [(w_gate[l], g * d) for g in range(3)]

        x1s, h2s, *f1_b = _ffn_half(xs, ffn1_norm[l], ffn1_w1[l], ffn1_w3[l], ffn1_w2[l], mix_norm[l],
                                    emit_x=True, post_dtype=BF16, emit_w=True)
        x1, h2 = _ffn_half(xp, ffn1_norm[l], *f1_b, mix_norm[l], emit_x=True, post_dtype=BF16)
        z3s, ka_s, va_s, w_in_b = _in_proj(h2s, w_in[l], emit_w=True)
        z3, ka, va = _in_proj(h2, w_in_b)

        z4s = z3s.reshape(z3s.shape[0], dec_batch, A_HEADS, A_DH)
        oa, gates_s = _moba_prompt(z3, ka, va, t5_table, batch, seq, paged=(page_table, z4s, cache_k[l]))
        oa_s = _moba_step(z4s, ka_s, va_s, cache_k[l], cache_v[l], page_table, t5_table, gates=gates_s)

        o_rs, s_new_s = _ret_step(z3s, cos_s, sin_s, state_ret[l])
        z3s_pad = jnp.broadcast_to(z3s[:, :, None, :], (z3s.shape[0], dec_batch, 8, SEC))
        z3s_pad = z3s_pad.reshape(z3s.shape[0], dec_batch * 8, SEC)
        om_s = _mem_attend(z3s_pad, cache_mem_k[l].reshape(dec_batch * n_mem, M_HEADS * M_DH),
                           cache_mem_v[l].reshape(dec_batch * n_mem, M_HEADS * M_DH), dec_batch, 8)
        om_s = om_s.reshape(dec_batch, 8, M_HEADS * M_DH)[:, 0, :]
        branches_s = [oa_s.reshape(dec_batch, -1).astype(BF16), o_rs.reshape(dec_batch, -1).astype(BF16), om_s]

        mem_h = _norm_rows(mem_prompt.reshape(batch * n_mem, d), mem_norm[l], BF16)
        mk_p = _matmul(mem_h, w_mem_kv[l], col_block=0, n_cols=M_HEADS * M_DH, name="mem_k")
        mv_p = _matmul(mem_h, w_mem_kv[l], col_block=1, n_cols=M_HEADS * M_DH, name="mem_v")
        o_r, s_new_p = _ret_prompt(z3, cos_p, sin_p, batch, seq)
        om = _mem_attend(z3, mk_p, mv_p, batch, seq)

        merged_s, gate_b, branch_b = _merge(h2s, branches_s, gate_ws, b_gate[l], branch_ws, emit_w=True)
        merged = _merge(h2, [oa, o_r, om], gate_b, b_gate[l], branch_b)
        x2s, w_out_b = _matmul(merged_s, w_out[l], residual=x1s, emit_w=True, name="out_proj")
        x2 = _matmul(merged, w_out_b, residual=x1, name="out_proj")
        xs, *rest = _ffn_half(x2s, ffn2_norm[l], ffn2_w1[l], ffn2_w3[l], ffn2_w2[l], g_next,
                              emit_x=not last, post_dtype=F32, emit_w=True)
        f2_b = rest[-3:]
        xp = _ffn_half(x2, ffn2_norm[l], *f2_b, g_next, emit_x=not last, post_dtype=F32)[0]

        new = (ka.reshape(batch, seq, A_HEADS, A_DH), va.reshape(batch, seq, A_HEADS, A_DH),
               mk_p.reshape(batch, n_mem, M_HEADS, M_DH), mv_p.reshape(batch, n_mem, M_HEADS, M_DH), s_new_p,
               ka_s.reshape(dec_batch, dec_seq, A_HEADS, A_DH), va_s.reshape(dec_batch, dec_seq, A_HEADS, A_DH),
               s_new_s)
        for acc, val in zip(outs, new):
            acc.append(val)

    y_prompt = xp.reshape(batch, seq, d)
    y_sample = xs.reshape(dec_batch, dec_seq, d)
    return (y_prompt, y_sample) + tuple(jnp.stack(o) for o in outs)
```

```python
import functools
import math

import jax
import jax.numpy as jnp
import numpy as np
from jax import lax
from jax.experimental import pallas as pl
from jax.experimental.pallas import tpu as pltpu

F32 = jnp.float32
BF16 = jnp.bfloat16
I32 = jnp.int32

A_HEADS, A_DH = 8, 128
MOBA_BLOCK, MOBA_TOPK = 256, 3
T5_BUCKETS, T5_MAX_DIST = 32, 128
R_HEADS, R_DK, R_DV = 4, 256, 256
RET_CHUNK = 128
RET_ROPE_BASE = 10000.0
M_HEADS, M_DH = 4, 256
PAGE_SIZE = 128
EPS = 1e-6
NEG_INF = -1e30
SEC = 1024
W_IN_KA, W_IN_VA = 1, 2
(SEC_QA, SEC_QR, SEC_KR, SEC_VR, SEC_GR, SEC_QM) = range(6)

V7X_VMEM_LIMIT_BYTES = 56 * 1024 * 1024

NT_DIMS = (((1,), (1,)), ((), ()))
TN_DIMS = (((0,), (0,)), ((), ()))


def _params(*sem):
    return pltpu.CompilerParams(dimension_semantics=sem, vmem_limit_bytes=V7X_VMEM_LIMIT_BYTES)


def _tile(n, pref):
    t = min(n, pref)
    while n % t:
        t -= 1
    return t


def _rms(x, g):
    return x * lax.rsqrt(jnp.mean(x * x, axis=-1, keepdims=True) + EPS) * g


def _silu(x):
    return x * jax.nn.sigmoid(x)


def _dot(a, b):
    return jnp.dot(a, b, preferred_element_type=F32)


def _bf16(w):
    return w if w.dtype == BF16 else w.astype(BF16)


def _ffn_body(x_ref, g_ref, w1_ref, w3_ref, w2_ref, gp_ref, *refs, emit_x, emit_w):
    refs = list(refs)
    xo_ref = refs.pop(0) if emit_x else None
    ho_ref = refs.pop(0)
    wb_refs = [refs.pop(0) for _ in range(3)] if emit_w else []
    h_sc, acc_sc = refs
    f = pl.program_id(1)

    @pl.when(f == 0)
    def _():
        h_sc[...] = _rms(x_ref[...], g_ref[...]).astype(BF16)
        acc_sc[...] = jnp.zeros_like(acc_sc)

    w1, w3, w2 = _bf16(w1_ref[...]), _bf16(w3_ref[...]), _bf16(w2_ref[...])
    for wb_ref, w in zip(wb_refs, (w1, w3, w2)):
        wb_ref[...] = w
    h = h_sc[...]
    t = _silu(_dot(h, w1)) * _dot(h, w3)
    acc_sc[...] += _dot(t.astype(BF16), w2)

    @pl.when(f == pl.num_programs(1) - 1)
    def _():
        xo = x_ref[...] + 0.5 * acc_sc[...]
        if emit_x:
            xo_ref[...] = xo
        ho_ref[...] = _rms(xo, gp_ref[...]).astype(ho_ref.dtype)


def _ffn_half(x, g, w1, w3, w2, g_post, *, emit_x, post_dtype, emit_w=False):
    rows, d = x.shape
    ff = w1.shape[1]
    tm, tf = _tile(rows, 512), _tile(ff, 512)
    assert not emit_w or rows == tm
    row_spec = pl.BlockSpec((tm, d), lambda i, f: (i, 0))
    vec_spec = pl.BlockSpec((1, d), lambda i, f: (0, 0))
    w13_spec = pl.BlockSpec((d, tf), lambda i, f: (0, f))
    w2_spec = pl.BlockSpec((tf, d), lambda i, f: (f, 0))
    out_shape = [jax.ShapeDtypeStruct((rows, d), post_dtype)]
    out_specs = [row_spec]
    if emit_x:
        out_shape = [jax.ShapeDtypeStruct((rows, d), F32)] + out_shape
        out_specs = [row_spec] + out_specs
    if emit_w:
        out_shape += [jax.ShapeDtypeStruct(w.shape, BF16) for w in (w1, w3, w2)]
        out_specs += [w13_spec, w13_spec, w2_spec]
    return pl.pallas_call(
        functools.partial(_ffn_body, emit_x=emit_x, emit_w=emit_w),
        grid=(rows // tm, ff // tf),
        in_specs=[row_spec, vec_spec, w13_spec, w13_spec, w2_spec, vec_spec],
        out_specs=out_specs,
        out_shape=out_shape,
        scratch_shapes=[pltpu.VMEM((tm, d), BF16), pltpu.VMEM((tm, d), F32)],
        compiler_params=_params("parallel", "arbitrary"),
        name="ffn_half",
    )(x, g.reshape(1, d), w1, w3, w2, g_post.reshape(1, d))


def _norm_body(x_ref, g_ref, o_ref):
    o_ref[...] = _rms(x_ref[...], g_ref[...]).astype(o_ref.dtype)


def _norm_rows(x, g, out_dtype):
    rows, d = x.shape
    tm = _tile(rows, 512)
    return pl.pallas_call(
        _norm_body,
        grid=(rows // tm,),
        in_specs=[pl.BlockSpec((tm, d), lambda i: (i, 0)), pl.BlockSpec((1, d), lambda i: (0, 0))],
        out_specs=pl.BlockSpec((tm, d), lambda i: (i, 0)),
        out_shape=jax.ShapeDtypeStruct((rows, d), out_dtype),
        compiler_params=_params("parallel"),
        name="rmsnorm_rows",
    )(x, g.reshape(1, d))


def _mm_body(a_ref, w_ref, *refs, has_residual, emit_w):
    refs = list(refs)
    r_ref = refs.pop(0) if has_residual else None
    o_ref = refs.pop(0)
    w = _bf16(w_ref[...])
    if emit_w:
        refs.pop(0)[...] = w
    acc = _dot(a_ref[...], w)
    if has_residual:
        acc = r_ref[...] + acc
    o_ref[...] = acc.astype(o_ref.dtype)


def _in_proj_body(a_ref, w_ref, z_ref, k_ref, v_ref, *wb_refs):
    j = pl.program_id(1)
    w = _bf16(w_ref[...])
    for wb_ref in wb_refs:
        wb_ref[...] = w
    acc = _dot(a_ref[...], w)

    @pl.when(j == W_IN_KA)
    def _():
        k_ref[...] = acc

    @pl.when(j == W_IN_VA)
    def _():
        v_ref[...] = acc

    @pl.when((j != W_IN_KA) & (j != W_IN_VA))
    def _():
        z_ref[...] = acc


def _in_proj(a, w, *, emit_w=False):
    rows, k = a.shape
    n_sec = w.shape[1] // SEC
    assert (W_IN_KA, W_IN_VA) == (1, 2) and n_sec == 8
    tm = _tile(rows, 1024)
    assert not emit_w or rows == tm
    w_spec = pl.BlockSpec((k, SEC), lambda i, j: (0, j))
    kv_spec = pl.BlockSpec((tm, SEC), lambda i, j: (i, 0))
    out_specs = [pl.BlockSpec((None, tm, SEC), lambda i, j: (jnp.maximum(j - 2, 0), i, 0)), kv_spec, kv_spec]
    out_shape = [jax.ShapeDtypeStruct((n_sec - 2, rows, SEC), F32)] + [jax.ShapeDtypeStruct((rows, SEC), F32)] * 2
    if emit_w:
        out_specs.append(w_spec)
        out_shape.append(jax.ShapeDtypeStruct(w.shape, BF16))
    return pl.pallas_call(
        _in_proj_body,
        grid=(rows // tm, n_sec),
        in_specs=[pl.BlockSpec((tm, k), lambda i, j: (i, 0)), w_spec],
        out_specs=out_specs,
        out_shape=out_shape,
        compiler_params=_params("parallel", "arbitrary"),
        name="in_proj",
    )(a, w)


def _matmul(a, w, *, col_block=0, n_cols=None, residual=None, out_dtype=F32, emit_w=False, name="matmul"):
    rows, k = a.shape
    n_cols = w.shape[1] if n_cols is None else n_cols
    tm, tn = _tile(rows, 1024), _tile(n_cols, 1024)
    assert not emit_w or (rows == tm and n_cols == w.shape[1] and col_block == 0)
    w_spec = pl.BlockSpec((k, tn), lambda i, j: (0, col_block + j))
    in_specs = [pl.BlockSpec((tm, k), lambda i, j: (i, 0)), w_spec]
    out_spec = pl.BlockSpec((tm, tn), lambda i, j: (i, j))
    args = [a, w]
    if residual is not None:
        in_specs.append(out_spec)
        args.append(residual)
    out_specs = [out_spec]
    out_shape = [jax.ShapeDtypeStruct((rows, n_cols), out_dtype)]
    if emit_w:
        out_specs.append(w_spec)
        out_shape.append(jax.ShapeDtypeStruct(w.shape, BF16))
    out = pl.pallas_call(
        functools.partial(_mm_body, has_residual=residual is not None, emit_w=emit_w),
        grid=(rows // tm, n_cols // tn),
        in_specs=in_specs,
        out_specs=out_specs,
        out_shape=out_shape,
        compiler_params=_params("parallel", "arbitrary"),
        name=name,
    )(*args)
    return out if emit_w else out[0]


def _t5_bucket(n):
    n = jnp.maximum(n, 0)
    max_exact = T5_BUCKETS // 2
    nf = jnp.maximum(n, 1).astype(F32)
    large = max_exact + (jnp.log(nf / max_exact) / math.log(T5_MAX_DIST / max_exact)
                         * (T5_BUCKETS - max_exact)).astype(I32)
    return jnp.where(n < max_exact, n, jnp.minimum(large, T5_BUCKETS - 1))


PAGES_PER_BLOCK = MOBA_BLOCK // PAGE_SIZE
SUBLANES = 8


def _block_gates(q, page_refs, first_block, gates):
    lane = lax.broadcasted_iota(I32, gates.shape, 1)
    for i in range(len(page_refs) // PAGES_PER_BLOCK):
        pages = [page_refs[i * PAGES_PER_BLOCK + t][...] for t in range(PAGES_PER_BLOCK)]
        ksum = functools.reduce(jnp.add, [jnp.sum(x, axis=0) for x in pages])
        gate = jnp.sum(q * ksum, axis=1, keepdims=True) / MOBA_BLOCK
        gates = jnp.where(lane == first_block + i, gate, gates)
    return gates


def _moba_prompt_body(*refs, nb, side_blocks):
    refs = list(refs)
    if side_blocks:
        refs.pop(0)
    t5_ref, q_ref, k_ref, v_ref = refs[:4]
    refs = refs[4:]
    if side_blocks:
        qside_ref, page_refs = refs[0], refs[1:1 + side_blocks * PAGES_PER_BLOCK]
        refs = refs[1 + side_blocks * PAGES_PER_BLOCK:]
        o_ref, gates_ref, kb_sc, vb_sc, kdiff_sc, bias_sc = refs
    else:
        o_ref, kb_sc, vb_sc, kdiff_sc, bias_sc = refs
    blk = MOBA_BLOCK
    nbp = SUBLANES
    assert nb <= nbp
    h, b, c = pl.program_id(0), pl.program_id(1), pl.program_id(2)
    scale = A_DH ** -0.5
    ii = lax.broadcasted_iota(I32, (blk, blk), 0)
    jj = lax.broadcasted_iota(I32, (blk, blk), 1)

    @pl.when((b == 0) & (c == 0))
    def _():
        for t in range(2):
            bucket = _t5_bucket(t * blk + ii - jj)
            tile = jnp.zeros((blk, blk), F32)
            for bkt in range(T5_BUCKETS):
                tile = jnp.where(bucket == bkt, t5_ref[bkt, h], tile)
            bias_sc[t] = jnp.where(ii >= jj, tile, NEG_INF) if t == 0 else tile

    @pl.when(c == 0)
    def _():
        km = jnp.concatenate(
            [jnp.mean(k_ref[n * blk:(n + 1) * blk, :], axis=0, keepdims=True) for n in range(nb)]
            + [jnp.zeros((1, A_DH), F32)] * (nbp - nb), axis=0)
        for n in range(nb):
            kdiff_sc[n * nbp:(n + 1) * nbp, :] = km[n:n + 1, :] - km
            kb_sc[n * blk:(n + 1) * blk, A_DH:2 * A_DH] = jnp.where(
                lax.broadcasted_iota(I32, (blk, 128), 1) == n, 1.0, 0.0).astype(BF16)
        kb_sc[:, 0:A_DH] = k_ref[...].astype(BF16)
        vb_sc[...] = v_ref[...].astype(BF16)
        if side_blocks:
            gates_ref[...] = jnp.zeros_like(gates_ref)

    far_bias = t5_ref[T5_BUCKETS - 1, h]

    def attend(cc):
        q = q_ref[cc * blk:(cc + 1) * blk, :]
        qs = (q * scale).astype(BF16)
        n_keys = (cc + 1) * blk
        if cc <= MOBA_TOPK:
            s_all = lax.dot_general(qs, kb_sc[0:n_keys, 0:A_DH], NT_DIMS, preferred_element_type=F32)
        else:
            diff_t = lax.dot_general(kdiff_sc[0:cc * nbp, :], q, NT_DIMS, precision=lax.Precision.HIGHEST,
                                     preferred_element_type=F32)
            pair = lax.broadcasted_iota(I32, diff_t.shape, 0)
            m_of, n_of = jnp.right_shift(pair, int(math.log2(nbp))), jnp.bitwise_and(pair, nbp - 1)
            beats_t = jnp.where(diff_t > 0, 1.0, jnp.where(diff_t == 0, jnp.where(n_of > m_of, 1.0, 0.0), 0.0))
            fold = jnp.where(jnp.bitwise_and(lax.broadcasted_iota(I32, (cc * nbp, 128), 0), nbp - 1)
                             == lax.broadcasted_iota(I32, (cc * nbp, 128), 1), 1.0, 0.0)
            rank = lax.dot_general(beats_t.astype(BF16), fold.astype(BF16), TN_DIMS,
                                   preferred_element_type=F32)
            lane = lax.broadcasted_iota(I32, rank.shape, 1)
            sel = jnp.where(lane < cc, jnp.where(rank < MOBA_TOPK, 0.0, NEG_INF), 0.0)
            q_sel = jnp.concatenate([qs, sel.astype(BF16)], axis=1)
            s_all = lax.dot_general(q_sel, kb_sc[0:n_keys, :], NT_DIMS, preferred_element_type=F32)
        tiles = []
        for n in range(cc + 1):
            t = s_all[:, n * blk:(n + 1) * blk]
            tiles.append(t + (bias_sc[0] if n == cc else bias_sc[1] if n == cc - 1 else far_bias))
        m = jnp.max(functools.reduce(jnp.maximum, tiles), axis=1, keepdims=True)
        ps = [jnp.exp(t - m) for t in tiles]
        l = jnp.sum(functools.reduce(jnp.add, ps), axis=1, keepdims=True)
        p_all = jnp.concatenate([p.astype(BF16) for p in ps], axis=1)
        o_ref[cc * blk:(cc + 1) * blk, :] = (_dot(p_all, vb_sc[0:n_keys, :]) / l).astype(o_ref.dtype)

    def step(s):
        if side_blocks:
            gates_ref[...] = _block_gates(qside_ref[...], page_refs, s * side_blocks, gates_ref[...])
        for cc in sorted({s, nb - 1 - s}):
            attend(cc)

    for s in range(_moba_steps(nb)):
        pl.when(c == s)(functools.partial(step, s))


def _moba_steps(nb):
    return (nb + 1) // 2


def _moba_prompt(z3, k, v, t5_table, batch, seq, paged=None):
    blk = MOBA_BLOCK
    nb = seq // blk
    steps = _moba_steps(nb)
    side_blocks = 0
    if paged is not None:
        page_table, q_side, cache_k = paged
        nb_past = page_table.shape[1] // PAGES_PER_BLOCK
        if page_table.shape[0] == A_HEADS * batch and nb_past % steps == 0:
            side_blocks = nb_past // steps

    seq_spec = pl.BlockSpec((seq, A_DH), lambda h, b, c, *_: (b, h))
    in_specs = [pl.BlockSpec(memory_space=pltpu.SMEM),
                pl.BlockSpec((None, seq, A_DH), lambda h, b, c, *_: (SEC_QA, b, h)), seq_spec, seq_spec]
    out_specs = [seq_spec]
    out_shape = [jax.ShapeDtypeStruct((batch * seq, A_HEADS * A_DH), BF16)]
    args = [t5_table, z3, k, v]
    if side_blocks:
        def page(t):
            return pl.BlockSpec((None, PAGE_SIZE, A_HEADS, A_DH),
                                lambda h, b, c, pt: (pt[h * batch + b, c * side_blocks * PAGES_PER_BLOCK + t], 0, 0, 0))

        pages = [page(t) for t in range(side_blocks * PAGES_PER_BLOCK)]
        in_specs += [pl.BlockSpec((None, None, A_HEADS, A_DH), lambda h, b, c, pt: (SEC_QA, h * batch + b, 0, 0))]
        in_specs += pages
        out_specs.append(pl.BlockSpec((None, A_HEADS, 128), lambda h, b, c, pt: (h * batch + b, 0, 0)))
        out_shape.append(jax.ShapeDtypeStruct((A_HEADS * batch, A_HEADS, 128), F32))
        args = [page_table] + args + [q_side] + [cache_k] * len(pages)
    out = pl.pallas_call(
        functools.partial(_moba_prompt_body, nb=nb, side_blocks=side_blocks),
        grid_spec=pltpu.PrefetchScalarGridSpec(
            num_scalar_prefetch=1 if side_blocks else 0,
            grid=(A_HEADS, batch, steps),
            in_specs=in_specs,
            out_specs=out_specs,
            scratch_shapes=[pltpu.VMEM((seq, 2 * A_DH), BF16), pltpu.VMEM((seq, A_DH), BF16),
                            pltpu.VMEM((SUBLANES * SUBLANES, A_DH), F32), pltpu.VMEM((2, blk, blk), F32)],
        ),
        out_shape=out_shape,
        compiler_params=_params("arbitrary", "arbitrary", "arbitrary"),
        name="moba_prompt",
    )(*args)
    if paged is None:
        return out[0]
    return out[0], (out[1] if side_blocks else None)


def _rope_body(inv_ref, cos_ref, sin_ref, *, pos0):
    pos = pos0 + lax.broadcasted_iota(I32, cos_ref.shape, 0)
    ang = pos.astype(F32) * inv_ref[...]
    cos_ref[...] = jnp.cos(ang)
    sin_ref[...] = jnp.sin(ang)


def _rope_tables(pos0, n_pos):
    half = R_DK // 2
    inv = 1.0 / (RET_ROPE_BASE ** jnp.linspace(0.0, 1.0, half, dtype=F32))
    rows = -(-n_pos // 8) * 8
    return pl.pallas_call(
        functools.partial(_rope_body, pos0=pos0),
        out_shape=[jax.ShapeDtypeStruct((rows, half), F32)] * 2,
        name="rope_tables",
    )(inv.reshape(1, half))


def _rotate(x, cos, sin):
    half = x.shape[-1] // 2
    x1, x2 = x[:, :half], x[:, half:]
    return jnp.concatenate([x1 * cos - x2 * sin, x1 * sin + x2 * cos], axis=1)


def _log_decay(h, shape):
    hf = jnp.full(shape, h, I32).astype(F32)
    return jnp.log(1.0 - jnp.exp2(-5.0 - hf))


def _ret_prompt_body(q_ref, k_ref, v_ref, g_ref, cos_ref, sin_ref, o_ref, so_ref, s_sc, *, c):
    n = pl.program_id(1)

    @pl.when(n == 0)
    def _():
        s_sc[...] = jnp.zeros_like(s_sc)

    i = lax.broadcasted_iota(I32, (c, c), 0).astype(F32)
    j = lax.broadcasted_iota(I32, (c, c), 1).astype(F32)
    diff = i - j
    i_col = lax.broadcasted_iota(I32, (c, 1), 0).astype(F32)
    cos, sin = cos_ref[...], sin_ref[...]

    for h in range(R_HEADS):
        dk, dv = slice(h * R_DK, (h + 1) * R_DK), slice(h * R_DV, (h + 1) * R_DV)
        dmask = jnp.where(diff >= 0, jnp.exp(jnp.maximum(diff, 0.0) * _log_decay(h, (c, c))), 0.0)
        lg_col = _log_decay(h, (c, 1))
        q_dec = jnp.exp((i_col + 1.0) * lg_col)
        k_dec = jnp.exp((c - 1.0 - i_col) * lg_col)
        c_dec = jnp.exp(c * _log_decay(h, (1, R_DV)))
        qr = _rotate(q_ref[:, dk], cos, sin)
        kr = _rotate(k_ref[:, dk], cos, sin) * (R_DK ** -0.5)
        vb = v_ref[:, dv].astype(BF16)
        s = s_sc[h]
        att = lax.dot_general(qr.astype(BF16), kr.astype(BF16), NT_DIMS, preferred_element_type=F32) * dmask
        o = _dot(att.astype(BF16), vb) + _dot((qr * q_dec).astype(BF16), s.astype(BF16))
        s_sc[h] = s * c_dec + lax.dot_general((kr * k_dec).astype(BF16), vb, TN_DIMS, preferred_element_type=F32)
        on = o * lax.rsqrt(jnp.mean(o * o, axis=-1, keepdims=True) + EPS)
        o_ref[:, dv] = (on * _silu(g_ref[:, dv])).astype(o_ref.dtype)

    @pl.when(n == pl.num_programs(1) - 1)
    def _():
        so_ref[...] = s_sc[...]


def _ret_prompt(z3, cos, sin, batch, seq):
    c = math.gcd(seq, RET_CHUNK)
    nc = seq // c

    def sec(s):
        return pl.BlockSpec((None, c, SEC), lambda b, n: (s, b * nc + n, 0))

    tab = pl.BlockSpec((c, R_DK // 2), lambda b, n: (n, 0))
    return pl.pallas_call(
        functools.partial(_ret_prompt_body, c=c),
        grid=(batch, nc),
        in_specs=[sec(SEC_QR), sec(SEC_KR), sec(SEC_VR), sec(SEC_GR), tab, tab],
        out_specs=[pl.BlockSpec((c, R_HEADS * R_DV), lambda b, n: (b * nc + n, 0)),
                   pl.BlockSpec((None, R_HEADS, R_DK, R_DV), lambda b, n: (b, 0, 0, 0))],
        out_shape=[jax.ShapeDtypeStruct((batch * seq, R_HEADS * R_DV), BF16),
                   jax.ShapeDtypeStruct((batch, R_HEADS, R_DK, R_DV), F32)],
        scratch_shapes=[pltpu.VMEM((R_HEADS, R_DK, R_DV), F32)],
        compiler_params=_params("parallel", "arbitrary"),
        name="retention_prompt",
    )(z3, z3, z3, z3, cos, sin)


def _ret_step_body(q_ref, k_ref, v_ref, g_ref, cos_ref, sin_ref, s_ref, o_ref, so_ref):
    cos, sin = cos_ref[0:1, :], sin_ref[0:1, :]
    row0 = lax.broadcasted_iota(I32, (8, R_DK), 0) == 0
    hi = lax.Precision.HIGHEST
    for h in range(R_HEADS):
        decay = jnp.exp(_log_decay(h, (1, R_DV)))
        qr = _rotate(q_ref[h:h + 1, :], cos, sin)
        kr = _rotate(k_ref[h:h + 1, :], cos, sin) * (R_DK ** -0.5)
        v = v_ref[h:h + 1, :]
        s = s_ref[h]
        q8 = jnp.broadcast_to(qr * decay, (8, R_DK))
        qs = lax.dot_general(q8, s, (((1,), (0,)), ((), ())), precision=hi, preferred_element_type=F32)[0:1, :]
        o = jnp.sum(qr * kr, axis=-1, keepdims=True) * v + qs
        k8 = jnp.where(row0, jnp.broadcast_to(kr, (8, R_DK)), 0.0)
        v8 = jnp.broadcast_to(v, (8, R_DV))
        so_ref[h] = s * decay + lax.dot_general(k8, v8, TN_DIMS, precision=hi, preferred_element_type=F32)
        on = o * lax.rsqrt(jnp.mean(o * o, axis=-1, keepdims=True) + EPS)
        o_ref[h:h + 1, :] = (on * _silu(g_ref[h:h + 1, :])).astype(o_ref.dtype)


def _ret_step(z3, cos, sin, state):
    batch = z3.shape[1]
    z4 = z3.reshape(z3.shape[0], batch, R_HEADS, R_DK)

    def sec(s):
        return pl.BlockSpec((None, None, R_HEADS, R_DK), lambda b: (s, b, 0, 0))

    tab = pl.BlockSpec((8, R_DK // 2), lambda b: (0, 0))
    st = pl.BlockSpec((None, R_HEADS, R_DK, R_DV), lambda b: (b, 0, 0, 0))
    return pl.pallas_call(
        _ret_step_body,
        grid=(batch,),
        in_specs=[sec(SEC_QR), sec(SEC_KR), sec(SEC_VR), sec(SEC_GR), tab, tab, st],
        out_specs=[pl.BlockSpec((None, R_HEADS, R_DV), lambda b: (b, 0, 0)), st],
        out_shape=[jax.ShapeDtypeStruct((batch, R_HEADS, R_DV), F32),
                   jax.ShapeDtypeStruct(state.shape, F32)],
        compiler_params=_params("parallel"),
        name="retention_step",
    )(z4, z4, z4, z4, cos, sin, state)


def _mem_body(q_ref, k_ref, v_ref, o_ref):
    for h in range(M_HEADS):
        dh = slice(h * M_DH, (h + 1) * M_DH)
        s = lax.dot_general(q_ref[:, dh].astype(BF16), k_ref[:, dh].astype(BF16), NT_DIMS,
                            preferred_element_type=F32) * (M_DH ** -0.5)
        m = jnp.max(s, axis=-1, keepdims=True)
        p = jnp.exp(s - m)
        l = jnp.sum(p, axis=-1, keepdims=True)
        o_ref[:, dh] = (_dot(p.astype(BF16), v_ref[:, dh].astype(BF16)) / l).astype(o_ref.dtype)


def _mem_attend(z3, mk, mv, batch, seq):
    n_mem = mk.shape[0] // batch
    width = M_HEADS * M_DH
    ts = _tile(seq, 512)
    nt = seq // ts
    kv = pl.BlockSpec((n_mem, width), lambda b, t: (b, 0))
    return pl.pallas_call(
        _mem_body,
        grid=(batch, nt),
        in_specs=[pl.BlockSpec((None, ts, width), lambda b, t: (SEC_QM, b * nt + t, 0)), kv, kv],
        out_specs=pl.BlockSpec((ts, width), lambda b, t: (b * nt + t, 0)),
        out_shape=jax.ShapeDtypeStruct((batch * seq, width), BF16),
        compiler_params=_params("parallel", "arbitrary"),
        name="mem_attend",
    )(z3, mk, mv)


def _merge_body(h_ref, *refs, emit_w):
    x_refs, wg_refs, bg_refs, wb_refs = refs[0:3], refs[3:6], refs[6:9], refs[9:12]
    o_ref = refs[12]
    wgo_refs, wbo_refs = (refs[13:16], refs[16:19]) if emit_w else ((), ())
    h = h_ref[...]
    merged = None
    for g in range(3):
        wg, wb = _bf16(wg_refs[g][...]), _bf16(wb_refs[g][...])
        if emit_w:
            wgo_refs[g][...] = wg
            wbo_refs[g][...] = wb
        term = jax.nn.sigmoid(_dot(h, wg) + bg_refs[g][...]) * _dot(x_refs[g][...], wb)
        merged = term if merged is None else merged + term
    o_ref[...] = merged.astype(o_ref.dtype)


def _merge(h, branches, gate_ws, b_gate, branch_ws, *, emit_w=False):
    rows, d = h.shape
    tm, tn = _tile(rows, 1024), _tile(d, 256 if emit_w else 512)
    assert not emit_w or rows == tm
    nj = d // tn
    b_gate = b_gate.reshape(1, 3 * d)

    def row(width):
        return pl.BlockSpec((tm, width), lambda i, j: (i, 0))

    def cols(n_rows, first_col=0):
        return pl.BlockSpec((n_rows, tn), lambda i, j: (0, first_col // tn + j))

    out_specs = [pl.BlockSpec((tm, tn), lambda i, j: (i, j))]
    out_shape = [jax.ShapeDtypeStruct((rows, d), BF16)]
    if emit_w:
        out_specs += [cols(d)] * 3 + [cols(w.shape[0]) for w in branch_ws]
        out_shape += [jax.ShapeDtypeStruct((d, d), BF16)] * 3
        out_shape += [jax.ShapeDtypeStruct(w.shape, BF16) for w in branch_ws]
    out = pl.pallas_call(
        functools.partial(_merge_body, emit_w=emit_w),
        grid=(rows // tm, nj),
        in_specs=([row(d)] + [row(x.shape[1]) for x in branches]
                  + [cols(d, first) for _, first in gate_ws]
                  + [cols(1, g * d) for g in range(3)]
                  + [cols(w.shape[0]) for w in branch_ws]),
        out_specs=out_specs,
        out_shape=out_shape,
        compiler_params=_params("parallel", "arbitrary"),
        name="gated_merge",
    )(h, *branches, *[w for w, _ in gate_ws], b_gate, b_gate, b_gate, *branch_ws)
    if emit_w:
        return out[0], [(w, 0) for w in out[1:4]], out[4:7]
    return out[0]


def _moba_gate_body(pt_ref, q_ref, *refs, bps):
    del pt_ref
    k_refs, gates_ref = refs[:-1], refs[-1]
    n = pl.program_id(1)

    @pl.when(n == 0)
    def _():
        gates_ref[...] = jnp.zeros_like(gates_ref)

    gates_ref[...] = _block_gates(q_ref[...], k_refs, n * bps, gates_ref[...])


def _moba_select_body(g_ref, sel_ref, *, nb_past):
    lane = lax.broadcasted_iota(I32, g_ref.shape, 1)
    gw = jnp.where(lane < nb_past, g_ref[...], NEG_INF)
    sel = jnp.zeros(gw.shape, I32)
    for r in range(MOBA_TOPK):
        mx = jnp.max(gw, axis=1, keepdims=True)
        idx = jnp.min(jnp.where(gw == mx, lane, 128), axis=1, keepdims=True)
        sel = jnp.where(lane == r, idx, sel)
        gw = jnp.where(lane == idx, -jnp.inf, gw)
    sel_ref[...] = sel


def _moba_attend_body(pt_ref, sel_ref, t5_ref, q_ref, kn_ref, vn_ref, ck_ref, cv_ref, o_ref, kbuf, vbuf, sem,
                      *, past_len):
    b = pl.program_id(0)
    slot = lax.rem(b, 2)
    scale = A_DH ** -0.5
    n_tiles = MOBA_TOPK * PAGES_PER_BLOCK

    def tile_copies(bb, to_slot):
        copies = []
        for h in range(A_HEADS):
            for r in range(MOBA_TOPK):
                block = sel_ref[bb, h * MOBA_TOPK + r]
                for t in range(PAGES_PER_BLOCK):
                    page = pt_ref[bb, block * PAGES_PER_BLOCK + t]
                    j = r * PAGES_PER_BLOCK + t
                    copies.append(pltpu.make_async_copy(ck_ref.at[page, :, h, :], kbuf.at[to_slot, h, j],
                                                        sem.at[to_slot]))
                    copies.append(pltpu.make_async_copy(cv_ref.at[page, :, h, :], vbuf.at[to_slot, h, j],
                                                        sem.at[to_slot]))
        return copies

    @pl.when(b == 0)
    def _():
        for cp in tile_copies(0, 0):
            cp.start()

    @pl.when(b + 1 < pl.num_programs(0))
    def _():
        for cp in tile_copies(b + 1, 1 - slot):
            cp.start()

    for cp in tile_copies(b, slot):
        cp.wait()

    q, kn, vn = q_ref[...], kn_ref[...], vn_ref[...]
    lane = lax.broadcasted_iota(I32, (1, PAGE_SIZE), 1)
    for h in range(A_HEADS):
        qh = q[h:h + 1, :]
        q8 = jnp.broadcast_to(qh, (8, A_DH)).astype(BF16)
        s = []
        for r in range(MOBA_TOPK):
            block = sel_ref[b, h * MOBA_TOPK + r]
            for t in range(PAGES_PER_BLOCK):
                kt = kbuf[slot, h, r * PAGES_PER_BLOCK + t].astype(BF16)
                raw = lax.dot_general(q8, kt, NT_DIMS, preferred_element_type=F32)[0:1, :]
                bucket = _t5_bucket(past_len - (block * MOBA_BLOCK + t * PAGE_SIZE + lane))
                bias = jnp.zeros((1, PAGE_SIZE), F32)
                for bkt in range(T5_BUCKETS):
                    bias = jnp.where(bucket == bkt, t5_ref[bkt, h], bias)
                s.append(raw * scale + bias)
        s_own = jnp.sum(qh * kn[h:h + 1, :], axis=1, keepdims=True) * scale + t5_ref[0, h]
        m = jnp.maximum(jnp.max(functools.reduce(jnp.maximum, s), axis=1, keepdims=True), s_own)
        p = [jnp.exp(x - m) for x in s]
        p_own = jnp.exp(s_own - m)
        l = jnp.sum(functools.reduce(jnp.add, p), axis=1, keepdims=True) + p_own
        acc = p_own * vn[h:h + 1, :]
        for j in range(n_tiles):
            p8 = jnp.broadcast_to(p[j], (8, PAGE_SIZE)).astype(BF16)
            acc = acc + _dot(p8, vbuf[slot, h, j].astype(BF16))[0:1, :]
        o_ref[h:h + 1, :] = acc / l


def _moba_gates(q4, cache_k, page_table):
    batch, n_pages = page_table.shape
    nb_past = n_pages // PAGES_PER_BLOCK
    bps = _tile(nb_past, 8)

    def page(t):
        return pl.BlockSpec((None, PAGE_SIZE, A_HEADS, A_DH),
                            lambda b, n, pt: (pt[b, n * bps * PAGES_PER_BLOCK + t], 0, 0, 0))

    pages = [page(t) for t in range(bps * PAGES_PER_BLOCK)]
    return pl.pallas_call(
        functools.partial(_moba_gate_body, bps=bps),
        grid_spec=pltpu.PrefetchScalarGridSpec(
            num_scalar_prefetch=1,
            grid=(batch, nb_past // bps),
            in_specs=[pl.BlockSpec((None, None, A_HEADS, A_DH), lambda b, n, pt: (SEC_QA, b, 0, 0))] + pages,
            out_specs=pl.BlockSpec((None, A_HEADS, 128), lambda b, n, pt: (b, 0, 0)),
        ),
        out_shape=jax.ShapeDtypeStruct((batch, A_HEADS, 128), F32),
        compiler_params=_params("parallel", "arbitrary"),
        name="moba_gate",
    )(page_table, q4, *([cache_k] * len(pages)))


def _moba_step(z4, k_new, v_new, cache_k, cache_v, page_table, t5_table, gates=None):
    batch, n_pages = page_table.shape
    past_len = n_pages * PAGE_SIZE
    assert past_len % MOBA_BLOCK == 0 and MOBA_BLOCK % PAGE_SIZE == 0
    nb_past = past_len // MOBA_BLOCK
    assert MOBA_TOPK <= nb_past <= 128
    k_new = k_new.reshape(batch, A_HEADS, A_DH)
    v_new = v_new.reshape(batch, A_HEADS, A_DH)
    if gates is None:
        gates = _moba_gates(z4, cache_k, page_table)
    sel = pl.pallas_call(
        functools.partial(_moba_select_body, nb_past=nb_past),
        out_shape=jax.ShapeDtypeStruct((batch * A_HEADS, 128), I32),
        name="moba_select",
    )(gates.reshape(batch * A_HEADS, 128))
    sel = sel[:, :MOBA_TOPK].reshape(batch, A_HEADS * MOBA_TOPK)

    new_tok = pl.BlockSpec((None, A_HEADS, A_DH), lambda b, pt, sl: (b, 0, 0))
    tiles = pltpu.VMEM((2, A_HEADS, MOBA_TOPK * PAGES_PER_BLOCK, PAGE_SIZE, A_DH), F32)
    return pl.pallas_call(
        functools.partial(_moba_attend_body, past_len=past_len),
        grid_spec=pltpu.PrefetchScalarGridSpec(
            num_scalar_prefetch=2,
            grid=(batch,),
            in_specs=[pl.BlockSpec(memory_space=pltpu.SMEM),
                      pl.BlockSpec((None, None, A_HEADS, A_DH), lambda b, pt, sl: (SEC_QA, b, 0, 0)),
                      new_tok, new_tok,
                      pl.BlockSpec(memory_space=pl.ANY), pl.BlockSpec(memory_space=pl.ANY)],
            out_specs=pl.BlockSpec((None, A_HEADS, A_DH), lambda b, pt, sl: (b, 0, 0)),
            scratch_shapes=[tiles, tiles, pltpu.SemaphoreType.DMA((2,))],
        ),
        out_shape=jax.ShapeDtypeStruct((batch, A_HEADS, A_DH), F32),
        compiler_params=_params("arbitrary"),
        name="moba_attend",
    )(page_table, sel, t5_table, z4, k_new, v_new, cache_k, cache_v)


def kernel(x_prompt, x_sample, mem_prompt, cache_k, cache_v, cache_mem_k, cache_mem_v, state_ret, page_table, t5_table, ffn1_norm, ffn1_w1, ffn1_w3, ffn1_w2, mix_norm, mem_norm, w_in, w_mem_kv, w_gate, b_gate, w_br_moba, w_br_ret, w_br_mem, w_out, ffn2_norm, ffn2_w1, ffn2_w3, ffn2_w2, final_norm):
    batch, seq, d = x_prompt.shape
    dec_batch, dec_seq, _ = x_sample.shape
    assert dec_seq == 1
    depth = w_in.shape[0]
    n_mem = mem_prompt.shape[1]
    past_len = page_table.shape[1] * PAGE_SIZE
    cos_p, sin_p = _rope_tables(0, seq)
    cos_s, sin_s = _rope_tables(past_len, dec_seq)

    xp = x_prompt.reshape(batch * seq, d)
    xs = x_sample.reshape(dec_batch, d)
    outs = [[] for _ in range(8)]
    for l in range(depth):
        last = l == depth - 1
        g_next = final_norm if last else ffn1_norm[l + 1]
        branch_ws = [w_br_moba[l], w_br_ret[l], w_br_mem[l]]
        gate_ws = [(w_gate[l], g * d) for g in range(3)]

        x1s, h2s, *f1_b = _ffn_half(xs, ffn1_norm[l], ffn1_w1[l], ffn1_w3[l], ffn1_w2[l], mix_norm[l],
                                    emit_x=True, post_dtype=BF16, emit_w=True)
        x1, h2 = _ffn_half(xp, ffn1_norm[l], *f1_b, mix_norm[l], emit_x=True, post_dtype=BF16)
        z3s, ka_s, va_s, w_in_b = _in_proj(h2s, w_in[l], emit_w=True)
        z3, ka, va = _in_proj(h2, w_in_b)

        z4s = z3s.reshape(z3s.shape[0], dec_batch, A_HEADS, A_DH)
        oa, gates_s = _moba_prompt(z3, ka, va, t5_table, batch, seq, paged=(page_table, z4s, cache_k[l]))
        oa_s = _moba_step(z4s, ka_s, va_s, cache_k[l], cache_v[l], page_table, t5_table, gates=gates_s)

        o_rs, s_new_s = _ret_step(z3s, cos_s, sin_s, state_ret[l])
        z3s_pad = jnp.broadcast_to(z3s[:, :, None, :], (z3s.shape[0], dec_batch, 8, SEC))
        z3s_pad = z3s_pad.reshape(z3s.shape[0], dec_batch * 8, SEC)
        om_s = _mem_attend(z3s_pad, cache_mem_k[l].reshape(dec_batch * n_mem, M_HEADS * M_DH),
                           cache_mem_v[l].reshape(dec_batch * n_mem, M_HEADS * M_DH), dec_batch, 8)
        om_s = om_s.reshape(dec_batch, 8, M_HEADS * M_DH)[:, 0, :]
        branches_s = [oa_s.reshape(dec_batch, -1).astype(BF16), o_rs.reshape(dec_batch, -1).astype(BF16), om_s]

        mem_h = _norm_rows(mem_prompt.reshape(batch * n_mem, d), mem_norm[l], BF16)
        mk_p = _matmul(mem_h, w_mem_kv[l], col_block=0, n_cols=M_HEADS * M_DH, name="mem_k")
        mv_p = _matmul(mem_h, w_mem_kv[l], col_block=1, n_cols=M_HEADS * M_DH, name="mem_v")
        o_r, s_new_p = _ret_prompt(z3, cos_p, sin_p, batch, seq)
        om = _mem_attend(z3, mk_p, mv_p, batch, seq)

        merged_s, gate_b, branch_b = _merge(h2s, branches_s, gate_ws, b_gate[l], branch_ws, emit_w=True)
        merged = _merge(h2, [oa, o_r, om], gate_b, b_gate[l], branch_b)
        x2s, w_out_b = _matmul(merged_s, w_out[l], residual=x1s, emit_w=True, name="out_proj")
        x2 = _matmul(merged, w_out_b, residual=x1, name="out_proj")
        xs, *rest = _ffn_half(x2s, ffn2_norm[l], ffn2_w1[l], ffn2_w3[l], ffn2_w2[l], g_next,
                              emit_x=not last, post_dtype=F32, emit_w=True)
        f2_b = rest[-3:]
        xp = _ffn_half(x2, ffn2_norm[l], *f2_b, g_next, emit_x=not last, post_dtype=F32)[0]

        new = (ka.reshape(batch, seq, A_HEADS, A_DH), va.reshape(batch, seq, A_HEADS, A_DH),
               mk_p.reshape(batch, n_mem, M_HEADS, M_DH), mv_p.reshape(batch, n_mem, M_HEADS, M_DH), s_new_p,
               ka_s.reshape(dec_batch, dec_seq, A_HEADS, A_DH), va_s.reshape(dec_batch, dec_seq, A_HEADS, A_DH),
               s_new_s)
        for acc, val in zip(outs, new):
            acc.append(val)

    y_prompt = xp.reshape(batch, seq, d)
    y_sample = xs.reshape(dec_batch, dec_seq, d)
    return (y_prompt, y_sample) + tuple(jnp.stack(o) for o in outs)
```

```python
import functools
import math

import jax
import jax.numpy as jnp
import numpy as np
from jax import lax
from jax.experimental import pallas as pl
from jax.experimental.pallas import tpu as pltpu

F32 = jnp.float32
BF16 = jnp.bfloat16
I32 = jnp.int32

A_HEADS, A_DH = 8, 128
MOBA_BLOCK, MOBA_TOPK = 256, 3
T5_BUCKETS, T5_MAX_DIST = 32, 128
R_HEADS, R_DK, R_DV = 4, 256, 256
RET_CHUNK = 128
RET_ROPE_BASE = 10000.0
M_HEADS, M_DH = 4, 256
PAGE_SIZE = 128
EPS = 1e-6
NEG_INF = -1e30
LOG2E = math.log2(math.e)
SEC = 1024
W_IN_KA, W_IN_VA = 1, 2
(SEC_QA, SEC_QR, SEC_KR, SEC_VR, SEC_GR, SEC_QM) = range(6)

V7X_VMEM_LIMIT_BYTES = 56 * 1024 * 1024

NT_DIMS = (((1,), (1,)), ((), ()))
TN_DIMS = (((0,), (0,)), ((), ()))


def _params(*sem):
    return pltpu.CompilerParams(dimension_semantics=sem, vmem_limit_bytes=V7X_VMEM_LIMIT_BYTES)


def _tile(n, pref):
    t = min(n, pref)
    while n % t:
        t -= 1
    return t


def _rms(x, g):
    return x * lax.rsqrt(jnp.mean(x * x, axis=-1, keepdims=True) + EPS) * g


def _silu(x):
    return x * jax.nn.sigmoid(x)


def _dot(a, b):
    return jnp.dot(a, b, preferred_element_type=F32)


def _bf16(w):
    return w if w.dtype == BF16 else w.astype(BF16)


def _ffn_body(x_ref, g_ref, w1_ref, w3_ref, w2_ref, gp_ref, *refs, emit_x, emit_w, n_cast):
    refs = list(refs)
    cast_in = [refs.pop(0) for _ in range(n_cast)]
    xo_ref = refs.pop(0) if emit_x else None
    ho_ref = refs.pop(0)
    wb_refs = [refs.pop(0) for _ in range(3)] if emit_w else []
    cast_out = [refs.pop(0) for _ in range(n_cast)]
    h_sc, acc_sc = refs
    f = pl.program_id(1)

    @pl.when(f == 0)
    def _():
        h_sc[...] = _rms(x_ref[...], g_ref[...]).astype(BF16)
        acc_sc[...] = jnp.zeros_like(acc_sc)

    for src_ref, dst_ref in zip(cast_in, cast_out):
        dst_ref[...] = src_ref[...].astype(BF16)

    w1, w3, w2 = _bf16(w1_ref[...]), _bf16(w3_ref[...]), _bf16(w2_ref[...])
    for wb_ref, w in zip(wb_refs, (w1, w3, w2)):
        wb_ref[...] = w
    h = h_sc[...]
    t = _silu(_dot(h, w1)) * _dot(h, w3)
    acc_sc[...] += _dot(t.astype(BF16), w2)

    @pl.when(f == pl.num_programs(1) - 1)
    def _():
        xo = x_ref[...] + 0.5 * acc_sc[...]
        if emit_x:
            xo_ref[...] = xo
        ho_ref[...] = _rms(xo, gp_ref[...]).astype(ho_ref.dtype)


BF16_SUBLANES = 16


def _ffn_half(x, g, w1, w3, w2, g_post, *, emit_x, post_dtype, emit_w=False, cast_jobs=()):
    rows, d = x.shape
    ff = w1.shape[1]
    tm, tf = _tile(rows, 512), _tile(ff, 512)
    assert not emit_w or rows == tm
    n_f = ff // tf
    n_steps = (rows // tm) * n_f
    row_spec = pl.BlockSpec((tm, d), lambda i, f: (i, 0))
    vec_spec = pl.BlockSpec((1, d), lambda i, f: (0, 0))
    w13_spec = pl.BlockSpec((d, tf), lambda i, f: (0, f))
    w2_spec = pl.BlockSpec((tf, d), lambda i, f: (f, 0))
    out_shape = [jax.ShapeDtypeStruct((rows, d), post_dtype)]
    out_specs = [row_spec]
    if emit_x:
        out_shape = [jax.ShapeDtypeStruct((rows, d), F32)] + out_shape
        out_specs = [row_spec] + out_specs
    if emit_w:
        out_shape += [jax.ShapeDtypeStruct(w.shape, BF16) for w in (w1, w3, w2)]
        out_specs += [w13_spec, w13_spec, w2_spec]

    def cast_spec(a):
        n_chunks = max(n for n in range(1, n_steps + 1)
                       if a.shape[0] % n == 0 and (a.shape[0] // n) % BF16_SUBLANES == 0)
        return pl.BlockSpec((a.shape[0] // n_chunks, a.shape[1]),
                            lambda i, f: (jnp.minimum(i * n_f + f, n_chunks - 1), 0))

    cast_specs = [cast_spec(a) for a in cast_jobs]
    out_shape += [jax.ShapeDtypeStruct(a.shape, BF16) for a in cast_jobs]
    out_specs += cast_specs
    return pl.pallas_call(
        functools.partial(_ffn_body, emit_x=emit_x, emit_w=emit_w, n_cast=len(cast_jobs)),
        grid=(rows // tm, n_f),
        in_specs=[row_spec, vec_spec, w13_spec, w13_spec, w2_spec, vec_spec] + cast_specs,
        out_specs=out_specs,
        out_shape=out_shape,
        scratch_shapes=[pltpu.VMEM((tm, d), BF16), pltpu.VMEM((tm, d), F32)],
        compiler_params=_params("parallel", "arbitrary"),
        name="ffn_half",
    )(x, g.reshape(1, d), w1, w3, w2, g_post.reshape(1, d), *cast_jobs)


def _norm_body(x_ref, g_ref, o_ref):
    o_ref[...] = _rms(x_ref[...], g_ref[...]).astype(o_ref.dtype)


def _norm_rows(x, g, out_dtype):
    rows, d = x.shape
    tm = _tile(rows, 512)
    return pl.pallas_call(
        _norm_body,
        grid=(rows // tm,),
        in_specs=[pl.BlockSpec((tm, d), lambda i: (i, 0)), pl.BlockSpec((1, d), lambda i: (0, 0))],
        out_specs=pl.BlockSpec((tm, d), lambda i: (i, 0)),
        out_shape=jax.ShapeDtypeStruct((rows, d), out_dtype),
        compiler_params=_params("parallel"),
        name="rmsnorm_rows",
    )(x, g.reshape(1, d))


def _mm_body(a_ref, w_ref, *refs, has_residual):
    o_ref = refs[-1]
    acc = _dot(a_ref[...], _bf16(w_ref[...]))
    if has_residual:
        acc = refs[0][...] + acc
    o_ref[...] = acc.astype(o_ref.dtype)


def _in_proj_body(a_ref, w_ref, z_ref, k_ref, v_ref):
    j = pl.program_id(1)
    acc = _dot(a_ref[...], w_ref[...])

    @pl.when(j == W_IN_KA)
    def _():
        k_ref[...] = acc

    @pl.when(j == W_IN_VA)
    def _():
        v_ref[...] = acc

    @pl.when((j != W_IN_KA) & (j != W_IN_VA))
    def _():
        z_ref[...] = acc


def _in_proj(a, w):
    rows, k = a.shape
    n_sec = w.shape[1] // SEC
    assert (W_IN_KA, W_IN_VA) == (1, 2) and n_sec == 8
    tm = _tile(rows, 1024)
    kv_spec = pl.BlockSpec((tm, SEC), lambda i, j: (i, 0))
    return pl.pallas_call(
        _in_proj_body,
        grid=(rows // tm, n_sec),
        in_specs=[pl.BlockSpec((tm, k), lambda i, j: (i, 0)), pl.BlockSpec((k, SEC), lambda i, j: (0, j))],
        out_specs=[pl.BlockSpec((None, tm, SEC), lambda i, j: (jnp.maximum(j - 2, 0), i, 0)), kv_spec, kv_spec],
        out_shape=[jax.ShapeDtypeStruct((n_sec - 2, rows, SEC), F32)] + [jax.ShapeDtypeStruct((rows, SEC), F32)] * 2,
        compiler_params=_params("parallel", "arbitrary"),
        name="in_proj",
    )(a, w)


def _matmul(a, w, *, residual=None, out_dtype=F32, tn_pref=1024, name="matmul"):
    rows, k = a.shape
    n_cols = w.shape[1]
    tm, tn = _tile(rows, 1024), _tile(n_cols, tn_pref)
    in_specs = [pl.BlockSpec((tm, k), lambda i, j: (i, 0)), pl.BlockSpec((k, tn), lambda i, j: (0, j))]
    out_spec = pl.BlockSpec((tm, tn), lambda i, j: (i, j))
    args = [a, w]
    if residual is not None:
        in_specs.append(out_spec)
        args.append(residual)
    return pl.pallas_call(
        functools.partial(_mm_body, has_residual=residual is not None),
        grid=(rows // tm, n_cols // tn),
        in_specs=in_specs,
        out_specs=out_spec,
        out_shape=jax.ShapeDtypeStruct((rows, n_cols), out_dtype),
        compiler_params=_params("parallel", "arbitrary"),
        name=name,
    )(*args)


def _t5_bucket(n):
    n = jnp.maximum(n, 0)
    max_exact = T5_BUCKETS // 2
    nf = jnp.maximum(n, 1).astype(F32)
    large = max_exact + (jnp.log(nf / max_exact) / math.log(T5_MAX_DIST / max_exact)
                         * (T5_BUCKETS - max_exact)).astype(I32)
    return jnp.where(n < max_exact, n, jnp.minimum(large, T5_BUCKETS - 1))


PAGES_PER_BLOCK = MOBA_BLOCK // PAGE_SIZE
SUBLANES = 8


def _block_gates(q, page_refs, first_block, gates):
    lane = lax.broadcasted_iota(I32, gates.shape, 1)
    for i in range(len(page_refs) // PAGES_PER_BLOCK):
        pages = [page_refs[i * PAGES_PER_BLOCK + t][...] for t in range(PAGES_PER_BLOCK)]
        ksum = functools.reduce(jnp.add, [jnp.sum(x, axis=0) for x in pages])
        gate = jnp.sum(q * ksum, axis=1, keepdims=True) / MOBA_BLOCK
        gates = jnp.where(lane == first_block + i, gate, gates)
    return gates


def _moba_prompt_body(*refs, nb, side_blocks):
    refs = list(refs)
    if side_blocks:
        refs.pop(0)
    t5_ref, q_ref, k_ref, v_ref = refs[:4]
    refs = refs[4:]
    if side_blocks:
        qside_ref, page_refs = refs[0], refs[1:1 + side_blocks * PAGES_PER_BLOCK]
        refs = refs[1 + side_blocks * PAGES_PER_BLOCK:]
        o_ref, gates_ref, kb_sc, vb_sc, kdiff_sc, bias_sc = refs
    else:
        o_ref, kb_sc, vb_sc, kdiff_sc, bias_sc = refs
    blk = MOBA_BLOCK
    nbp = SUBLANES
    assert nb <= nbp
    h, b, c = pl.program_id(0), pl.program_id(1), pl.program_id(2)
    scale = A_DH ** -0.5
    ii = lax.broadcasted_iota(I32, (blk, blk), 0)
    jj = lax.broadcasted_iota(I32, (blk, blk), 1)

    @pl.when((b == 0) & (c == 0))
    def _():
        for t in range(2):
            bucket = _t5_bucket(t * blk + ii - jj)
            tile = jnp.zeros((blk, blk), F32)
            for bkt in range(T5_BUCKETS):
                tile = jnp.where(bucket == bkt, t5_ref[bkt, h], tile)
            tile = tile * LOG2E
            bias_sc[t] = jnp.where(ii >= jj, tile, NEG_INF) if t == 0 else tile

    @pl.when(c == 0)
    def _():
        km = jnp.concatenate(
            [jnp.mean(k_ref[n * blk:(n + 1) * blk, :], axis=0, keepdims=True) for n in range(nb)]
            + [jnp.zeros((1, A_DH), F32)] * (nbp - nb), axis=0)
        for n in range(nb):
            kdiff_sc[n * nbp:(n + 1) * nbp, :] = km[n:n + 1, :] - km
            kb_sc[n * blk:(n + 1) * blk, A_DH:2 * A_DH] = jnp.where(
                lax.broadcasted_iota(I32, (blk, 128), 1) == n, 1.0, 0.0).astype(BF16)
        kb_sc[:, 0:A_DH] = k_ref[...].astype(BF16)
        vb_sc[:, 0:A_DH] = v_ref[...].astype(BF16)
        vb_sc[:, A_DH:2 * A_DH] = jnp.ones((nb * blk, A_DH), BF16)
        if side_blocks:
            gates_ref[...] = jnp.zeros_like(gates_ref)

    far_bias = t5_ref[T5_BUCKETS - 1, h] * LOG2E

    def attend(cc):
        q = q_ref[cc * blk:(cc + 1) * blk, :]
        qs = (q * (scale * LOG2E)).astype(BF16)
        n_keys = (cc + 1) * blk
        if cc <= MOBA_TOPK:
            s_all = lax.dot_general(qs, kb_sc[0:n_keys, 0:A_DH], NT_DIMS, preferred_element_type=F32)
        else:
            diff_t = lax.dot_general(kdiff_sc[0:cc * nbp, :], q, NT_DIMS, precision=lax.Precision.HIGHEST,
                                     preferred_element_type=F32)
            pair = lax.broadcasted_iota(I32, diff_t.shape, 0)
            m_of, n_of = jnp.right_shift(pair, int(math.log2(nbp))), jnp.bitwise_and(pair, nbp - 1)
            beats_t = jnp.where(diff_t > 0, 1.0, jnp.where(diff_t == 0, jnp.where(n_of > m_of, 1.0, 0.0), 0.0))
            fold = jnp.where(jnp.bitwise_and(lax.broadcasted_iota(I32, (cc * nbp, 128), 0), nbp - 1)
                             == lax.broadcasted_iota(I32, (cc * nbp, 128), 1), 1.0, 0.0)
            rank = lax.dot_general(beats_t.astype(BF16), fold.astype(BF16), TN_DIMS,
                                   preferred_element_type=F32)
            lane = lax.broadcasted_iota(I32, rank.shape, 1)
            sel = jnp.where(lane < cc, jnp.where(rank < MOBA_TOPK, 0.0, NEG_INF), 0.0)
            q_sel = jnp.concatenate([qs, sel.astype(BF16)], axis=1)
            s_all = lax.dot_general(q_sel, kb_sc[0:n_keys, :], NT_DIMS, preferred_element_type=F32)
        tiles = []
        for n in range(cc + 1):
            t = s_all[:, n * blk:(n + 1) * blk]
            tiles.append(t + (bias_sc[0] if n == cc else bias_sc[1] if n == cc - 1 else far_bias))
        m = jnp.max(functools.reduce(jnp.maximum, tiles), axis=1, keepdims=True)
        p_all = jnp.concatenate([jnp.exp2(t - m).astype(BF16) for t in tiles], axis=1)
        pv = _dot(p_all, vb_sc[0:n_keys, :])
        o_ref[cc * blk:(cc + 1) * blk, :] = (pv[:, 0:A_DH] / pv[:, A_DH:2 * A_DH]).astype(o_ref.dtype)

    def step(s):
        if side_blocks:
            gates_ref[...] = _block_gates(qside_ref[...], page_refs, s * side_blocks, gates_ref[...])
        for cc in sorted({s, nb - 1 - s}):
            attend(cc)

    for s in range(_moba_steps(nb)):
        pl.when(c == s)(functools.partial(step, s))


def _moba_steps(nb):
    return (nb + 1) // 2


def _moba_prompt(z3, k, v, t5_table, batch, seq, paged=None):
    blk = MOBA_BLOCK
    nb = seq // blk
    steps = _moba_steps(nb)
    side_blocks = 0
    if paged is not None:
        page_table, q_side, cache_k = paged
        nb_past = page_table.shape[1] // PAGES_PER_BLOCK
        if page_table.shape[0] == A_HEADS * batch and nb_past % steps == 0:
            side_blocks = nb_past // steps

    seq_spec = pl.BlockSpec((seq, A_DH), lambda h, b, c, *_: (b, h))
    in_specs = [pl.BlockSpec(memory_space=pltpu.SMEM),
                pl.BlockSpec((None, seq, A_DH), lambda h, b, c, *_: (SEC_QA, b, h)), seq_spec, seq_spec]
    out_specs = [seq_spec]
    out_shape = [jax.ShapeDtypeStruct((batch * seq, A_HEADS * A_DH), BF16)]
    args = [t5_table, z3, k, v]
    if side_blocks:
        def page(t):
            return pl.BlockSpec((None, PAGE_SIZE, A_HEADS, A_DH),
                                lambda h, b, c, pt: (pt[h * batch + b, c * side_blocks * PAGES_PER_BLOCK + t], 0, 0, 0))

        pages = [page(t) for t in range(side_blocks * PAGES_PER_BLOCK)]
        in_specs += [pl.BlockSpec((None, None, A_HEADS, A_DH), lambda h, b, c, pt: (SEC_QA, h * batch + b, 0, 0))]
        in_specs += pages
        out_specs.append(pl.BlockSpec((None, A_HEADS, 128), lambda h, b, c, pt: (h * batch + b, 0, 0)))
        out_shape.append(jax.ShapeDtypeStruct((A_HEADS * batch, A_HEADS, 128), F32))
        args = [page_table] + args + [q_side] + [cache_k] * len(pages)
    out = pl.pallas_call(
        functools.partial(_moba_prompt_body, nb=nb, side_blocks=side_blocks),
        grid_spec=pltpu.PrefetchScalarGridSpec(
            num_scalar_prefetch=1 if side_blocks else 0,
            grid=(A_HEADS, batch, steps),
            in_specs=in_specs,
            out_specs=out_specs,
            scratch_shapes=[pltpu.VMEM((seq, 2 * A_DH), BF16), pltpu.VMEM((seq, 2 * A_DH), BF16),
                            pltpu.VMEM((SUBLANES * SUBLANES, A_DH), F32), pltpu.VMEM((2, blk, blk), F32)],
        ),
        out_shape=out_shape,
        compiler_params=_params("arbitrary", "arbitrary", "arbitrary"),
        name="moba_prompt",
    )(*args)
    if paged is None:
        return out[0]
    return out[0], (out[1] if side_blocks else None)


def _rope_body(inv_ref, cos_ref, sin_ref, *, pos0):
    pos = pos0 + lax.broadcasted_iota(I32, cos_ref.shape, 0)
    ang = pos.astype(F32) * inv_ref[...]
    cos_ref[...] = jnp.cos(ang)
    sin_ref[...] = jnp.sin(ang)


def _rope_tables(pos0, n_pos):
    half = R_DK // 2
    inv = 1.0 / (RET_ROPE_BASE ** jnp.linspace(0.0, 1.0, half, dtype=F32))
    rows = -(-n_pos // 8) * 8
    return pl.pallas_call(
        functools.partial(_rope_body, pos0=pos0),
        out_shape=[jax.ShapeDtypeStruct((rows, half), F32)] * 2,
        name="rope_tables",
    )(inv.reshape(1, half))


def _rotate(x, cos, sin):
    half = x.shape[-1] // 2
    x1, x2 = x[:, :half], x[:, half:]
    return jnp.concatenate([x1 * cos - x2 * sin, x1 * sin + x2 * cos], axis=1)


def _log_decay(h, shape):
    hf = jnp.full(shape, h, I32).astype(F32)
    return jnp.log(1.0 - jnp.exp2(-5.0 - hf))


def _ret_prompt_body(q_ref, k_ref, v_ref, g_ref, cos_ref, sin_ref, o_ref, so_ref, s_sc, *, c):
    n = pl.program_id(1)

    @pl.when(n == 0)
    def _():
        s_sc[...] = jnp.zeros_like(s_sc)

    i = lax.broadcasted_iota(I32, (c, c), 0).astype(F32)
    j = lax.broadcasted_iota(I32, (c, c), 1).astype(F32)
    diff = i - j
    i_col = lax.broadcasted_iota(I32, (c, 1), 0).astype(F32)
    cos, sin = cos_ref[...], sin_ref[...]

    for h in range(R_HEADS):
        dk, dv = slice(h * R_DK, (h + 1) * R_DK), slice(h * R_DV, (h + 1) * R_DV)
        dmask = jnp.where(diff >= 0, jnp.exp(jnp.maximum(diff, 0.0) * _log_decay(h, (c, c))), 0.0)
        lg_col = _log_decay(h, (c, 1))
        q_dec = jnp.exp((i_col + 1.0) * lg_col)
        k_dec = jnp.exp((c - 1.0 - i_col) * lg_col)
        c_dec = jnp.exp(c * _log_decay(h, (1, R_DV)))
        qr = _rotate(q_ref[:, dk], cos, sin)
        kr = _rotate(k_ref[:, dk], cos, sin) * (R_DK ** -0.5)
        vb = v_ref[:, dv].astype(BF16)
        s = s_sc[h]
        att = lax.dot_general(qr.astype(BF16), kr.astype(BF16), NT_DIMS, preferred_element_type=F32) * dmask
        o = _dot(att.astype(BF16), vb) + _dot((qr * q_dec).astype(BF16), s.astype(BF16))
        s_sc[h] = s * c_dec + lax.dot_general((kr * k_dec).astype(BF16), vb, TN_DIMS, preferred_element_type=F32)
        on = o * lax.rsqrt(jnp.mean(o * o, axis=-1, keepdims=True) + EPS)
        o_ref[:, dv] = (on * _silu(g_ref[:, dv])).astype(o_ref.dtype)

    @pl.when(n == pl.num_programs(1) - 1)
    def _():
        so_ref[...] = s_sc[...]


def _ret_prompt(z3, cos, sin, batch, seq):
    c = math.gcd(seq, RET_CHUNK)
    nc = seq // c

    def sec(s):
        return pl.BlockSpec((None, c, SEC), lambda b, n: (s, b * nc + n, 0))

    tab = pl.BlockSpec((c, R_DK // 2), lambda b, n: (n, 0))
    return pl.pallas_call(
        functools.partial(_ret_prompt_body, c=c),
        grid=(batch, nc),
        in_specs=[sec(SEC_QR), sec(SEC_KR), sec(SEC_VR), sec(SEC_GR), tab, tab],
        out_specs=[pl.BlockSpec((c, R_HEADS * R_DV), lambda b, n: (b * nc + n, 0)),
                   pl.BlockSpec((None, R_HEADS, R_DK, R_DV), lambda b, n: (b, 0, 0, 0))],
        out_shape=[jax.ShapeDtypeStruct((batch * seq, R_HEADS * R_DV), BF16),
                   jax.ShapeDtypeStruct((batch, R_HEADS, R_DK, R_DV), F32)],
        scratch_shapes=[pltpu.VMEM((R_HEADS, R_DK, R_DV), F32)],
        compiler_params=_params("parallel", "arbitrary"),
        name="retention_prompt",
    )(z3, z3, z3, z3, cos, sin)


def _ret_step_body(q_ref, k_ref, v_ref, g_ref, cos_ref, sin_ref, s_ref, o_ref, so_ref):
    cos, sin = cos_ref[0:1, :], sin_ref[0:1, :]
    rows = 16
    row0 = lax.broadcasted_iota(I32, (rows, R_DK), 0) == 0
    for h in range(R_HEADS):
        decay = jnp.exp(_log_decay(h, (1, R_DV)))
        qr = _rotate(q_ref[h:h + 1, :], cos, sin)
        kr = _rotate(k_ref[h:h + 1, :], cos, sin) * (R_DK ** -0.5)
        v = v_ref[h:h + 1, :]
        s = s_ref[h]
        q_rows = jnp.broadcast_to(qr * decay, (rows, R_DK)).astype(BF16)
        qs = _dot(q_rows, s.astype(BF16))[0:1, :]
        o = jnp.sum(qr * kr, axis=-1, keepdims=True) * v + qs
        k_rows = jnp.where(row0, jnp.broadcast_to(kr, (rows, R_DK)), 0.0).astype(BF16)
        v_rows = jnp.broadcast_to(v, (rows, R_DV)).astype(BF16)
        so_ref[h] = s * decay + lax.dot_general(k_rows, v_rows, TN_DIMS, preferred_element_type=F32)
        on = o * lax.rsqrt(jnp.mean(o * o, axis=-1, keepdims=True) + EPS)
        o_ref[h:h + 1, :] = (on * _silu(g_ref[h:h + 1, :])).astype(o_ref.dtype)


def _ret_step(z3, cos, sin, state):
    batch = z3.shape[1]
    z4 = z3.reshape(z3.shape[0], batch, R_HEADS, R_DK)

    def sec(s):
        return pl.BlockSpec((None, None, R_HEADS, R_DK), lambda b: (s, b, 0, 0))

    tab = pl.BlockSpec((8, R_DK // 2), lambda b: (0, 0))
    st = pl.BlockSpec((None, R_HEADS, R_DK, R_DV), lambda b: (b, 0, 0, 0))
    return pl.pallas_call(
        _ret_step_body,
        grid=(batch,),
        in_specs=[sec(SEC_QR), sec(SEC_KR), sec(SEC_VR), sec(SEC_GR), tab, tab, st],
        out_specs=[pl.BlockSpec((None, R_HEADS, R_DV), lambda b: (b, 0, 0)), st],
        out_shape=[jax.ShapeDtypeStruct((batch, R_HEADS, R_DV), F32),
                   jax.ShapeDtypeStruct(state.shape, F32)],
        compiler_params=_params("parallel"),
        name="retention_step",
    )(z4, z4, z4, z4, cos, sin, state)


def _mem_body(q_ref, k_ref, v_ref, o_ref):
    for h in range(M_HEADS):
        dh = slice(h * M_DH, (h + 1) * M_DH)
        s = lax.dot_general(q_ref[:, dh].astype(BF16), k_ref[:, dh].astype(BF16), NT_DIMS,
                            preferred_element_type=F32) * (M_DH ** -0.5)
        m = jnp.max(s, axis=-1, keepdims=True)
        p = jnp.exp(s - m)
        l = jnp.sum(p, axis=-1, keepdims=True)
        o_ref[:, dh] = (_dot(p.astype(BF16), v_ref[:, dh].astype(BF16)) / l).astype(o_ref.dtype)


def _mem_attend(z3, mk, mv, batch, seq):
    n_mem = mk.shape[0] // batch
    width = M_HEADS * M_DH
    ts = _tile(seq, 512)
    nt = seq // ts
    kv = pl.BlockSpec((n_mem, width), lambda b, t: (b, 0))
    return pl.pallas_call(
        _mem_body,
        grid=(batch, nt),
        in_specs=[pl.BlockSpec((None, ts, width), lambda b, t: (SEC_QM, b * nt + t, 0)), kv, kv],
        out_specs=pl.BlockSpec((ts, width), lambda b, t: (b * nt + t, 0)),
        out_shape=jax.ShapeDtypeStruct((batch * seq, width), BF16),
        compiler_params=_params("parallel", "arbitrary"),
        name="mem_attend",
    )(z3, mk, mv)


def _merge_body(h_ref, *refs):
    x_refs, wg_refs, bg_refs, wb_refs, o_ref = refs[0:3], refs[3:6], refs[6:9], refs[9:12], refs[12]
    h = h_ref[...]
    merged = None
    for g in range(3):
        term = jax.nn.sigmoid(_dot(h, wg_refs[g][...]) + bg_refs[g][...]) * _dot(x_refs[g][...], wb_refs[g][...])
        merged = term if merged is None else merged + term
    o_ref[...] = merged.astype(o_ref.dtype)


def _merge(h, branches, w_gate, b_gate, branch_ws):
    rows, d = h.shape
    tm, tn = _tile(rows, 1024), _tile(d, 512)
    nj = d // tn
    b_gate = b_gate.reshape(1, 3 * d)

    def row(width):
        return pl.BlockSpec((tm, width), lambda i, j: (i, 0))

    def cols(n_rows, group=0):
        return pl.BlockSpec((n_rows, tn), lambda i, j: (0, group * nj + j))

    return pl.pallas_call(
        _merge_body,
        grid=(rows // tm, nj),
        in_specs=([row(d)] + [row(x.shape[1]) for x in branches]
                  + [cols(d, g) for g in range(3)] + [cols(1, g) for g in range(3)]
                  + [cols(w.shape[0]) for w in branch_ws]),
        out_specs=pl.BlockSpec((tm, tn), lambda i, j: (i, j)),
        out_shape=jax.ShapeDtypeStruct((rows, d), BF16),
        compiler_params=_params("parallel", "arbitrary"),
        name="gated_merge",
    )(h, *branches, w_gate, w_gate, w_gate, b_gate, b_gate, b_gate, *branch_ws)


def _moba_gate_body(pt_ref, q_ref, *refs, bps):
    del pt_ref
    k_refs, gates_ref = refs[:-1], refs[-1]
    n = pl.program_id(1)

    @pl.when(n == 0)
    def _():
        gates_ref[...] = jnp.zeros_like(gates_ref)

    gates_ref[...] = _block_gates(q_ref[...], k_refs, n * bps, gates_ref[...])


def _moba_select_body(g_ref, sel_ref, *, nb_past):
    lane = lax.broadcasted_iota(I32, g_ref.shape, 1)
    gw = jnp.where(lane < nb_past, g_ref[...], NEG_INF)
    sel = jnp.zeros(gw.shape, I32)
    for r in range(MOBA_TOPK):
        mx = jnp.max(gw, axis=1, keepdims=True)
        idx = jnp.min(jnp.where(gw == mx, lane, 128), axis=1, keepdims=True)
        sel = jnp.where(lane == r, idx, sel)
        gw = jnp.where(lane == idx, -jnp.inf, gw)
    sel_ref[...] = sel


def _moba_attend_body(pt_ref, sel_ref, t5_ref, q_ref, kn_ref, vn_ref, ck_ref, cv_ref, o_ref, kbuf, vbuf, sem,
                      *, past_len):
    b = pl.program_id(0)
    slot = lax.rem(b, 2)
    scale = A_DH ** -0.5
    n_tiles = MOBA_TOPK * PAGES_PER_BLOCK

    def tile_copies(bb, to_slot):
        copies = []
        for h in range(A_HEADS):
            for r in range(MOBA_TOPK):
                block = sel_ref[bb, h * MOBA_TOPK + r]
                for t in range(PAGES_PER_BLOCK):
                    page = pt_ref[bb, block * PAGES_PER_BLOCK + t]
                    j = r * PAGES_PER_BLOCK + t
                    copies.append(pltpu.make_async_copy(ck_ref.at[page, :, h, :], kbuf.at[to_slot, h, j],
                                                        sem.at[to_slot]))
                    copies.append(pltpu.make_async_copy(cv_ref.at[page, :, h, :], vbuf.at[to_slot, h, j],
                                                        sem.at[to_slot]))
        return copies

    @pl.when(b == 0)
    def _():
        for cp in tile_copies(0, 0):
            cp.start()

    @pl.when(b + 1 < pl.num_programs(0))
    def _():
        for cp in tile_copies(b + 1, 1 - slot):
            cp.start()

    for cp in tile_copies(b, slot):
        cp.wait()

    q, kn, vn = q_ref[...], kn_ref[...], vn_ref[...]
    lane = lax.broadcasted_iota(I32, (1, PAGE_SIZE), 1)
    for h in range(A_HEADS):
        qh = q[h:h + 1, :]
        q8 = jnp.broadcast_to(qh, (8, A_DH)).astype(BF16)
        s = []
        for r in range(MOBA_TOPK):
            block = sel_ref[b, h * MOBA_TOPK + r]
            for t in range(PAGES_PER_BLOCK):
                kt = kbuf[slot, h, r * PAGES_PER_BLOCK + t].astype(BF16)
                raw = lax.dot_general(q8, kt, NT_DIMS, preferred_element_type=F32)[0:1, :]
                bucket = _t5_bucket(past_len - (block * MOBA_BLOCK + t * PAGE_SIZE + lane))
                bias = jnp.zeros((1, PAGE_SIZE), F32)
                for bkt in range(T5_BUCKETS):
                    bias = jnp.where(bucket == bkt, t5_ref[bkt, h], bias)
                s.append(raw * scale + bias)
        s_own = jnp.sum(qh * kn[h:h + 1, :], axis=1, keepdims=True) * scale + t5_ref[0, h]
        m = jnp.maximum(jnp.max(functools.reduce(jnp.maximum, s), axis=1, keepdims=True), s_own)
        p = [jnp.exp(x - m) for x in s]
        p_own = jnp.exp(s_own - m)
        l = jnp.sum(functools.reduce(jnp.add, p), axis=1, keepdims=True) + p_own
        acc = p_own * vn[h:h + 1, :]
        for j in range(n_tiles):
            p8 = jnp.broadcast_to(p[j], (8, PAGE_SIZE)).astype(BF16)
            acc = acc + _dot(p8, vbuf[slot, h, j].astype(BF16))[0:1, :]
        o_ref[h:h + 1, :] = acc / l


def _moba_gates(q4, cache_k, page_table):
    batch, n_pages = page_table.shape
    nb_past = n_pages // PAGES_PER_BLOCK
    bps = _tile(nb_past, 8)

    def page(t):
        return pl.BlockSpec((None, PAGE_SIZE, A_HEADS, A_DH),
                            lambda b, n, pt: (pt[b, n * bps * PAGES_PER_BLOCK + t], 0, 0, 0))

    pages = [page(t) for t in range(bps * PAGES_PER_BLOCK)]
    return pl.pallas_call(
        functools.partial(_moba_gate_body, bps=bps),
        grid_spec=pltpu.PrefetchScalarGridSpec(
            num_scalar_prefetch=1,
            grid=(batch, nb_past // bps),
            in_specs=[pl.BlockSpec((None, None, A_HEADS, A_DH), lambda b, n, pt: (SEC_QA, b, 0, 0))] + pages,
            out_specs=pl.BlockSpec((None, A_HEADS, 128), lambda b, n, pt: (b, 0, 0)),
        ),
        out_shape=jax.ShapeDtypeStruct((batch, A_HEADS, 128), F32),
        compiler_params=_params("parallel", "arbitrary"),
        name="moba_gate",
    )(page_table, q4, *([cache_k] * len(pages)))


def _moba_step(z4, k_new, v_new, cache_k, cache_v, page_table, t5_table, gates=None):
    batch, n_pages = page_table.shape
    past_len = n_pages * PAGE_SIZE
    assert past_len % MOBA_BLOCK == 0 and MOBA_BLOCK % PAGE_SIZE == 0
    nb_past = past_len // MOBA_BLOCK
    assert MOBA_TOPK <= nb_past <= 128
    k_new = k_new.reshape(batch, A_HEADS, A_DH)
    v_new = v_new.reshape(batch, A_HEADS, A_DH)
    if gates is None:
        gates = _moba_gates(z4, cache_k, page_table)
    sel = pl.pallas_call(
        functools.partial(_moba_select_body, nb_past=nb_past),
        out_shape=jax.ShapeDtypeStruct((batch * A_HEADS, 128), I32),
        name="moba_select",
    )(gates.reshape(batch * A_HEADS, 128))
    sel = sel[:, :MOBA_TOPK].reshape(batch, A_HEADS * MOBA_TOPK)

    new_tok = pl.BlockSpec((None, A_HEADS, A_DH), lambda b, pt, sl: (b, 0, 0))
    tiles = pltpu.VMEM((2, A_HEADS, MOBA_TOPK * PAGES_PER_BLOCK, PAGE_SIZE, A_DH), F32)
    return pl.pallas_call(
        functools.partial(_moba_attend_body, past_len=past_len),
        grid_spec=pltpu.PrefetchScalarGridSpec(
            num_scalar_prefetch=2,
            grid=(batch,),
            in_specs=[pl.BlockSpec(memory_space=pltpu.SMEM),
                      pl.BlockSpec((None, None, A_HEADS, A_DH), lambda b, pt, sl: (SEC_QA, b, 0, 0)),
                      new_tok, new_tok,
                      pl.BlockSpec(memory_space=pl.ANY), pl.BlockSpec(memory_space=pl.ANY)],
            out_specs=pl.BlockSpec((None, A_HEADS, A_DH), lambda b, pt, sl: (b, 0, 0)),
            scratch_shapes=[tiles, tiles, pltpu.SemaphoreType.DMA((2,))],
        ),
        out_shape=jax.ShapeDtypeStruct((batch, A_HEADS, A_DH), F32),
        compiler_params=_params("arbitrary"),
        name="moba_attend",
    )(page_table, sel, t5_table, z4, k_new, v_new, cache_k, cache_v)


def kernel(x_prompt, x_sample, mem_prompt, cache_k, cache_v, cache_mem_k, cache_mem_v, state_ret, page_table, t5_table, ffn1_norm, ffn1_w1, ffn1_w3, ffn1_w2, mix_norm, mem_norm, w_in, w_mem_kv, w_gate, b_gate, w_br_moba, w_br_ret, w_br_mem, w_out, ffn2_norm, ffn2_w1, ffn2_w3, ffn2_w2, final_norm):
    batch, seq, d = x_prompt.shape
    dec_batch, dec_seq, _ = x_sample.shape
    assert dec_seq == 1
    depth = w_in.shape[0]
    n_mem = mem_prompt.shape[1]
    past_len = page_table.shape[1] * PAGE_SIZE
    cos_p, sin_p = _rope_tables(0, seq)
    cos_s, sin_s = _rope_tables(past_len, dec_seq)

    xp = x_prompt.reshape(batch * seq, d)
    xs = x_sample.reshape(dec_batch, d)
    outs = [[] for _ in range(8)]
    for l in range(depth):
        last = l == depth - 1
        g_next = final_norm if last else ffn1_norm[l + 1]

        x1s, h2s, *f1_b = _ffn_half(xs, ffn1_norm[l], ffn1_w1[l], ffn1_w3[l], ffn1_w2[l], mix_norm[l],
                                    emit_x=True, post_dtype=BF16, emit_w=True)
        later_ws = [w_in[l], w_gate[l], w_br_moba[l], w_br_ret[l], w_br_mem[l], w_out[l],
                    ffn2_w1[l], ffn2_w3[l], ffn2_w2[l]]
        x1, h2, *later_b = _ffn_half(xp, ffn1_norm[l], *f1_b, mix_norm[l], emit_x=True, post_dtype=BF16,
                                     cast_jobs=later_ws)
        w_in_b, w_gate_b, w_out_b = later_b[0], later_b[1], later_b[5]
        branch_b, f2_b = later_b[2:5], later_b[6:9]
        z3s, ka_s, va_s = _in_proj(h2s, w_in_b)
        z3, ka, va = _in_proj(h2, w_in_b)

        z4s = z3s.reshape(z3s.shape[0], dec_batch, A_HEADS, A_DH)
        oa, gates_s = _moba_prompt(z3, ka, va, t5_table, batch, seq, paged=(page_table, z4s, cache_k[l]))
        oa_s = _moba_step(z4s, ka_s, va_s, cache_k[l], cache_v[l], page_table, t5_table, gates=gates_s)

        o_rs, s_new_s = _ret_step(z3s, cos_s, sin_s, state_ret[l])
        z3s_pad = jnp.broadcast_to(z3s[:, :, None, :], (z3s.shape[0], dec_batch, 8, SEC))
        z3s_pad = z3s_pad.reshape(z3s.shape[0], dec_batch * 8, SEC)
        om_s = _mem_attend(z3s_pad, cache_mem_k[l].reshape(dec_batch * n_mem, M_HEADS * M_DH),
                           cache_mem_v[l].reshape(dec_batch * n_mem, M_HEADS * M_DH), dec_batch, 8)
        om_s = om_s.reshape(dec_batch, 8, M_HEADS * M_DH)[:, 0, :]
        branches_s = [oa_s.reshape(dec_batch, -1).astype(BF16), o_rs.reshape(dec_batch, -1).astype(BF16), om_s]

        mem_h = _norm_rows(mem_prompt.reshape(batch * n_mem, d), mem_norm[l], BF16)
        mkv_p = _matmul(mem_h, w_mem_kv[l], tn_pref=512, name="mem_kv")
        mk_p, mv_p = mkv_p[:, :M_HEADS * M_DH], mkv_p[:, M_HEADS * M_DH:]
        o_r, s_new_p = _ret_prompt(z3, cos_p, sin_p, batch, seq)
        om = _mem_attend(z3, mk_p, mv_p, batch, seq)

        merged_s = _merge(h2s, branches_s, w_gate_b, b_gate[l], branch_b)
        merged = _merge(h2, [oa, o_r, om], w_gate_b, b_gate[l], branch_b)
        x2s = _matmul(merged_s, w_out_b, residual=x1s, name="out_proj")
        x2 = _matmul(merged, w_out_b, residual=x1, name="out_proj")
        xs = _ffn_half(x2s, ffn2_norm[l], *f2_b, g_next, emit_x=not last, post_dtype=F32)[0]
        xp = _ffn_half(x2, ffn2_norm[l], *f2_b, g_next, emit_x=not last, post_dtype=F32)[0]

        new = (ka.reshape(batch, seq, A_HEADS, A_DH), va.reshape(batch, seq, A_HEADS, A_DH),
               mk_p.reshape(batch, n_mem, M_HEADS, M_DH), mv_p.reshape(batch, n_mem, M_HEADS, M_DH), s_new_p,
               ka_s.reshape(dec_batch, dec_seq, A_HEADS, A_DH), va_s.reshape(dec_batch, dec_seq, A_HEADS, A_DH),
               s_new_s)
        for acc, val in zip(outs, new):
            acc.append(val)

    y_prompt = xp.reshape(batch, seq, d)
    y_sample = xs.reshape(dec_batch, dec_seq, d)
    return (y_prompt, y_sample) + tuple(jnp.stack(o) for o in outs)
```

```python
import functools
import math

import jax
import jax.numpy as jnp
import numpy as np
from jax import lax
from jax.experimental import pallas as pl
from jax.experimental.pallas import tpu as pltpu

F32 = jnp.float32
BF16 = jnp.bfloat16
I32 = jnp.int32

A_HEADS, A_DH = 8, 128
MOBA_BLOCK, MOBA_TOPK = 256, 3
T5_BUCKETS, T5_MAX_DIST = 32, 128
R_HEADS, R_DK, R_DV = 4, 256, 256
RET_CHUNK = 128
RET_ROPE_BASE = 10000.0
M_HEADS, M_DH = 4, 256
PAGE_SIZE = 128
EPS = 1e-6
NEG_INF = -1e30
LOG2E = math.log2(math.e)
SEC = 1024
W_IN_KA, W_IN_VA = 1, 2
(SEC_QA, SEC_QR, SEC_KR, SEC_VR, SEC_GR, SEC_QM) = range(6)

V7X_VMEM_LIMIT_BYTES = 56 * 1024 * 1024

NT_DIMS = (((1,), (1,)), ((), ()))
TN_DIMS = (((0,), (0,)), ((), ()))


def _params(*sem):
    return pltpu.CompilerParams(dimension_semantics=sem, vmem_limit_bytes=V7X_VMEM_LIMIT_BYTES)


def _tile(n, pref):
    t = min(n, pref)
    while n % t:
        t -= 1
    return t


def _rms(x, g):
    return x * lax.rsqrt(jnp.mean(x * x, axis=-1, keepdims=True) + EPS) * g


def _silu(x):
    return x * jax.nn.sigmoid(x)


def _dot(a, b):
    return jnp.dot(a, b, preferred_element_type=F32)


def _bf16(w):
    return w if w.dtype == BF16 else w.astype(BF16)


def _ffn_body(x_ref, g_ref, w1_ref, w3_ref, w2_ref, gp_ref, *refs, emit_x, emit_w, n_cast):
    refs = list(refs)
    cast_in = [refs.pop(0) for _ in range(n_cast)]
    xo_ref = refs.pop(0) if emit_x else None
    ho_ref = refs.pop(0)
    wb_refs = [refs.pop(0) for _ in range(3)] if emit_w else []
    cast_out = [refs.pop(0) for _ in range(n_cast)]
    h_sc, acc_sc = refs
    f = pl.program_id(1)

    @pl.when(f == 0)
    def _():
        h_sc[...] = _rms(x_ref[...], g_ref[...]).astype(BF16)
        acc_sc[...] = jnp.zeros_like(acc_sc)

    for src_ref, dst_ref in zip(cast_in, cast_out):
        dst_ref[...] = src_ref[...].astype(BF16)

    w1, w3, w2 = _bf16(w1_ref[...]), _bf16(w3_ref[...]), _bf16(w2_ref[...])
    for wb_ref, w in zip(wb_refs, (w1, w3, w2)):
        wb_ref[...] = w
    h = h_sc[...]
    t = _silu(_dot(h, w1)) * _dot(h, w3)
    acc_sc[...] += _dot(t.astype(BF16), w2)

    @pl.when(f == pl.num_programs(1) - 1)
    def _():
        xo = x_ref[...] + 0.5 * acc_sc[...]
        if emit_x:
            xo_ref[...] = xo
        ho_ref[...] = _rms(xo, gp_ref[...]).astype(ho_ref.dtype)


BF16_SUBLANES = 16


def _ffn_half(x, g, w1, w3, w2, g_post, *, emit_x, post_dtype, emit_w=False, cast_jobs=()):
    rows, d = x.shape
    ff = w1.shape[1]
    tm, tf = _tile(rows, 512), _tile(ff, 512)
    assert not emit_w or rows == tm
    n_f = ff // tf
    n_steps = (rows // tm) * n_f
    row_spec = pl.BlockSpec((tm, d), lambda i, f: (i, 0))
    vec_spec = pl.BlockSpec((1, d), lambda i, f: (0, 0))
    w13_spec = pl.BlockSpec((d, tf), lambda i, f: (0, f))
    w2_spec = pl.BlockSpec((tf, d), lambda i, f: (f, 0))
    out_shape = [jax.ShapeDtypeStruct((rows, d), post_dtype)]
    out_specs = [row_spec]
    if emit_x:
        out_shape = [jax.ShapeDtypeStruct((rows, d), F32)] + out_shape
        out_specs = [row_spec] + out_specs
    if emit_w:
        out_shape += [jax.ShapeDtypeStruct(w.shape, BF16) for w in (w1, w3, w2)]
        out_specs += [w13_spec, w13_spec, w2_spec]

    def cast_spec(a):
        n_chunks = max(n for n in range(1, n_steps + 1)
                       if a.shape[0] % n == 0 and (a.shape[0] // n) % BF16_SUBLANES == 0)
        return pl.BlockSpec((a.shape[0] // n_chunks, a.shape[1]),
                            lambda i, f: (jnp.minimum(i * n_f + f, n_chunks - 1), 0))

    cast_specs = [cast_spec(a) for a in cast_jobs]
    out_shape += [jax.ShapeDtypeStruct(a.shape, BF16) for a in cast_jobs]
    out_specs += cast_specs
    return pl.pallas_call(
        functools.partial(_ffn_body, emit_x=emit_x, emit_w=emit_w, n_cast=len(cast_jobs)),
        grid=(rows // tm, n_f),
        in_specs=[row_spec, vec_spec, w13_spec, w13_spec, w2_spec, vec_spec] + cast_specs,
        out_specs=out_specs,
        out_shape=out_shape,
        scratch_shapes=[pltpu.VMEM((tm, d), BF16), pltpu.VMEM((tm, d), F32)],
        compiler_params=_params("parallel", "arbitrary"),
        name="ffn_half",
    )(x, g.reshape(1, d), w1, w3, w2, g_post.reshape(1, d), *cast_jobs)


def _norm_body(x_ref, g_ref, o_ref):
    o_ref[...] = _rms(x_ref[...], g_ref[...]).astype(o_ref.dtype)


def _norm_rows(x, g, out_dtype):
    rows, d = x.shape
    tm = _tile(rows, 512)
    return pl.pallas_call(
        _norm_body,
        grid=(rows // tm,),
        in_specs=[pl.BlockSpec((tm, d), lambda i: (i, 0)), pl.BlockSpec((1, d), lambda i: (0, 0))],
        out_specs=pl.BlockSpec((tm, d), lambda i: (i, 0)),
        out_shape=jax.ShapeDtypeStruct((rows, d), out_dtype),
        compiler_params=_params("parallel"),
        name="rmsnorm_rows",
    )(x, g.reshape(1, d))


def _mm_body(a_ref, w_ref, *refs, has_residual):
    o_ref = refs[-1]
    acc = _dot(a_ref[...], _bf16(w_ref[...]))
    if has_residual:
        acc = refs[0][...] + acc
    o_ref[...] = acc.astype(o_ref.dtype)


def _in_proj_body(a_ref, w_ref, z_ref, k_ref, v_ref):
    j = pl.program_id(1)
    acc = _dot(a_ref[...], w_ref[...])

    @pl.when(j == W_IN_KA)
    def _():
        k_ref[...] = acc

    @pl.when(j == W_IN_VA)
    def _():
        v_ref[...] = acc

    @pl.when((j != W_IN_KA) & (j != W_IN_VA))
    def _():
        z_ref[...] = acc


def _in_proj(a, w):
    rows, k = a.shape
    n_sec = w.shape[1] // SEC
    assert (W_IN_KA, W_IN_VA) == (1, 2) and n_sec == 8
    tm = _tile(rows, 1024)
    kv_spec = pl.BlockSpec((tm, SEC), lambda i, j: (i, 0))
    return pl.pallas_call(
        _in_proj_body,
        grid=(rows // tm, n_sec),
        in_specs=[pl.BlockSpec((tm, k), lambda i, j: (i, 0)), pl.BlockSpec((k, SEC), lambda i, j: (0, j))],
        out_specs=[pl.BlockSpec((None, tm, SEC), lambda i, j: (jnp.maximum(j - 2, 0), i, 0)), kv_spec, kv_spec],
        out_shape=[jax.ShapeDtypeStruct((n_sec - 2, rows, SEC), F32)] + [jax.ShapeDtypeStruct((rows, SEC), F32)] * 2,
        compiler_params=_params("parallel", "arbitrary"),
        name="in_proj",
    )(a, w)


def _matmul(a, w, *, residual=None, out_dtype=F32, tn_pref=1024, name="matmul"):
    rows, k = a.shape
    n_cols = w.shape[1]
    tm, tn = _tile(rows, 1024), _tile(n_cols, tn_pref)
    in_specs = [pl.BlockSpec((tm, k), lambda i, j: (i, 0)), pl.BlockSpec((k, tn), lambda i, j: (0, j))]
    out_spec = pl.BlockSpec((tm, tn), lambda i, j: (i, j))
    args = [a, w]
    if residual is not None:
        in_specs.append(out_spec)
        args.append(residual)
    return pl.pallas_call(
        functools.partial(_mm_body, has_residual=residual is not None),
        grid=(rows // tm, n_cols // tn),
        in_specs=in_specs,
        out_specs=out_spec,
        out_shape=jax.ShapeDtypeStruct((rows, n_cols), out_dtype),
        compiler_params=_params("parallel", "arbitrary"),
        name=name,
    )(*args)


def _t5_bucket(n):
    n = jnp.maximum(n, 0)
    max_exact = T5_BUCKETS // 2
    nf = jnp.maximum(n, 1).astype(F32)
    large = max_exact + (jnp.log(nf / max_exact) / math.log(T5_MAX_DIST / max_exact)
                         * (T5_BUCKETS - max_exact)).astype(I32)
    return jnp.where(n < max_exact, n, jnp.minimum(large, T5_BUCKETS - 1))


PAGES_PER_BLOCK = MOBA_BLOCK // PAGE_SIZE
SUBLANES = 8


def _block_gates(q, page_refs, first_block, gates):
    lane = lax.broadcasted_iota(I32, gates.shape, 1)
    for i in range(len(page_refs) // PAGES_PER_BLOCK):
        pages = [page_refs[i * PAGES_PER_BLOCK + t][...] for t in range(PAGES_PER_BLOCK)]
        ksum = functools.reduce(jnp.add, [jnp.sum(x, axis=0) for x in pages])
        gate = jnp.sum(q * ksum, axis=1, keepdims=True) / MOBA_BLOCK
        gates = jnp.where(lane == first_block + i, gate, gates)
    return gates


PAGE_RING = 3


def _moba_prompt_body(*refs, nb, side_blocks, grid):
    h, b, c = pl.program_id(0), pl.program_id(1), pl.program_id(2)
    if side_blocks:
        (pt_ref, t5_ref, q_ref, k_ref, v_ref, qside_ref, cache_ref, o_ref, gates_ref,
         kb_sc, vb_sc, kdiff_sc, bias_sc, pages_sc, page_sem) = refs
        n_pages = side_blocks * PAGES_PER_BLOCK
        steps_per_seq = grid[2]
        n_steps = grid[0] * grid[1] * grid[2]
        step_idx = (h * grid[1] + b) * steps_per_seq + c

        def page_copies(s):
            seq, part, slot = s // steps_per_seq, s % steps_per_seq, s % PAGE_RING
            return [pltpu.make_async_copy(cache_ref.at[pt_ref[seq, part * n_pages + t]], pages_sc.at[slot, t],
                                          page_sem.at[slot]) for t in range(n_pages)]

        @pl.when(step_idx == 0)
        def _():
            for s in range(min(PAGE_RING - 1, n_steps)):
                for cp in page_copies(s):
                    cp.start()

        @pl.when(step_idx + PAGE_RING - 1 < n_steps)
        def _():
            for cp in page_copies(step_idx + PAGE_RING - 1):
                cp.start()

        for cp in page_copies(step_idx):
            cp.wait()
        slot = step_idx % PAGE_RING
        page_refs = [pages_sc.at[slot, t] for t in range(n_pages)]
    else:
        t5_ref, q_ref, k_ref, v_ref, o_ref, kb_sc, vb_sc, kdiff_sc, bias_sc = refs
    blk = MOBA_BLOCK
    nbp = SUBLANES
    assert nb <= nbp
    scale = A_DH ** -0.5
    ii = lax.broadcasted_iota(I32, (blk, blk), 0)
    jj = lax.broadcasted_iota(I32, (blk, blk), 1)

    @pl.when((b == 0) & (c == 0))
    def _():
        for t in range(2):
            bucket = _t5_bucket(t * blk + ii - jj)
            tile = jnp.zeros((blk, blk), F32)
            for bkt in range(T5_BUCKETS):
                tile = jnp.where(bucket == bkt, t5_ref[bkt, h], tile)
            tile = tile * LOG2E
            bias_sc[t] = jnp.where(ii >= jj, tile, NEG_INF) if t == 0 else tile

    @pl.when(c == 0)
    def _():
        km = jnp.concatenate(
            [jnp.mean(k_ref[n * blk:(n + 1) * blk, :], axis=0, keepdims=True) for n in range(nb)]
            + [jnp.zeros((1, A_DH), F32)] * (nbp - nb), axis=0)
        for n in range(nb):
            kdiff_sc[n * nbp:(n + 1) * nbp, :] = km[n:n + 1, :] - km
            kb_sc[n * blk:(n + 1) * blk, A_DH:2 * A_DH] = jnp.where(
                lax.broadcasted_iota(I32, (blk, 128), 1) == n, 1.0, 0.0).astype(BF16)
        kb_sc[:, 0:A_DH] = k_ref[...].astype(BF16)
        vb_sc[:, 0:A_DH] = v_ref[...].astype(BF16)
        vb_sc[:, A_DH:2 * A_DH] = jnp.ones((nb * blk, A_DH), BF16)
        if side_blocks:
            gates_ref[...] = jnp.zeros_like(gates_ref)

    far_bias = t5_ref[T5_BUCKETS - 1, h] * LOG2E

    def attend(cc):
        q = q_ref[cc * blk:(cc + 1) * blk, :]
        qs = (q * (scale * LOG2E)).astype(BF16)
        n_keys = (cc + 1) * blk
        if cc <= MOBA_TOPK:
            s_all = lax.dot_general(qs, kb_sc[0:n_keys, 0:A_DH], NT_DIMS, preferred_element_type=F32)
        else:
            diff_t = lax.dot_general(kdiff_sc[0:cc * nbp, :], q, NT_DIMS, precision=lax.Precision.HIGHEST,
                                     preferred_element_type=F32)
            pair = lax.broadcasted_iota(I32, diff_t.shape, 0)
            m_of, n_of = jnp.right_shift(pair, int(math.log2(nbp))), jnp.bitwise_and(pair, nbp - 1)
            beats_t = jnp.where(diff_t > 0, 1.0, jnp.where(diff_t == 0, jnp.where(n_of > m_of, 1.0, 0.0), 0.0))
            fold = jnp.where(jnp.bitwise_and(lax.broadcasted_iota(I32, (cc * nbp, 128), 0), nbp - 1)
                             == lax.broadcasted_iota(I32, (cc * nbp, 128), 1), 1.0, 0.0)
            rank = lax.dot_general(beats_t.astype(BF16), fold.astype(BF16), TN_DIMS,
                                   preferred_element_type=F32)
            lane = lax.broadcasted_iota(I32, rank.shape, 1)
            sel = jnp.where(lane < cc, jnp.where(rank < MOBA_TOPK, 0.0, NEG_INF), 0.0)
            q_sel = jnp.concatenate([qs, sel.astype(BF16)], axis=1)
            s_all = lax.dot_general(q_sel, kb_sc[0:n_keys, :], NT_DIMS, preferred_element_type=F32)
        tiles = []
        for n in range(cc + 1):
            t = s_all[:, n * blk:(n + 1) * blk]
            tiles.append(t + (bias_sc[0] if n == cc else bias_sc[1] if n == cc - 1 else far_bias))
        m = jnp.max(functools.reduce(jnp.maximum, tiles), axis=1, keepdims=True)
        p_all = jnp.concatenate([jnp.exp2(t - m).astype(BF16) for t in tiles], axis=1)
        pv = _dot(p_all, vb_sc[0:n_keys, :])
        o_ref[cc * blk:(cc + 1) * blk, :] = (pv[:, 0:A_DH] / pv[:, A_DH:2 * A_DH]).astype(o_ref.dtype)

    def step(s):
        if side_blocks:
            gates_ref[...] = _block_gates(qside_ref[...], page_refs, s * side_blocks, gates_ref[...])
        for cc in sorted({s, nb - 1 - s}):
            attend(cc)

    for s in range(_moba_steps(nb)):
        pl.when(c == s)(functools.partial(step, s))


def _moba_steps(nb):
    return (nb + 1) // 2


def _moba_prompt(z3, k, v, t5_table, batch, seq, paged=None):
    blk = MOBA_BLOCK
    nb = seq // blk
    steps = _moba_steps(nb)
    side_blocks = 0
    if paged is not None:
        page_table, q_side, cache_k = paged
        nb_past = page_table.shape[1] // PAGES_PER_BLOCK
        if page_table.shape[0] == A_HEADS * batch and nb_past % steps == 0:
            side_blocks = nb_past // steps

    seq_spec = pl.BlockSpec((seq, A_DH), lambda h, b, c, *_: (b, h))
    in_specs = [pl.BlockSpec(memory_space=pltpu.SMEM),
                pl.BlockSpec((None, seq, A_DH), lambda h, b, c, *_: (SEC_QA, b, h)), seq_spec, seq_spec]
    out_specs = [seq_spec]
    out_shape = [jax.ShapeDtypeStruct((batch * seq, A_HEADS * A_DH), BF16)]
    args = [t5_table, z3, k, v]
    scratch = [pltpu.VMEM((seq, 2 * A_DH), BF16), pltpu.VMEM((seq, 2 * A_DH), BF16),
               pltpu.VMEM((SUBLANES * SUBLANES, A_DH), F32), pltpu.VMEM((2, blk, blk), F32)]
    grid = (A_HEADS, batch, steps)
    if side_blocks:
        in_specs += [pl.BlockSpec((None, None, A_HEADS, A_DH), lambda h, b, c, pt: (SEC_QA, h * batch + b, 0, 0)),
                     pl.BlockSpec(memory_space=pl.ANY)]
        out_specs.append(pl.BlockSpec((None, A_HEADS, 128), lambda h, b, c, pt: (h * batch + b, 0, 0)))
        out_shape.append(jax.ShapeDtypeStruct((A_HEADS * batch, A_HEADS, 128), F32))
        args = [page_table] + args + [q_side, cache_k]
        scratch += [pltpu.VMEM((PAGE_RING, side_blocks * PAGES_PER_BLOCK, PAGE_SIZE, A_HEADS, A_DH), F32),
                    pltpu.SemaphoreType.DMA((PAGE_RING,))]
    out = pl.pallas_call(
        functools.partial(_moba_prompt_body, nb=nb, side_blocks=side_blocks, grid=grid),
        grid_spec=pltpu.PrefetchScalarGridSpec(
            num_scalar_prefetch=1 if side_blocks else 0,
            grid=grid,
            in_specs=in_specs,
            out_specs=out_specs,
            scratch_shapes=scratch,
        ),
        out_shape=out_shape,
        compiler_params=_params("arbitrary", "arbitrary", "arbitrary"),
        name="moba_prompt",
    )(*args)
    if paged is None:
        return out[0]
    return out[0], (out[1] if side_blocks else None)


def _rope_body(inv_ref, cos_ref, sin_ref, *, pos0):
    pos = pos0 + lax.broadcasted_iota(I32, cos_ref.shape, 0)
    ang = pos.astype(F32) * inv_ref[...]
    cos_ref[...] = jnp.cos(ang)
    sin_ref[...] = jnp.sin(ang)


def _rope_tables(pos0, n_pos):
    half = R_DK // 2
    inv = 1.0 / (RET_ROPE_BASE ** jnp.linspace(0.0, 1.0, half, dtype=F32))
    rows = -(-n_pos // 8) * 8
    return pl.pallas_call(
        functools.partial(_rope_body, pos0=pos0),
        out_shape=[jax.ShapeDtypeStruct((rows, half), F32)] * 2,
        name="rope_tables",
    )(inv.reshape(1, half))


def _rotate(x, cos, sin):
    half = x.shape[-1] // 2
    x1, x2 = x[:, :half], x[:, half:]
    return jnp.concatenate([x1 * cos - x2 * sin, x1 * sin + x2 * cos], axis=1)


def _log_decay(h, shape):
    hf = jnp.full(shape, h, I32).astype(F32)
    return jnp.log(1.0 - jnp.exp2(-5.0 - hf))


def _ret_prompt_body(q_ref, k_ref, v_ref, g_ref, cos_ref, sin_ref, o_ref, so_ref, s_sc, *, c):
    n = pl.program_id(1)

    @pl.when(n == 0)
    def _():
        s_sc[...] = jnp.zeros_like(s_sc)

    i = lax.broadcasted_iota(I32, (c, c), 0).astype(F32)
    j = lax.broadcasted_iota(I32, (c, c), 1).astype(F32)
    diff = i - j
    i_col = lax.broadcasted_iota(I32, (c, 1), 0).astype(F32)
    cos, sin = cos_ref[...], sin_ref[...]

    new_states, outs = [], []
    for h in range(R_HEADS):
        dk, dv = slice(h * R_DK, (h + 1) * R_DK), slice(h * R_DV, (h + 1) * R_DV)
        dmask = jnp.where(diff >= 0, jnp.exp(jnp.maximum(diff, 0.0) * _log_decay(h, (c, c))), 0.0)
        lg_col = _log_decay(h, (c, 1))
        q_dec = jnp.exp((i_col + 1.0) * lg_col)
        k_dec = jnp.exp((c - 1.0 - i_col) * lg_col)
        c_dec = jnp.exp(c * _log_decay(h, (1, R_DV)))
        qr = _rotate(q_ref[:, dk], cos, sin)
        kr = _rotate(k_ref[:, dk], cos, sin) * (R_DK ** -0.5)
        vb = v_ref[:, dv].astype(BF16)
        s = s_sc[h]
        att = lax.dot_general(qr.astype(BF16), kr.astype(BF16), NT_DIMS, preferred_element_type=F32) * dmask
        o = _dot(att.astype(BF16), vb) + _dot((qr * q_dec).astype(BF16), s.astype(BF16))
        new_states.append(s * c_dec + lax.dot_general((kr * k_dec).astype(BF16), vb, TN_DIMS,
                                                      preferred_element_type=F32))
        on = o * lax.rsqrt(jnp.mean(o * o, axis=-1, keepdims=True) + EPS)
        outs.append((on * _silu(g_ref[:, dv])).astype(o_ref.dtype))
    for h in range(R_HEADS):
        s_sc[h] = new_states[h]
    o_ref[...] = jnp.concatenate(outs, axis=1)

    @pl.when(n == pl.num_programs(1) - 1)
    def _():
        so_ref[...] = s_sc[...]


def _ret_prompt(z3, cos, sin, batch, seq):
    c = math.gcd(seq, RET_CHUNK)
    nc = seq // c

    def sec(s):
        return pl.BlockSpec((None, c, SEC), lambda b, n: (s, b * nc + n, 0))

    tab = pl.BlockSpec((c, R_DK // 2), lambda b, n: (n, 0))
    return pl.pallas_call(
        functools.partial(_ret_prompt_body, c=c),
        grid=(batch, nc),
        in_specs=[sec(SEC_QR), sec(SEC_KR), sec(SEC_VR), sec(SEC_GR), tab, tab],
        out_specs=[pl.BlockSpec((c, R_HEADS * R_DV), lambda b, n: (b * nc + n, 0)),
                   pl.BlockSpec((None, R_HEADS, R_DK, R_DV), lambda b, n: (b, 0, 0, 0))],
        out_shape=[jax.ShapeDtypeStruct((batch * seq, R_HEADS * R_DV), BF16),
                   jax.ShapeDtypeStruct((batch, R_HEADS, R_DK, R_DV), F32)],
        scratch_shapes=[pltpu.VMEM((R_HEADS, R_DK, R_DV), F32)],
        compiler_params=_params("parallel", "arbitrary"),
        name="retention_prompt",
    )(z3, z3, z3, z3, cos, sin)


def _ret_step_body(q_ref, k_ref, v_ref, g_ref, cos_ref, sin_ref, s_ref, o_ref, so_ref):
    cos, sin = cos_ref[0:1, :], sin_ref[0:1, :]
    rows = 16
    row0 = lax.broadcasted_iota(I32, (rows, R_DK), 0) == 0
    for h in range(R_HEADS):
        decay = jnp.exp(_log_decay(h, (1, R_DV)))
        qr = _rotate(q_ref[h:h + 1, :], cos, sin)
        kr = _rotate(k_ref[h:h + 1, :], cos, sin) * (R_DK ** -0.5)
        v = v_ref[h:h + 1, :]
        s = s_ref[h]
        q_rows = jnp.broadcast_to(qr * decay, (rows, R_DK)).astype(BF16)
        qs = _dot(q_rows, s.astype(BF16))[0:1, :]
        o = jnp.sum(qr * kr, axis=-1, keepdims=True) * v + qs
        k_rows = jnp.where(row0, jnp.broadcast_to(kr, (rows, R_DK)), 0.0).astype(BF16)
        v_rows = jnp.broadcast_to(v, (rows, R_DV)).astype(BF16)
        so_ref[h] = s * decay + lax.dot_general(k_rows, v_rows, TN_DIMS, preferred_element_type=F32)
        on = o * lax.rsqrt(jnp.mean(o * o, axis=-1, keepdims=True) + EPS)
        o_ref[h:h + 1, :] = (on * _silu(g_ref[h:h + 1, :])).astype(o_ref.dtype)


def _ret_step(z3, cos, sin, state):
    batch = z3.shape[1]
    z4 = z3.reshape(z3.shape[0], batch, R_HEADS, R_DK)

    def sec(s):
        return pl.BlockSpec((None, None, R_HEADS, R_DK), lambda b: (s, b, 0, 0))

    tab = pl.BlockSpec((8, R_DK // 2), lambda b: (0, 0))
    st = pl.BlockSpec((None, R_HEADS, R_DK, R_DV), lambda b: (b, 0, 0, 0))
    return pl.pallas_call(
        _ret_step_body,
        grid=(batch,),
        in_specs=[sec(SEC_QR), sec(SEC_KR), sec(SEC_VR), sec(SEC_GR), tab, tab, st],
        out_specs=[pl.BlockSpec((None, R_HEADS, R_DV), lambda b: (b, 0, 0)), st],
        out_shape=[jax.ShapeDtypeStruct((batch, R_HEADS, R_DV), F32),
                   jax.ShapeDtypeStruct(state.shape, F32)],
        compiler_params=_params("parallel"),
        name="retention_step",
    )(z4, z4, z4, z4, cos, sin, state)


def _mem_body(q_ref, k_ref, v_ref, o_ref):
    outs = []
    for h in range(M_HEADS):
        dh = slice(h * M_DH, (h + 1) * M_DH)
        s = lax.dot_general(q_ref[:, dh].astype(BF16), k_ref[:, dh].astype(BF16), NT_DIMS,
                            preferred_element_type=F32) * (M_DH ** -0.5)
        m = jnp.max(s, axis=-1, keepdims=True)
        p = jnp.exp(s - m)
        l = jnp.sum(p, axis=-1, keepdims=True)
        outs.append((_dot(p.astype(BF16), v_ref[:, dh].astype(BF16)) / l).astype(o_ref.dtype))
    o_ref[...] = jnp.concatenate(outs, axis=1)


def _mem_attend(z3, mk, mv, batch, seq):
    n_mem = mk.shape[0] // batch
    width = M_HEADS * M_DH
    ts = _tile(seq, 512)
    nt = seq // ts
    kv = pl.BlockSpec((n_mem, width), lambda b, t: (b, 0))
    return pl.pallas_call(
        _mem_body,
        grid=(batch, nt),
        in_specs=[pl.BlockSpec((None, ts, width), lambda b, t: (SEC_QM, b * nt + t, 0)), kv, kv],
        out_specs=pl.BlockSpec((ts, width), lambda b, t: (b * nt + t, 0)),
        out_shape=jax.ShapeDtypeStruct((batch * seq, width), BF16),
        compiler_params=_params("parallel", "arbitrary"),
        name="mem_attend",
    )(z3, mk, mv)


def _merge_body(h_ref, *refs):
    x_refs, wg_refs, bg_refs, wb_refs, o_ref = refs[0:3], refs[3:6], refs[6:9], refs[9:12], refs[12]
    h = h_ref[...]
    merged = None
    for g in range(3):
        term = jax.nn.sigmoid(_dot(h, wg_refs[g][...]) + bg_refs[g][...]) * _dot(x_refs[g][...], wb_refs[g][...])
        merged = term if merged is None else merged + term
    o_ref[...] = merged.astype(o_ref.dtype)


def _merge(h, branches, w_gate, b_gate, branch_ws):
    rows, d = h.shape
    tm, tn = _tile(rows, 1024), _tile(d, 512)
    nj = d // tn
    b_gate = b_gate.reshape(1, 3 * d)

    def row(width):
        return pl.BlockSpec((tm, width), lambda i, j: (i, 0))

    def cols(n_rows, group=0):
        return pl.BlockSpec((n_rows, tn), lambda i, j: (0, group * nj + j))

    return pl.pallas_call(
        _merge_body,
        grid=(rows // tm, nj),
        in_specs=([row(d)] + [row(x.shape[1]) for x in branches]
                  + [cols(d, g) for g in range(3)] + [cols(1, g) for g in range(3)]
                  + [cols(w.shape[0]) for w in branch_ws]),
        out_specs=pl.BlockSpec((tm, tn), lambda i, j: (i, j)),
        out_shape=jax.ShapeDtypeStruct((rows, d), BF16),
        compiler_params=_params("parallel", "arbitrary"),
        name="gated_merge",
    )(h, *branches, w_gate, w_gate, w_gate, b_gate, b_gate, b_gate, *branch_ws)


def _moba_gate_body(pt_ref, q_ref, *refs, bps):
    del pt_ref
    k_refs, gates_ref = refs[:-1], refs[-1]
    n = pl.program_id(1)

    @pl.when(n == 0)
    def _():
        gates_ref[...] = jnp.zeros_like(gates_ref)

    gates_ref[...] = _block_gates(q_ref[...], k_refs, n * bps, gates_ref[...])


def _moba_select_body(g_ref, sel_ref, *, nb_past):
    lane = lax.broadcasted_iota(I32, g_ref.shape, 1)
    gw = jnp.where(lane < nb_past, g_ref[...], NEG_INF)
    sel = jnp.zeros(gw.shape, I32)
    for r in range(MOBA_TOPK):
        mx = jnp.max(gw, axis=1, keepdims=True)
        idx = jnp.min(jnp.where(gw == mx, lane, 128), axis=1, keepdims=True)
        sel = jnp.where(lane == r, idx, sel)
        gw = jnp.where(lane == idx, -jnp.inf, gw)
    sel_ref[...] = sel


def _moba_attend_body(pt_ref, sel_ref, t5_ref, q_ref, kn_ref, vn_ref, ck_ref, cv_ref, o_ref, kbuf, vbuf, sem,
                      *, past_len):
    b = pl.program_id(0)
    slot = lax.rem(b, 2)
    scale = A_DH ** -0.5
    n_tiles = MOBA_TOPK * PAGES_PER_BLOCK

    def tile_copies(bb, to_slot):
        copies = []
        for h in range(A_HEADS):
            for r in range(MOBA_TOPK):
                block = sel_ref[bb, h * MOBA_TOPK + r]
                for t in range(PAGES_PER_BLOCK):
                    page = pt_ref[bb, block * PAGES_PER_BLOCK + t]
                    j = r * PAGES_PER_BLOCK + t
                    copies.append(pltpu.make_async_copy(ck_ref.at[page, :, h, :], kbuf.at[to_slot, h, j],
                                                        sem.at[to_slot]))
                    copies.append(pltpu.make_async_copy(cv_ref.at[page, :, h, :], vbuf.at[to_slot, h, j],
                                                        sem.at[to_slot]))
        return copies

    def start_all(copies):
        for i, cp in enumerate(copies):
            cp.start(priority=i % 2)

    @pl.when(b == 0)
    def _():
        start_all(tile_copies(0, 0))

    @pl.when(b + 1 < pl.num_programs(0))
    def _():
        start_all(tile_copies(b + 1, 1 - slot))

    for cp in tile_copies(b, slot):
        cp.wait()

    q, kn, vn = q_ref[...], kn_ref[...], vn_ref[...]
    outs = []
    lane = lax.broadcasted_iota(I32, (1, PAGE_SIZE), 1)
    for h in range(A_HEADS):
        qh = q[h:h + 1, :]
        q8 = jnp.broadcast_to(qh, (8, A_DH)).astype(BF16)
        s = []
        for r in range(MOBA_TOPK):
            block = sel_ref[b, h * MOBA_TOPK + r]
            for t in range(PAGES_PER_BLOCK):
                kt = kbuf[slot, h, r * PAGES_PER_BLOCK + t].astype(BF16)
                raw = lax.dot_general(q8, kt, NT_DIMS, preferred_element_type=F32)[0:1, :]
                bucket = _t5_bucket(past_len - (block * MOBA_BLOCK + t * PAGE_SIZE + lane))
                bias = jnp.zeros((1, PAGE_SIZE), F32)
                for bkt in range(T5_BUCKETS):
                    bias = jnp.where(bucket == bkt, t5_ref[bkt, h], bias)
                s.append(raw * scale + bias)
        s_own = jnp.sum(qh * kn[h:h + 1, :], axis=1, keepdims=True) * scale + t5_ref[0, h]
        m = jnp.maximum(jnp.max(functools.reduce(jnp.maximum, s), axis=1, keepdims=True), s_own)
        p = [jnp.exp(x - m) for x in s]
        p_own = jnp.exp(s_own - m)
        l = jnp.sum(functools.reduce(jnp.add, p), axis=1, keepdims=True) + p_own
        acc = p_own * vn[h:h + 1, :]
        for j in range(n_tiles):
            p8 = jnp.broadcast_to(p[j], (8, PAGE_SIZE)).astype(BF16)
            acc = acc + _dot(p8, vbuf[slot, h, j].astype(BF16))[0:1, :]
        outs.append(acc / l)
    o_ref[...] = jnp.concatenate(outs, axis=0)


def _moba_gates(q4, cache_k, page_table):
    batch, n_pages = page_table.shape
    nb_past = n_pages // PAGES_PER_BLOCK
    bps = _tile(nb_past, 8)

    def page(t):
        return pl.BlockSpec((None, PAGE_SIZE, A_HEADS, A_DH),
                            lambda b, n, pt: (pt[b, n * bps * PAGES_PER_BLOCK + t], 0, 0, 0))

    pages = [page(t) for t in range(bps * PAGES_PER_BLOCK)]
    return pl.pallas_call(
        functools.partial(_moba_gate_body, bps=bps),
        grid_spec=pltpu.PrefetchScalarGridSpec(
            num_scalar_prefetch=1,
            grid=(batch, nb_past // bps),
            in_specs=[pl.BlockSpec((None, None, A_HEADS, A_DH), lambda b, n, pt: (SEC_QA, b, 0, 0))] + pages,
            out_specs=pl.BlockSpec((None, A_HEADS, 128), lambda b, n, pt: (b, 0, 0)),
        ),
        out_shape=jax.ShapeDtypeStruct((batch, A_HEADS, 128), F32),
        compiler_params=_params("parallel", "arbitrary"),
        name="moba_gate",
    )(page_table, q4, *([cache_k] * len(pages)))


def _moba_step(z4, k_new, v_new, cache_k, cache_v, page_table, t5_table, gates=None):
    batch, n_pages = page_table.shape
    past_len = n_pages * PAGE_SIZE
    assert past_len % MOBA_BLOCK == 0 and MOBA_BLOCK % PAGE_SIZE == 0
    nb_past = past_len // MOBA_BLOCK
    assert MOBA_TOPK <= nb_past <= 128
    k_new = k_new.reshape(batch, A_HEADS, A_DH)
    v_new = v_new.reshape(batch, A_HEADS, A_DH)
    if gates is None:
        gates = _moba_gates(z4, cache_k, page_table)
    sel = pl.pallas_call(
        functools.partial(_moba_select_body, nb_past=nb_past),
        out_shape=jax.ShapeDtypeStruct((batch * A_HEADS, 128), I32),
        name="moba_select",
    )(gates.reshape(batch * A_HEADS, 128))
    sel = sel[:, :MOBA_TOPK].reshape(batch, A_HEADS * MOBA_TOPK)

    new_tok = pl.BlockSpec((None, A_HEADS, A_DH), lambda b, pt, sl: (b, 0, 0))
    tiles = pltpu.VMEM((2, A_HEADS, MOBA_TOPK * PAGES_PER_BLOCK, PAGE_SIZE, A_DH), F32)
    return pl.pallas_call(
        functools.partial(_moba_attend_body, past_len=past_len),
        grid_spec=pltpu.PrefetchScalarGridSpec(
            num_scalar_prefetch=2,
            grid=(batch,),
            in_specs=[pl.BlockSpec(memory_space=pltpu.SMEM),
                      pl.BlockSpec((None, None, A_HEADS, A_DH), lambda b, pt, sl: (SEC_QA, b, 0, 0)),
                      new_tok, new_tok,
                      pl.BlockSpec(memory_space=pl.ANY), pl.BlockSpec(memory_space=pl.ANY)],
            out_specs=pl.BlockSpec((None, A_HEADS, A_DH), lambda b, pt, sl: (b, 0, 0)),
            scratch_shapes=[tiles, tiles, pltpu.SemaphoreType.DMA((2,))],
        ),
        out_shape=jax.ShapeDtypeStruct((batch, A_HEADS, A_DH), F32),
        compiler_params=_params("arbitrary"),
        name="moba_attend",
    )(page_table, sel, t5_table, z4, k_new, v_new, cache_k, cache_v)


def kernel(x_prompt, x_sample, mem_prompt, cache_k, cache_v, cache_mem_k, cache_mem_v, state_ret, page_table, t5_table, ffn1_norm, ffn1_w1, ffn1_w3, ffn1_w2, mix_norm, mem_norm, w_in, w_mem_kv, w_gate, b_gate, w_br_moba, w_br_ret, w_br_mem, w_out, ffn2_norm, ffn2_w1, ffn2_w3, ffn2_w2, final_norm):
    batch, seq, d = x_prompt.shape
    dec_batch, dec_seq, _ = x_sample.shape
    assert dec_seq == 1
    depth = w_in.shape[0]
    n_mem = mem_prompt.shape[1]
    past_len = page_table.shape[1] * PAGE_SIZE
    cos_p, sin_p = _rope_tables(0, seq)
    cos_s, sin_s = _rope_tables(past_len, dec_seq)

    xp = x_prompt.reshape(batch * seq, d)
    xs = x_sample.reshape(dec_batch, d)
    outs = [[] for _ in range(8)]
    for l in range(depth):
        last = l == depth - 1
        g_next = final_norm if last else ffn1_norm[l + 1]

        x1s, h2s, *f1_b = _ffn_half(xs, ffn1_norm[l], ffn1_w1[l], ffn1_w3[l], ffn1_w2[l], mix_norm[l],
                                    emit_x=True, post_dtype=BF16, emit_w=True)
        later_ws = [w_in[l], w_gate[l], w_br_moba[l], w_br_ret[l], w_br_mem[l], w_out[l],
                    ffn2_w1[l], ffn2_w3[l], ffn2_w2[l]]
        x1, h2, *later_b = _ffn_half(xp, ffn1_norm[l], *f1_b, mix_norm[l], emit_x=True, post_dtype=BF16,
                                     cast_jobs=later_ws)
        w_in_b, w_gate_b, w_out_b = later_b[0], later_b[1], later_b[5]
        branch_b, f2_b = later_b[2:5], later_b[6:9]
        z3s, ka_s, va_s = _in_proj(h2s, w_in_b)
        z3, ka, va = _in_proj(h2, w_in_b)

        z4s = z3s.reshape(z3s.shape[0], dec_batch, A_HEADS, A_DH)
        oa, gates_s = _moba_prompt(z3, ka, va, t5_table, batch, seq, paged=(page_table, z4s, cache_k[l]))
        oa_s = _moba_step(z4s, ka_s, va_s, cache_k[l], cache_v[l], page_table, t5_table, gates=gates_s)

        o_rs, s_new_s = _ret_step(z3s, cos_s, sin_s, state_ret[l])
        z3s_pad = jnp.broadcast_to(z3s[:, :, None, :], (z3s.shape[0], dec_batch, 8, SEC))
        z3s_pad = z3s_pad.reshape(z3s.shape[0], dec_batch * 8, SEC)
        om_s = _mem_attend(z3s_pad, cache_mem_k[l].reshape(dec_batch * n_mem, M_HEADS * M_DH),
                           cache_mem_v[l].reshape(dec_batch * n_mem, M_HEADS * M_DH), dec_batch, 8)
        om_s = om_s.reshape(dec_batch, 8, M_HEADS * M_DH)[:, 0, :]
        branches_s = [oa_s.reshape(dec_batch, -1).astype(BF16), o_rs.reshape(dec_batch, -1).astype(BF16), om_s]

        mem_h = _norm_rows(mem_prompt.reshape(batch * n_mem, d), mem_norm[l], BF16)
        mkv_p = _matmul(mem_h, w_mem_kv[l], tn_pref=512, name="mem_kv")
        mk_p, mv_p = mkv_p[:, :M_HEADS * M_DH], mkv_p[:, M_HEADS * M_DH:]
        o_r, s_new_p = _ret_prompt(z3, cos_p, sin_p, batch, seq)
        om = _mem_attend(z3, mk_p, mv_p, batch, seq)

        merged_s = _merge(h2s, branches_s, w_gate_b, b_gate[l], branch_b)
        merged = _merge(h2, [oa, o_r, om], w_gate_b, b_gate[l], branch_b)
        x2s = _matmul(merged_s, w_out_b, residual=x1s, name="out_proj")
        x2 = _matmul(merged, w_out_b, residual=x1, name="out_proj")
        xs = _ffn_half(x2s, ffn2_norm[l], *f2_b, g_next, emit_x=not last, post_dtype=F32)[0]
        xp = _ffn_half(x2, ffn2_norm[l], *f2_b, g_next, emit_x=not last, post_dtype=F32)[0]

        new = (ka.reshape(batch, seq, A_HEADS, A_DH), va.reshape(batch, seq, A_HEADS, A_DH),
               mk_p.reshape(batch, n_mem, M_HEADS, M_DH), mv_p.reshape(batch, n_mem, M_HEADS, M_DH), s_new_p,
               ka_s.reshape(dec_batch, dec_seq, A_HEADS, A_DH), va_s.reshape(dec_batch, dec_seq, A_HEADS, A_DH),
               s_new_s)
        for acc, val in zip(outs, new):
            acc.append(val)

    y_prompt = xp.reshape(batch, seq, d)
    y_sample = xs.reshape(dec_batch, dec_seq, d)
    return (y_prompt, y_sample) + tuple(jnp.stack(o) for o in outs)
```

```python
import functools
import math

import jax
import jax.numpy as jnp
import numpy as np
from jax import lax
from jax.experimental import pallas as pl
from jax.experimental.pallas import tpu as pltpu

F32 = jnp.float32
BF16 = jnp.bfloat16
I32 = jnp.int32

A_HEADS, A_DH = 8, 128
MOBA_BLOCK, MOBA_TOPK = 256, 3
T5_BUCKETS, T5_MAX_DIST = 32, 128
R_HEADS, R_DK, R_DV = 4, 256, 256
RET_CHUNK = 128
RET_ROPE_BASE = 10000.0
M_HEADS, M_DH = 4, 256
PAGE_SIZE = 128
EPS = 1e-6
NEG_INF = -1e30
LOG2E = math.log2(math.e)
SEC = 1024
W_IN_KA, W_IN_VA = 1, 2
(SEC_QA, SEC_QR, SEC_KR, SEC_VR, SEC_GR, SEC_QM) = range(6)

V7X_VMEM_LIMIT_BYTES = 56 * 1024 * 1024

NT_DIMS = (((1,), (1,)), ((), ()))
TN_DIMS = (((0,), (0,)), ((), ()))


def _params(*sem):
    return pltpu.CompilerParams(dimension_semantics=sem, vmem_limit_bytes=V7X_VMEM_LIMIT_BYTES)


def _tile(n, pref):
    t = min(n, pref)
    while n % t:
        t -= 1
    return t


def _rms(x, g):
    return x * lax.rsqrt(jnp.mean(x * x, axis=-1, keepdims=True) + EPS) * g


def _silu(x):
    return x * jax.nn.sigmoid(x)


def _dot(a, b):
    return jnp.dot(a, b, preferred_element_type=F32)


def _bf16(w):
    return w if w.dtype == BF16 else w.astype(BF16)


def _ffn_body(x_ref, g_ref, w1_ref, w3_ref, w2_ref, gp_ref, *refs, emit_x, emit_w, n_cast):
    refs = list(refs)
    cast_in = [refs.pop(0) for _ in range(n_cast)]
    xo_ref = refs.pop(0) if emit_x else None
    ho_ref = refs.pop(0)
    wb_refs = [refs.pop(0) for _ in range(3)] if emit_w else []
    cast_out = [refs.pop(0) for _ in range(n_cast)]
    h_sc, acc_sc = refs
    f = pl.program_id(1)

    @pl.when(f == 0)
    def _():
        h_sc[...] = _rms(x_ref[...], g_ref[...]).astype(BF16)
        acc_sc[...] = jnp.zeros_like(acc_sc)

    _cast_rows(cast_in, cast_out)

    w1, w3, w2 = _bf16(w1_ref[...]), _bf16(w3_ref[...]), _bf16(w2_ref[...])
    for wb_ref, w in zip(wb_refs, (w1, w3, w2)):
        wb_ref[...] = w
    h = h_sc[...]
    t = _silu(_dot(h, w1)) * _dot(h, w3)
    acc_sc[...] += _dot(t.astype(BF16), w2)

    @pl.when(f == pl.num_programs(1) - 1)
    def _():
        xo = x_ref[...] + 0.5 * acc_sc[...]
        if emit_x:
            xo_ref[...] = xo
        ho_ref[...] = _rms(xo, gp_ref[...]).astype(ho_ref.dtype)


BF16_SUBLANES = 16


def _cast_spec(a, grid):
    n_steps = grid[0] * grid[1]
    n_chunks = max(n for n in range(1, n_steps + 1)
                   if a.shape[0] % n == 0 and (a.shape[0] // n) % BF16_SUBLANES == 0)
    return pl.BlockSpec((a.shape[0] // n_chunks, a.shape[1]),
                        lambda i, j: (jnp.minimum(i * grid[1] + j, n_chunks - 1), 0))


def _cast_rows(src_refs, dst_refs):
    for src_ref, dst_ref in zip(src_refs, dst_refs):
        dst_ref[...] = src_ref[...].astype(BF16)


def _ffn_half(x, g, w1, w3, w2, g_post, *, emit_x, post_dtype, emit_w=False, cast_jobs=()):
    rows, d = x.shape
    ff = w1.shape[1]
    tm, tf = _tile(rows, 512), _tile(ff, 512)
    assert not emit_w or rows == tm
    grid = (rows // tm, ff // tf)
    row_spec = pl.BlockSpec((tm, d), lambda i, f: (i, 0))
    vec_spec = pl.BlockSpec((1, d), lambda i, f: (0, 0))
    w13_spec = pl.BlockSpec((d, tf), lambda i, f: (0, f))
    w2_spec = pl.BlockSpec((tf, d), lambda i, f: (f, 0))
    out_shape = [jax.ShapeDtypeStruct((rows, d), post_dtype)]
    out_specs = [row_spec]
    if emit_x:
        out_shape = [jax.ShapeDtypeStruct((rows, d), F32)] + out_shape
        out_specs = [row_spec] + out_specs
    if emit_w:
        out_shape += [jax.ShapeDtypeStruct(w.shape, BF16) for w in (w1, w3, w2)]
        out_specs += [w13_spec, w13_spec, w2_spec]

    cast_specs = [_cast_spec(a, grid) for a in cast_jobs]
    out_shape += [jax.ShapeDtypeStruct(a.shape, BF16) for a in cast_jobs]
    out_specs += cast_specs
    return pl.pallas_call(
        functools.partial(_ffn_body, emit_x=emit_x, emit_w=emit_w, n_cast=len(cast_jobs)),
        grid=grid,
        in_specs=[row_spec, vec_spec, w13_spec, w13_spec, w2_spec, vec_spec] + cast_specs,
        out_specs=out_specs,
        out_shape=out_shape,
        scratch_shapes=[pltpu.VMEM((tm, d), BF16), pltpu.VMEM((tm, d), F32)],
        compiler_params=_params("parallel", "arbitrary"),
        name="ffn_half",
    )(x, g.reshape(1, d), w1, w3, w2, g_post.reshape(1, d), *cast_jobs)


def _norm_body(x_ref, g_ref, o_ref):
    o_ref[...] = _rms(x_ref[...], g_ref[...]).astype(o_ref.dtype)


def _norm_rows(x, g, out_dtype):
    rows, d = x.shape
    tm = _tile(rows, 512)
    return pl.pallas_call(
        _norm_body,
        grid=(rows // tm,),
        in_specs=[pl.BlockSpec((tm, d), lambda i: (i, 0)), pl.BlockSpec((1, d), lambda i: (0, 0))],
        out_specs=pl.BlockSpec((tm, d), lambda i: (i, 0)),
        out_shape=jax.ShapeDtypeStruct((rows, d), out_dtype),
        compiler_params=_params("parallel"),
        name="rmsnorm_rows",
    )(x, g.reshape(1, d))


def _mm_body(a_ref, w_ref, *refs, has_residual):
    o_ref = refs[-1]
    acc = _dot(a_ref[...], _bf16(w_ref[...]))
    if has_residual:
        acc = refs[0][...] + acc
    o_ref[...] = acc.astype(o_ref.dtype)


def _in_proj_body(a_ref, w_ref, *refs):
    n_cast = (len(refs) - 3) // 2
    z_ref, k_ref, v_ref = refs[n_cast:n_cast + 3]
    _cast_rows(refs[:n_cast], refs[n_cast + 3:])
    j = pl.program_id(1)
    acc = _dot(a_ref[...], w_ref[...])

    @pl.when(j == W_IN_KA)
    def _():
        k_ref[...] = acc

    @pl.when(j == W_IN_VA)
    def _():
        v_ref[...] = acc

    @pl.when((j != W_IN_KA) & (j != W_IN_VA))
    def _():
        z_ref[...] = acc


def _in_proj(a, w, cast_jobs=()):
    rows, k = a.shape
    n_sec = w.shape[1] // SEC
    assert (W_IN_KA, W_IN_VA) == (1, 2) and n_sec == 8
    tm = _tile(rows, 1024)
    grid = (rows // tm, n_sec)
    kv_spec = pl.BlockSpec((tm, SEC), lambda i, j: (i, 0))
    cast_specs = [_cast_spec(c, grid) for c in cast_jobs]
    return pl.pallas_call(
        _in_proj_body,
        grid=grid,
        in_specs=[pl.BlockSpec((tm, k), lambda i, j: (i, 0)), pl.BlockSpec((k, SEC), lambda i, j: (0, j))]
        + cast_specs,
        out_specs=[pl.BlockSpec((None, tm, SEC), lambda i, j: (jnp.maximum(j - 2, 0), i, 0)), kv_spec, kv_spec]
        + cast_specs,
        out_shape=[jax.ShapeDtypeStruct((n_sec - 2, rows, SEC), F32)] + [jax.ShapeDtypeStruct((rows, SEC), F32)] * 2
        + [jax.ShapeDtypeStruct(c.shape, BF16) for c in cast_jobs],
        compiler_params=_params("parallel", "arbitrary"),
        name="in_proj",
    )(a, w, *cast_jobs)


def _matmul(a, w, *, residual=None, out_dtype=F32, tn_pref=1024, name="matmul"):
    rows, k = a.shape
    n_cols = w.shape[1]
    tm, tn = _tile(rows, 1024), _tile(n_cols, tn_pref)
    in_specs = [pl.BlockSpec((tm, k), lambda i, j: (i, 0)), pl.BlockSpec((k, tn), lambda i, j: (0, j))]
    out_spec = pl.BlockSpec((tm, tn), lambda i, j: (i, j))
    args = [a, w]
    if residual is not None:
        in_specs.append(out_spec)
        args.append(residual)
    return pl.pallas_call(
        functools.partial(_mm_body, has_residual=residual is not None),
        grid=(rows // tm, n_cols // tn),
        in_specs=in_specs,
        out_specs=out_spec,
        out_shape=jax.ShapeDtypeStruct((rows, n_cols), out_dtype),
        compiler_params=_params("parallel", "arbitrary"),
        name=name,
    )(*args)


def _t5_bucket(n):
    n = jnp.maximum(n, 0)
    max_exact = T5_BUCKETS // 2
    nf = jnp.maximum(n, 1).astype(F32)
    large = max_exact + (jnp.log(nf / max_exact) / math.log(T5_MAX_DIST / max_exact)
                         * (T5_BUCKETS - max_exact)).astype(I32)
    return jnp.where(n < max_exact, n, jnp.minimum(large, T5_BUCKETS - 1))


PAGES_PER_BLOCK = MOBA_BLOCK // PAGE_SIZE
SUBLANES = 8


def _block_gates(q, page_refs, first_block, gates):
    lane = lax.broadcasted_iota(I32, gates.shape, 1)
    for i in range(len(page_refs) // PAGES_PER_BLOCK):
        pages = [page_refs[i * PAGES_PER_BLOCK + t][...] for t in range(PAGES_PER_BLOCK)]
        ksum = functools.reduce(jnp.add, [jnp.sum(x, axis=0) for x in pages])
        gate = jnp.sum(q * ksum, axis=1, keepdims=True) / MOBA_BLOCK
        gates = jnp.where(lane == first_block + i, gate, gates)
    return gates


PAGE_RING = 3


def _moba_prompt_body(*refs, nb, side_blocks, grid):
    h, b, c = pl.program_id(0), pl.program_id(1), pl.program_id(2)
    if side_blocks:
        (pt_ref, t5_ref, q_ref, k_ref, v_ref, qside_ref, cache_ref, o_ref, gates_ref,
         kb_sc, vb_sc, kdiff_sc, bias_sc, pages_sc, page_sem) = refs
        n_pages = side_blocks * PAGES_PER_BLOCK
        steps_per_seq = grid[2]
        n_steps = grid[0] * grid[1] * grid[2]
        step_idx = (h * grid[1] + b) * steps_per_seq + c

        def page_copies(s):
            seq, part, slot = s // steps_per_seq, s % steps_per_seq, s % PAGE_RING
            return [pltpu.make_async_copy(cache_ref.at[pt_ref[seq, part * n_pages + t]], pages_sc.at[slot, t],
                                          page_sem.at[slot]) for t in range(n_pages)]

        @pl.when(step_idx == 0)
        def _():
            for s in range(min(PAGE_RING - 1, n_steps)):
                for cp in page_copies(s):
                    cp.start()

        @pl.when(step_idx + PAGE_RING - 1 < n_steps)
        def _():
            for cp in page_copies(step_idx + PAGE_RING - 1):
                cp.start()

        for cp in page_copies(step_idx):
            cp.wait()
        slot = step_idx % PAGE_RING
        page_refs = [pages_sc.at[slot, t] for t in range(n_pages)]
    else:
        t5_ref, q_ref, k_ref, v_ref, o_ref, kb_sc, vb_sc, kdiff_sc, bias_sc = refs
    blk = MOBA_BLOCK
    nbp = SUBLANES
    assert nb <= nbp
    scale = A_DH ** -0.5
    ii = lax.broadcasted_iota(I32, (blk, blk), 0)
    jj = lax.broadcasted_iota(I32, (blk, blk), 1)

    @pl.when((b == 0) & (c == 0))
    def _():
        for t in range(2):
            bucket = _t5_bucket(t * blk + ii - jj)
            tile = jnp.zeros((blk, blk), F32)
            for bkt in range(T5_BUCKETS):
                tile = jnp.where(bucket == bkt, t5_ref[bkt, h], tile)
            tile = tile * LOG2E
            bias_sc[t] = jnp.where(ii >= jj, tile, NEG_INF) if t == 0 else tile

    @pl.when(c == 0)
    def _():
        km = jnp.concatenate(
            [jnp.mean(k_ref[n * blk:(n + 1) * blk, :], axis=0, keepdims=True) for n in range(nb)]
            + [jnp.zeros((1, A_DH), F32)] * (nbp - nb), axis=0)
        for n in range(nb):
            kdiff_sc[n * nbp:(n + 1) * nbp, :] = km[n:n + 1, :] - km
            kb_sc[n * blk:(n + 1) * blk, A_DH:2 * A_DH] = jnp.where(
                lax.broadcasted_iota(I32, (blk, 128), 1) == n, 1.0, 0.0).astype(BF16)
        kb_sc[:, 0:A_DH] = k_ref[...].astype(BF16)
        vb_sc[:, 0:A_DH] = v_ref[...].astype(BF16)
        vb_sc[:, A_DH:2 * A_DH] = jnp.ones((nb * blk, A_DH), BF16)
        if side_blocks:
            gates_ref[...] = jnp.zeros_like(gates_ref)

    far_bias = t5_ref[T5_BUCKETS - 1, h] * LOG2E

    def attend(cc):
        q = q_ref[cc * blk:(cc + 1) * blk, :]
        qs = (q * (scale * LOG2E)).astype(BF16)
        n_keys = (cc + 1) * blk
        if cc <= MOBA_TOPK:
            s_all = lax.dot_general(qs, kb_sc[0:n_keys, 0:A_DH], NT_DIMS, preferred_element_type=F32)
        else:
            diff_t = lax.dot_general(kdiff_sc[0:cc * nbp, :], q, NT_DIMS, precision=lax.Precision.HIGHEST,
                                     preferred_element_type=F32)
            pair = lax.broadcasted_iota(I32, diff_t.shape, 0)
            m_of, n_of = jnp.right_shift(pair, int(math.log2(nbp))), jnp.bitwise_and(pair, nbp - 1)
            beats_t = jnp.where(diff_t > 0, 1.0, jnp.where(diff_t == 0, jnp.where(n_of > m_of, 1.0, 0.0), 0.0))
            fold = jnp.where(jnp.bitwise_and(lax.broadcasted_iota(I32, (cc * nbp, 128), 0), nbp - 1)
                             == lax.broadcasted_iota(I32, (cc * nbp, 128), 1), 1.0, 0.0)
            rank = lax.dot_general(beats_t.astype(BF16), fold.astype(BF16), TN_DIMS,
                                   preferred_element_type=F32)
            lane = lax.broadcasted_iota(I32, rank.shape, 1)
            sel = jnp.where(lane < cc, jnp.where(rank < MOBA_TOPK, 0.0, NEG_INF), 0.0)
            q_sel = jnp.concatenate([qs, sel.astype(BF16)], axis=1)
            s_all = lax.dot_general(q_sel, kb_sc[0:n_keys, :], NT_DIMS, preferred_element_type=F32)
        tiles = []
        for n in range(cc + 1):
            t = s_all[:, n * blk:(n + 1) * blk]
            tiles.append(t + (bias_sc[0] if n == cc else bias_sc[1] if n == cc - 1 else far_bias))
        m = jnp.max(functools.reduce(jnp.maximum, tiles), axis=1, keepdims=True)
        p_all = jnp.concatenate([jnp.exp2(t - m).astype(BF16) for t in tiles], axis=1)
        pv = _dot(p_all, vb_sc[0:n_keys, :])
        o_ref[cc * blk:(cc + 1) * blk, :] = (pv[:, 0:A_DH] / pv[:, A_DH:2 * A_DH]).astype(o_ref.dtype)

    def step(s):
        if side_blocks:
            gates_ref[...] = _block_gates(qside_ref[...], page_refs, s * side_blocks, gates_ref[...])
        for cc in sorted({s, nb - 1 - s}):
            attend(cc)

    for s in range(_moba_steps(nb)):
        pl.when(c == s)(functools.partial(step, s))


def _moba_steps(nb):
    return (nb + 1) // 2


def _moba_prompt(z3, k, v, t5_table, batch, seq, paged=None):
    blk = MOBA_BLOCK
    nb = seq // blk
    steps = _moba_steps(nb)
    side_blocks = 0
    if paged is not None:
        page_table, q_side, cache_k = paged
        nb_past = page_table.shape[1] // PAGES_PER_BLOCK
        if page_table.shape[0] == A_HEADS * batch and nb_past % steps == 0:
            side_blocks = nb_past // steps

    seq_spec = pl.BlockSpec((seq, A_DH), lambda h, b, c, *_: (b, h))
    in_specs = [pl.BlockSpec(memory_space=pltpu.SMEM),
                pl.BlockSpec((None, seq, A_DH), lambda h, b, c, *_: (SEC_QA, b, h)), seq_spec, seq_spec]
    out_specs = [seq_spec]
    out_shape = [jax.ShapeDtypeStruct((batch * seq, A_HEADS * A_DH), BF16)]
    args = [t5_table, z3, k, v]
    scratch = [pltpu.VMEM((seq, 2 * A_DH), BF16), pltpu.VMEM((seq, 2 * A_DH), BF16),
               pltpu.VMEM((SUBLANES * SUBLANES, A_DH), F32), pltpu.VMEM((2, blk, blk), F32)]
    grid = (A_HEADS, batch, steps)
    if side_blocks:
        in_specs += [pl.BlockSpec((None, None, A_HEADS, A_DH), lambda h, b, c, pt: (SEC_QA, h * batch + b, 0, 0)),
                     pl.BlockSpec(memory_space=pl.ANY)]
        out_specs.append(pl.BlockSpec((None, A_HEADS, 128), lambda h, b, c, pt: (h * batch + b, 0, 0)))
        out_shape.append(jax.ShapeDtypeStruct((A_HEADS * batch, A_HEADS, 128), F32))
        args = [page_table] + args + [q_side, cache_k]
        scratch += [pltpu.VMEM((PAGE_RING, side_blocks * PAGES_PER_BLOCK, PAGE_SIZE, A_HEADS, A_DH), F32),
                    pltpu.SemaphoreType.DMA((PAGE_RING,))]
    out = pl.pallas_call(
        functools.partial(_moba_prompt_body, nb=nb, side_blocks=side_blocks, grid=grid),
        grid_spec=pltpu.PrefetchScalarGridSpec(
            num_scalar_prefetch=1 if side_blocks else 0,
            grid=grid,
            in_specs=in_specs,
            out_specs=out_specs,
            scratch_shapes=scratch,
        ),
        out_shape=out_shape,
        compiler_params=_params("arbitrary", "arbitrary", "arbitrary"),
        name="moba_prompt",
    )(*args)
    if paged is None:
        return out[0]
    return out[0], (out[1] if side_blocks else None)


def _rope_body(inv_ref, cos_ref, sin_ref, *, pos0):
    pos = pos0 + lax.broadcasted_iota(I32, cos_ref.shape, 0)
    ang = pos.astype(F32) * inv_ref[...]
    cos_ref[...] = jnp.cos(ang)
    sin_ref[...] = jnp.sin(ang)


def _rope_tables(pos0, n_pos):
    half = R_DK // 2
    inv = 1.0 / (RET_ROPE_BASE ** jnp.linspace(0.0, 1.0, half, dtype=F32))
    rows = -(-n_pos // 8) * 8
    return pl.pallas_call(
        functools.partial(_rope_body, pos0=pos0),
        out_shape=[jax.ShapeDtypeStruct((rows, half), F32)] * 2,
        name="rope_tables",
    )(inv.reshape(1, half))


def _rotate(x, cos, sin):
    half = x.shape[-1] // 2
    x1, x2 = x[:, :half], x[:, half:]
    return jnp.concatenate([x1 * cos - x2 * sin, x1 * sin + x2 * cos], axis=1)


def _log_decay(h, shape):
    hf = jnp.full(shape, h, I32).astype(F32)
    return jnp.log(1.0 - jnp.exp2(-5.0 - hf))


def _ret_prompt_body(q_ref, k_ref, v_ref, g_ref, cos_ref, sin_ref, o_ref, so_ref, s_sc, *, c):
    n = pl.program_id(1)

    @pl.when(n == 0)
    def _():
        s_sc[...] = jnp.zeros_like(s_sc)

    i = lax.broadcasted_iota(I32, (c, c), 0).astype(F32)
    j = lax.broadcasted_iota(I32, (c, c), 1).astype(F32)
    diff = i - j
    i_col = lax.broadcasted_iota(I32, (c, 1), 0).astype(F32)
    cos, sin = cos_ref[...], sin_ref[...]

    new_states, outs = [], []
    for h in range(R_HEADS):
        dk, dv = slice(h * R_DK, (h + 1) * R_DK), slice(h * R_DV, (h + 1) * R_DV)
        dmask = jnp.where(diff >= 0, jnp.exp(jnp.maximum(diff, 0.0) * _log_decay(h, (c, c))), 0.0)
        lg_col = _log_decay(h, (c, 1))
        q_dec = jnp.exp((i_col + 1.0) * lg_col)
        k_dec = jnp.exp((c - 1.0 - i_col) * lg_col)
        c_dec = jnp.exp(c * _log_decay(h, (1, R_DV)))
        qr = _rotate(q_ref[:, dk], cos, sin)
        kr = _rotate(k_ref[:, dk], cos, sin) * (R_DK ** -0.5)
        vb = v_ref[:, dv].astype(BF16)
        s = s_sc[h]
        att = lax.dot_general(qr.astype(BF16), kr.astype(BF16), NT_DIMS, preferred_element_type=F32) * dmask
        o = _dot(att.astype(BF16), vb) + _dot((qr * q_dec).astype(BF16), s.astype(BF16))
        new_states.append(s * c_dec + lax.dot_general((kr * k_dec).astype(BF16), vb, TN_DIMS,
                                                      preferred_element_type=F32))
        on = o * lax.rsqrt(jnp.mean(o * o, axis=-1, keepdims=True) + EPS)
        outs.append((on * _silu(g_ref[:, dv])).astype(o_ref.dtype))
    for h in range(R_HEADS):
        s_sc[h] = new_states[h]
    o_ref[...] = jnp.concatenate(outs, axis=1)

    @pl.when(n == pl.num_programs(1) - 1)
    def _():
        so_ref[...] = s_sc[...]


def _ret_prompt(z3, cos, sin, batch, seq):
    c = math.gcd(seq, RET_CHUNK)
    nc = seq // c

    def sec(s):
        return pl.BlockSpec((None, c, SEC), lambda b, n: (s, b * nc + n, 0))

    tab = pl.BlockSpec((c, R_DK // 2), lambda b, n: (n, 0))
    return pl.pallas_call(
        functools.partial(_ret_prompt_body, c=c),
        grid=(batch, nc),
        in_specs=[sec(SEC_QR), sec(SEC_KR), sec(SEC_VR), sec(SEC_GR), tab, tab],
        out_specs=[pl.BlockSpec((c, R_HEADS * R_DV), lambda b, n: (b * nc + n, 0)),
                   pl.BlockSpec((None, R_HEADS, R_DK, R_DV), lambda b, n: (b, 0, 0, 0))],
        out_shape=[jax.ShapeDtypeStruct((batch * seq, R_HEADS * R_DV), BF16),
                   jax.ShapeDtypeStruct((batch, R_HEADS, R_DK, R_DV), F32)],
        scratch_shapes=[pltpu.VMEM((R_HEADS, R_DK, R_DV), F32)],
        compiler_params=_params("parallel", "arbitrary"),
        name="retention_prompt",
    )(z3, z3, z3, z3, cos, sin)


def _ret_step_body(q_ref, k_ref, v_ref, g_ref, cos_ref, sin_ref, s_ref, o_ref, so_ref):
    cos, sin = cos_ref[0:1, :], sin_ref[0:1, :]
    rows = 16
    row0 = lax.broadcasted_iota(I32, (rows, R_DK), 0) == 0
    for h in range(R_HEADS):
        decay = jnp.exp(_log_decay(h, (1, R_DV)))
        qr = _rotate(q_ref[h:h + 1, :], cos, sin)
        kr = _rotate(k_ref[h:h + 1, :], cos, sin) * (R_DK ** -0.5)
        v = v_ref[h:h + 1, :]
        s = s_ref[h]
        q_rows = jnp.broadcast_to(qr * decay, (rows, R_DK)).astype(BF16)
        qs = _dot(q_rows, s.astype(BF16))[0:1, :]
        o = jnp.sum(qr * kr, axis=-1, keepdims=True) * v + qs
        k_rows = jnp.where(row0, jnp.broadcast_to(kr, (rows, R_DK)), 0.0).astype(BF16)
        v_rows = jnp.broadcast_to(v, (rows, R_DV)).astype(BF16)
        so_ref[h] = s * decay + lax.dot_general(k_rows, v_rows, TN_DIMS, preferred_element_type=F32)
        on = o * lax.rsqrt(jnp.mean(o * o, axis=-1, keepdims=True) + EPS)
        o_ref[h:h + 1, :] = (on * _silu(g_ref[h:h + 1, :])).astype(o_ref.dtype)


def _ret_step(z3, cos, sin, state):
    batch = z3.shape[1]
    z4 = z3.reshape(z3.shape[0], batch, R_HEADS, R_DK)

    def sec(s):
        return pl.BlockSpec((None, None, R_HEADS, R_DK), lambda b: (s, b, 0, 0))

    tab = pl.BlockSpec((8, R_DK // 2), lambda b: (0, 0))
    st = pl.BlockSpec((None, R_HEADS, R_DK, R_DV), lambda b: (b, 0, 0, 0))
    return pl.pallas_call(
        _ret_step_body,
        grid=(batch,),
        in_specs=[sec(SEC_QR), sec(SEC_KR), sec(SEC_VR), sec(SEC_GR), tab, tab, st],
        out_specs=[pl.BlockSpec((None, R_HEADS, R_DV), lambda b: (b, 0, 0)), st],
        out_shape=[jax.ShapeDtypeStruct((batch, R_HEADS, R_DV), F32),
                   jax.ShapeDtypeStruct(state.shape, F32)],
        compiler_params=_params("parallel"),
        name="retention_step",
    )(z4, z4, z4, z4, cos, sin, state)


def _mem_body(q_ref, k_ref, v_ref, o_ref):
    outs = []
    for h in range(M_HEADS):
        dh = slice(h * M_DH, (h + 1) * M_DH)
        s = lax.dot_general(q_ref[:, dh].astype(BF16), k_ref[:, dh].astype(BF16), NT_DIMS,
                            preferred_element_type=F32) * (M_DH ** -0.5)
        m = jnp.max(s, axis=-1, keepdims=True)
        p = jnp.exp(s - m)
        l = jnp.sum(p, axis=-1, keepdims=True)
        outs.append((_dot(p.astype(BF16), v_ref[:, dh].astype(BF16)) / l).astype(o_ref.dtype))
    o_ref[...] = jnp.concatenate(outs, axis=1)


def _mem_attend(z3, mk, mv, batch, seq):
    n_mem = mk.shape[0] // batch
    width = M_HEADS * M_DH
    ts = _tile(seq, 512)
    nt = seq // ts
    kv = pl.BlockSpec((n_mem, width), lambda b, t: (b, 0))
    return pl.pallas_call(
        _mem_body,
        grid=(batch, nt),
        in_specs=[pl.BlockSpec((None, ts, width), lambda b, t: (SEC_QM, b * nt + t, 0)), kv, kv],
        out_specs=pl.BlockSpec((ts, width), lambda b, t: (b * nt + t, 0)),
        out_shape=jax.ShapeDtypeStruct((batch * seq, width), BF16),
        compiler_params=_params("parallel", "arbitrary"),
        name="mem_attend",
    )(z3, mk, mv)


def _merge_body(h_ref, *refs):
    x_refs, wg_refs, bg_refs, wb_refs, o_ref = refs[0:3], refs[3:6], refs[6:9], refs[9:12], refs[12]
    h = h_ref[...]
    merged = None
    for g in range(3):
        term = jax.nn.sigmoid(_dot(h, wg_refs[g][...]) + bg_refs[g][...]) * _dot(x_refs[g][...], wb_refs[g][...])
        merged = term if merged is None else merged + term
    o_ref[...] = merged.astype(o_ref.dtype)


def _merge(h, branches, w_gate, b_gate, branch_ws):
    rows, d = h.shape
    tm, tn = _tile(rows, 1024), _tile(d, 512)
    nj = d // tn
    b_gate = b_gate.reshape(1, 3 * d)

    def row(width):
        return pl.BlockSpec((tm, width), lambda i, j: (i, 0))

    def cols(n_rows, group=0):
        return pl.BlockSpec((n_rows, tn), lambda i, j: (0, group * nj + j))

    return pl.pallas_call(
        _merge_body,
        grid=(rows // tm, nj),
        in_specs=([row(d)] + [row(x.shape[1]) for x in branches]
                  + [cols(d, g) for g in range(3)] + [cols(1, g) for g in range(3)]
                  + [cols(w.shape[0]) for w in branch_ws]),
        out_specs=pl.BlockSpec((tm, tn), lambda i, j: (i, j)),
        out_shape=jax.ShapeDtypeStruct((rows, d), BF16),
        compiler_params=_params("parallel", "arbitrary"),
        name="gated_merge",
    )(h, *branches, w_gate, w_gate, w_gate, b_gate, b_gate, b_gate, *branch_ws)


def _moba_gate_body(pt_ref, q_ref, *refs, bps):
    del pt_ref
    k_refs, gates_ref = refs[:-1], refs[-1]
    n = pl.program_id(1)

    @pl.when(n == 0)
    def _():
        gates_ref[...] = jnp.zeros_like(gates_ref)

    gates_ref[...] = _block_gates(q_ref[...], k_refs, n * bps, gates_ref[...])


def _moba_select_body(g_ref, sel_ref, *, nb_past):
    lane = lax.broadcasted_iota(I32, g_ref.shape, 1)
    gw = jnp.where(lane < nb_past, g_ref[...], NEG_INF)
    sel = jnp.zeros(gw.shape, I32)
    for r in range(MOBA_TOPK):
        mx = jnp.max(gw, axis=1, keepdims=True)
        idx = jnp.min(jnp.where(gw == mx, lane, 128), axis=1, keepdims=True)
        sel = jnp.where(lane == r, idx, sel)
        gw = jnp.where(lane == idx, -jnp.inf, gw)
    sel_ref[...] = sel


def _moba_attend_body(pt_ref, sel_ref, t5_ref, q_ref, kn_ref, vn_ref, ck_ref, cv_ref, o_ref, kbuf, vbuf, sem,
                      *, past_len):
    b = pl.program_id(0)
    slot = lax.rem(b, 2)
    scale = A_DH ** -0.5
    n_tiles = MOBA_TOPK * PAGES_PER_BLOCK

    def tile_copies(bb, to_slot):
        copies = []
        for h in range(A_HEADS):
            for r in range(MOBA_TOPK):
                block = sel_ref[bb, h * MOBA_TOPK + r]
                for t in range(PAGES_PER_BLOCK):
                    page = pt_ref[bb, block * PAGES_PER_BLOCK + t]
                    j = r * PAGES_PER_BLOCK + t
                    copies.append(pltpu.make_async_copy(ck_ref.at[page, :, h, :], kbuf.at[to_slot, h, j],
                                                        sem.at[to_slot]))
                    copies.append(pltpu.make_async_copy(cv_ref.at[page, :, h, :], vbuf.at[to_slot, h, j],
                                                        sem.at[to_slot]))
        return copies

    def start_all(copies):
        for i, cp in enumerate(copies):
            cp.start(priority=i % 2)

    @pl.when(b == 0)
    def _():
        start_all(tile_copies(0, 0))

    @pl.when(b + 1 < pl.num_programs(0))
    def _():
        start_all(tile_copies(b + 1, 1 - slot))

    for cp in tile_copies(b, slot):
        cp.wait()

    q, kn, vn = q_ref[...], kn_ref[...], vn_ref[...]
    outs = []
    lane = lax.broadcasted_iota(I32, (1, PAGE_SIZE), 1)
    for h in range(A_HEADS):
        qh = q[h:h + 1, :]
        q8 = jnp.broadcast_to(qh, (8, A_DH)).astype(BF16)
        s = []
        for r in range(MOBA_TOPK):
            block = sel_ref[b, h * MOBA_TOPK + r]
            for t in range(PAGES_PER_BLOCK):
                kt = kbuf[slot, h, r * PAGES_PER_BLOCK + t].astype(BF16)
                raw = lax.dot_general(q8, kt, NT_DIMS, preferred_element_type=F32)[0:1, :]
                bucket = _t5_bucket(past_len - (block * MOBA_BLOCK + t * PAGE_SIZE + lane))
                bias = jnp.zeros((1, PAGE_SIZE), F32)
                for bkt in range(T5_BUCKETS):
                    bias = jnp.where(bucket == bkt, t5_ref[bkt, h], bias)
                s.append(raw * scale + bias)
        s_own = jnp.sum(qh * kn[h:h + 1, :], axis=1, keepdims=True) * scale + t5_ref[0, h]
        m = jnp.maximum(jnp.max(functools.reduce(jnp.maximum, s), axis=1, keepdims=True), s_own)
        p = [jnp.exp(x - m) for x in s]
        p_own = jnp.exp(s_own - m)
        l = jnp.sum(functools.reduce(jnp.add, p), axis=1, keepdims=True) + p_own
        acc = p_own * vn[h:h + 1, :]
        for j in range(n_tiles):
            p8 = jnp.broadcast_to(p[j], (8, PAGE_SIZE)).astype(BF16)
            acc = acc + _dot(p8, vbuf[slot, h, j].astype(BF16))[0:1, :]
        outs.append(acc / l)
    o_ref[...] = jnp.concatenate(outs, axis=0)


def _moba_gates(q4, cache_k, page_table):
    batch, n_pages = page_table.shape
    nb_past = n_pages // PAGES_PER_BLOCK
    bps = _tile(nb_past, 8)

    def page(t):
        return pl.BlockSpec((None, PAGE_SIZE, A_HEADS, A_DH),
                            lambda b, n, pt: (pt[b, n * bps * PAGES_PER_BLOCK + t], 0, 0, 0))

    pages = [page(t) for t in range(bps * PAGES_PER_BLOCK)]
    return pl.pallas_call(
        functools.partial(_moba_gate_body, bps=bps),
        grid_spec=pltpu.PrefetchScalarGridSpec(
            num_scalar_prefetch=1,
            grid=(batch, nb_past // bps),
            in_specs=[pl.BlockSpec((None, None, A_HEADS, A_DH), lambda b, n, pt: (SEC_QA, b, 0, 0))] + pages,
            out_specs=pl.BlockSpec((None, A_HEADS, 128), lambda b, n, pt: (b, 0, 0)),
        ),
        out_shape=jax.ShapeDtypeStruct((batch, A_HEADS, 128), F32),
        compiler_params=_params("parallel", "arbitrary"),
        name="moba_gate",
    )(page_table, q4, *([cache_k] * len(pages)))


def _moba_step(z4, k_new, v_new, cache_k, cache_v, page_table, t5_table, gates=None):
    batch, n_pages = page_table.shape
    past_len = n_pages * PAGE_SIZE
    assert past_len % MOBA_BLOCK == 0 and MOBA_BLOCK % PAGE_SIZE == 0
    nb_past = past_len // MOBA_BLOCK
    assert MOBA_TOPK <= nb_past <= 128
    k_new = k_new.reshape(batch, A_HEADS, A_DH)
    v_new = v_new.reshape(batch, A_HEADS, A_DH)
    if gates is None:
        gates = _moba_gates(z4, cache_k, page_table)
    sel = pl.pallas_call(
        functools.partial(_moba_select_body, nb_past=nb_past),
        out_shape=jax.ShapeDtypeStruct((batch * A_HEADS, 128), I32),
        name="moba_select",
    )(gates.reshape(batch * A_HEADS, 128))
    sel = sel[:, :MOBA_TOPK].reshape(batch, A_HEADS * MOBA_TOPK)

    new_tok = pl.BlockSpec((None, A_HEADS, A_DH), lambda b, pt, sl: (b, 0, 0))
    tiles = pltpu.VMEM((2, A_HEADS, MOBA_TOPK * PAGES_PER_BLOCK, PAGE_SIZE, A_DH), F32)
    return pl.pallas_call(
        functools.partial(_moba_attend_body, past_len=past_len),
        grid_spec=pltpu.PrefetchScalarGridSpec(
            num_scalar_prefetch=2,
            grid=(batch,),
            in_specs=[pl.BlockSpec(memory_space=pltpu.SMEM),
                      pl.BlockSpec((None, None, A_HEADS, A_DH), lambda b, pt, sl: (SEC_QA, b, 0, 0)),
                      new_tok, new_tok,
                      pl.BlockSpec(memory_space=pl.ANY), pl.BlockSpec(memory_space=pl.ANY)],
            out_specs=pl.BlockSpec((None, A_HEADS, A_DH), lambda b, pt, sl: (b, 0, 0)),
            scratch_shapes=[tiles, tiles, pltpu.SemaphoreType.DMA((2,))],
        ),
        out_shape=jax.ShapeDtypeStruct((batch, A_HEADS, A_DH), F32),
        compiler_params=_params("arbitrary"),
        name="moba_attend",
    )(page_table, sel, t5_table, z4, k_new, v_new, cache_k, cache_v)


def kernel(x_prompt, x_sample, mem_prompt, cache_k, cache_v, cache_mem_k, cache_mem_v, state_ret, page_table, t5_table, ffn1_norm, ffn1_w1, ffn1_w3, ffn1_w2, mix_norm, mem_norm, w_in, w_mem_kv, w_gate, b_gate, w_br_moba, w_br_ret, w_br_mem, w_out, ffn2_norm, ffn2_w1, ffn2_w3, ffn2_w2, final_norm):
    batch, seq, d = x_prompt.shape
    dec_batch, dec_seq, _ = x_sample.shape
    assert dec_seq == 1
    depth = w_in.shape[0]
    n_mem = mem_prompt.shape[1]
    past_len = page_table.shape[1] * PAGE_SIZE
    cos_p, sin_p = _rope_tables(0, seq)
    cos_s, sin_s = _rope_tables(past_len, dec_seq)

    xp = x_prompt.reshape(batch * seq, d)
    xs = x_sample.reshape(dec_batch, d)
    outs = [[] for _ in range(8)]
    for l in range(depth):
        last = l == depth - 1
        g_next = final_norm if last else ffn1_norm[l + 1]

        x1s, h2s, *f1_b = _ffn_half(xs, ffn1_norm[l], ffn1_w1[l], ffn1_w3[l], ffn1_w2[l], mix_norm[l],
                                    emit_x=True, post_dtype=BF16, emit_w=True)
        x1, h2, w_in_b, *f2_b = _ffn_half(xp, ffn1_norm[l], *f1_b, mix_norm[l], emit_x=True, post_dtype=BF16,
                                          cast_jobs=[w_in[l], ffn2_w1[l], ffn2_w3[l], ffn2_w2[l]])
        z3s, ka_s, va_s = _in_proj(h2s, w_in_b)
        z3, ka, va, w_gate_b, w_out_b, *branch_b = _in_proj(
            h2, w_in_b, cast_jobs=[w_gate[l], w_out[l], w_br_moba[l], w_br_ret[l], w_br_mem[l]])
        mem_k_s, mem_v_s, x1s = lax.optimization_barrier((cache_mem_k[l], cache_mem_v[l], x1s))

        z4s = z3s.reshape(z3s.shape[0], dec_batch, A_HEADS, A_DH)
        oa, gates_s = _moba_prompt(z3, ka, va, t5_table, batch, seq, paged=(page_table, z4s, cache_k[l]))
        oa_s = _moba_step(z4s, ka_s, va_s, cache_k[l], cache_v[l], page_table, t5_table, gates=gates_s)

        o_rs, s_new_s = _ret_step(z3s, cos_s, sin_s, state_ret[l])
        z3s_pad = jnp.broadcast_to(z3s[:, :, None, :], (z3s.shape[0], dec_batch, 8, SEC))
        z3s_pad = z3s_pad.reshape(z3s.shape[0], dec_batch * 8, SEC)
        om_s = _mem_attend(z3s_pad, mem_k_s.reshape(dec_batch * n_mem, M_HEADS * M_DH),
                           mem_v_s.reshape(dec_batch * n_mem, M_HEADS * M_DH), dec_batch, 8)
        om_s = om_s.reshape(dec_batch, 8, M_HEADS * M_DH)[:, 0, :]
        branches_s = [oa_s.reshape(dec_batch, -1).astype(BF16), o_rs.reshape(dec_batch, -1).astype(BF16), om_s]

        mem_h = _norm_rows(mem_prompt.reshape(batch * n_mem, d), mem_norm[l], BF16)
        mkv_p = _matmul(mem_h, w_mem_kv[l], tn_pref=512, name="mem_kv")
        mk_p, mv_p = mkv_p[:, :M_HEADS * M_DH], mkv_p[:, M_HEADS * M_DH:]
        o_r, s_new_p = _ret_prompt(z3, cos_p, sin_p, batch, seq)
        om = _mem_attend(z3, mk_p, mv_p, batch, seq)

        merged_s = _merge(h2s, branches_s, w_gate_b, b_gate[l], branch_b)
        merged = _merge(h2, [oa, o_r, om], w_gate_b, b_gate[l], branch_b)
        x2s = _matmul(merged_s, w_out_b, residual=x1s, name="out_proj")
        x2 = _matmul(merged, w_out_b, residual=x1, name="out_proj")
        xs = _ffn_half(x2s, ffn2_norm[l], *f2_b, g_next, emit_x=not last, post_dtype=F32)[0]
        xp = _ffn_half(x2, ffn2_norm[l], *f2_b, g_next, emit_x=not last, post_dtype=F32)[0]

        new = (ka.reshape(batch, seq, A_HEADS, A_DH), va.reshape(batch, seq, A_HEADS, A_DH),
               mk_p.reshape(batch, n_mem, M_HEADS, M_DH), mv_p.reshape(batch, n_mem, M_HEADS, M_DH), s_new_p,
               ka_s.reshape(dec_batch, dec_seq, A_HEADS, A_DH), va_s.reshape(dec_batch, dec_seq, A_HEADS, A_DH),
               s_new_s)
        for acc, val in zip(outs, new):
            acc.append(val)

    y_prompt = xp.reshape(batch, seq, d)
    y_sample = xs.reshape(dec_batch, dec_seq, d)
    return (y_prompt, y_sample) + tuple(jnp.stack(o) for o in outs)
```

```python
import functools
import math

import jax
import jax.numpy as jnp
import numpy as np
from jax import lax
from jax.experimental import pallas as pl
from jax.experimental.pallas import tpu as pltpu

F32 = jnp.float32
BF16 = jnp.bfloat16
I32 = jnp.int32

A_HEADS, A_DH = 8, 128
MOBA_BLOCK, MOBA_TOPK = 256, 3
T5_BUCKETS, T5_MAX_DIST = 32, 128
R_HEADS, R_DK, R_DV = 4, 256, 256
RET_CHUNK = 128
RET_ROPE_BASE = 10000.0
M_HEADS, M_DH = 4, 256
PAGE_SIZE = 128
EPS = 1e-6
NEG_INF = -1e30
LOG2E = math.log2(math.e)
SEC = 1024
W_IN_KA, W_IN_VA = 1, 2
(SEC_QA, SEC_QR, SEC_KR, SEC_VR, SEC_GR, SEC_QM) = range(6)

V7X_VMEM_LIMIT_BYTES = 56 * 1024 * 1024

NT_DIMS = (((1,), (1,)), ((), ()))
TN_DIMS = (((0,), (0,)), ((), ()))


def _params(*sem):
    return pltpu.CompilerParams(dimension_semantics=sem, vmem_limit_bytes=V7X_VMEM_LIMIT_BYTES)


def _tile(n, pref):
    t = min(n, pref)
    while n % t:
        t -= 1
    return t


def _rms(x, g):
    return x * lax.rsqrt(jnp.mean(x * x, axis=-1, keepdims=True) + EPS) * g


def _silu(x):
    return x * jax.nn.sigmoid(x)


def _dot(a, b):
    return jnp.dot(a, b, preferred_element_type=F32)


def _bf16(w):
    return w if w.dtype == BF16 else w.astype(BF16)


def _ffn_body(x_ref, g_ref, w1_ref, w3_ref, w2_ref, gp_ref, *refs, emit_x, emit_w, n_cast):
    refs = list(refs)
    cast_in = [refs.pop(0) for _ in range(n_cast)]
    xo_ref = refs.pop(0) if emit_x else None
    ho_ref = refs.pop(0)
    wb_refs = [refs.pop(0) for _ in range(3)] if emit_w else []
    cast_out = [refs.pop(0) for _ in range(n_cast)]
    h_sc, acc_sc = refs
    f = pl.program_id(1)

    @pl.when(f == 0)
    def _():
        h_sc[...] = _rms(x_ref[...], g_ref[...]).astype(BF16)
        acc_sc[...] = jnp.zeros_like(acc_sc)

    _cast_rows(cast_in, cast_out)

    w1, w3, w2 = _bf16(w1_ref[...]), _bf16(w3_ref[...]), _bf16(w2_ref[...])
    for wb_ref, w in zip(wb_refs, (w1, w3, w2)):
        wb_ref[...] = w
    h = h_sc[...]
    t = _silu(_dot(h, w1)) * _dot(h, w3)
    acc_sc[...] += _dot(t.astype(BF16), w2)

    @pl.when(f == pl.num_programs(1) - 1)
    def _():
        xo = x_ref[...] + 0.5 * acc_sc[...]
        if emit_x:
            xo_ref[...] = xo
        ho_ref[...] = _rms(xo, gp_ref[...]).astype(ho_ref.dtype)


BF16_SUBLANES = 16


def _cast_spec(a, grid):
    n_steps = grid[0] * grid[1]
    n_chunks = max(n for n in range(1, n_steps + 1)
                   if a.shape[0] % n == 0 and (a.shape[0] // n) % BF16_SUBLANES == 0)
    return pl.BlockSpec((a.shape[0] // n_chunks, a.shape[1]),
                        lambda i, j: (jnp.minimum(i * grid[1] + j, n_chunks - 1), 0))


def _cast_rows(src_refs, dst_refs):
    for src_ref, dst_ref in zip(src_refs, dst_refs):
        dst_ref[...] = src_ref[...].astype(BF16)


def _ffn_half(x, g, w1, w3, w2, g_post, *, emit_x, post_dtype, emit_w=False, cast_jobs=()):
    rows, d = x.shape
    ff = w1.shape[1]
    tm, tf = _tile(rows, 512), _tile(ff, 512)
    assert not emit_w or rows == tm
    grid = (rows // tm, ff // tf)
    row_spec = pl.BlockSpec((tm, d), lambda i, f: (i, 0))
    vec_spec = pl.BlockSpec((1, d), lambda i, f: (0, 0))
    w13_spec = pl.BlockSpec((d, tf), lambda i, f: (0, f))
    w2_spec = pl.BlockSpec((tf, d), lambda i, f: (f, 0))
    out_shape = [jax.ShapeDtypeStruct((rows, d), post_dtype)]
    out_specs = [row_spec]
    if emit_x:
        out_shape = [jax.ShapeDtypeStruct((rows, d), F32)] + out_shape
        out_specs = [row_spec] + out_specs
    if emit_w:
        out_shape += [jax.ShapeDtypeStruct(w.shape, BF16) for w in (w1, w3, w2)]
        out_specs += [w13_spec, w13_spec, w2_spec]

    cast_specs = [_cast_spec(a, grid) for a in cast_jobs]
    out_shape += [jax.ShapeDtypeStruct(a.shape, BF16) for a in cast_jobs]
    out_specs += cast_specs
    return pl.pallas_call(
        functools.partial(_ffn_body, emit_x=emit_x, emit_w=emit_w, n_cast=len(cast_jobs)),
        grid=grid,
        in_specs=[row_spec, vec_spec, w13_spec, w13_spec, w2_spec, vec_spec] + cast_specs,
        out_specs=out_specs,
        out_shape=out_shape,
        scratch_shapes=[pltpu.VMEM((tm, d), BF16), pltpu.VMEM((tm, d), F32)],
        compiler_params=_params("parallel", "arbitrary"),
        name="ffn_half",
    )(x, g.reshape(1, d), w1, w3, w2, g_post.reshape(1, d), *cast_jobs)


def _norm_body(x_ref, g_ref, o_ref):
    o_ref[...] = _rms(x_ref[...], g_ref[...]).astype(o_ref.dtype)


def _norm_rows(x, g, out_dtype):
    rows, d = x.shape
    tm = _tile(rows, 512)
    return pl.pallas_call(
        _norm_body,
        grid=(rows // tm,),
        in_specs=[pl.BlockSpec((tm, d), lambda i: (i, 0)), pl.BlockSpec((1, d), lambda i: (0, 0))],
        out_specs=pl.BlockSpec((tm, d), lambda i: (i, 0)),
        out_shape=jax.ShapeDtypeStruct((rows, d), out_dtype),
        compiler_params=_params("parallel"),
        name="rmsnorm_rows",
    )(x, g.reshape(1, d))


def _mm_body(a_ref, w_ref, *refs, has_residual):
    o_ref = refs[-1]
    acc = _dot(a_ref[...], _bf16(w_ref[...]))
    if has_residual:
        acc = refs[0][...] + acc
    o_ref[...] = acc.astype(o_ref.dtype)


def _in_proj_body(a_ref, w_ref, *refs):
    n_cast = (len(refs) - 3) // 2
    z_ref, k_ref, v_ref = refs[n_cast:n_cast + 3]
    _cast_rows(refs[:n_cast], refs[n_cast + 3:])
    j = pl.program_id(1)
    acc = _dot(a_ref[...], w_ref[...])

    @pl.when(j == W_IN_KA)
    def _():
        k_ref[...] = acc

    @pl.when(j == W_IN_VA)
    def _():
        v_ref[...] = acc

    @pl.when((j != W_IN_KA) & (j != W_IN_VA))
    def _():
        z_ref[...] = acc


def _in_proj(a, w, cast_jobs=()):
    rows, k = a.shape
    n_sec = w.shape[1] // SEC
    assert (W_IN_KA, W_IN_VA) == (1, 2) and n_sec == 8
    tm = _tile(rows, 1024)
    grid = (rows // tm, n_sec)
    kv_spec = pl.BlockSpec((tm, SEC), lambda i, j: (i, 0))
    cast_specs = [_cast_spec(c, grid) for c in cast_jobs]
    return pl.pallas_call(
        _in_proj_body,
        grid=grid,
        in_specs=[pl.BlockSpec((tm, k), lambda i, j: (i, 0)), pl.BlockSpec((k, SEC), lambda i, j: (0, j))]
        + cast_specs,
        out_specs=[pl.BlockSpec((None, tm, SEC), lambda i, j: (jnp.maximum(j - 2, 0), i, 0)), kv_spec, kv_spec]
        + cast_specs,
        out_shape=[jax.ShapeDtypeStruct((n_sec - 2, rows, SEC), F32)] + [jax.ShapeDtypeStruct((rows, SEC), F32)] * 2
        + [jax.ShapeDtypeStruct(c.shape, BF16) for c in cast_jobs],
        compiler_params=_params("parallel", "arbitrary"),
        name="in_proj",
    )(a, w, *cast_jobs)


def _matmul(a, w, *, residual=None, out_dtype=F32, tn_pref=1024, name="matmul"):
    rows, k = a.shape
    n_cols = w.shape[1]
    tm, tn = _tile(rows, 1024), _tile(n_cols, tn_pref)
    in_specs = [pl.BlockSpec((tm, k), lambda i, j: (i, 0)), pl.BlockSpec((k, tn), lambda i, j: (0, j))]
    out_spec = pl.BlockSpec((tm, tn), lambda i, j: (i, j))
    args = [a, w]
    if residual is not None:
        in_specs.append(out_spec)
        args.append(residual)
    return pl.pallas_call(
        functools.partial(_mm_body, has_residual=residual is not None),
        grid=(rows // tm, n_cols // tn),
        in_specs=in_specs,
        out_specs=out_spec,
        out_shape=jax.ShapeDtypeStruct((rows, n_cols), out_dtype),
        compiler_params=_params("parallel", "arbitrary"),
        name=name,
    )(*args)


def _t5_bucket(n):
    n = jnp.maximum(n, 0)
    max_exact = T5_BUCKETS // 2
    nf = jnp.maximum(n, 1).astype(F32)
    large = max_exact + (jnp.log(nf / max_exact) / math.log(T5_MAX_DIST / max_exact)
                         * (T5_BUCKETS - max_exact)).astype(I32)
    return jnp.where(n < max_exact, n, jnp.minimum(large, T5_BUCKETS - 1))


PAGES_PER_BLOCK = MOBA_BLOCK // PAGE_SIZE
SUBLANES = 8


def _block_gates(q, page_refs, first_block, gates):
    lane = lax.broadcasted_iota(I32, gates.shape, 1)
    for i in range(len(page_refs) // PAGES_PER_BLOCK):
        pages = [page_refs[i * PAGES_PER_BLOCK + t][...] for t in range(PAGES_PER_BLOCK)]
        ksum = functools.reduce(jnp.add, [jnp.sum(x, axis=0) for x in pages])
        gate = jnp.sum(q * ksum, axis=1, keepdims=True) / MOBA_BLOCK
        gates = jnp.where(lane == first_block + i, gate, gates)
    return gates


PAGE_RING = 3


def _moba_prompt_body(*refs, nb, side_blocks, grid):
    h, b, c = pl.program_id(0), pl.program_id(1), pl.program_id(2)
    if side_blocks:
        (pt_ref, t5_ref, q_ref, k_ref, v_ref, qside_ref, cache_ref, o_ref, gates_ref,
         kb_sc, vb_sc, kdiff_sc, bias_sc, pages_sc, page_sem) = refs
        n_pages = side_blocks * PAGES_PER_BLOCK
        steps_per_seq = grid[2]
        n_steps = grid[0] * grid[1] * grid[2]
        step_idx = (h * grid[1] + b) * steps_per_seq + c

        def page_copies(s):
            seq, part, slot = s // steps_per_seq, s % steps_per_seq, s % PAGE_RING
            return [pltpu.make_async_copy(cache_ref.at[pt_ref[seq, part * n_pages + t]], pages_sc.at[slot, t],
                                          page_sem.at[slot]) for t in range(n_pages)]

        @pl.when(step_idx == 0)
        def _():
            for s in range(min(PAGE_RING - 1, n_steps)):
                for cp in page_copies(s):
                    cp.start()

        @pl.when(step_idx + PAGE_RING - 1 < n_steps)
        def _():
            for cp in page_copies(step_idx + PAGE_RING - 1):
                cp.start()

        for cp in page_copies(step_idx):
            cp.wait()
        slot = step_idx % PAGE_RING
        page_refs = [pages_sc.at[slot, t] for t in range(n_pages)]
    else:
        t5_ref, q_ref, k_ref, v_ref, o_ref, kb_sc, vb_sc, kdiff_sc, bias_sc = refs
    blk = MOBA_BLOCK
    nbp = SUBLANES
    assert nb <= nbp
    scale = A_DH ** -0.5
    ii = lax.broadcasted_iota(I32, (blk, blk), 0)
    jj = lax.broadcasted_iota(I32, (blk, blk), 1)

    @pl.when((b == 0) & (c == 0))
    def _():
        for t in range(2):
            bucket = _t5_bucket(t * blk + ii - jj)
            tile = jnp.zeros((blk, blk), F32)
            for bkt in range(T5_BUCKETS):
                tile = jnp.where(bucket == bkt, t5_ref[bkt, h], tile)
            tile = tile * LOG2E
            bias_sc[t] = jnp.where(ii >= jj, tile, NEG_INF) if t == 0 else tile

    @pl.when(c == 0)
    def _():
        km = jnp.concatenate(
            [jnp.mean(k_ref[n * blk:(n + 1) * blk, :], axis=0, keepdims=True) for n in range(nb)]
            + [jnp.zeros((1, A_DH), F32)] * (nbp - nb), axis=0)
        for n in range(nb):
            kdiff_sc[n * nbp:(n + 1) * nbp, :] = km[n:n + 1, :] - km
            kb_sc[n * blk:(n + 1) * blk, A_DH:2 * A_DH] = jnp.where(
                lax.broadcasted_iota(I32, (blk, 128), 1) == n, 1.0, 0.0).astype(BF16)
        kb_sc[:, 0:A_DH] = k_ref[...].astype(BF16)
        vb_sc[:, 0:A_DH] = v_ref[...].astype(BF16)
        vb_sc[:, A_DH:2 * A_DH] = jnp.ones((nb * blk, A_DH), BF16)
        if side_blocks:
            gates_ref[...] = jnp.zeros_like(gates_ref)

    far_bias = t5_ref[T5_BUCKETS - 1, h] * LOG2E

    def attend(cc):
        q = q_ref[cc * blk:(cc + 1) * blk, :]
        qs = (q * (scale * LOG2E)).astype(BF16)
        n_keys = (cc + 1) * blk
        if cc <= MOBA_TOPK:
            s_all = lax.dot_general(qs, kb_sc[0:n_keys, 0:A_DH], NT_DIMS, preferred_element_type=F32)
        else:
            diff_t = lax.dot_general(kdiff_sc[0:cc * nbp, :], q, NT_DIMS, precision=lax.Precision.HIGHEST,
                                     preferred_element_type=F32)
            pair = lax.broadcasted_iota(I32, diff_t.shape, 0)
            m_of, n_of = jnp.right_shift(pair, int(math.log2(nbp))), jnp.bitwise_and(pair, nbp - 1)
            beats_t = jnp.where(diff_t > 0, 1.0, jnp.where(diff_t == 0, jnp.where(n_of > m_of, 1.0, 0.0), 0.0))
            fold = jnp.where(jnp.bitwise_and(lax.broadcasted_iota(I32, (cc * nbp, 128), 0), nbp - 1)
                             == lax.broadcasted_iota(I32, (cc * nbp, 128), 1), 1.0, 0.0)
            rank = lax.dot_general(beats_t.astype(BF16), fold.astype(BF16), TN_DIMS,
                                   preferred_element_type=F32)
            lane = lax.broadcasted_iota(I32, rank.shape, 1)
            sel = jnp.where(lane < cc, jnp.where(rank < MOBA_TOPK, 0.0, NEG_INF), 0.0)
            q_sel = jnp.concatenate([qs, sel.astype(BF16)], axis=1)
            s_all = lax.dot_general(q_sel, kb_sc[0:n_keys, :], NT_DIMS, preferred_element_type=F32)
        tiles = []
        for n in range(cc + 1):
            t = s_all[:, n * blk:(n + 1) * blk]
            tiles.append(t + (bias_sc[0] if n == cc else bias_sc[1] if n == cc - 1 else far_bias))
        m = jnp.max(functools.reduce(jnp.maximum, tiles), axis=1, keepdims=True)
        p_all = jnp.concatenate([jnp.exp2(t - m).astype(BF16) for t in tiles], axis=1)
        pv = _dot(p_all, vb_sc[0:n_keys, :])
        o_ref[cc * blk:(cc + 1) * blk, :] = (pv[:, 0:A_DH] / pv[:, A_DH:2 * A_DH]).astype(o_ref.dtype)

    def step(s):
        if side_blocks:
            gates_ref[...] = _block_gates(qside_ref[...], page_refs, s * side_blocks, gates_ref[...])
        for cc in sorted({s, nb - 1 - s}):
            attend(cc)

    for s in range(_moba_steps(nb)):
        pl.when(c == s)(functools.partial(step, s))


def _moba_steps(nb):
    return (nb + 1) // 2


def _moba_prompt(z3, k, v, t5_table, batch, seq, paged=None):
    blk = MOBA_BLOCK
    nb = seq // blk
    steps = _moba_steps(nb)
    side_blocks = 0
    if paged is not None:
        page_table, q_side, cache_k = paged
        nb_past = page_table.shape[1] // PAGES_PER_BLOCK
        if page_table.shape[0] == A_HEADS * batch and nb_past % steps == 0:
            side_blocks = nb_past // steps

    seq_spec = pl.BlockSpec((seq, A_DH), lambda h, b, c, *_: (b, h))
    in_specs = [pl.BlockSpec(memory_space=pltpu.SMEM),
                pl.BlockSpec((None, seq, A_DH), lambda h, b, c, *_: (SEC_QA, b, h)), seq_spec, seq_spec]
    out_specs = [seq_spec]
    out_shape = [jax.ShapeDtypeStruct((batch * seq, A_HEADS * A_DH), BF16)]
    args = [t5_table, z3, k, v]
    scratch = [pltpu.VMEM((seq, 2 * A_DH), BF16), pltpu.VMEM((seq, 2 * A_DH), BF16),
               pltpu.VMEM((SUBLANES * SUBLANES, A_DH), F32), pltpu.VMEM((2, blk, blk), F32)]
    grid = (A_HEADS, batch, steps)
    if side_blocks:
        in_specs += [pl.BlockSpec((None, None, A_HEADS, A_DH), lambda h, b, c, pt: (SEC_QA, h * batch + b, 0, 0)),
                     pl.BlockSpec(memory_space=pl.ANY)]
        out_specs.append(pl.BlockSpec((None, A_HEADS, 128), lambda h, b, c, pt: (h * batch + b, 0, 0)))
        out_shape.append(jax.ShapeDtypeStruct((A_HEADS * batch, A_HEADS, 128), F32))
        args = [page_table] + args + [q_side, cache_k]
        scratch += [pltpu.VMEM((PAGE_RING, side_blocks * PAGES_PER_BLOCK, PAGE_SIZE, A_HEADS, A_DH), F32),
                    pltpu.SemaphoreType.DMA((PAGE_RING,))]
    out = pl.pallas_call(
        functools.partial(_moba_prompt_body, nb=nb, side_blocks=side_blocks, grid=grid),
        grid_spec=pltpu.PrefetchScalarGridSpec(
            num_scalar_prefetch=1 if side_blocks else 0,
            grid=grid,
            in_specs=in_specs,
            out_specs=out_specs,
            scratch_shapes=scratch,
        ),
        out_shape=out_shape,
        compiler_params=_params("arbitrary", "arbitrary", "arbitrary"),
        name="moba_prompt",
    )(*args)
    if paged is None:
        return out[0]
    return out[0], (out[1] if side_blocks else None)


def _rope_body(inv_ref, cos_ref, sin_ref, *, pos0):
    pos = pos0 + lax.broadcasted_iota(I32, cos_ref.shape, 0)
    ang = pos.astype(F32) * inv_ref[...]
    cos_ref[...] = jnp.cos(ang)
    sin_ref[...] = jnp.sin(ang)


def _rope_tables(pos0, n_pos):
    half = R_DK // 2
    inv = 1.0 / (RET_ROPE_BASE ** jnp.linspace(0.0, 1.0, half, dtype=F32))
    rows = -(-n_pos // 8) * 8
    return pl.pallas_call(
        functools.partial(_rope_body, pos0=pos0),
        out_shape=[jax.ShapeDtypeStruct((rows, half), F32)] * 2,
        name="rope_tables",
    )(inv.reshape(1, half))


def _rotate(x, cos, sin):
    half = x.shape[-1] // 2
    x1, x2 = x[:, :half], x[:, half:]
    return jnp.concatenate([x1 * cos - x2 * sin, x1 * sin + x2 * cos], axis=1)


def _log_decay(h, shape):
    hf = jnp.full(shape, h, I32).astype(F32)
    return jnp.log(1.0 - jnp.exp2(-5.0 - hf))


def _ret_prompt_body(q_ref, k_ref, v_ref, g_ref, cos_ref, sin_ref, *refs, c):
    n_cast = (len(refs) - 3) // 2
    o_ref, so_ref = refs[n_cast:n_cast + 2]
    s_sc = refs[-1]
    _cast_rows(refs[:n_cast], refs[n_cast + 2:-1])
    n = pl.program_id(1)

    @pl.when(n == 0)
    def _():
        s_sc[...] = jnp.zeros_like(s_sc)

    i = lax.broadcasted_iota(I32, (c, c), 0).astype(F32)
    j = lax.broadcasted_iota(I32, (c, c), 1).astype(F32)
    diff = i - j
    i_col = lax.broadcasted_iota(I32, (c, 1), 0).astype(F32)
    cos, sin = cos_ref[...], sin_ref[...]

    new_states, outs = [], []
    for h in range(R_HEADS):
        dk, dv = slice(h * R_DK, (h + 1) * R_DK), slice(h * R_DV, (h + 1) * R_DV)
        dmask = jnp.where(diff >= 0, jnp.exp(jnp.maximum(diff, 0.0) * _log_decay(h, (c, c))), 0.0)
        lg_col = _log_decay(h, (c, 1))
        q_dec = jnp.exp((i_col + 1.0) * lg_col)
        k_dec = jnp.exp((c - 1.0 - i_col) * lg_col)
        c_dec = jnp.exp(c * _log_decay(h, (1, R_DV)))
        qr = _rotate(q_ref[:, dk], cos, sin)
        kr = _rotate(k_ref[:, dk], cos, sin) * (R_DK ** -0.5)
        vb = v_ref[:, dv].astype(BF16)
        s = s_sc[h]
        att = lax.dot_general(qr.astype(BF16), kr.astype(BF16), NT_DIMS, preferred_element_type=F32) * dmask
        o = _dot(att.astype(BF16), vb) + _dot((qr * q_dec).astype(BF16), s.astype(BF16))
        new_states.append(s * c_dec + lax.dot_general((kr * k_dec).astype(BF16), vb, TN_DIMS,
                                                      preferred_element_type=F32))
        on = o * lax.rsqrt(jnp.mean(o * o, axis=-1, keepdims=True) + EPS)
        outs.append((on * _silu(g_ref[:, dv])).astype(o_ref.dtype))
    for h in range(R_HEADS):
        s_sc[h] = new_states[h]
    o_ref[...] = jnp.concatenate(outs, axis=1)

    @pl.when(n == pl.num_programs(1) - 1)
    def _():
        so_ref[...] = s_sc[...]


def _ret_prompt(z3, cos, sin, batch, seq, cast_jobs=()):
    c = math.gcd(seq, RET_CHUNK)
    nc = seq // c
    grid = (batch, nc)

    def sec(s):
        return pl.BlockSpec((None, c, SEC), lambda b, n: (s, b * nc + n, 0))

    tab = pl.BlockSpec((c, R_DK // 2), lambda b, n: (n, 0))
    cast_specs = [_cast_spec(a, grid) for a in cast_jobs]
    return pl.pallas_call(
        functools.partial(_ret_prompt_body, c=c),
        grid=grid,
        in_specs=[sec(SEC_QR), sec(SEC_KR), sec(SEC_VR), sec(SEC_GR), tab, tab] + cast_specs,
        out_specs=[pl.BlockSpec((c, R_HEADS * R_DV), lambda b, n: (b * nc + n, 0)),
                   pl.BlockSpec((None, R_HEADS, R_DK, R_DV), lambda b, n: (b, 0, 0, 0))] + cast_specs,
        out_shape=[jax.ShapeDtypeStruct((batch * seq, R_HEADS * R_DV), BF16),
                   jax.ShapeDtypeStruct((batch, R_HEADS, R_DK, R_DV), F32)]
        + [jax.ShapeDtypeStruct(a.shape, BF16) for a in cast_jobs],
        scratch_shapes=[pltpu.VMEM((R_HEADS, R_DK, R_DV), F32)],
        compiler_params=_params("parallel", "arbitrary"),
        name="retention_prompt",
    )(z3, z3, z3, z3, cos, sin, *cast_jobs)


def _ret_step_body(q_ref, k_ref, v_ref, g_ref, cos_ref, sin_ref, s_ref, o_ref, so_ref):
    cos, sin = cos_ref[0:1, :], sin_ref[0:1, :]
    rows = 16
    row0 = lax.broadcasted_iota(I32, (rows, R_DK), 0) == 0
    for h in range(R_HEADS):
        decay = jnp.exp(_log_decay(h, (1, R_DV)))
        qr = _rotate(q_ref[h:h + 1, :], cos, sin)
        kr = _rotate(k_ref[h:h + 1, :], cos, sin) * (R_DK ** -0.5)
        v = v_ref[h:h + 1, :]
        s = s_ref[h]
        q_rows = jnp.broadcast_to(qr * decay, (rows, R_DK)).astype(BF16)
        qs = _dot(q_rows, s.astype(BF16))[0:1, :]
        o = jnp.sum(qr * kr, axis=-1, keepdims=True) * v + qs
        k_rows = jnp.where(row0, jnp.broadcast_to(kr, (rows, R_DK)), 0.0).astype(BF16)
        v_rows = jnp.broadcast_to(v, (rows, R_DV)).astype(BF16)
        so_ref[h] = s * decay + lax.dot_general(k_rows, v_rows, TN_DIMS, preferred_element_type=F32)
        on = o * lax.rsqrt(jnp.mean(o * o, axis=-1, keepdims=True) + EPS)
        o_ref[h:h + 1, :] = (on * _silu(g_ref[h:h + 1, :])).astype(o_ref.dtype)


def _ret_step(z3, cos, sin, state):
    batch = z3.shape[1]
    z4 = z3.reshape(z3.shape[0], batch, R_HEADS, R_DK)

    def sec(s):
        return pl.BlockSpec((None, None, R_HEADS, R_DK), lambda b: (s, b, 0, 0))

    tab = pl.BlockSpec((8, R_DK // 2), lambda b: (0, 0))
    st = pl.BlockSpec((None, R_HEADS, R_DK, R_DV), lambda b: (b, 0, 0, 0))
    return pl.pallas_call(
        _ret_step_body,
        grid=(batch,),
        in_specs=[sec(SEC_QR), sec(SEC_KR), sec(SEC_VR), sec(SEC_GR), tab, tab, st],
        out_specs=[pl.BlockSpec((None, R_HEADS, R_DV), lambda b: (b, 0, 0)), st],
        out_shape=[jax.ShapeDtypeStruct((batch, R_HEADS, R_DV), F32),
                   jax.ShapeDtypeStruct(state.shape, F32)],
        compiler_params=_params("parallel"),
        name="retention_step",
    )(z4, z4, z4, z4, cos, sin, state)


def _mem_body(q_ref, k_ref, v_ref, o_ref):
    outs = []
    for h in range(M_HEADS):
        dh = slice(h * M_DH, (h + 1) * M_DH)
        s = lax.dot_general(q_ref[:, dh].astype(BF16), k_ref[:, dh].astype(BF16), NT_DIMS,
                            preferred_element_type=F32) * (M_DH ** -0.5)
        m = jnp.max(s, axis=-1, keepdims=True)
        p = jnp.exp(s - m)
        l = jnp.sum(p, axis=-1, keepdims=True)
        outs.append((_dot(p.astype(BF16), v_ref[:, dh].astype(BF16)) / l).astype(o_ref.dtype))
    o_ref[...] = jnp.concatenate(outs, axis=1)


def _mem_attend(z3, mk, mv, batch, seq):
    n_mem = mk.shape[0] // batch
    width = M_HEADS * M_DH
    ts = _tile(seq, 512)
    nt = seq // ts
    kv = pl.BlockSpec((n_mem, width), lambda b, t: (b, 0))
    return pl.pallas_call(
        _mem_body,
        grid=(batch, nt),
        in_specs=[pl.BlockSpec((None, ts, width), lambda b, t: (SEC_QM, b * nt + t, 0)), kv, kv],
        out_specs=pl.BlockSpec((ts, width), lambda b, t: (b * nt + t, 0)),
        out_shape=jax.ShapeDtypeStruct((batch * seq, width), BF16),
        compiler_params=_params("parallel", "arbitrary"),
        name="mem_attend",
    )(z3, mk, mv)


def _merge_body(h_ref, *refs):
    x_refs, wg_refs, bg_refs, wb_refs, o_ref = refs[0:3], refs[3:6], refs[6:9], refs[9:12], refs[12]
    h = h_ref[...]
    merged = None
    for g in range(3):
        term = jax.nn.sigmoid(_dot(h, wg_refs[g][...]) + bg_refs[g][...]) * _dot(x_refs[g][...], wb_refs[g][...])
        merged = term if merged is None else merged + term
    o_ref[...] = merged.astype(o_ref.dtype)


def _merge(h, branches, w_gate, b_gate, branch_ws):
    rows, d = h.shape
    tm, tn = _tile(rows, 1024), _tile(d, 512)
    nj = d // tn
    b_gate = b_gate.reshape(1, 3 * d)

    def row(width):
        return pl.BlockSpec((tm, width), lambda i, j: (i, 0))

    def cols(n_rows, group=0):
        return pl.BlockSpec((n_rows, tn), lambda i, j: (0, group * nj + j))

    return pl.pallas_call(
        _merge_body,
        grid=(rows // tm, nj),
        in_specs=([row(d)] + [row(x.shape[1]) for x in branches]
                  + [cols(d, g) for g in range(3)] + [cols(1, g) for g in range(3)]
                  + [cols(w.shape[0]) for w in branch_ws]),
        out_specs=pl.BlockSpec((tm, tn), lambda i, j: (i, j)),
        out_shape=jax.ShapeDtypeStruct((rows, d), BF16),
        compiler_params=_params("parallel", "arbitrary"),
        name="gated_merge",
    )(h, *branches, w_gate, w_gate, w_gate, b_gate, b_gate, b_gate, *branch_ws)


def _moba_gate_body(pt_ref, q_ref, *refs, bps):
    del pt_ref
    k_refs, gates_ref = refs[:-1], refs[-1]
    n = pl.program_id(1)

    @pl.when(n == 0)
    def _():
        gates_ref[...] = jnp.zeros_like(gates_ref)

    gates_ref[...] = _block_gates(q_ref[...], k_refs, n * bps, gates_ref[...])


def _moba_select_body(g_ref, sel_ref, *, nb_past):
    lane = lax.broadcasted_iota(I32, g_ref.shape, 1)
    gw = jnp.where(lane < nb_past, g_ref[...], NEG_INF)
    sel = jnp.zeros(gw.shape, I32)
    for r in range(MOBA_TOPK):
        mx = jnp.max(gw, axis=1, keepdims=True)
        idx = jnp.min(jnp.where(gw == mx, lane, 128), axis=1, keepdims=True)
        sel = jnp.where(lane == r, idx, sel)
        gw = jnp.where(lane == idx, -jnp.inf, gw)
    sel_ref[...] = sel


def _moba_attend_body(pt_ref, sel_ref, t5_ref, q_ref, kn_ref, vn_ref, ck_ref, cv_ref, o_ref, kbuf, vbuf, sem,
                      *, past_len):
    b = pl.program_id(0)
    slot = lax.rem(b, 2)
    scale = A_DH ** -0.5
    n_tiles = MOBA_TOPK * PAGES_PER_BLOCK

    def tile_copies(bb, to_slot):
        copies = []
        for h in range(A_HEADS):
            for r in range(MOBA_TOPK):
                block = sel_ref[bb, h * MOBA_TOPK + r]
                for t in range(PAGES_PER_BLOCK):
                    page = pt_ref[bb, block * PAGES_PER_BLOCK + t]
                    j = r * PAGES_PER_BLOCK + t
                    copies.append(pltpu.make_async_copy(ck_ref.at[page, :, h, :], kbuf.at[to_slot, h, j],
                                                        sem.at[to_slot]))
                    copies.append(pltpu.make_async_copy(cv_ref.at[page, :, h, :], vbuf.at[to_slot, h, j],
                                                        sem.at[to_slot]))
        return copies

    def start_all(copies):
        for i, cp in enumerate(copies):
            cp.start(priority=i % 2)

    @pl.when(b == 0)
    def _():
        start_all(tile_copies(0, 0))

    @pl.when(b + 1 < pl.num_programs(0))
    def _():
        start_all(tile_copies(b + 1, 1 - slot))

    for cp in tile_copies(b, slot):
        cp.wait()

    q, kn, vn = q_ref[...], kn_ref[...], vn_ref[...]
    outs = []
    lane = lax.broadcasted_iota(I32, (1, PAGE_SIZE), 1)
    for h in range(A_HEADS):
        qh = q[h:h + 1, :]
        q8 = jnp.broadcast_to(qh, (8, A_DH)).astype(BF16)
        s = []
        for r in range(MOBA_TOPK):
            block = sel_ref[b, h * MOBA_TOPK + r]
            for t in range(PAGES_PER_BLOCK):
                kt = kbuf[slot, h, r * PAGES_PER_BLOCK + t].astype(BF16)
                raw = lax.dot_general(q8, kt, NT_DIMS, preferred_element_type=F32)[0:1, :]
                bucket = _t5_bucket(past_len - (block * MOBA_BLOCK + t * PAGE_SIZE + lane))
                bias = jnp.zeros((1, PAGE_SIZE), F32)
                for bkt in range(T5_BUCKETS):
                    bias = jnp.where(bucket == bkt, t5_ref[bkt, h], bias)
                s.append(raw * scale + bias)
        s_own = jnp.sum(qh * kn[h:h + 1, :], axis=1, keepdims=True) * scale + t5_ref[0, h]
        m = jnp.maximum(jnp.max(functools.reduce(jnp.maximum, s), axis=1, keepdims=True), s_own)
        p = [jnp.exp(x - m) for x in s]
        p_own = jnp.exp(s_own - m)
        l = jnp.sum(functools.reduce(jnp.add, p), axis=1, keepdims=True) + p_own
        acc = p_own * vn[h:h + 1, :]
        for j in range(n_tiles):
            p8 = jnp.broadcast_to(p[j], (8, PAGE_SIZE)).astype(BF16)
            acc = acc + _dot(p8, vbuf[slot, h, j].astype(BF16))[0:1, :]
        outs.append(acc / l)
    o_ref[...] = jnp.concatenate(outs, axis=0)


def _moba_gates(q4, cache_k, page_table):
    batch, n_pages = page_table.shape
    nb_past = n_pages // PAGES_PER_BLOCK
    bps = _tile(nb_past, 8)

    def page(t):
        return pl.BlockSpec((None, PAGE_SIZE, A_HEADS, A_DH),
                            lambda b, n, pt: (pt[b, n * bps * PAGES_PER_BLOCK + t], 0, 0, 0))

    pages = [page(t) for t in range(bps * PAGES_PER_BLOCK)]
    return pl.pallas_call(
        functools.partial(_moba_gate_body, bps=bps),
        grid_spec=pltpu.PrefetchScalarGridSpec(
            num_scalar_prefetch=1,
            grid=(batch, nb_past // bps),
            in_specs=[pl.BlockSpec((None, None, A_HEADS, A_DH), lambda b, n, pt: (SEC_QA, b, 0, 0))] + pages,
            out_specs=pl.BlockSpec((None, A_HEADS, 128), lambda b, n, pt: (b, 0, 0)),
        ),
        out_shape=jax.ShapeDtypeStruct((batch, A_HEADS, 128), F32),
        compiler_params=_params("parallel", "arbitrary"),
        name="moba_gate",
    )(page_table, q4, *([cache_k] * len(pages)))


def _moba_step(z4, k_new, v_new, cache_k, cache_v, page_table, t5_table, gates=None):
    batch, n_pages = page_table.shape
    past_len = n_pages * PAGE_SIZE
    assert past_len % MOBA_BLOCK == 0 and MOBA_BLOCK % PAGE_SIZE == 0
    nb_past = past_len // MOBA_BLOCK
    assert MOBA_TOPK <= nb_past <= 128
    k_new = k_new.reshape(batch, A_HEADS, A_DH)
    v_new = v_new.reshape(batch, A_HEADS, A_DH)
    if gates is None:
        gates = _moba_gates(z4, cache_k, page_table)
    sel = pl.pallas_call(
        functools.partial(_moba_select_body, nb_past=nb_past),
        out_shape=jax.ShapeDtypeStruct((batch * A_HEADS, 128), I32),
        name="moba_select",
    )(gates.reshape(batch * A_HEADS, 128))
    sel = sel[:, :MOBA_TOPK].reshape(batch, A_HEADS * MOBA_TOPK)

    new_tok = pl.BlockSpec((None, A_HEADS, A_DH), lambda b, pt, sl: (b, 0, 0))
    tiles = pltpu.VMEM((2, A_HEADS, MOBA_TOPK * PAGES_PER_BLOCK, PAGE_SIZE, A_DH), F32)
    return pl.pallas_call(
        functools.partial(_moba_attend_body, past_len=past_len),
        grid_spec=pltpu.PrefetchScalarGridSpec(
            num_scalar_prefetch=2,
            grid=(batch,),
            in_specs=[pl.BlockSpec(memory_space=pltpu.SMEM),
                      pl.BlockSpec((None, None, A_HEADS, A_DH), lambda b, pt, sl: (SEC_QA, b, 0, 0)),
                      new_tok, new_tok,
                      pl.BlockSpec(memory_space=pl.ANY), pl.BlockSpec(memory_space=pl.ANY)],
            out_specs=pl.BlockSpec((None, A_HEADS, A_DH), lambda b, pt, sl: (b, 0, 0)),
            scratch_shapes=[tiles, tiles, pltpu.SemaphoreType.DMA((2,))],
        ),
        out_shape=jax.ShapeDtypeStruct((batch, A_HEADS, A_DH), F32),
        compiler_params=_params("arbitrary"),
        name="moba_attend",
    )(page_table, sel, t5_table, z4, k_new, v_new, cache_k, cache_v)


def kernel(x_prompt, x_sample, mem_prompt, cache_k, cache_v, cache_mem_k, cache_mem_v, state_ret, page_table, t5_table, ffn1_norm, ffn1_w1, ffn1_w3, ffn1_w2, mix_norm, mem_norm, w_in, w_mem_kv, w_gate, b_gate, w_br_moba, w_br_ret, w_br_mem, w_out, ffn2_norm, ffn2_w1, ffn2_w3, ffn2_w2, final_norm):
    batch, seq, d = x_prompt.shape
    dec_batch, dec_seq, _ = x_sample.shape
    assert dec_seq == 1
    depth = w_in.shape[0]
    n_mem = mem_prompt.shape[1]
    past_len = page_table.shape[1] * PAGE_SIZE
    cos_p, sin_p = _rope_tables(0, seq)
    cos_s, sin_s = _rope_tables(past_len, dec_seq)

    xp = x_prompt.reshape(batch * seq, d)
    xs = x_sample.reshape(dec_batch, d)
    outs = [[] for _ in range(8)]
    for l in range(depth):
        last = l == depth - 1
        g_next = final_norm if last else ffn1_norm[l + 1]

        x1s, h2s, *f1_b = _ffn_half(xs, ffn1_norm[l], ffn1_w1[l], ffn1_w3[l], ffn1_w2[l], mix_norm[l],
                                    emit_x=True, post_dtype=BF16, emit_w=True)
        x1, h2, w_in_b = _ffn_half(xp, ffn1_norm[l], *f1_b, mix_norm[l], emit_x=True, post_dtype=BF16,
                                   cast_jobs=[w_in[l]])
        z3s, ka_s, va_s = _in_proj(h2s, w_in_b)
        z3, ka, va, w_gate_b, w_out_b, *branch_b = _in_proj(
            h2, w_in_b, cast_jobs=[w_gate[l], w_out[l], w_br_moba[l], w_br_ret[l], w_br_mem[l]])
        mem_k_s, mem_v_s, x1s = lax.optimization_barrier((cache_mem_k[l], cache_mem_v[l], x1s))

        z4s = z3s.reshape(z3s.shape[0], dec_batch, A_HEADS, A_DH)
        oa, gates_s = _moba_prompt(z3, ka, va, t5_table, batch, seq, paged=(page_table, z4s, cache_k[l]))
        oa_s = _moba_step(z4s, ka_s, va_s, cache_k[l], cache_v[l], page_table, t5_table, gates=gates_s)

        o_rs, s_new_s = _ret_step(z3s, cos_s, sin_s, state_ret[l])
        z3s_pad = jnp.broadcast_to(z3s[:, :, None, :], (z3s.shape[0], dec_batch, 8, SEC))
        z3s_pad = z3s_pad.reshape(z3s.shape[0], dec_batch * 8, SEC)
        om_s = _mem_attend(z3s_pad, mem_k_s.reshape(dec_batch * n_mem, M_HEADS * M_DH),
                           mem_v_s.reshape(dec_batch * n_mem, M_HEADS * M_DH), dec_batch, 8)
        om_s = om_s.reshape(dec_batch, 8, M_HEADS * M_DH)[:, 0, :]
        branches_s = [oa_s.reshape(dec_batch, -1).astype(BF16), o_rs.reshape(dec_batch, -1).astype(BF16), om_s]

        mem_h = _norm_rows(mem_prompt.reshape(batch * n_mem, d), mem_norm[l], BF16)
        mkv_p = _matmul(mem_h, w_mem_kv[l], tn_pref=512, name="mem_kv")
        mk_p, mv_p = mkv_p[:, :M_HEADS * M_DH], mkv_p[:, M_HEADS * M_DH:]
        o_r, s_new_p, *f2_b = _ret_prompt(z3, cos_p, sin_p, batch, seq,
                                          cast_jobs=[ffn2_w1[l], ffn2_w3[l], ffn2_w2[l]])
        om = _mem_attend(z3, mk_p, mv_p, batch, seq)

        merged_s = _merge(h2s, branches_s, w_gate_b, b_gate[l], branch_b)
        merged = _merge(h2, [oa, o_r, om], w_gate_b, b_gate[l], branch_b)
        x2s = _matmul(merged_s, w_out_b, residual=x1s, name="out_proj")
        x2 = _matmul(merged, w_out_b, residual=x1, name="out_proj")
        xs = _ffn_half(x2s, ffn2_norm[l], *f2_b, g_next, emit_x=not last, post_dtype=F32)[0]
        xp = _ffn_half(x2, ffn2_norm[l], *f2_b, g_next, emit_x=not last, post_dtype=F32)[0]

        new = (ka.reshape(batch, seq, A_HEADS, A_DH), va.reshape(batch, seq, A_HEADS, A_DH),
               mk_p.reshape(batch, n_mem, M_HEADS, M_DH), mv_p.reshape(batch, n_mem, M_HEADS, M_DH), s_new_p,
               ka_s.reshape(dec_batch, dec_seq, A_HEADS, A_DH), va_s.reshape(dec_batch, dec_seq, A_HEADS, A_DH),
               s_new_s)
        for acc, val in zip(outs, new):
            acc.append(val)

    y_prompt = xp.reshape(batch, seq, d)
    y_sample = xs.reshape(dec_batch, dec_seq, d)
    return (y_prompt, y_sample) + tuple(jnp.stack(o) for o in outs)
```

```python
import functools
import math

import jax
import jax.numpy as jnp
import numpy as np
from jax import lax
from jax.experimental import pallas as pl
from jax.experimental.pallas import tpu as pltpu

F32 = jnp.float32
BF16 = jnp.bfloat16
I32 = jnp.int32

A_HEADS, A_DH = 8, 128
MOBA_BLOCK, MOBA_TOPK = 256, 3
T5_BUCKETS, T5_MAX_DIST = 32, 128
R_HEADS, R_DK, R_DV = 4, 256, 256
RET_CHUNK = 128
RET_ROPE_BASE = 10000.0
M_HEADS, M_DH = 4, 256
PAGE_SIZE = 128
EPS = 1e-6
NEG_INF = -1e30
LOG2E = math.log2(math.e)
SEC = 1024
W_IN_KA, W_IN_VA = 1, 2
(SEC_QA, SEC_QR, SEC_KR, SEC_VR, SEC_GR, SEC_QM) = range(6)

V7X_VMEM_LIMIT_BYTES = 56 * 1024 * 1024

NT_DIMS = (((1,), (1,)), ((), ()))
TN_DIMS = (((0,), (0,)), ((), ()))


def _params(*sem):
    return pltpu.CompilerParams(dimension_semantics=sem, vmem_limit_bytes=V7X_VMEM_LIMIT_BYTES)


def _tile(n, pref):
    t = min(n, pref)
    while n % t:
        t -= 1
    return t


def _rms(x, g):
    return x * lax.rsqrt(jnp.mean(x * x, axis=-1, keepdims=True) + EPS) * g


def _silu(x):
    return x * jax.nn.sigmoid(x)


def _dot(a, b):
    return jnp.dot(a, b, preferred_element_type=F32)


def _bf16(w):
    return w if w.dtype == BF16 else w.astype(BF16)


def _ffn_body(x_ref, g_ref, w1_ref, w3_ref, w2_ref, gp_ref, *refs, emit_x, emit_w, n_cast):
    refs = list(refs)
    cast_in = [refs.pop(0) for _ in range(n_cast)]
    xo_ref = refs.pop(0) if emit_x else None
    ho_ref = refs.pop(0)
    wb_refs = [refs.pop(0) for _ in range(3)] if emit_w else []
    cast_out = [refs.pop(0) for _ in range(n_cast)]
    h_sc, acc_sc = refs
    f = pl.program_id(1)

    @pl.when(f == 0)
    def _():
        h_sc[...] = _rms(x_ref[...], g_ref[...]).astype(BF16)
        acc_sc[...] = jnp.zeros_like(acc_sc)

    _cast_rows(cast_in, cast_out)

    w1, w3, w2 = _bf16(w1_ref[...]), _bf16(w3_ref[...]), _bf16(w2_ref[...])
    for wb_ref, w in zip(wb_refs, (w1, w3, w2)):
        wb_ref[...] = w
    h = h_sc[...]
    t = _silu(_dot(h, w1)) * _dot(h, w3)
    acc_sc[...] += _dot(t.astype(BF16), w2)

    @pl.when(f == pl.num_programs(1) - 1)
    def _():
        xo = x_ref[...] + 0.5 * acc_sc[...]
        if emit_x:
            xo_ref[...] = xo
        ho_ref[...] = _rms(xo, gp_ref[...]).astype(ho_ref.dtype)


BF16_SUBLANES = 16


def _cast_spec(a, grid):
    n_steps = grid[0] * grid[1]
    n_chunks = max(n for n in range(1, n_steps + 1)
                   if a.shape[0] % n == 0 and (a.shape[0] // n) % BF16_SUBLANES == 0)
    return pl.BlockSpec((a.shape[0] // n_chunks, a.shape[1]),
                        lambda i, j: (jnp.minimum(i * grid[1] + j, n_chunks - 1), 0))


def _cast_rows(src_refs, dst_refs):
    for src_ref, dst_ref in zip(src_refs, dst_refs):
        dst_ref[...] = src_ref[...].astype(BF16)


def _ffn_half(x, g, w1, w3, w2, g_post, *, emit_x, post_dtype, emit_w=False, cast_jobs=()):
    rows, d = x.shape
    ff = w1.shape[1]
    tm, tf = _tile(rows, 512), _tile(ff, 512)
    assert not emit_w or rows == tm
    grid = (rows // tm, ff // tf)
    row_spec = pl.BlockSpec((tm, d), lambda i, f: (i, 0))
    vec_spec = pl.BlockSpec((1, d), lambda i, f: (0, 0))
    w13_spec = pl.BlockSpec((d, tf), lambda i, f: (0, f))
    w2_spec = pl.BlockSpec((tf, d), lambda i, f: (f, 0))
    out_shape = [jax.ShapeDtypeStruct((rows, d), post_dtype)]
    out_specs = [row_spec]
    if emit_x:
        out_shape = [jax.ShapeDtypeStruct((rows, d), F32)] + out_shape
        out_specs = [row_spec] + out_specs
    if emit_w:
        out_shape += [jax.ShapeDtypeStruct(w.shape, BF16) for w in (w1, w3, w2)]
        out_specs += [w13_spec, w13_spec, w2_spec]

    cast_specs = [_cast_spec(a, grid) for a in cast_jobs]
    out_shape += [jax.ShapeDtypeStruct(a.shape, BF16) for a in cast_jobs]
    out_specs += cast_specs
    return pl.pallas_call(
        functools.partial(_ffn_body, emit_x=emit_x, emit_w=emit_w, n_cast=len(cast_jobs)),
        grid=grid,
        in_specs=[row_spec, vec_spec, w13_spec, w13_spec, w2_spec, vec_spec] + cast_specs,
        out_specs=out_specs,
        out_shape=out_shape,
        scratch_shapes=[pltpu.VMEM((tm, d), BF16), pltpu.VMEM((tm, d), F32)],
        compiler_params=_params("parallel", "arbitrary"),
        name="ffn_half",
    )(x, g.reshape(1, d), w1, w3, w2, g_post.reshape(1, d), *cast_jobs)


def _norm_body(x_ref, g_ref, o_ref):
    o_ref[...] = _rms(x_ref[...], g_ref[...]).astype(o_ref.dtype)


def _norm_rows(x, g, out_dtype):
    rows, d = x.shape
    tm = _tile(rows, 512)
    return pl.pallas_call(
        _norm_body,
        grid=(rows // tm,),
        in_specs=[pl.BlockSpec((tm, d), lambda i: (i, 0)), pl.BlockSpec((1, d), lambda i: (0, 0))],
        out_specs=pl.BlockSpec((tm, d), lambda i: (i, 0)),
        out_shape=jax.ShapeDtypeStruct((rows, d), out_dtype),
        compiler_params=_params("parallel"),
        name="rmsnorm_rows",
    )(x, g.reshape(1, d))


def _mm_body(a_ref, w_ref, *refs, has_residual):
    o_ref = refs[-1]
    acc = _dot(a_ref[...], _bf16(w_ref[...]))
    if has_residual:
        acc = refs[0][...] + acc
    o_ref[...] = acc.astype(o_ref.dtype)


def _in_proj_body(a_ref, w_ref, *refs):
    n_cast = (len(refs) - 3) // 2
    z_ref, k_ref, v_ref = refs[n_cast:n_cast + 3]
    _cast_rows(refs[:n_cast], refs[n_cast + 3:])
    j = pl.program_id(1)
    acc = _dot(a_ref[...], w_ref[...])

    @pl.when(j == W_IN_KA)
    def _():
        k_ref[...] = acc

    @pl.when(j == W_IN_VA)
    def _():
        v_ref[...] = acc

    @pl.when((j != W_IN_KA) & (j != W_IN_VA))
    def _():
        z_ref[...] = acc


def _in_proj(a, w, cast_jobs=()):
    rows, k = a.shape
    n_sec = w.shape[1] // SEC
    assert (W_IN_KA, W_IN_VA) == (1, 2) and n_sec == 8
    tm = _tile(rows, 1024)
    grid = (rows // tm, n_sec)
    kv_spec = pl.BlockSpec((tm, SEC), lambda i, j: (i, 0))
    cast_specs = [_cast_spec(c, grid) for c in cast_jobs]
    return pl.pallas_call(
        _in_proj_body,
        grid=grid,
        in_specs=[pl.BlockSpec((tm, k), lambda i, j: (i, 0)), pl.BlockSpec((k, SEC), lambda i, j: (0, j))]
        + cast_specs,
        out_specs=[pl.BlockSpec((None, tm, SEC), lambda i, j: (jnp.maximum(j - 2, 0), i, 0)), kv_spec, kv_spec]
        + cast_specs,
        out_shape=[jax.ShapeDtypeStruct((n_sec - 2, rows, SEC), F32)] + [jax.ShapeDtypeStruct((rows, SEC), F32)] * 2
        + [jax.ShapeDtypeStruct(c.shape, BF16) for c in cast_jobs],
        compiler_params=_params("parallel", "arbitrary"),
        name="in_proj",
    )(a, w, *cast_jobs)


def _matmul(a, w, *, residual=None, out_dtype=F32, tn_pref=1024, name="matmul"):
    rows, k = a.shape
    n_cols = w.shape[1]
    tm, tn = _tile(rows, 1024), _tile(n_cols, tn_pref)
    in_specs = [pl.BlockSpec((tm, k), lambda i, j: (i, 0)), pl.BlockSpec((k, tn), lambda i, j: (0, j))]
    out_spec = pl.BlockSpec((tm, tn), lambda i, j: (i, j))
    args = [a, w]
    if residual is not None:
        in_specs.append(out_spec)
        args.append(residual)
    return pl.pallas_call(
        functools.partial(_mm_body, has_residual=residual is not None),
        grid=(rows // tm, n_cols // tn),
        in_specs=in_specs,
        out_specs=out_spec,
        out_shape=jax.ShapeDtypeStruct((rows, n_cols), out_dtype),
        compiler_params=_params("parallel", "arbitrary"),
        name=name,
    )(*args)


def _t5_bucket(n):
    n = jnp.maximum(n, 0)
    max_exact = T5_BUCKETS // 2
    nf = jnp.maximum(n, 1).astype(F32)
    large = max_exact + (jnp.log(nf / max_exact) / math.log(T5_MAX_DIST / max_exact)
                         * (T5_BUCKETS - max_exact)).astype(I32)
    return jnp.where(n < max_exact, n, jnp.minimum(large, T5_BUCKETS - 1))


PAGES_PER_BLOCK = MOBA_BLOCK // PAGE_SIZE
SUBLANES = 8


def _block_gates(q, page_refs, first_block, gates):
    lane = lax.broadcasted_iota(I32, gates.shape, 1)
    for i in range(len(page_refs) // PAGES_PER_BLOCK):
        pages = [page_refs[i * PAGES_PER_BLOCK + t][...] for t in range(PAGES_PER_BLOCK)]
        ksum = functools.reduce(jnp.add, [jnp.sum(x, axis=0) for x in pages])
        gate = jnp.sum(q * ksum, axis=1, keepdims=True) / MOBA_BLOCK
        gates = jnp.where(lane == first_block + i, gate, gates)
    return gates


PAGE_RING = 3


def _moba_prompt_body(*refs, nb, side_blocks, grid):
    h, b, c = pl.program_id(0), pl.program_id(1), pl.program_id(2)
    if side_blocks:
        (pt_ref, t5_ref, q_ref, k_ref, v_ref, qside_ref, cache_ref, o_ref, gates_ref,
         kb_sc, vb_sc, kdiff_sc, bias_sc, pages_sc, page_sem) = refs
        n_pages = side_blocks * PAGES_PER_BLOCK
        steps_per_seq = grid[2]
        n_steps = grid[0] * grid[1] * grid[2]
        step_idx = (h * grid[1] + b) * steps_per_seq + c

        def page_copies(s):
            seq, part, slot = s // steps_per_seq, s % steps_per_seq, s % PAGE_RING
            return [pltpu.make_async_copy(cache_ref.at[pt_ref[seq, part * n_pages + t]], pages_sc.at[slot, t],
                                          page_sem.at[slot]) for t in range(n_pages)]

        @pl.when(step_idx == 0)
        def _():
            for s in range(min(PAGE_RING - 1, n_steps)):
                for cp in page_copies(s):
                    cp.start()

        @pl.when(step_idx + PAGE_RING - 1 < n_steps)
        def _():
            for cp in page_copies(step_idx + PAGE_RING - 1):
                cp.start()

        for cp in page_copies(step_idx):
            cp.wait()
        slot = step_idx % PAGE_RING
        page_refs = [pages_sc.at[slot, t] for t in range(n_pages)]
    else:
        t5_ref, q_ref, k_ref, v_ref, o_ref, kb_sc, vb_sc, kdiff_sc, bias_sc = refs
    blk = MOBA_BLOCK
    nbp = SUBLANES
    assert nb <= nbp
    scale = A_DH ** -0.5
    ii = lax.broadcasted_iota(I32, (blk, blk), 0)
    jj = lax.broadcasted_iota(I32, (blk, blk), 1)

    @pl.when((b == 0) & (c == 0))
    def _():
        for t in range(2):
            bucket = _t5_bucket(t * blk + ii - jj)
            tile = jnp.zeros((blk, blk), F32)
            for bkt in range(T5_BUCKETS):
                tile = jnp.where(bucket == bkt, t5_ref[bkt, h], tile)
            tile = tile * LOG2E
            bias_sc[t] = jnp.where(ii >= jj, tile, NEG_INF) if t == 0 else tile

    @pl.when(c == 0)
    def _():
        km = jnp.concatenate(
            [jnp.mean(k_ref[n * blk:(n + 1) * blk, :], axis=0, keepdims=True) for n in range(nb)]
            + [jnp.zeros((1, A_DH), F32)] * (nbp - nb), axis=0)
        for n in range(nb):
            kdiff_sc[n * nbp:(n + 1) * nbp, :] = km[n:n + 1, :] - km
            kb_sc[n * blk:(n + 1) * blk, A_DH:2 * A_DH] = jnp.where(
                lax.broadcasted_iota(I32, (blk, 128), 1) == n, 1.0, 0.0).astype(BF16)
        kb_sc[:, 0:A_DH] = k_ref[...].astype(BF16)
        vb_sc[:, 0:A_DH] = v_ref[...].astype(BF16)
        vb_sc[:, A_DH:2 * A_DH] = jnp.ones((nb * blk, A_DH), BF16)
        if side_blocks:
            gates_ref[...] = jnp.zeros_like(gates_ref)

    far_bias = t5_ref[T5_BUCKETS - 1, h] * LOG2E

    def attend(cc):
        q = q_ref[cc * blk:(cc + 1) * blk, :]
        qs = (q * (scale * LOG2E)).astype(BF16)
        n_keys = (cc + 1) * blk
        if cc <= MOBA_TOPK:
            s_all = lax.dot_general(qs, kb_sc[0:n_keys, 0:A_DH], NT_DIMS, preferred_element_type=F32)
        else:
            diff_t = lax.dot_general(kdiff_sc[0:cc * nbp, :], q, NT_DIMS, precision=lax.Precision.HIGHEST,
                                     preferred_element_type=F32)
            pair = lax.broadcasted_iota(I32, diff_t.shape, 0)
            m_of, n_of = jnp.right_shift(pair, int(math.log2(nbp))), jnp.bitwise_and(pair, nbp - 1)
            beats_t = jnp.where(diff_t > 0, 1.0, jnp.where(diff_t == 0, jnp.where(n_of > m_of, 1.0, 0.0), 0.0))
            fold = jnp.where(jnp.bitwise_and(lax.broadcasted_iota(I32, (cc * nbp, 128), 0), nbp - 1)
                             == lax.broadcasted_iota(I32, (cc * nbp, 128), 1), 1.0, 0.0)
            rank = lax.dot_general(beats_t.astype(BF16), fold.astype(BF16), TN_DIMS,
                                   preferred_element_type=F32)
            lane = lax.broadcasted_iota(I32, rank.shape, 1)
            sel = jnp.where(lane < cc, jnp.where(rank < MOBA_TOPK, 0.0, NEG_INF), 0.0)
            q_sel = jnp.concatenate([qs, sel.astype(BF16)], axis=1)
            s_all = lax.dot_general(q_sel, kb_sc[0:n_keys, :], NT_DIMS, preferred_element_type=F32)
        tiles = []
        for n in range(cc + 1):
            t = s_all[:, n * blk:(n + 1) * blk]
            tiles.append(t + (bias_sc[0] if n == cc else bias_sc[1] if n == cc - 1 else far_bias))
        m = jnp.max(functools.reduce(jnp.maximum, tiles), axis=1, keepdims=True)
        p_all = jnp.concatenate([jnp.exp2(t - m).astype(BF16) for t in tiles], axis=1)
        pv = _dot(p_all, vb_sc[0:n_keys, :])
        o_ref[cc * blk:(cc + 1) * blk, :] = (pv[:, 0:A_DH] / pv[:, A_DH:2 * A_DH]).astype(o_ref.dtype)

    def step(s):
        if side_blocks:
            gates_ref[...] = _block_gates(qside_ref[...], page_refs, s * side_blocks, gates_ref[...])
        for cc in sorted({s, nb - 1 - s}):
            attend(cc)

    for s in range(_moba_steps(nb)):
        pl.when(c == s)(functools.partial(step, s))


def _moba_steps(nb):
    return (nb + 1) // 2


def _moba_prompt(z3, k, v, t5_table, batch, seq, paged=None):
    blk = MOBA_BLOCK
    nb = seq // blk
    steps = _moba_steps(nb)
    side_blocks = 0
    if paged is not None:
        page_table, q_side, cache_k = paged
        nb_past = page_table.shape[1] // PAGES_PER_BLOCK
        if page_table.shape[0] == A_HEADS * batch and nb_past % steps == 0:
            side_blocks = nb_past // steps

    seq_spec = pl.BlockSpec((seq, A_DH), lambda h, b, c, *_: (b, h))
    in_specs = [pl.BlockSpec(memory_space=pltpu.SMEM),
                pl.BlockSpec((None, seq, A_DH), lambda h, b, c, *_: (SEC_QA, b, h)), seq_spec, seq_spec]
    out_specs = [seq_spec]
    out_shape = [jax.ShapeDtypeStruct((batch * seq, A_HEADS * A_DH), BF16)]
    args = [t5_table, z3, k, v]
    scratch = [pltpu.VMEM((seq, 2 * A_DH), BF16), pltpu.VMEM((seq, 2 * A_DH), BF16),
               pltpu.VMEM((SUBLANES * SUBLANES, A_DH), F32), pltpu.VMEM((2, blk, blk), F32)]
    grid = (A_HEADS, batch, steps)
    if side_blocks:
        in_specs += [pl.BlockSpec((None, None, A_HEADS, A_DH), lambda h, b, c, pt: (SEC_QA, h * batch + b, 0, 0)),
                     pl.BlockSpec(memory_space=pl.ANY)]
        out_specs.append(pl.BlockSpec((None, A_HEADS, 128), lambda h, b, c, pt: (h * batch + b, 0, 0)))
        out_shape.append(jax.ShapeDtypeStruct((A_HEADS * batch, A_HEADS, 128), F32))
        args = [page_table] + args + [q_side, cache_k]
        scratch += [pltpu.VMEM((PAGE_RING, side_blocks * PAGES_PER_BLOCK, PAGE_SIZE, A_HEADS, A_DH), F32),
                    pltpu.SemaphoreType.DMA((PAGE_RING,))]
    out = pl.pallas_call(
        functools.partial(_moba_prompt_body, nb=nb, side_blocks=side_blocks, grid=grid),
        grid_spec=pltpu.PrefetchScalarGridSpec(
            num_scalar_prefetch=1 if side_blocks else 0,
            grid=grid,
            in_specs=in_specs,
            out_specs=out_specs,
            scratch_shapes=scratch,
        ),
        out_shape=out_shape,
        compiler_params=_params("arbitrary", "arbitrary", "arbitrary"),
        name="moba_prompt",
    )(*args)
    if paged is None:
        return out[0]
    return out[0], (out[1] if side_blocks else None)


def _rope_body(inv_ref, cos_ref, sin_ref, *, pos0):
    pos = pos0 + lax.broadcasted_iota(I32, cos_ref.shape, 0)
    ang = pos.astype(F32) * inv_ref[...]
    cos_ref[...] = jnp.cos(ang)
    sin_ref[...] = jnp.sin(ang)


def _rope_tables(pos0, n_pos):
    half = R_DK // 2
    inv = 1.0 / (RET_ROPE_BASE ** jnp.linspace(0.0, 1.0, half, dtype=F32))
    rows = -(-n_pos // 8) * 8
    return pl.pallas_call(
        functools.partial(_rope_body, pos0=pos0),
        out_shape=[jax.ShapeDtypeStruct((rows, half), F32)] * 2,
        name="rope_tables",
    )(inv.reshape(1, half))


def _rotate(x, cos, sin):
    half = x.shape[-1] // 2
    x1, x2 = x[:, :half], x[:, half:]
    return jnp.concatenate([x1 * cos - x2 * sin, x1 * sin + x2 * cos], axis=1)


def _log_decay(h, shape):
    hf = jnp.full(shape, h, I32).astype(F32)
    return jnp.log(1.0 - jnp.exp2(-5.0 - hf))


def _ret_prompt_body(q_ref, k_ref, v_ref, g_ref, cos_ref, sin_ref, o_ref, so_ref, s_sc, *, c):
    n = pl.program_id(1)

    @pl.when(n == 0)
    def _():
        s_sc[...] = jnp.zeros_like(s_sc)

    i = lax.broadcasted_iota(I32, (c, c), 0).astype(F32)
    j = lax.broadcasted_iota(I32, (c, c), 1).astype(F32)
    diff = i - j
    i_col = lax.broadcasted_iota(I32, (c, 1), 0).astype(F32)
    cos, sin = cos_ref[...], sin_ref[...]

    new_states, outs = [], []
    for h in range(R_HEADS):
        dk, dv = slice(h * R_DK, (h + 1) * R_DK), slice(h * R_DV, (h + 1) * R_DV)
        dmask = jnp.where(diff >= 0, jnp.exp(jnp.maximum(diff, 0.0) * _log_decay(h, (c, c))), 0.0)
        lg_col = _log_decay(h, (c, 1))
        q_dec = jnp.exp((i_col + 1.0) * lg_col)
        k_dec = jnp.exp((c - 1.0 - i_col) * lg_col)
        c_dec = jnp.exp(c * _log_decay(h, (1, R_DV)))
        qr = _rotate(q_ref[:, dk], cos, sin)
        kr = _rotate(k_ref[:, dk], cos, sin) * (R_DK ** -0.5)
        vb = v_ref[:, dv].astype(BF16)
        s = s_sc[h]
        att = lax.dot_general(qr.astype(BF16), kr.astype(BF16), NT_DIMS, preferred_element_type=F32) * dmask
        o = _dot(att.astype(BF16), vb) + _dot((qr * q_dec).astype(BF16), s.astype(BF16))
        new_states.append(s * c_dec + lax.dot_general((kr * k_dec).astype(BF16), vb, TN_DIMS,
                                                      preferred_element_type=F32))
        on = o * lax.rsqrt(jnp.mean(o * o, axis=-1, keepdims=True) + EPS)
        outs.append((on * _silu(g_ref[:, dv])).astype(o_ref.dtype))
    for h in range(R_HEADS):
        s_sc[h] = new_states[h]
    o_ref[...] = jnp.concatenate(outs, axis=1)

    @pl.when(n == pl.num_programs(1) - 1)
    def _():
        so_ref[...] = s_sc[...]


def _ret_prompt(z3, cos, sin, batch, seq):
    c = math.gcd(seq, RET_CHUNK)
    nc = seq // c

    def sec(s):
        return pl.BlockSpec((None, c, SEC), lambda b, n: (s, b * nc + n, 0))

    tab = pl.BlockSpec((c, R_DK // 2), lambda b, n: (n, 0))
    return pl.pallas_call(
        functools.partial(_ret_prompt_body, c=c),
        grid=(batch, nc),
        in_specs=[sec(SEC_QR), sec(SEC_KR), sec(SEC_VR), sec(SEC_GR), tab, tab],
        out_specs=[pl.BlockSpec((c, R_HEADS * R_DV), lambda b, n: (b * nc + n, 0)),
                   pl.BlockSpec((None, R_HEADS, R_DK, R_DV), lambda b, n: (b, 0, 0, 0))],
        out_shape=[jax.ShapeDtypeStruct((batch * seq, R_HEADS * R_DV), BF16),
                   jax.ShapeDtypeStruct((batch, R_HEADS, R_DK, R_DV), F32)],
        scratch_shapes=[pltpu.VMEM((R_HEADS, R_DK, R_DV), F32)],
        compiler_params=_params("parallel", "arbitrary"),
        name="retention_prompt",
    )(z3, z3, z3, z3, cos, sin)


def _ret_step_body(q_ref, k_ref, v_ref, g_ref, cos_ref, sin_ref, s_ref, o_ref, so_ref):
    cos, sin = cos_ref[0:1, :], sin_ref[0:1, :]
    rows = 16
    row0 = lax.broadcasted_iota(I32, (rows, R_DK), 0) == 0
    for h in range(R_HEADS):
        decay = jnp.exp(_log_decay(h, (1, R_DV)))
        qr = _rotate(q_ref[h:h + 1, :], cos, sin)
        kr = _rotate(k_ref[h:h + 1, :], cos, sin) * (R_DK ** -0.5)
        v = v_ref[h:h + 1, :]
        s = s_ref[h]
        q_rows = jnp.broadcast_to(qr * decay, (rows, R_DK)).astype(BF16)
        qs = _dot(q_rows, s.astype(BF16))[0:1, :]
        o = jnp.sum(qr * kr, axis=-1, keepdims=True) * v + qs
        k_rows = jnp.where(row0, jnp.broadcast_to(kr, (rows, R_DK)), 0.0).astype(BF16)
        v_rows = jnp.broadcast_to(v, (rows, R_DV)).astype(BF16)
        so_ref[h] = s * decay + lax.dot_general(k_rows, v_rows, TN_DIMS, preferred_element_type=F32)
        on = o * lax.rsqrt(jnp.mean(o * o, axis=-1, keepdims=True) + EPS)
        o_ref[h:h + 1, :] = (on * _silu(g_ref[h:h + 1, :])).astype(o_ref.dtype)


def _mem_rows(q, k_ref, v_ref):
    outs = []
    for h in range(M_HEADS):
        dh = slice(h * M_DH, (h + 1) * M_DH)
        s = lax.dot_general(q[:, dh].astype(BF16), k_ref[:, dh].astype(BF16), NT_DIMS,
                            preferred_element_type=F32) * (M_DH ** -0.5)
        m = jnp.max(s, axis=-1, keepdims=True)
        p = jnp.exp(s - m)
        l = jnp.sum(p, axis=-1, keepdims=True)
        outs.append(_dot(p.astype(BF16), v_ref[:, dh].astype(BF16)) / l)
    return jnp.concatenate(outs, axis=1)


def _mem_body(q_ref, k_ref, v_ref, o_ref):
    o_ref[...] = _mem_rows(q_ref[...], k_ref, v_ref).astype(o_ref.dtype)


def _mem_attend(z3, mk, mv, batch, seq):
    n_mem = mk.shape[0] // batch
    width = M_HEADS * M_DH
    ts = _tile(seq, 512)
    nt = seq // ts
    kv = pl.BlockSpec((n_mem, width), lambda b, t: (b, 0))
    return pl.pallas_call(
        _mem_body,
        grid=(batch, nt),
        in_specs=[pl.BlockSpec((None, ts, width), lambda b, t: (SEC_QM, b * nt + t, 0)), kv, kv],
        out_specs=pl.BlockSpec((ts, width), lambda b, t: (b * nt + t, 0)),
        out_shape=jax.ShapeDtypeStruct((batch * seq, width), BF16),
        compiler_params=_params("parallel", "arbitrary"),
        name="mem_attend",
    )(z3, mk, mv)


def _merge_body(h_ref, *refs):
    x_refs, wg_refs, bg_refs, wb_refs, o_ref = refs[0:3], refs[3:6], refs[6:9], refs[9:12], refs[12]
    h = h_ref[...]
    merged = None
    for g in range(3):
        term = jax.nn.sigmoid(_dot(h, wg_refs[g][...]) + bg_refs[g][...]) * _dot(x_refs[g][...], wb_refs[g][...])
        merged = term if merged is None else merged + term
    o_ref[...] = merged.astype(o_ref.dtype)


def _merge(h, branches, w_gate, b_gate, branch_ws):
    rows, d = h.shape
    tm, tn = _tile(rows, 1024), _tile(d, 512)
    nj = d // tn
    b_gate = b_gate.reshape(1, 3 * d)

    def row(width):
        return pl.BlockSpec((tm, width), lambda i, j: (i, 0))

    def cols(n_rows, group=0):
        return pl.BlockSpec((n_rows, tn), lambda i, j: (0, group * nj + j))

    return pl.pallas_call(
        _merge_body,
        grid=(rows // tm, nj),
        in_specs=([row(d)] + [row(x.shape[1]) for x in branches]
                  + [cols(d, g) for g in range(3)] + [cols(1, g) for g in range(3)]
                  + [cols(w.shape[0]) for w in branch_ws]),
        out_specs=pl.BlockSpec((tm, tn), lambda i, j: (i, j)),
        out_shape=jax.ShapeDtypeStruct((rows, d), BF16),
        compiler_params=_params("parallel", "arbitrary"),
        name="gated_merge",
    )(h, *branches, w_gate, w_gate, w_gate, b_gate, b_gate, b_gate, *branch_ws)


def _moba_gate_body(pt_ref, q_ref, *refs, bps):
    del pt_ref
    k_refs, gates_ref = refs[:-1], refs[-1]
    n = pl.program_id(1)

    @pl.when(n == 0)
    def _():
        gates_ref[...] = jnp.zeros_like(gates_ref)

    gates_ref[...] = _block_gates(q_ref[...], k_refs, n * bps, gates_ref[...])


def _moba_select_body(g_ref, sel_ref, *, nb_past):
    lane = lax.broadcasted_iota(I32, g_ref.shape, 1)
    gw = jnp.where(lane < nb_past, g_ref[...], NEG_INF)
    sel = jnp.zeros(gw.shape, I32)
    for r in range(MOBA_TOPK):
        mx = jnp.max(gw, axis=1, keepdims=True)
        idx = jnp.min(jnp.where(gw == mx, lane, 128), axis=1, keepdims=True)
        sel = jnp.where(lane == r, idx, sel)
        gw = jnp.where(lane == idx, -jnp.inf, gw)
    sel_ref[...] = sel


def _moba_attend_body(pt_ref, sel_ref, t5_ref, q_ref, kn_ref, vn_ref, ck_ref, cv_ref, o_ref, kbuf, vbuf, sem,
                      *, past_len):
    b = pl.program_id(0)
    slot = lax.rem(b, 2)
    scale = A_DH ** -0.5
    n_tiles = MOBA_TOPK * PAGES_PER_BLOCK

    def tile_copies(bb, to_slot):
        copies = []
        for h in range(A_HEADS):
            for r in range(MOBA_TOPK):
                block = sel_ref[bb, h * MOBA_TOPK + r]
                for t in range(PAGES_PER_BLOCK):
                    page = pt_ref[bb, block * PAGES_PER_BLOCK + t]
                    j = r * PAGES_PER_BLOCK + t
                    copies.append(pltpu.make_async_copy(ck_ref.at[page, :, h, :], kbuf.at[to_slot, h, j],
                                                        sem.at[to_slot]))
                    copies.append(pltpu.make_async_copy(cv_ref.at[page, :, h, :], vbuf.at[to_slot, h, j],
                                                        sem.at[to_slot]))
        return copies

    def start_all(copies):
        for i, cp in enumerate(copies):
            cp.start(priority=i % 2)

    @pl.when(b == 0)
    def _():
        start_all(tile_copies(0, 0))

    @pl.when(b + 1 < pl.num_programs(0))
    def _():
        start_all(tile_copies(b + 1, 1 - slot))

    for cp in tile_copies(b, slot):
        cp.wait()

    q, kn, vn = q_ref[...], kn_ref[...], vn_ref[...]
    outs = []
    lane = lax.broadcasted_iota(I32, (1, PAGE_SIZE), 1)
    for h in range(A_HEADS):
        qh = q[h:h + 1, :]
        q8 = jnp.broadcast_to(qh, (8, A_DH)).astype(BF16)
        s = []
        for r in range(MOBA_TOPK):
            block = sel_ref[b, h * MOBA_TOPK + r]
            for t in range(PAGES_PER_BLOCK):
                kt = kbuf[slot, h, r * PAGES_PER_BLOCK + t].astype(BF16)
                raw = lax.dot_general(q8, kt, NT_DIMS, preferred_element_type=F32)[0:1, :]
                bucket = _t5_bucket(past_len - (block * MOBA_BLOCK + t * PAGE_SIZE + lane))
                bias = jnp.zeros((1, PAGE_SIZE), F32)
                for bkt in range(T5_BUCKETS):
                    bias = jnp.where(bucket == bkt, t5_ref[bkt, h], bias)
                s.append(raw * scale + bias)
        s_own = jnp.sum(qh * kn[h:h + 1, :], axis=1, keepdims=True) * scale + t5_ref[0, h]
        m = jnp.maximum(jnp.max(functools.reduce(jnp.maximum, s), axis=1, keepdims=True), s_own)
        p = [jnp.exp(x - m) for x in s]
        p_own = jnp.exp(s_own - m)
        l = jnp.sum(functools.reduce(jnp.add, p), axis=1, keepdims=True) + p_own
        acc = p_own * vn[h:h + 1, :]
        for j in range(n_tiles):
            p8 = jnp.broadcast_to(p[j], (8, PAGE_SIZE)).astype(BF16)
            acc = acc + _dot(p8, vbuf[slot, h, j].astype(BF16))[0:1, :]
        outs.append(acc / l)
    o_ref[...] = jnp.concatenate(outs, axis=0)


def _moba_gates(q4, cache_k, page_table):
    batch, n_pages = page_table.shape
    nb_past = n_pages // PAGES_PER_BLOCK
    bps = _tile(nb_past, 8)

    def page(t):
        return pl.BlockSpec((None, PAGE_SIZE, A_HEADS, A_DH),
                            lambda b, n, pt: (pt[b, n * bps * PAGES_PER_BLOCK + t], 0, 0, 0))

    pages = [page(t) for t in range(bps * PAGES_PER_BLOCK)]
    return pl.pallas_call(
        functools.partial(_moba_gate_body, bps=bps),
        grid_spec=pltpu.PrefetchScalarGridSpec(
            num_scalar_prefetch=1,
            grid=(batch, nb_past // bps),
            in_specs=[pl.BlockSpec((None, None, A_HEADS, A_DH), lambda b, n, pt: (SEC_QA, b, 0, 0))] + pages,
            out_specs=pl.BlockSpec((None, A_HEADS, 128), lambda b, n, pt: (b, 0, 0)),
        ),
        out_shape=jax.ShapeDtypeStruct((batch, A_HEADS, 128), F32),
        compiler_params=_params("parallel", "arbitrary"),
        name="moba_gate",
    )(page_table, q4, *([cache_k] * len(pages)))


def _sample_mixers_body(pt_ref, sel_ref, t5_ref, qa_ref, kn_ref, vn_ref, ck_ref, cv_ref,
                        qr_ref, kr_ref, vr_ref, gr_ref, cos_ref, sin_ref, state_ref, qm_ref, mk_ref, mv_ref,
                        oa_ref, or_ref, state_out_ref, om_ref, kbuf, vbuf, sem, *, past_len):
    _moba_attend_body(pt_ref, sel_ref, t5_ref, qa_ref, kn_ref, vn_ref, ck_ref, cv_ref, oa_ref, kbuf, vbuf, sem,
                      past_len=past_len)
    _ret_step_body(qr_ref, kr_ref, vr_ref, gr_ref, cos_ref, sin_ref, state_ref, or_ref, state_out_ref)
    q_rows = jnp.broadcast_to(qm_ref[...], (BF16_SUBLANES, qm_ref.shape[-1]))
    om_ref[...] = _mem_rows(q_rows, mk_ref, mv_ref)[0:1, :]


def _sample_mixers(z3, k_new, v_new, cache_k, cache_v, page_table, t5_table, gates, cos, sin, state, mem_k, mem_v):
    batch, n_pages = page_table.shape
    past_len = n_pages * PAGE_SIZE
    assert past_len % MOBA_BLOCK == 0 and MOBA_BLOCK % PAGE_SIZE == 0
    nb_past = past_len // MOBA_BLOCK
    assert MOBA_TOPK <= nb_past <= 128
    n_sec = z3.shape[0]
    n_mem = mem_k.shape[0] // batch
    z4a = z3.reshape(n_sec, batch, A_HEADS, A_DH)
    z4r = z3.reshape(n_sec, batch, R_HEADS, R_DK)
    z4m = z3.reshape(n_sec, batch, 1, M_HEADS * M_DH)
    k_new = k_new.reshape(batch, A_HEADS, A_DH)
    v_new = v_new.reshape(batch, A_HEADS, A_DH)
    if gates is None:
        gates = _moba_gates(z4a, cache_k, page_table)
    sel = pl.pallas_call(
        functools.partial(_moba_select_body, nb_past=nb_past),
        out_shape=jax.ShapeDtypeStruct((batch * A_HEADS, 128), I32),
        name="moba_select",
    )(gates.reshape(batch * A_HEADS, 128))
    sel = sel[:, :MOBA_TOPK].reshape(batch, A_HEADS * MOBA_TOPK)

    def sec(s, rows, width):
        return pl.BlockSpec((None, None, rows, width), lambda b, pt, sl: (s, b, 0, 0))

    def per_seq(*shape):
        return pl.BlockSpec((None,) + shape, lambda b, pt, sl: (b,) + (0,) * len(shape))

    tab = pl.BlockSpec((SUBLANES, R_DK // 2), lambda b, pt, sl: (0, 0))
    mem = pl.BlockSpec((n_mem, M_HEADS * M_DH), lambda b, pt, sl: (b, 0))
    tiles = pltpu.VMEM((2, A_HEADS, MOBA_TOPK * PAGES_PER_BLOCK, PAGE_SIZE, A_DH), F32)
    return pl.pallas_call(
        functools.partial(_sample_mixers_body, past_len=past_len),
        grid_spec=pltpu.PrefetchScalarGridSpec(
            num_scalar_prefetch=2,
            grid=(batch,),
            in_specs=[pl.BlockSpec(memory_space=pltpu.SMEM), sec(SEC_QA, A_HEADS, A_DH),
                      per_seq(A_HEADS, A_DH), per_seq(A_HEADS, A_DH),
                      pl.BlockSpec(memory_space=pl.ANY), pl.BlockSpec(memory_space=pl.ANY),
                      sec(SEC_QR, R_HEADS, R_DK), sec(SEC_KR, R_HEADS, R_DK), sec(SEC_VR, R_HEADS, R_DK),
                      sec(SEC_GR, R_HEADS, R_DK), tab, tab, per_seq(R_HEADS, R_DK, R_DV),
                      sec(SEC_QM, 1, M_HEADS * M_DH), mem, mem],
            out_specs=[per_seq(A_HEADS, A_DH), per_seq(R_HEADS, R_DV), per_seq(R_HEADS, R_DK, R_DV),
                       per_seq(1, M_HEADS * M_DH)],
            scratch_shapes=[tiles, tiles, pltpu.SemaphoreType.DMA((2,))],
        ),
        out_shape=[jax.ShapeDtypeStruct((batch, A_HEADS, A_DH), F32),
                   jax.ShapeDtypeStruct((batch, R_HEADS, R_DV), F32),
                   jax.ShapeDtypeStruct(state.shape, F32),
                   jax.ShapeDtypeStruct((batch, 1, M_HEADS * M_DH), F32)],
        compiler_params=_params("arbitrary"),
        name="sample_mixers",
    )(page_table, sel, t5_table, z4a, k_new, v_new, cache_k, cache_v, z4r, z4r, z4r, z4r, cos, sin, state,
      z4m, mem_k, mem_v)


def kernel(x_prompt, x_sample, mem_prompt, cache_k, cache_v, cache_mem_k, cache_mem_v, state_ret, page_table, t5_table, ffn1_norm, ffn1_w1, ffn1_w3, ffn1_w2, mix_norm, mem_norm, w_in, w_mem_kv, w_gate, b_gate, w_br_moba, w_br_ret, w_br_mem, w_out, ffn2_norm, ffn2_w1, ffn2_w3, ffn2_w2, final_norm):
    batch, seq, d = x_prompt.shape
    dec_batch, dec_seq, _ = x_sample.shape
    assert dec_seq == 1
    depth = w_in.shape[0]
    n_mem = mem_prompt.shape[1]
    past_len = page_table.shape[1] * PAGE_SIZE
    cos_p, sin_p = _rope_tables(0, seq)
    cos_s, sin_s = _rope_tables(past_len, dec_seq)

    xp = x_prompt.reshape(batch * seq, d)
    xs = x_sample.reshape(dec_batch, d)
    outs = [[] for _ in range(8)]
    for l in range(depth):
        last = l == depth - 1
        g_next = final_norm if last else ffn1_norm[l + 1]

        x1s, h2s, *f1_b = _ffn_half(xs, ffn1_norm[l], ffn1_w1[l], ffn1_w3[l], ffn1_w2[l], mix_norm[l],
                                    emit_x=True, post_dtype=BF16, emit_w=True)
        x1, h2, w_in_b, *f2_b = _ffn_half(xp, ffn1_norm[l], *f1_b, mix_norm[l], emit_x=True, post_dtype=BF16,
                                          cast_jobs=[w_in[l], ffn2_w1[l], ffn2_w3[l], ffn2_w2[l]])
        z3s, ka_s, va_s = _in_proj(h2s, w_in_b)
        z3, ka, va, w_gate_b, w_out_b, *branch_b = _in_proj(
            h2, w_in_b, cast_jobs=[w_gate[l], w_out[l], w_br_moba[l], w_br_ret[l], w_br_mem[l]])
        mem_k_s, mem_v_s, x1s = lax.optimization_barrier((cache_mem_k[l], cache_mem_v[l], x1s))

        z4s = z3s.reshape(z3s.shape[0], dec_batch, A_HEADS, A_DH)
        oa, gates_s = _moba_prompt(z3, ka, va, t5_table, batch, seq, paged=(page_table, z4s, cache_k[l]))

        oa_s, o_rs, s_new_s, om_s = _sample_mixers(
            z3s, ka_s, va_s, cache_k[l], cache_v[l], page_table, t5_table, gates_s, cos_s, sin_s, state_ret[l],
            mem_k_s.reshape(dec_batch * n_mem, M_HEADS * M_DH), mem_v_s.reshape(dec_batch * n_mem, M_HEADS * M_DH))
        branches_s = [o.reshape(dec_batch, -1).astype(BF16) for o in (oa_s, o_rs, om_s)]

        mem_h = _norm_rows(mem_prompt.reshape(batch * n_mem, d), mem_norm[l], BF16)
        mkv_p = _matmul(mem_h, w_mem_kv[l], tn_pref=512, name="mem_kv")
        mk_p, mv_p = mkv_p[:, :M_HEADS * M_DH], mkv_p[:, M_HEADS * M_DH:]
        o_r, s_new_p = _ret_prompt(z3, cos_p, sin_p, batch, seq)
        om = _mem_attend(z3, mk_p, mv_p, batch, seq)

        merged_s = _merge(h2s, branches_s, w_gate_b, b_gate[l], branch_b)
        merged = _merge(h2, [oa, o_r, om], w_gate_b, b_gate[l], branch_b)
        x2s = _matmul(merged_s, w_out_b, residual=x1s, name="out_proj")
        x2 = _matmul(merged, w_out_b, residual=x1, name="out_proj")
        xs = _ffn_half(x2s, ffn2_norm[l], *f2_b, g_next, emit_x=not last, post_dtype=F32)[0]
        xp = _ffn_half(x2, ffn2_norm[l], *f2_b, g_next, emit_x=not last, post_dtype=F32)[0]

        new = (ka.reshape(batch, seq, A_HEADS, A_DH), va.reshape(batch, seq, A_HEADS, A_DH),
               mk_p.reshape(batch, n_mem, M_HEADS, M_DH), mv_p.reshape(batch, n_mem, M_HEADS, M_DH), s_new_p,
               ka_s.reshape(dec_batch, dec_seq, A_HEADS, A_DH), va_s.reshape(dec_batch, dec_seq, A_HEADS, A_DH),
               s_new_s)
        for acc, val in zip(outs, new):
            acc.append(val)

    y_prompt = xp.reshape(batch, seq, d)
    y_sample = xs.reshape(dec_batch, dec_seq, d)
    return (y_prompt, y_sample) + tuple(jnp.stack(o) for o in outs)
```

```python
import functools
import math

import jax
import jax.numpy as jnp
import numpy as np
from jax import lax
from jax.experimental import pallas as pl
from jax.experimental.pallas import tpu as pltpu

F32 = jnp.float32
BF16 = jnp.bfloat16
I32 = jnp.int32

A_HEADS, A_DH = 8, 128
MOBA_BLOCK, MOBA_TOPK = 256, 3
T5_BUCKETS, T5_MAX_DIST = 32, 128
R_HEADS, R_DK, R_DV = 4, 256, 256
RET_CHUNK = 128
RET_ROPE_BASE = 10000.0
M_HEADS, M_DH = 4, 256
PAGE_SIZE = 128
EPS = 1e-6
NEG_INF = -1e30
LOG2E = math.log2(math.e)
SEC = 1024
W_IN_KA, W_IN_VA = 1, 2
(SEC_QA, SEC_QR, SEC_KR, SEC_VR, SEC_GR, SEC_QM) = range(6)

V7X_VMEM_LIMIT_BYTES = 56 * 1024 * 1024

NT_DIMS = (((1,), (1,)), ((), ()))
TN_DIMS = (((0,), (0,)), ((), ()))


def _params(*sem):
    return pltpu.CompilerParams(dimension_semantics=sem, vmem_limit_bytes=V7X_VMEM_LIMIT_BYTES)


def _tile(n, pref):
    t = min(n, pref)
    while n % t:
        t -= 1
    return t


def _rms(x, g):
    return x * lax.rsqrt(jnp.mean(x * x, axis=-1, keepdims=True) + EPS) * g


def _silu(x):
    return x * jax.nn.sigmoid(x)


def _dot(a, b):
    return jnp.dot(a, b, preferred_element_type=F32)


def _bf16(w):
    return w if w.dtype == BF16 else w.astype(BF16)


def _ffn_body(x_ref, g_ref, w1_ref, w3_ref, w2_ref, gp_ref, *refs, emit_x, emit_w, n_cast):
    refs = list(refs)
    cast_in = [refs.pop(0) for _ in range(n_cast)]
    xo_ref = refs.pop(0) if emit_x else None
    ho_ref = refs.pop(0)
    wb_refs = [refs.pop(0) for _ in range(3)] if emit_w else []
    cast_out = [refs.pop(0) for _ in range(n_cast)]
    h_sc, acc_sc = refs
    f = pl.program_id(1)

    @pl.when(f == 0)
    def _():
        h_sc[...] = _rms(x_ref[...], g_ref[...]).astype(BF16)
        acc_sc[...] = jnp.zeros_like(acc_sc)

    _cast_rows(cast_in, cast_out)

    w1, w3, w2 = _bf16(w1_ref[...]), _bf16(w3_ref[...]), _bf16(w2_ref[...])
    for wb_ref, w in zip(wb_refs, (w1, w3, w2)):
        wb_ref[...] = w
    h = h_sc[...]
    t = _silu(_dot(h, w1)) * _dot(h, w3)
    acc_sc[...] += _dot(t.astype(BF16), w2)

    @pl.when(f == pl.num_programs(1) - 1)
    def _():
        xo = x_ref[...] + 0.5 * acc_sc[...]
        if emit_x:
            xo_ref[...] = xo
        ho_ref[...] = _rms(xo, gp_ref[...]).astype(ho_ref.dtype)


BF16_SUBLANES = 16


def _cast_spec(a, grid):
    n_steps = grid[0] * grid[1]
    n_chunks = max(n for n in range(1, n_steps + 1)
                   if a.shape[0] % n == 0 and (a.shape[0] // n) % BF16_SUBLANES == 0)
    return pl.BlockSpec((a.shape[0] // n_chunks, a.shape[1]),
                        lambda i, j: (jnp.minimum(i * grid[1] + j, n_chunks - 1), 0))


def _cast_rows(src_refs, dst_refs):
    for src_ref, dst_ref in zip(src_refs, dst_refs):
        dst_ref[...] = src_ref[...].astype(BF16)


def _ffn_half(x, g, w1, w3, w2, g_post, *, emit_x, post_dtype, emit_w=False, cast_jobs=()):
    rows, d = x.shape
    ff = w1.shape[1]
    tm, tf = _tile(rows, 512), _tile(ff, 512)
    assert not emit_w or rows == tm
    grid = (rows // tm, ff // tf)
    row_spec = pl.BlockSpec((tm, d), lambda i, f: (i, 0))
    vec_spec = pl.BlockSpec((1, d), lambda i, f: (0, 0))
    w13_spec = pl.BlockSpec((d, tf), lambda i, f: (0, f))
    w2_spec = pl.BlockSpec((tf, d), lambda i, f: (f, 0))
    out_shape = [jax.ShapeDtypeStruct((rows, d), post_dtype)]
    out_specs = [row_spec]
    if emit_x:
        out_shape = [jax.ShapeDtypeStruct((rows, d), F32)] + out_shape
        out_specs = [row_spec] + out_specs
    if emit_w:
        out_shape += [jax.ShapeDtypeStruct(w.shape, BF16) for w in (w1, w3, w2)]
        out_specs += [w13_spec, w13_spec, w2_spec]

    cast_specs = [_cast_spec(a, grid) for a in cast_jobs]
    out_shape += [jax.ShapeDtypeStruct(a.shape, BF16) for a in cast_jobs]
    out_specs += cast_specs
    return pl.pallas_call(
        functools.partial(_ffn_body, emit_x=emit_x, emit_w=emit_w, n_cast=len(cast_jobs)),
        grid=grid,
        in_specs=[row_spec, vec_spec, w13_spec, w13_spec, w2_spec, vec_spec] + cast_specs,
        out_specs=out_specs,
        out_shape=out_shape,
        scratch_shapes=[pltpu.VMEM((tm, d), BF16), pltpu.VMEM((tm, d), F32)],
        compiler_params=_params("parallel", "arbitrary"),
        name="ffn_half",
    )(x, g.reshape(1, d), w1, w3, w2, g_post.reshape(1, d), *cast_jobs)


def _norm_mm_body(x_ref, g_ref, w_ref, o_ref, h_sc):
    @pl.when(pl.program_id(1) == 0)
    def _():
        h_sc[...] = _rms(x_ref[...], g_ref[...]).astype(BF16)

    o_ref[...] = _dot(h_sc[...], _bf16(w_ref[...]))


def _norm_matmul(x, g, w):
    rows, d = x.shape
    n_cols = w.shape[1]
    tm, tn = _tile(rows, 1024), _tile(n_cols, 512)
    return pl.pallas_call(
        _norm_mm_body,
        grid=(rows // tm, n_cols // tn),
        in_specs=[pl.BlockSpec((tm, d), lambda i, j: (i, 0)), pl.BlockSpec((1, d), lambda i, j: (0, 0)),
                  pl.BlockSpec((d, tn), lambda i, j: (0, j))],
        out_specs=pl.BlockSpec((tm, tn), lambda i, j: (i, j)),
        out_shape=jax.ShapeDtypeStruct((rows, n_cols), F32),
        scratch_shapes=[pltpu.VMEM((tm, d), BF16)],
        compiler_params=_params("parallel", "arbitrary"),
        name="norm_matmul",
    )(x, g.reshape(1, d), w)


def _mm_body(a_ref, w_ref, *refs, has_residual):
    o_ref = refs[-1]
    acc = _dot(a_ref[...], _bf16(w_ref[...]))
    if has_residual:
        acc = refs[0][...] + acc
    o_ref[...] = acc.astype(o_ref.dtype)


def _in_proj_body(a_ref, w_ref, *refs):
    n_cast = (len(refs) - 3) // 2
    z_ref, k_ref, v_ref = refs[n_cast:n_cast + 3]
    _cast_rows(refs[:n_cast], refs[n_cast + 3:])
    j = pl.program_id(1)
    acc = _dot(a_ref[...], w_ref[...])

    @pl.when(j == W_IN_KA)
    def _():
        k_ref[...] = acc

    @pl.when(j == W_IN_VA)
    def _():
        v_ref[...] = acc

    @pl.when((j != W_IN_KA) & (j != W_IN_VA))
    def _():
        z_ref[...] = acc


def _in_proj(a, w, cast_jobs=()):
    rows, k = a.shape
    n_sec = w.shape[1] // SEC
    assert (W_IN_KA, W_IN_VA) == (1, 2) and n_sec == 8
    tm = _tile(rows, 1024)
    grid = (rows // tm, n_sec)
    kv_spec = pl.BlockSpec((tm, SEC), lambda i, j: (i, 0))
    cast_specs = [_cast_spec(c, grid) for c in cast_jobs]
    return pl.pallas_call(
        _in_proj_body,
        grid=grid,
        in_specs=[pl.BlockSpec((tm, k), lambda i, j: (i, 0)), pl.BlockSpec((k, SEC), lambda i, j: (0, j))]
        + cast_specs,
        out_specs=[pl.BlockSpec((None, tm, SEC), lambda i, j: (jnp.maximum(j - 2, 0), i, 0)), kv_spec, kv_spec]
        + cast_specs,
        out_shape=[jax.ShapeDtypeStruct((n_sec - 2, rows, SEC), F32)] + [jax.ShapeDtypeStruct((rows, SEC), F32)] * 2
        + [jax.ShapeDtypeStruct(c.shape, BF16) for c in cast_jobs],
        compiler_params=_params("parallel", "arbitrary"),
        name="in_proj",
    )(a, w, *cast_jobs)


def _matmul(a, w, *, residual=None, out_dtype=F32, name="matmul"):
    rows, k = a.shape
    n_cols = w.shape[1]
    tm, tn = _tile(rows, 1024), _tile(n_cols, 1024)
    in_specs = [pl.BlockSpec((tm, k), lambda i, j: (i, 0)), pl.BlockSpec((k, tn), lambda i, j: (0, j))]
    out_spec = pl.BlockSpec((tm, tn), lambda i, j: (i, j))
    args = [a, w]
    if residual is not None:
        in_specs.append(out_spec)
        args.append(residual)
    return pl.pallas_call(
        functools.partial(_mm_body, has_residual=residual is not None),
        grid=(rows // tm, n_cols // tn),
        in_specs=in_specs,
        out_specs=out_spec,
        out_shape=jax.ShapeDtypeStruct((rows, n_cols), out_dtype),
        compiler_params=_params("parallel", "arbitrary"),
        name=name,
    )(*args)


def _t5_bucket(n):
    n = jnp.maximum(n, 0)
    max_exact = T5_BUCKETS // 2
    nf = jnp.maximum(n, 1).astype(F32)
    large = max_exact + (jnp.log(nf / max_exact) / math.log(T5_MAX_DIST / max_exact)
                         * (T5_BUCKETS - max_exact)).astype(I32)
    return jnp.where(n < max_exact, n, jnp.minimum(large, T5_BUCKETS - 1))


PAGES_PER_BLOCK = MOBA_BLOCK // PAGE_SIZE
SUBLANES = 8


def _block_gates(q, page_refs, first_block, gates):
    lane = lax.broadcasted_iota(I32, gates.shape, 1)
    for i in range(len(page_refs) // PAGES_PER_BLOCK):
        pages = [page_refs[i * PAGES_PER_BLOCK + t][...] for t in range(PAGES_PER_BLOCK)]
        ksum = functools.reduce(jnp.add, [jnp.sum(x, axis=0) for x in pages])
        gate = jnp.sum(q * ksum, axis=1, keepdims=True) / MOBA_BLOCK
        gates = jnp.where(lane == first_block + i, gate, gates)
    return gates


PAGE_RING = 3


def _moba_prompt_body(*refs, nb, side_blocks, grid):
    h, b, c = pl.program_id(0), pl.program_id(1), pl.program_id(2)
    if side_blocks:
        (pt_ref, t5_ref, q_ref, k_ref, v_ref, qside_ref, cache_ref, o_ref, gates_ref,
         kb_sc, vb_sc, kdiff_sc, bias_sc, pages_sc, page_sem) = refs
        n_pages = side_blocks * PAGES_PER_BLOCK
        steps_per_seq = grid[2]
        n_steps = grid[0] * grid[1] * grid[2]
        step_idx = (h * grid[1] + b) * steps_per_seq + c

        def page_copies(s):
            seq, part, slot = s // steps_per_seq, s % steps_per_seq, s % PAGE_RING
            return [pltpu.make_async_copy(cache_ref.at[pt_ref[seq, part * n_pages + t]], pages_sc.at[slot, t],
                                          page_sem.at[slot]) for t in range(n_pages)]

        @pl.when(step_idx == 0)
        def _():
            for s in range(min(PAGE_RING - 1, n_steps)):
                for cp in page_copies(s):
                    cp.start()

        @pl.when(step_idx + PAGE_RING - 1 < n_steps)
        def _():
            for cp in page_copies(step_idx + PAGE_RING - 1):
                cp.start()

        for cp in page_copies(step_idx):
            cp.wait()
        slot = step_idx % PAGE_RING
        page_refs = [pages_sc.at[slot, t] for t in range(n_pages)]
    else:
        t5_ref, q_ref, k_ref, v_ref, o_ref, kb_sc, vb_sc, kdiff_sc, bias_sc = refs
    blk = MOBA_BLOCK
    nbp = SUBLANES
    assert nb <= nbp
    scale = A_DH ** -0.5
    ii = lax.broadcasted_iota(I32, (blk, blk), 0)
    jj = lax.broadcasted_iota(I32, (blk, blk), 1)

    @pl.when((b == 0) & (c == 0))
    def _():
        for t in range(2):
            bucket = _t5_bucket(t * blk + ii - jj)
            tile = jnp.zeros((blk, blk), F32)
            for bkt in range(T5_BUCKETS):
                tile = jnp.where(bucket == bkt, t5_ref[bkt, h], tile)
            tile = tile * LOG2E
            bias_sc[t] = jnp.where(ii >= jj, tile, NEG_INF) if t == 0 else tile

    @pl.when(c == 0)
    def _():
        km = jnp.concatenate(
            [jnp.mean(k_ref[n * blk:(n + 1) * blk, :], axis=0, keepdims=True) for n in range(nb)]
            + [jnp.zeros((1, A_DH), F32)] * (nbp - nb), axis=0)
        for n in range(nb):
            kdiff_sc[n * nbp:(n + 1) * nbp, :] = km[n:n + 1, :] - km
            kb_sc[n * blk:(n + 1) * blk, A_DH:2 * A_DH] = jnp.where(
                lax.broadcasted_iota(I32, (blk, 128), 1) == n, 1.0, 0.0).astype(BF16)
        kb_sc[:, 0:A_DH] = k_ref[...].astype(BF16)
        vb_sc[:, 0:A_DH] = v_ref[...].astype(BF16)
        vb_sc[:, A_DH:2 * A_DH] = jnp.ones((nb * blk, A_DH), BF16)
        if side_blocks:
            gates_ref[...] = jnp.zeros_like(gates_ref)

    far_bias = t5_ref[T5_BUCKETS - 1, h] * LOG2E

    def attend(cc):
        q = q_ref[cc * blk:(cc + 1) * blk, :]
        qs = (q * (scale * LOG2E)).astype(BF16)
        n_keys = (cc + 1) * blk
        if cc <= MOBA_TOPK:
            s_all = lax.dot_general(qs, kb_sc[0:n_keys, 0:A_DH], NT_DIMS, preferred_element_type=F32)
        else:
            diff_t = lax.dot_general(kdiff_sc[0:cc * nbp, :], q, NT_DIMS, precision=lax.Precision.HIGHEST,
                                     preferred_element_type=F32)
            pair = lax.broadcasted_iota(I32, diff_t.shape, 0)
            m_of, n_of = jnp.right_shift(pair, int(math.log2(nbp))), jnp.bitwise_and(pair, nbp - 1)
            beats_t = jnp.where(diff_t > 0, 1.0, jnp.where(diff_t == 0, jnp.where(n_of > m_of, 1.0, 0.0), 0.0))
            fold = jnp.where(jnp.bitwise_and(lax.broadcasted_iota(I32, (cc * nbp, 128), 0), nbp - 1)
                             == lax.broadcasted_iota(I32, (cc * nbp, 128), 1), 1.0, 0.0)
            rank = lax.dot_general(beats_t.astype(BF16), fold.astype(BF16), TN_DIMS,
                                   preferred_element_type=F32)
            lane = lax.broadcasted_iota(I32, rank.shape, 1)
            sel = jnp.where(lane < cc, jnp.where(rank < MOBA_TOPK, 0.0, NEG_INF), 0.0)
            q_sel = jnp.concatenate([qs, sel.astype(BF16)], axis=1)
            s_all = lax.dot_general(q_sel, kb_sc[0:n_keys, :], NT_DIMS, preferred_element_type=F32)
        tiles = []
        for n in range(cc + 1):
            t = s_all[:, n * blk:(n + 1) * blk]
            tiles.append(t + (bias_sc[0] if n == cc else bias_sc[1] if n == cc - 1 else far_bias))
        m = jnp.max(functools.reduce(jnp.maximum, tiles), axis=1, keepdims=True)
        p_all = jnp.concatenate([jnp.exp2(t - m).astype(BF16) for t in tiles], axis=1)
        pv = _dot(p_all, vb_sc[0:n_keys, :])
        o_ref[cc * blk:(cc + 1) * blk, :] = (pv[:, 0:A_DH] / pv[:, A_DH:2 * A_DH]).astype(o_ref.dtype)

    def step(s):
        if side_blocks:
            gates_ref[...] = _block_gates(qside_ref[...], page_refs, s * side_blocks, gates_ref[...])
        for cc in sorted({s, nb - 1 - s}):
            attend(cc)

    for s in range(_moba_steps(nb)):
        pl.when(c == s)(functools.partial(step, s))


def _moba_steps(nb):
    return (nb + 1) // 2


def _moba_prompt(z3, k, v, t5_table, batch, seq, paged=None):
    blk = MOBA_BLOCK
    nb = seq // blk
    steps = _moba_steps(nb)
    side_blocks = 0
    if paged is not None:
        page_table, q_side, cache_k = paged
        nb_past = page_table.shape[1] // PAGES_PER_BLOCK
        if page_table.shape[0] == A_HEADS * batch and nb_past % steps == 0:
            side_blocks = nb_past // steps

    seq_spec = pl.BlockSpec((seq, A_DH), lambda h, b, c, *_: (b, h))
    in_specs = [pl.BlockSpec(memory_space=pltpu.SMEM),
                pl.BlockSpec((None, seq, A_DH), lambda h, b, c, *_: (SEC_QA, b, h)), seq_spec, seq_spec]
    out_specs = [seq_spec]
    out_shape = [jax.ShapeDtypeStruct((batch * seq, A_HEADS * A_DH), BF16)]
    args = [t5_table, z3, k, v]
    scratch = [pltpu.VMEM((seq, 2 * A_DH), BF16), pltpu.VMEM((seq, 2 * A_DH), BF16),
               pltpu.VMEM((SUBLANES * SUBLANES, A_DH), F32), pltpu.VMEM((2, blk, blk), F32)]
    grid = (A_HEADS, batch, steps)
    if side_blocks:
        in_specs += [pl.BlockSpec((None, None, A_HEADS, A_DH), lambda h, b, c, pt: (SEC_QA, h * batch + b, 0, 0)),
                     pl.BlockSpec(memory_space=pl.ANY)]
        out_specs.append(pl.BlockSpec((None, A_HEADS, 128), lambda h, b, c, pt: (h * batch + b, 0, 0)))
        out_shape.append(jax.ShapeDtypeStruct((A_HEADS * batch, A_HEADS, 128), F32))
        args = [page_table] + args + [q_side, cache_k]
        scratch += [pltpu.VMEM((PAGE_RING, side_blocks * PAGES_PER_BLOCK, PAGE_SIZE, A_HEADS, A_DH), F32),
                    pltpu.SemaphoreType.DMA((PAGE_RING,))]
    out = pl.pallas_call(
        functools.partial(_moba_prompt_body, nb=nb, side_blocks=side_blocks, grid=grid),
        grid_spec=pltpu.PrefetchScalarGridSpec(
            num_scalar_prefetch=1 if side_blocks else 0,
            grid=grid,
            in_specs=in_specs,
            out_specs=out_specs,
            scratch_shapes=scratch,
        ),
        out_shape=out_shape,
        compiler_params=_params("arbitrary", "arbitrary", "arbitrary"),
        name="moba_prompt",
    )(*args)
    if paged is None:
        return out[0]
    return out[0], (out[1] if side_blocks else None)


def _rope_body(inv_ref, cos_ref, sin_ref, *, pos0):
    pos = pos0 + lax.broadcasted_iota(I32, cos_ref.shape, 0)
    ang = pos.astype(F32) * inv_ref[...]
    cos_ref[...] = jnp.cos(ang)
    sin_ref[...] = jnp.sin(ang)


def _rope_tables(pos0, n_pos):
    half = R_DK // 2
    inv = 1.0 / (RET_ROPE_BASE ** jnp.linspace(0.0, 1.0, half, dtype=F32))
    rows = -(-n_pos // 8) * 8
    return pl.pallas_call(
        functools.partial(_rope_body, pos0=pos0),
        out_shape=[jax.ShapeDtypeStruct((rows, half), F32)] * 2,
        name="rope_tables",
    )(inv.reshape(1, half))


def _rotate(x, cos, sin):
    half = x.shape[-1] // 2
    x1, x2 = x[:, :half], x[:, half:]
    return jnp.concatenate([x1 * cos - x2 * sin, x1 * sin + x2 * cos], axis=1)


def _log_decay(h, shape):
    hf = jnp.full(shape, h, I32).astype(F32)
    return jnp.log(1.0 - jnp.exp2(-5.0 - hf))


def _ret_prompt_body(q_ref, k_ref, v_ref, g_ref, cos_ref, sin_ref, qm_ref, mk_ref, mv_ref,
                     o_ref, so_ref, om_ref, s_sc, *, c):
    n = pl.program_id(1)

    @pl.when(n == 0)
    def _():
        s_sc[...] = jnp.zeros_like(s_sc)

    om = _mem_rows(qm_ref[...], mk_ref, mv_ref).astype(om_ref.dtype)

    i = lax.broadcasted_iota(I32, (c, c), 0).astype(F32)
    j = lax.broadcasted_iota(I32, (c, c), 1).astype(F32)
    diff = i - j
    i_col = lax.broadcasted_iota(I32, (c, 1), 0).astype(F32)
    cos, sin = cos_ref[...], sin_ref[...]

    new_states, outs = [], []
    for h in range(R_HEADS):
        dk, dv = slice(h * R_DK, (h + 1) * R_DK), slice(h * R_DV, (h + 1) * R_DV)
        dmask = jnp.where(diff >= 0, jnp.exp(jnp.maximum(diff, 0.0) * _log_decay(h, (c, c))), 0.0)
        lg_col = _log_decay(h, (c, 1))
        q_dec = jnp.exp((i_col + 1.0) * lg_col)
        k_dec = jnp.exp((c - 1.0 - i_col) * lg_col)
        c_dec = jnp.exp(c * _log_decay(h, (1, R_DV)))
        qr = _rotate(q_ref[:, dk], cos, sin)
        kr = _rotate(k_ref[:, dk], cos, sin) * (R_DK ** -0.5)
        vb = v_ref[:, dv].astype(BF16)
        s = s_sc[h]
        att = lax.dot_general(qr.astype(BF16), kr.astype(BF16), NT_DIMS, preferred_element_type=F32) * dmask
        o = _dot(att.astype(BF16), vb) + _dot((qr * q_dec).astype(BF16), s.astype(BF16))
        new_states.append(s * c_dec + lax.dot_general((kr * k_dec).astype(BF16), vb, TN_DIMS,
                                                      preferred_element_type=F32))
        on = o * lax.rsqrt(jnp.mean(o * o, axis=-1, keepdims=True) + EPS)
        outs.append((on * _silu(g_ref[:, dv])).astype(o_ref.dtype))
    for h in range(R_HEADS):
        s_sc[h] = new_states[h]
    o_ref[...] = jnp.concatenate(outs, axis=1)
    om_ref[...] = om

    @pl.when(n == pl.num_programs(1) - 1)
    def _():
        so_ref[...] = s_sc[...]


def _ret_mem_prompt(z3, cos, sin, mk, mv, batch, seq):
    c = math.gcd(seq, RET_CHUNK)
    nc = seq // c
    n_mem = mk.shape[0] // batch

    def sec(s):
        return pl.BlockSpec((None, c, SEC), lambda b, n: (s, b * nc + n, 0))

    tab = pl.BlockSpec((c, R_DK // 2), lambda b, n: (n, 0))
    mem = pl.BlockSpec((n_mem, M_HEADS * M_DH), lambda b, n: (b, 0))
    rows = pl.BlockSpec((c, SEC), lambda b, n: (b * nc + n, 0))
    return pl.pallas_call(
        functools.partial(_ret_prompt_body, c=c),
        grid=(batch, nc),
        in_specs=[sec(SEC_QR), sec(SEC_KR), sec(SEC_VR), sec(SEC_GR), tab, tab, sec(SEC_QM), mem, mem],
        out_specs=[rows, pl.BlockSpec((None, R_HEADS, R_DK, R_DV), lambda b, n: (b, 0, 0, 0)), rows],
        out_shape=[jax.ShapeDtypeStruct((batch * seq, R_HEADS * R_DV), BF16),
                   jax.ShapeDtypeStruct((batch, R_HEADS, R_DK, R_DV), F32),
                   jax.ShapeDtypeStruct((batch * seq, M_HEADS * M_DH), BF16)],
        scratch_shapes=[pltpu.VMEM((R_HEADS, R_DK, R_DV), F32)],
        compiler_params=_params("parallel", "arbitrary"),
        name="retention_mem_prompt",
    )(z3, z3, z3, z3, cos, sin, z3, mk, mv)


def _ret_step_body(q_ref, k_ref, v_ref, g_ref, cos_ref, sin_ref, s_ref, o_ref, so_ref):
    cos, sin = cos_ref[0:1, :], sin_ref[0:1, :]
    rows = 16
    row0 = lax.broadcasted_iota(I32, (rows, R_DK), 0) == 0
    for h in range(R_HEADS):
        decay = jnp.exp(_log_decay(h, (1, R_DV)))
        qr = _rotate(q_ref[h:h + 1, :], cos, sin)
        kr = _rotate(k_ref[h:h + 1, :], cos, sin) * (R_DK ** -0.5)
        v = v_ref[h:h + 1, :]
        s = s_ref[h]
        q_rows = jnp.broadcast_to(qr * decay, (rows, R_DK)).astype(BF16)
        qs = _dot(q_rows, s.astype(BF16))[0:1, :]
        o = jnp.sum(qr * kr, axis=-1, keepdims=True) * v + qs
        k_rows = jnp.where(row0, jnp.broadcast_to(kr, (rows, R_DK)), 0.0).astype(BF16)
        v_rows = jnp.broadcast_to(v, (rows, R_DV)).astype(BF16)
        so_ref[h] = s * decay + lax.dot_general(k_rows, v_rows, TN_DIMS, preferred_element_type=F32)
        on = o * lax.rsqrt(jnp.mean(o * o, axis=-1, keepdims=True) + EPS)
        o_ref[h:h + 1, :] = (on * _silu(g_ref[h:h + 1, :])).astype(o_ref.dtype)


def _mem_rows(q, k_ref, v_ref):
    outs = []
    for h in range(M_HEADS):
        dh = slice(h * M_DH, (h + 1) * M_DH)
        s = lax.dot_general(q[:, dh].astype(BF16), k_ref[:, dh].astype(BF16), NT_DIMS,
                            preferred_element_type=F32) * (M_DH ** -0.5)
        m = jnp.max(s, axis=-1, keepdims=True)
        p = jnp.exp(s - m)
        l = jnp.sum(p, axis=-1, keepdims=True)
        outs.append(_dot(p.astype(BF16), v_ref[:, dh].astype(BF16)) / l)
    return jnp.concatenate(outs, axis=1)


def _merge_body(h_ref, *refs):
    x_refs, wg_refs, bg_refs, wb_refs, o_ref = refs[0:3], refs[3:6], refs[6:9], refs[9:12], refs[12]
    h = h_ref[...]
    merged = None
    for g in range(3):
        term = jax.nn.sigmoid(_dot(h, wg_refs[g][...]) + bg_refs[g][...]) * _dot(x_refs[g][...], wb_refs[g][...])
        merged = term if merged is None else merged + term
    o_ref[...] = merged.astype(o_ref.dtype)


def _merge(h, branches, w_gate, b_gate, branch_ws):
    rows, d = h.shape
    tm, tn = _tile(rows, 1024), _tile(d, 512)
    nj = d // tn
    b_gate = b_gate.reshape(1, 3 * d)

    def row(width):
        return pl.BlockSpec((tm, width), lambda i, j: (i, 0))

    def cols(n_rows, group=0):
        return pl.BlockSpec((n_rows, tn), lambda i, j: (0, group * nj + j))

    return pl.pallas_call(
        _merge_body,
        grid=(rows // tm, nj),
        in_specs=([row(d)] + [row(x.shape[1]) for x in branches]
                  + [cols(d, g) for g in range(3)] + [cols(1, g) for g in range(3)]
                  + [cols(w.shape[0]) for w in branch_ws]),
        out_specs=pl.BlockSpec((tm, tn), lambda i, j: (i, j)),
        out_shape=jax.ShapeDtypeStruct((rows, d), BF16),
        compiler_params=_params("parallel", "arbitrary"),
        name="gated_merge",
    )(h, *branches, w_gate, w_gate, w_gate, b_gate, b_gate, b_gate, *branch_ws)


def _moba_gate_body(pt_ref, q_ref, *refs, bps):
    del pt_ref
    k_refs, gates_ref = refs[:-1], refs[-1]
    n = pl.program_id(1)

    @pl.when(n == 0)
    def _():
        gates_ref[...] = jnp.zeros_like(gates_ref)

    gates_ref[...] = _block_gates(q_ref[...], k_refs, n * bps, gates_ref[...])


def _moba_select_body(g_ref, sel_ref, *, nb_past):
    lane = lax.broadcasted_iota(I32, g_ref.shape, 1)
    gw = jnp.where(lane < nb_past, g_ref[...], NEG_INF)
    sel = jnp.zeros(gw.shape, I32)
    for r in range(MOBA_TOPK):
        mx = jnp.max(gw, axis=1, keepdims=True)
        idx = jnp.min(jnp.where(gw == mx, lane, 128), axis=1, keepdims=True)
        sel = jnp.where(lane == r, idx, sel)
        gw = jnp.where(lane == idx, -jnp.inf, gw)
    sel_ref[...] = sel


def _moba_attend_body(pt_ref, sel_ref, t5_ref, q_ref, kn_ref, vn_ref, ck_ref, cv_ref, o_ref, kbuf, vbuf, sem,
                      *, past_len):
    b = pl.program_id(0)
    slot = lax.rem(b, 2)
    scale = A_DH ** -0.5
    n_tiles = MOBA_TOPK * PAGES_PER_BLOCK

    def tile_copies(bb, to_slot):
        copies = []
        for h in range(A_HEADS):
            for r in range(MOBA_TOPK):
                block = sel_ref[bb, h * MOBA_TOPK + r]
                for t in range(PAGES_PER_BLOCK):
                    page = pt_ref[bb, block * PAGES_PER_BLOCK + t]
                    j = r * PAGES_PER_BLOCK + t
                    copies.append(pltpu.make_async_copy(ck_ref.at[page, :, h, :], kbuf.at[to_slot, h, j],
                                                        sem.at[to_slot]))
                    copies.append(pltpu.make_async_copy(cv_ref.at[page, :, h, :], vbuf.at[to_slot, h, j],
                                                        sem.at[to_slot]))
        return copies

    def start_all(copies):
        for i, cp in enumerate(copies):
            cp.start(priority=i % 2)

    @pl.when(b == 0)
    def _():
        start_all(tile_copies(0, 0))

    @pl.when(b + 1 < pl.num_programs(0))
    def _():
        start_all(tile_copies(b + 1, 1 - slot))

    for cp in tile_copies(b, slot):
        cp.wait()

    q, kn, vn = q_ref[...], kn_ref[...], vn_ref[...]
    outs = []
    lane = lax.broadcasted_iota(I32, (1, PAGE_SIZE), 1)
    for h in range(A_HEADS):
        qh = q[h:h + 1, :]
        q8 = jnp.broadcast_to(qh, (8, A_DH)).astype(BF16)
        s = []
        for r in range(MOBA_TOPK):
            block = sel_ref[b, h * MOBA_TOPK + r]
            for t in range(PAGES_PER_BLOCK):
                kt = kbuf[slot, h, r * PAGES_PER_BLOCK + t].astype(BF16)
                raw = lax.dot_general(q8, kt, NT_DIMS, preferred_element_type=F32)[0:1, :]
                bucket = _t5_bucket(past_len - (block * MOBA_BLOCK + t * PAGE_SIZE + lane))
                bias = jnp.zeros((1, PAGE_SIZE), F32)
                for bkt in range(T5_BUCKETS):
                    bias = jnp.where(bucket == bkt, t5_ref[bkt, h], bias)
                s.append(raw * scale + bias)
        s_own = jnp.sum(qh * kn[h:h + 1, :], axis=1, keepdims=True) * scale + t5_ref[0, h]
        m = jnp.maximum(jnp.max(functools.reduce(jnp.maximum, s), axis=1, keepdims=True), s_own)
        p = [jnp.exp(x - m) for x in s]
        p_own = jnp.exp(s_own - m)
        l = jnp.sum(functools.reduce(jnp.add, p), axis=1, keepdims=True) + p_own
        acc = p_own * vn[h:h + 1, :]
        for j in range(n_tiles):
            p8 = jnp.broadcast_to(p[j], (8, PAGE_SIZE)).astype(BF16)
            acc = acc + _dot(p8, vbuf[slot, h, j].astype(BF16))[0:1, :]
        outs.append(acc / l)
    o_ref[...] = jnp.concatenate(outs, axis=0)


def _moba_gates(q4, cache_k, page_table):
    batch, n_pages = page_table.shape
    nb_past = n_pages // PAGES_PER_BLOCK
    bps = _tile(nb_past, 8)

    def page(t):
        return pl.BlockSpec((None, PAGE_SIZE, A_HEADS, A_DH),
                            lambda b, n, pt: (pt[b, n * bps * PAGES_PER_BLOCK + t], 0, 0, 0))

    pages = [page(t) for t in range(bps * PAGES_PER_BLOCK)]
    return pl.pallas_call(
        functools.partial(_moba_gate_body, bps=bps),
        grid_spec=pltpu.PrefetchScalarGridSpec(
            num_scalar_prefetch=1,
            grid=(batch, nb_past // bps),
            in_specs=[pl.BlockSpec((None, None, A_HEADS, A_DH), lambda b, n, pt: (SEC_QA, b, 0, 0))] + pages,
            out_specs=pl.BlockSpec((None, A_HEADS, 128), lambda b, n, pt: (b, 0, 0)),
        ),
        out_shape=jax.ShapeDtypeStruct((batch, A_HEADS, 128), F32),
        compiler_params=_params("parallel", "arbitrary"),
        name="moba_gate",
    )(page_table, q4, *([cache_k] * len(pages)))


def _sample_mixers_body(pt_ref, sel_ref, t5_ref, qa_ref, kn_ref, vn_ref, ck_ref, cv_ref,
                        qr_ref, kr_ref, vr_ref, gr_ref, cos_ref, sin_ref, state_ref, qm_ref, mk_ref, mv_ref,
                        oa_ref, or_ref, state_out_ref, om_ref, kbuf, vbuf, sem, *, past_len):
    _moba_attend_body(pt_ref, sel_ref, t5_ref, qa_ref, kn_ref, vn_ref, ck_ref, cv_ref, oa_ref, kbuf, vbuf, sem,
                      past_len=past_len)
    _ret_step_body(qr_ref, kr_ref, vr_ref, gr_ref, cos_ref, sin_ref, state_ref, or_ref, state_out_ref)
    q_rows = jnp.broadcast_to(qm_ref[...], (BF16_SUBLANES, qm_ref.shape[-1]))
    om_ref[...] = _mem_rows(q_rows, mk_ref, mv_ref)[0:1, :]


def _sample_mixers(z3, k_new, v_new, cache_k, cache_v, page_table, t5_table, gates, cos, sin, state, mem_k, mem_v):
    batch, n_pages = page_table.shape
    past_len = n_pages * PAGE_SIZE
    assert past_len % MOBA_BLOCK == 0 and MOBA_BLOCK % PAGE_SIZE == 0
    nb_past = past_len // MOBA_BLOCK
    assert MOBA_TOPK <= nb_past <= 128
    n_sec = z3.shape[0]
    n_mem = mem_k.shape[0] // batch
    z4a = z3.reshape(n_sec, batch, A_HEADS, A_DH)
    z4r = z3.reshape(n_sec, batch, R_HEADS, R_DK)
    z4m = z3.reshape(n_sec, batch, 1, M_HEADS * M_DH)
    k_new = k_new.reshape(batch, A_HEADS, A_DH)
    v_new = v_new.reshape(batch, A_HEADS, A_DH)
    if gates is None:
        gates = _moba_gates(z4a, cache_k, page_table)
    sel = pl.pallas_call(
        functools.partial(_moba_select_body, nb_past=nb_past),
        out_shape=jax.ShapeDtypeStruct((batch * A_HEADS, 128), I32),
        name="moba_select",
    )(gates.reshape(batch * A_HEADS, 128))
    sel = sel[:, :MOBA_TOPK].reshape(batch, A_HEADS * MOBA_TOPK)

    def sec(s, rows, width):
        return pl.BlockSpec((None, None, rows, width), lambda b, pt, sl: (s, b, 0, 0))

    def per_seq(*shape):
        return pl.BlockSpec((None,) + shape, lambda b, pt, sl: (b,) + (0,) * len(shape))

    tab = pl.BlockSpec((SUBLANES, R_DK // 2), lambda b, pt, sl: (0, 0))
    mem = pl.BlockSpec((n_mem, M_HEADS * M_DH), lambda b, pt, sl: (b, 0))
    tiles = pltpu.VMEM((2, A_HEADS, MOBA_TOPK * PAGES_PER_BLOCK, PAGE_SIZE, A_DH), F32)
    return pl.pallas_call(
        functools.partial(_sample_mixers_body, past_len=past_len),
        grid_spec=pltpu.PrefetchScalarGridSpec(
            num_scalar_prefetch=2,
            grid=(batch,),
            in_specs=[pl.BlockSpec(memory_space=pltpu.SMEM), sec(SEC_QA, A_HEADS, A_DH),
                      per_seq(A_HEADS, A_DH), per_seq(A_HEADS, A_DH),
                      pl.BlockSpec(memory_space=pl.ANY), pl.BlockSpec(memory_space=pl.ANY),
                      sec(SEC_QR, R_HEADS, R_DK), sec(SEC_KR, R_HEADS, R_DK), sec(SEC_VR, R_HEADS, R_DK),
                      sec(SEC_GR, R_HEADS, R_DK), tab, tab, per_seq(R_HEADS, R_DK, R_DV),
                      sec(SEC_QM, 1, M_HEADS * M_DH), mem, mem],
            out_specs=[per_seq(A_HEADS, A_DH), per_seq(R_HEADS, R_DV), per_seq(R_HEADS, R_DK, R_DV),
                       per_seq(1, M_HEADS * M_DH)],
            scratch_shapes=[tiles, tiles, pltpu.SemaphoreType.DMA((2,))],
        ),
        out_shape=[jax.ShapeDtypeStruct((batch, A_HEADS, A_DH), F32),
                   jax.ShapeDtypeStruct((batch, R_HEADS, R_DV), F32),
                   jax.ShapeDtypeStruct(state.shape, F32),
                   jax.ShapeDtypeStruct((batch, 1, M_HEADS * M_DH), F32)],
        compiler_params=_params("arbitrary"),
        name="sample_mixers",
    )(page_table, sel, t5_table, z4a, k_new, v_new, cache_k, cache_v, z4r, z4r, z4r, z4r, cos, sin, state,
      z4m, mem_k, mem_v)


def kernel(x_prompt, x_sample, mem_prompt, cache_k, cache_v, cache_mem_k, cache_mem_v, state_ret, page_table, t5_table, ffn1_norm, ffn1_w1, ffn1_w3, ffn1_w2, mix_norm, mem_norm, w_in, w_mem_kv, w_gate, b_gate, w_br_moba, w_br_ret, w_br_mem, w_out, ffn2_norm, ffn2_w1, ffn2_w3, ffn2_w2, final_norm):
    batch, seq, d = x_prompt.shape
    dec_batch, dec_seq, _ = x_sample.shape
    assert dec_seq == 1
    depth = w_in.shape[0]
    n_mem = mem_prompt.shape[1]
    past_len = page_table.shape[1] * PAGE_SIZE
    cos_p, sin_p = _rope_tables(0, seq)
    cos_s, sin_s = _rope_tables(past_len, dec_seq)

    xp = x_prompt.reshape(batch * seq, d)
    xs = x_sample.reshape(dec_batch, d)
    outs = [[] for _ in range(8)]
    for l in range(depth):
        last = l == depth - 1
        g_next = final_norm if last else ffn1_norm[l + 1]

        x1s, h2s, *f1_b = _ffn_half(xs, ffn1_norm[l], ffn1_w1[l], ffn1_w3[l], ffn1_w2[l], mix_norm[l],
                                    emit_x=True, post_dtype=BF16, emit_w=True)
        x1, h2, w_in_b, *f2_b = _ffn_half(xp, ffn1_norm[l], *f1_b, mix_norm[l], emit_x=True, post_dtype=BF16,
                                          cast_jobs=[w_in[l], ffn2_w1[l], ffn2_w3[l], ffn2_w2[l]])
        z3s, ka_s, va_s = _in_proj(h2s, w_in_b)
        z3, ka, va, w_gate_b, w_out_b, *branch_b = _in_proj(
            h2, w_in_b, cast_jobs=[w_gate[l], w_out[l], w_br_moba[l], w_br_ret[l], w_br_mem[l]])
        mem_k_s, mem_v_s, x1s = lax.optimization_barrier((cache_mem_k[l], cache_mem_v[l], x1s))

        z4s = z3s.reshape(z3s.shape[0], dec_batch, A_HEADS, A_DH)
        oa, gates_s = _moba_prompt(z3, ka, va, t5_table, batch, seq, paged=(page_table, z4s, cache_k[l]))

        oa_s, o_rs, s_new_s, om_s = _sample_mixers(
            z3s, ka_s, va_s, cache_k[l], cache_v[l], page_table, t5_table, gates_s, cos_s, sin_s, state_ret[l],
            mem_k_s.reshape(dec_batch * n_mem, M_HEADS * M_DH), mem_v_s.reshape(dec_batch * n_mem, M_HEADS * M_DH))
        branches_s = [o.reshape(dec_batch, -1).astype(BF16) for o in (oa_s, o_rs, om_s)]

        mkv_p = _norm_matmul(mem_prompt.reshape(batch * n_mem, d), mem_norm[l], w_mem_kv[l])
        mk_p, mv_p = mkv_p[:, :M_HEADS * M_DH], mkv_p[:, M_HEADS * M_DH:]
        o_r, s_new_p, om = _ret_mem_prompt(z3, cos_p, sin_p, mk_p, mv_p, batch, seq)

        merged_s = _merge(h2s, branches_s, w_gate_b, b_gate[l], branch_b)
        merged = _merge(h2, [oa, o_r, om], w_gate_b, b_gate[l], branch_b)
        x2s = _matmul(merged_s, w_out_b, residual=x1s, name="out_proj")
        x2 = _matmul(merged, w_out_b, residual=x1, name="out_proj")
        xs = _ffn_half(x2s, ffn2_norm[l], *f2_b, g_next, emit_x=not last, post_dtype=F32)[0]
        xp = _ffn_half(x2, ffn2_norm[l], *f2_b, g_next, emit_x=not last, post_dtype=F32)[0]

        new = (ka.reshape(batch, seq, A_HEADS, A_DH), va.reshape(batch, seq, A_HEADS, A_DH),
               mk_p.reshape(batch, n_mem, M_HEADS, M_DH), mv_p.reshape(batch, n_mem, M_HEADS, M_DH), s_new_p,
               ka_s.reshape(dec_batch, dec_seq, A_HEADS, A_DH), va_s.reshape(dec_batch, dec_seq, A_HEADS, A_DH),
               s_new_s)
        for acc, val in zip(outs, new):
            acc.append(val)

    y_prompt = xp.reshape(batch, seq, d)
    y_sample = xs.reshape(dec_batch, dec_seq, d)
    return (y_prompt, y_sample) + tuple(jnp.stack(o) for o in outs)
```

```python
import functools
import math

import jax
import jax.numpy as jnp
import numpy as np
from jax import lax
from jax.experimental import pallas as pl
from jax.experimental.pallas import tpu as pltpu

F32 = jnp.float32
BF16 = jnp.bfloat16
I32 = jnp.int32

A_HEADS, A_DH = 8, 128
MOBA_BLOCK, MOBA_TOPK = 256, 3
T5_BUCKETS, T5_MAX_DIST = 32, 128
R_HEADS, R_DK, R_DV = 4, 256, 256
RET_CHUNK = 128
RET_ROPE_BASE = 10000.0
M_HEADS, M_DH = 4, 256
PAGE_SIZE = 128
EPS = 1e-6
NEG_INF = -1e30
LOG2E = math.log2(math.e)
SEC = 1024
W_IN_KA, W_IN_VA = 1, 2
(SEC_QA, SEC_QR, SEC_KR, SEC_VR, SEC_GR, SEC_QM) = range(6)

V7X_VMEM_LIMIT_BYTES = 56 * 1024 * 1024

NT_DIMS = (((1,), (1,)), ((), ()))
TN_DIMS = (((0,), (0,)), ((), ()))


def _params(*sem):
    return pltpu.CompilerParams(dimension_semantics=sem, vmem_limit_bytes=V7X_VMEM_LIMIT_BYTES)


def _tile(n, pref):
    t = min(n, pref)
    while n % t:
        t -= 1
    return t


def _rms(x, g):
    return x * lax.rsqrt(jnp.mean(x * x, axis=-1, keepdims=True) + EPS) * g


def _silu(x):
    return x * jax.nn.sigmoid(x)


def _dot(a, b):
    return jnp.dot(a, b, preferred_element_type=F32)


def _bf16(w):
    return w if w.dtype == BF16 else w.astype(BF16)


def _ffn_body(x_ref, g_ref, w1_ref, w3_ref, w2_ref, gp_ref, *refs, emit_x, emit_w, n_cast):
    refs = list(refs)
    cast_in = [refs.pop(0) for _ in range(n_cast)]
    xo_ref = refs.pop(0) if emit_x else None
    ho_ref = refs.pop(0)
    wb_refs = [refs.pop(0) for _ in range(3)] if emit_w else []
    cast_out = [refs.pop(0) for _ in range(n_cast)]
    h_sc, acc_sc = refs
    f = pl.program_id(1)

    @pl.when(f == 0)
    def _():
        h_sc[...] = _rms(x_ref[...], g_ref[...]).astype(BF16)
        acc_sc[...] = jnp.zeros_like(acc_sc)

    _cast_rows(cast_in, cast_out)

    w1, w3, w2 = _bf16(w1_ref[...]), _bf16(w3_ref[...]), _bf16(w2_ref[...])
    for wb_ref, w in zip(wb_refs, (w1, w3, w2)):
        wb_ref[...] = w
    h = h_sc[...]
    t = _silu(_dot(h, w1)) * _dot(h, w3)
    acc_sc[...] += _dot(t.astype(BF16), w2)

    @pl.when(f == pl.num_programs(1) - 1)
    def _():
        xo = x_ref[...] + 0.5 * acc_sc[...]
        if emit_x:
            xo_ref[...] = xo
        ho_ref[...] = _rms(xo, gp_ref[...]).astype(ho_ref.dtype)


BF16_SUBLANES = 16


def _cast_spec(a, grid):
    n_steps = grid[0] * grid[1]
    n_chunks = max(n for n in range(1, n_steps + 1)
                   if a.shape[0] % n == 0 and (a.shape[0] // n) % BF16_SUBLANES == 0)
    return pl.BlockSpec((a.shape[0] // n_chunks, a.shape[1]),
                        lambda i, j: (jnp.minimum(i * grid[1] + j, n_chunks - 1), 0))


def _cast_rows(src_refs, dst_refs):
    for src_ref, dst_ref in zip(src_refs, dst_refs):
        dst_ref[...] = src_ref[...].astype(BF16)


def _ffn_half(x, g, w1, w3, w2, g_post, *, emit_x, post_dtype, emit_w=False, cast_jobs=()):
    rows, d = x.shape
    ff = w1.shape[1]
    tm, tf = _tile(rows, 512), _tile(ff, 512)
    assert not emit_w or rows == tm
    grid = (rows // tm, ff // tf)
    row_spec = pl.BlockSpec((tm, d), lambda i, f: (i, 0))
    vec_spec = pl.BlockSpec((1, d), lambda i, f: (0, 0))
    w13_spec = pl.BlockSpec((d, tf), lambda i, f: (0, f))
    w2_spec = pl.BlockSpec((tf, d), lambda i, f: (f, 0))
    out_shape = [jax.ShapeDtypeStruct((rows, d), post_dtype)]
    out_specs = [row_spec]
    if emit_x:
        out_shape = [jax.ShapeDtypeStruct((rows, d), F32)] + out_shape
        out_specs = [row_spec] + out_specs
    if emit_w:
        out_shape += [jax.ShapeDtypeStruct(w.shape, BF16) for w in (w1, w3, w2)]
        out_specs += [w13_spec, w13_spec, w2_spec]

    cast_specs = [_cast_spec(a, grid) for a in cast_jobs]
    out_shape += [jax.ShapeDtypeStruct(a.shape, BF16) for a in cast_jobs]
    out_specs += cast_specs
    return pl.pallas_call(
        functools.partial(_ffn_body, emit_x=emit_x, emit_w=emit_w, n_cast=len(cast_jobs)),
        grid=grid,
        in_specs=[row_spec, vec_spec, w13_spec, w13_spec, w2_spec, vec_spec] + cast_specs,
        out_specs=out_specs,
        out_shape=out_shape,
        scratch_shapes=[pltpu.VMEM((tm, d), BF16), pltpu.VMEM((tm, d), F32)],
        compiler_params=_params("parallel", "arbitrary"),
        name="ffn_half",
    )(x, g.reshape(1, d), w1, w3, w2, g_post.reshape(1, d), *cast_jobs)


def _norm_mm_body(x_ref, g_ref, w_ref, o_ref, h_sc):
    @pl.when(pl.program_id(1) == 0)
    def _():
        h_sc[...] = _rms(x_ref[...], g_ref[...]).astype(BF16)

    o_ref[...] = _dot(h_sc[...], _bf16(w_ref[...]))


def _norm_matmul(x, g, w):
    rows, d = x.shape
    n_cols = w.shape[1]
    tm, tn = _tile(rows, 1024), _tile(n_cols, 512)
    return pl.pallas_call(
        _norm_mm_body,
        grid=(rows // tm, n_cols // tn),
        in_specs=[pl.BlockSpec((tm, d), lambda i, j: (i, 0)), pl.BlockSpec((1, d), lambda i, j: (0, 0)),
                  pl.BlockSpec((d, tn), lambda i, j: (0, j))],
        out_specs=pl.BlockSpec((tm, tn), lambda i, j: (i, j)),
        out_shape=jax.ShapeDtypeStruct((rows, n_cols), F32),
        scratch_shapes=[pltpu.VMEM((tm, d), BF16)],
        compiler_params=_params("parallel", "arbitrary"),
        name="norm_matmul",
    )(x, g.reshape(1, d), w)


def _mm_body(a_ref, w_ref, *refs, has_residual):
    o_ref = refs[-1]
    acc = _dot(a_ref[...], _bf16(w_ref[...]))
    if has_residual:
        acc = refs[0][...] + acc
    o_ref[...] = acc.astype(o_ref.dtype)


def _in_proj_body(a_ref, w_ref, *refs):
    n_cast = (len(refs) - 3) // 2
    z_ref, k_ref, v_ref = refs[n_cast:n_cast + 3]
    _cast_rows(refs[:n_cast], refs[n_cast + 3:])
    j = pl.program_id(1)
    acc = _dot(a_ref[...], w_ref[...])

    @pl.when(j == W_IN_KA)
    def _():
        k_ref[...] = acc

    @pl.when(j == W_IN_VA)
    def _():
        v_ref[...] = acc

    @pl.when((j != W_IN_KA) & (j != W_IN_VA))
    def _():
        z_ref[...] = acc


def _in_proj(a, w, cast_jobs=()):
    rows, k = a.shape
    n_sec = w.shape[1] // SEC
    assert (W_IN_KA, W_IN_VA) == (1, 2) and n_sec == 8
    tm = _tile(rows, 1024)
    grid = (rows // tm, n_sec)
    kv_spec = pl.BlockSpec((tm, SEC), lambda i, j: (i, 0))
    cast_specs = [_cast_spec(c, grid) for c in cast_jobs]
    return pl.pallas_call(
        _in_proj_body,
        grid=grid,
        in_specs=[pl.BlockSpec((tm, k), lambda i, j: (i, 0)), pl.BlockSpec((k, SEC), lambda i, j: (0, j))]
        + cast_specs,
        out_specs=[pl.BlockSpec((None, tm, SEC), lambda i, j: (jnp.maximum(j - 2, 0), i, 0)), kv_spec, kv_spec]
        + cast_specs,
        out_shape=[jax.ShapeDtypeStruct((n_sec - 2, rows, SEC), F32)] + [jax.ShapeDtypeStruct((rows, SEC), F32)] * 2
        + [jax.ShapeDtypeStruct(c.shape, BF16) for c in cast_jobs],
        compiler_params=_params("parallel", "arbitrary"),
        name="in_proj",
    )(a, w, *cast_jobs)


def _matmul(a, w, *, residual=None, out_dtype=F32, name="matmul"):
    rows, k = a.shape
    n_cols = w.shape[1]
    tm, tn = _tile(rows, 1024), _tile(n_cols, 1024)
    in_specs = [pl.BlockSpec((tm, k), lambda i, j: (i, 0)), pl.BlockSpec((k, tn), lambda i, j: (0, j))]
    out_spec = pl.BlockSpec((tm, tn), lambda i, j: (i, j))
    args = [a, w]
    if residual is not None:
        in_specs.append(out_spec)
        args.append(residual)
    return pl.pallas_call(
        functools.partial(_mm_body, has_residual=residual is not None),
        grid=(rows // tm, n_cols // tn),
        in_specs=in_specs,
        out_specs=out_spec,
        out_shape=jax.ShapeDtypeStruct((rows, n_cols), out_dtype),
        compiler_params=_params("parallel", "arbitrary"),
        name=name,
    )(*args)


def _t5_bucket(n):
    n = jnp.maximum(n, 0)
    max_exact = T5_BUCKETS // 2
    nf = jnp.maximum(n, 1).astype(F32)
    large = max_exact + (jnp.log(nf / max_exact) / math.log(T5_MAX_DIST / max_exact)
                         * (T5_BUCKETS - max_exact)).astype(I32)
    return jnp.where(n < max_exact, n, jnp.minimum(large, T5_BUCKETS - 1))


PAGES_PER_BLOCK = MOBA_BLOCK // PAGE_SIZE
SUBLANES = 8


def _block_gates(q, page_refs, first_block, gates):
    lane = lax.broadcasted_iota(I32, gates.shape, 1)
    for i in range(len(page_refs) // PAGES_PER_BLOCK):
        pages = [page_refs[i * PAGES_PER_BLOCK + t][...] for t in range(PAGES_PER_BLOCK)]
        ksum = functools.reduce(jnp.add, [jnp.sum(x, axis=0) for x in pages])
        gate = jnp.sum(q * ksum, axis=1, keepdims=True) / MOBA_BLOCK
        gates = jnp.where(lane == first_block + i, gate, gates)
    return gates


PAGE_RING = 3


def _moba_prompt_body(*refs, nb, side_blocks, grid):
    h, b, c = pl.program_id(0), pl.program_id(1), pl.program_id(2)
    if side_blocks:
        (pt_ref, t5_ref, q_ref, k_ref, v_ref, qside_ref, cache_ref, o_ref, gates_ref,
         kb_sc, vb_sc, kdiff_sc, bias_sc, pages_sc, page_sem) = refs
        n_pages = side_blocks * PAGES_PER_BLOCK
        steps_per_seq = grid[2]
        n_steps = grid[0] * grid[1] * grid[2]
        step_idx = (h * grid[1] + b) * steps_per_seq + c

        def page_copies(s):
            seq, part, slot = s // steps_per_seq, s % steps_per_seq, s % PAGE_RING
            return [pltpu.make_async_copy(cache_ref.at[pt_ref[seq, part * n_pages + t]], pages_sc.at[slot, t],
                                          page_sem.at[slot]) for t in range(n_pages)]

        @pl.when(step_idx == 0)
        def _():
            for s in range(min(PAGE_RING - 1, n_steps)):
                for cp in page_copies(s):
                    cp.start()

        @pl.when(step_idx + PAGE_RING - 1 < n_steps)
        def _():
            for cp in page_copies(step_idx + PAGE_RING - 1):
                cp.start()

        for cp in page_copies(step_idx):
            cp.wait()
        slot = step_idx % PAGE_RING
        page_refs = [pages_sc.at[slot, t] for t in range(n_pages)]
    else:
        t5_ref, q_ref, k_ref, v_ref, o_ref, kb_sc, vb_sc, kdiff_sc, bias_sc = refs
    blk = MOBA_BLOCK
    nbp = SUBLANES
    assert nb <= nbp
    scale = A_DH ** -0.5
    ii = lax.broadcasted_iota(I32, (blk, blk), 0)
    jj = lax.broadcasted_iota(I32, (blk, blk), 1)

    @pl.when((b == 0) & (c == 0))
    def _():
        for t in range(2):
            bucket = _t5_bucket(t * blk + ii - jj)
            tile = jnp.zeros((blk, blk), F32)
            for bkt in range(T5_BUCKETS):
                tile = jnp.where(bucket == bkt, t5_ref[bkt, h], tile)
            tile = tile * LOG2E
            bias_sc[t] = jnp.where(ii >= jj, tile, NEG_INF) if t == 0 else tile

    @pl.when(c == 0)
    def _():
        km = jnp.concatenate(
            [jnp.mean(k_ref[n * blk:(n + 1) * blk, :], axis=0, keepdims=True) for n in range(nb)]
            + [jnp.zeros((1, A_DH), F32)] * (nbp - nb), axis=0)
        for n in range(nb):
            kdiff_sc[n * nbp:(n + 1) * nbp, :] = km[n:n + 1, :] - km
            kb_sc[n * blk:(n + 1) * blk, A_DH:2 * A_DH] = jnp.where(
                lax.broadcasted_iota(I32, (blk, 128), 1) == n, 1.0, 0.0).astype(BF16)
        kb_sc[:, 0:A_DH] = k_ref[...].astype(BF16)
        vb_sc[:, 0:A_DH] = v_ref[...].astype(BF16)
        vb_sc[:, A_DH:2 * A_DH] = jnp.ones((nb * blk, A_DH), BF16)
        if side_blocks:
            gates_ref[...] = jnp.zeros_like(gates_ref)

    far_bias = t5_ref[T5_BUCKETS - 1, h] * LOG2E

    def attend(cc):
        q = q_ref[cc * blk:(cc + 1) * blk, :]
        qs = (q * (scale * LOG2E)).astype(BF16)
        n_keys = (cc + 1) * blk
        if cc <= MOBA_TOPK:
            s_all = lax.dot_general(qs, kb_sc[0:n_keys, 0:A_DH], NT_DIMS, preferred_element_type=F32)
        else:
            diff_t = lax.dot_general(kdiff_sc[0:cc * nbp, :], q, NT_DIMS, precision=lax.Precision.HIGHEST,
                                     preferred_element_type=F32)
            pair = lax.broadcasted_iota(I32, diff_t.shape, 0)
            m_of, n_of = jnp.right_shift(pair, int(math.log2(nbp))), jnp.bitwise_and(pair, nbp - 1)
            beats_t = jnp.where(diff_t > 0, 1.0, jnp.where(diff_t == 0, jnp.where(n_of > m_of, 1.0, 0.0), 0.0))
            fold = jnp.where(jnp.bitwise_and(lax.broadcasted_iota(I32, (cc * nbp, 128), 0), nbp - 1)
                             == lax.broadcasted_iota(I32, (cc * nbp, 128), 1), 1.0, 0.0)
            rank = lax.dot_general(beats_t.astype(BF16), fold.astype(BF16), TN_DIMS,
                                   preferred_element_type=F32)
            lane = lax.broadcasted_iota(I32, rank.shape, 1)
            sel = jnp.where(lane < cc, jnp.where(rank < MOBA_TOPK, 0.0, NEG_INF), 0.0)
            q_sel = jnp.concatenate([qs, sel.astype(BF16)], axis=1)
            s_all = lax.dot_general(q_sel, kb_sc[0:n_keys, :], NT_DIMS, preferred_element_type=F32)
        tiles = []
        for n in range(cc + 1):
            t = s_all[:, n * blk:(n + 1) * blk]
            tiles.append(t + (bias_sc[0] if n == cc else bias_sc[1] if n == cc - 1 else far_bias))
        m = jnp.max(functools.reduce(jnp.maximum, tiles), axis=1, keepdims=True)
        p_all = jnp.concatenate([jnp.exp2(t - m).astype(BF16) for t in tiles], axis=1)
        pv = _dot(p_all, vb_sc[0:n_keys, :])
        o_ref[cc * blk:(cc + 1) * blk, :] = (pv[:, 0:A_DH] / pv[:, A_DH:2 * A_DH]).astype(o_ref.dtype)

    def step(s):
        if side_blocks:
            gates_ref[...] = _block_gates(qside_ref[...], page_refs, s * side_blocks, gates_ref[...])
        for cc in sorted({s, nb - 1 - s}):
            attend(cc)

    for s in range(_moba_steps(nb)):
        pl.when(c == s)(functools.partial(step, s))


def _moba_steps(nb):
    return (nb + 1) // 2


def _moba_prompt(z3, k, v, t5_table, batch, seq, paged=None):
    blk = MOBA_BLOCK
    nb = seq // blk
    steps = _moba_steps(nb)
    side_blocks = 0
    if paged is not None:
        page_table, q_side, cache_k = paged
        nb_past = page_table.shape[1] // PAGES_PER_BLOCK
        if page_table.shape[0] == A_HEADS * batch and nb_past % steps == 0:
            side_blocks = nb_past // steps

    seq_spec = pl.BlockSpec((seq, A_DH), lambda h, b, c, *_: (b, h))
    in_specs = [pl.BlockSpec(memory_space=pltpu.SMEM),
                pl.BlockSpec((None, seq, A_DH), lambda h, b, c, *_: (SEC_QA, b, h)), seq_spec, seq_spec]
    out_specs = [seq_spec]
    out_shape = [jax.ShapeDtypeStruct((batch * seq, A_HEADS * A_DH), BF16)]
    args = [t5_table, z3, k, v]
    scratch = [pltpu.VMEM((seq, 2 * A_DH), BF16), pltpu.VMEM((seq, 2 * A_DH), BF16),
               pltpu.VMEM((SUBLANES * SUBLANES, A_DH), F32), pltpu.VMEM((2, blk, blk), F32)]
    grid = (A_HEADS, batch, steps)
    if side_blocks:
        in_specs += [pl.BlockSpec((None, None, A_HEADS, A_DH), lambda h, b, c, pt: (SEC_QA, h * batch + b, 0, 0)),
                     pl.BlockSpec(memory_space=pl.ANY)]
        out_specs.append(pl.BlockSpec((None, A_HEADS, 128), lambda h, b, c, pt: (h * batch + b, 0, 0)))
        out_shape.append(jax.ShapeDtypeStruct((A_HEADS * batch, A_HEADS, 128), F32))
        args = [page_table] + args + [q_side, cache_k]
        scratch += [pltpu.VMEM((PAGE_RING, side_blocks * PAGES_PER_BLOCK, PAGE_SIZE, A_HEADS, A_DH), F32),
                    pltpu.SemaphoreType.DMA((PAGE_RING,))]
    out = pl.pallas_call(
        functools.partial(_moba_prompt_body, nb=nb, side_blocks=side_blocks, grid=grid),
        grid_spec=pltpu.PrefetchScalarGridSpec(
            num_scalar_prefetch=1 if side_blocks else 0,
            grid=grid,
            in_specs=in_specs,
            out_specs=out_specs,
            scratch_shapes=scratch,
        ),
        out_shape=out_shape,
        compiler_params=_params("arbitrary", "arbitrary", "arbitrary"),
        name="moba_prompt",
    )(*args)
    if paged is None:
        return out[0]
    return out[0], (out[1] if side_blocks else None)


def _rope_body(inv_ref, cos_ref, sin_ref, *, pos0):
    pos = pos0 + lax.broadcasted_iota(I32, cos_ref.shape, 0)
    ang = pos.astype(F32) * inv_ref[...]
    cos_ref[...] = jnp.cos(ang)
    sin_ref[...] = jnp.sin(ang)


def _rope_tables(pos0, n_pos):
    half = R_DK // 2
    inv = 1.0 / (RET_ROPE_BASE ** jnp.linspace(0.0, 1.0, half, dtype=F32))
    rows = -(-n_pos // 8) * 8
    return pl.pallas_call(
        functools.partial(_rope_body, pos0=pos0),
        out_shape=[jax.ShapeDtypeStruct((rows, half), F32)] * 2,
        name="rope_tables",
    )(inv.reshape(1, half))


def _rotate(x, cos, sin):
    half = x.shape[-1] // 2
    x1, x2 = x[:, :half], x[:, half:]
    return jnp.concatenate([x1 * cos - x2 * sin, x1 * sin + x2 * cos], axis=1)


def _log_decay(h, shape):
    hf = jnp.full(shape, h, I32).astype(F32)
    return jnp.log(1.0 - jnp.exp2(-5.0 - hf))


def _ret_prompt_body(q_ref, k_ref, v_ref, g_ref, cos_ref, sin_ref, o_ref, so_ref, s_sc, *, c):
    n = pl.program_id(1)

    @pl.when(n == 0)
    def _():
        s_sc[...] = jnp.zeros_like(s_sc)

    i = lax.broadcasted_iota(I32, (c, c), 0).astype(F32)
    j = lax.broadcasted_iota(I32, (c, c), 1).astype(F32)
    diff = i - j
    i_col = lax.broadcasted_iota(I32, (c, 1), 0).astype(F32)
    cos, sin = cos_ref[...], sin_ref[...]

    new_states, outs = [], []
    for h in range(R_HEADS):
        dk, dv = slice(h * R_DK, (h + 1) * R_DK), slice(h * R_DV, (h + 1) * R_DV)
        dmask = jnp.where(diff >= 0, jnp.exp(jnp.maximum(diff, 0.0) * _log_decay(h, (c, c))), 0.0)
        lg_col = _log_decay(h, (c, 1))
        q_dec = jnp.exp((i_col + 1.0) * lg_col)
        k_dec = jnp.exp((c - 1.0 - i_col) * lg_col)
        c_dec = jnp.exp(c * _log_decay(h, (1, R_DV)))
        qr = _rotate(q_ref[:, dk], cos, sin)
        kr = _rotate(k_ref[:, dk], cos, sin) * (R_DK ** -0.5)
        vb = v_ref[:, dv].astype(BF16)
        s = s_sc[h]
        att = lax.dot_general(qr.astype(BF16), kr.astype(BF16), NT_DIMS, preferred_element_type=F32) * dmask
        o = _dot(att.astype(BF16), vb) + _dot((qr * q_dec).astype(BF16), s.astype(BF16))
        new_states.append(s * c_dec + lax.dot_general((kr * k_dec).astype(BF16), vb, TN_DIMS,
                                                      preferred_element_type=F32))
        on = o * lax.rsqrt(jnp.mean(o * o, axis=-1, keepdims=True) + EPS)
        outs.append((on * _silu(g_ref[:, dv])).astype(o_ref.dtype))
    for h in range(R_HEADS):
        s_sc[h] = new_states[h]
    o_ref[...] = jnp.concatenate(outs, axis=1)

    @pl.when(n == pl.num_programs(1) - 1)
    def _():
        so_ref[...] = s_sc[...]


def _ret_prompt(z3, cos, sin, batch, seq):
    c = math.gcd(seq, RET_CHUNK)
    nc = seq // c

    def sec(s):
        return pl.BlockSpec((None, c, SEC), lambda b, n: (s, b * nc + n, 0))

    tab = pl.BlockSpec((c, R_DK // 2), lambda b, n: (n, 0))
    return pl.pallas_call(
        functools.partial(_ret_prompt_body, c=c),
        grid=(batch, nc),
        in_specs=[sec(SEC_QR), sec(SEC_KR), sec(SEC_VR), sec(SEC_GR), tab, tab],
        out_specs=[pl.BlockSpec((c, R_HEADS * R_DV), lambda b, n: (b * nc + n, 0)),
                   pl.BlockSpec((None, R_HEADS, R_DK, R_DV), lambda b, n: (b, 0, 0, 0))],
        out_shape=[jax.ShapeDtypeStruct((batch * seq, R_HEADS * R_DV), BF16),
                   jax.ShapeDtypeStruct((batch, R_HEADS, R_DK, R_DV), F32)],
        scratch_shapes=[pltpu.VMEM((R_HEADS, R_DK, R_DV), F32)],
        compiler_params=_params("parallel", "arbitrary"),
        name="retention_prompt",
    )(z3, z3, z3, z3, cos, sin)


def _ret_step_body(q_ref, k_ref, v_ref, g_ref, cos_ref, sin_ref, s_ref, o_ref, so_ref):
    cos, sin = cos_ref[0:1, :], sin_ref[0:1, :]
    rows = 16
    row0 = lax.broadcasted_iota(I32, (rows, R_DK), 0) == 0
    for h in range(R_HEADS):
        decay = jnp.exp(_log_decay(h, (1, R_DV)))
        qr = _rotate(q_ref[h:h + 1, :], cos, sin)
        kr = _rotate(k_ref[h:h + 1, :], cos, sin) * (R_DK ** -0.5)
        v = v_ref[h:h + 1, :]
        s = s_ref[h]
        q_rows = jnp.broadcast_to(qr * decay, (rows, R_DK)).astype(BF16)
        qs = _dot(q_rows, s.astype(BF16))[0:1, :]
        o = jnp.sum(qr * kr, axis=-1, keepdims=True) * v + qs
        k_rows = jnp.where(row0, jnp.broadcast_to(kr, (rows, R_DK)), 0.0).astype(BF16)
        v_rows = jnp.broadcast_to(v, (rows, R_DV)).astype(BF16)
        so_ref[h] = s * decay + lax.dot_general(k_rows, v_rows, TN_DIMS, preferred_element_type=F32)
        on = o * lax.rsqrt(jnp.mean(o * o, axis=-1, keepdims=True) + EPS)
        o_ref[h:h + 1, :] = (on * _silu(g_ref[h:h + 1, :])).astype(o_ref.dtype)


def _mem_rows(q, k_ref, v_ref):
    outs = []
    for h in range(M_HEADS):
        dh = slice(h * M_DH, (h + 1) * M_DH)
        s = lax.dot_general(q[:, dh].astype(BF16), k_ref[:, dh].astype(BF16), NT_DIMS,
                            preferred_element_type=F32) * (M_DH ** -0.5)
        m = jnp.max(s, axis=-1, keepdims=True)
        p = jnp.exp(s - m)
        l = jnp.sum(p, axis=-1, keepdims=True)
        outs.append(_dot(p.astype(BF16), v_ref[:, dh].astype(BF16)) / l)
    return jnp.concatenate(outs, axis=1)


def _mem_body(q_ref, k_ref, v_ref, o_ref):
    o_ref[...] = _mem_rows(q_ref[...], k_ref, v_ref).astype(o_ref.dtype)


def _mem_attend(z3, mk, mv, batch, seq):
    n_mem = mk.shape[0] // batch
    width = M_HEADS * M_DH
    ts = _tile(seq, 512)
    nt = seq // ts
    kv = pl.BlockSpec((n_mem, width), lambda b, t: (b, 0))
    return pl.pallas_call(
        _mem_body,
        grid=(batch, nt),
        in_specs=[pl.BlockSpec((None, ts, width), lambda b, t: (SEC_QM, b * nt + t, 0)), kv, kv],
        out_specs=pl.BlockSpec((ts, width), lambda b, t: (b * nt + t, 0)),
        out_shape=jax.ShapeDtypeStruct((batch * seq, width), BF16),
        compiler_params=_params("parallel", "arbitrary"),
        name="mem_attend",
    )(z3, mk, mv)


def _merge_body(h_ref, *refs):
    x_refs, wg_refs, bg_refs, wb_refs, o_ref = refs[0:3], refs[3:6], refs[6:9], refs[9:12], refs[12]
    h = h_ref[...]
    merged = None
    for g in range(3):
        term = jax.nn.sigmoid(_dot(h, wg_refs[g][...]) + bg_refs[g][...]) * _dot(x_refs[g][...], wb_refs[g][...])
        merged = term if merged is None else merged + term
    o_ref[...] = merged.astype(o_ref.dtype)


def _merge(h, branches, w_gate, b_gate, branch_ws):
    rows, d = h.shape
    tm, tn = _tile(rows, 1024), _tile(d, 512)
    nj = d // tn
    b_gate = b_gate.reshape(1, 3 * d)

    def row(width):
        return pl.BlockSpec((tm, width), lambda i, j: (i, 0))

    def cols(n_rows, group=0):
        return pl.BlockSpec((n_rows, tn), lambda i, j: (0, group * nj + j))

    return pl.pallas_call(
        _merge_body,
        grid=(rows // tm, nj),
        in_specs=([row(d)] + [row(x.shape[1]) for x in branches]
                  + [cols(d, g) for g in range(3)] + [cols(1, g) for g in range(3)]
                  + [cols(w.shape[0]) for w in branch_ws]),
        out_specs=pl.BlockSpec((tm, tn), lambda i, j: (i, j)),
        out_shape=jax.ShapeDtypeStruct((rows, d), BF16),
        compiler_params=_params("parallel", "arbitrary"),
        name="gated_merge",
    )(h, *branches, w_gate, w_gate, w_gate, b_gate, b_gate, b_gate, *branch_ws)


def _moba_gate_body(pt_ref, q_ref, *refs, bps):
    del pt_ref
    k_refs, gates_ref = refs[:-1], refs[-1]
    n = pl.program_id(1)

    @pl.when(n == 0)
    def _():
        gates_ref[...] = jnp.zeros_like(gates_ref)

    gates_ref[...] = _block_gates(q_ref[...], k_refs, n * bps, gates_ref[...])


def _moba_select_body(g_ref, sel_ref, *, nb_past):
    lane = lax.broadcasted_iota(I32, g_ref.shape, 1)
    gw = jnp.where(lane < nb_past, g_ref[...], NEG_INF)
    sel = jnp.zeros(gw.shape, I32)
    for r in range(MOBA_TOPK):
        mx = jnp.max(gw, axis=1, keepdims=True)
        idx = jnp.min(jnp.where(gw == mx, lane, 128), axis=1, keepdims=True)
        sel = jnp.where(lane == r, idx, sel)
        gw = jnp.where(lane == idx, -jnp.inf, gw)
    sel_ref[...] = sel


def _moba_attend_body(pt_ref, sel_ref, t5_ref, q_ref, kn_ref, vn_ref, ck_ref, cv_ref, o_ref, kbuf, vbuf, sem,
                      *, past_len):
    b = pl.program_id(0)
    slot = lax.rem(b, 2)
    scale = A_DH ** -0.5
    n_tiles = MOBA_TOPK * PAGES_PER_BLOCK

    def tile_copies(bb, to_slot):
        copies = []
        for h in range(A_HEADS):
            for r in range(MOBA_TOPK):
                block = sel_ref[bb, h * MOBA_TOPK + r]
                for t in range(PAGES_PER_BLOCK):
                    page = pt_ref[bb, block * PAGES_PER_BLOCK + t]
                    j = r * PAGES_PER_BLOCK + t
                    copies.append(pltpu.make_async_copy(ck_ref.at[page, :, h, :], kbuf.at[to_slot, h, j],
                                                        sem.at[to_slot]))
                    copies.append(pltpu.make_async_copy(cv_ref.at[page, :, h, :], vbuf.at[to_slot, h, j],
                                                        sem.at[to_slot]))
        return copies

    def start_all(copies):
        for i, cp in enumerate(copies):
            cp.start(priority=i % 2)

    @pl.when(b == 0)
    def _():
        start_all(tile_copies(0, 0))

    @pl.when(b + 1 < pl.num_programs(0))
    def _():
        start_all(tile_copies(b + 1, 1 - slot))

    for cp in tile_copies(b, slot):
        cp.wait()

    q, kn, vn = q_ref[...], kn_ref[...], vn_ref[...]
    outs = []
    assert n_tiles <= SUBLANES
    lane = lax.broadcasted_iota(I32, (SUBLANES, PAGE_SIZE), 1)
    row = lax.broadcasted_iota(I32, (SUBLANES, PAGE_SIZE), 0)
    for h in range(A_HEADS):
        qh = q[h:h + 1, :]
        q8 = jnp.broadcast_to(qh, (8, A_DH)).astype(BF16)
        first_key = jnp.zeros((SUBLANES, PAGE_SIZE), I32)
        for r in range(MOBA_TOPK):
            block = sel_ref[b, h * MOBA_TOPK + r]
            for t in range(PAGES_PER_BLOCK):
                first_key = jnp.where(row == r * PAGES_PER_BLOCK + t, block * MOBA_BLOCK + t * PAGE_SIZE, first_key)
        bucket = _t5_bucket(past_len - (first_key + lane))
        bias = jnp.zeros((SUBLANES, PAGE_SIZE), F32)
        for bkt in range(T5_BUCKETS):
            bias = jnp.where(bucket == bkt, t5_ref[bkt, h], bias)
        s = []
        for j in range(n_tiles):
            kt = kbuf[slot, h, j].astype(BF16)
            raw = lax.dot_general(q8, kt, NT_DIMS, preferred_element_type=F32)[0:1, :]
            s.append(raw * scale + bias[j:j + 1, :])
        s_own = jnp.sum(qh * kn[h:h + 1, :], axis=1, keepdims=True) * scale + t5_ref[0, h]
        m = jnp.maximum(jnp.max(functools.reduce(jnp.maximum, s), axis=1, keepdims=True), s_own)
        p = [jnp.exp(x - m) for x in s]
        p_own = jnp.exp(s_own - m)
        l = jnp.sum(functools.reduce(jnp.add, p), axis=1, keepdims=True) + p_own
        acc = p_own * vn[h:h + 1, :]
        for j in range(n_tiles):
            p8 = jnp.broadcast_to(p[j], (8, PAGE_SIZE)).astype(BF16)
            acc = acc + _dot(p8, vbuf[slot, h, j].astype(BF16))[0:1, :]
        outs.append(acc / l)
    o_ref[...] = jnp.concatenate(outs, axis=0)


def _moba_gates(q4, cache_k, page_table):
    batch, n_pages = page_table.shape
    nb_past = n_pages // PAGES_PER_BLOCK
    bps = _tile(nb_past, 8)

    def page(t):
        return pl.BlockSpec((None, PAGE_SIZE, A_HEADS, A_DH),
                            lambda b, n, pt: (pt[b, n * bps * PAGES_PER_BLOCK + t], 0, 0, 0))

    pages = [page(t) for t in range(bps * PAGES_PER_BLOCK)]
    return pl.pallas_call(
        functools.partial(_moba_gate_body, bps=bps),
        grid_spec=pltpu.PrefetchScalarGridSpec(
            num_scalar_prefetch=1,
            grid=(batch, nb_past // bps),
            in_specs=[pl.BlockSpec((None, None, A_HEADS, A_DH), lambda b, n, pt: (SEC_QA, b, 0, 0))] + pages,
            out_specs=pl.BlockSpec((None, A_HEADS, 128), lambda b, n, pt: (b, 0, 0)),
        ),
        out_shape=jax.ShapeDtypeStruct((batch, A_HEADS, 128), F32),
        compiler_params=_params("parallel", "arbitrary"),
        name="moba_gate",
    )(page_table, q4, *([cache_k] * len(pages)))


def _sample_mixers_body(pt_ref, sel_ref, t5_ref, qa_ref, kn_ref, vn_ref, ck_ref, cv_ref,
                        qr_ref, kr_ref, vr_ref, gr_ref, cos_ref, sin_ref, state_ref, qm_ref, mk_ref, mv_ref,
                        oa_ref, or_ref, state_out_ref, om_ref, kbuf, vbuf, sem, *, past_len):
    _moba_attend_body(pt_ref, sel_ref, t5_ref, qa_ref, kn_ref, vn_ref, ck_ref, cv_ref, oa_ref, kbuf, vbuf, sem,
                      past_len=past_len)
    _ret_step_body(qr_ref, kr_ref, vr_ref, gr_ref, cos_ref, sin_ref, state_ref, or_ref, state_out_ref)
    q_rows = jnp.broadcast_to(qm_ref[...], (BF16_SUBLANES, qm_ref.shape[-1]))
    om_ref[...] = _mem_rows(q_rows, mk_ref, mv_ref)[0:1, :]


def _sample_mixers(z3, k_new, v_new, cache_k, cache_v, page_table, t5_table, gates, cos, sin, state, mem_k, mem_v):
    batch, n_pages = page_table.shape
    past_len = n_pages * PAGE_SIZE
    assert past_len % MOBA_BLOCK == 0 and MOBA_BLOCK % PAGE_SIZE == 0
    nb_past = past_len // MOBA_BLOCK
    assert MOBA_TOPK <= nb_past <= 128
    n_sec = z3.shape[0]
    n_mem = mem_k.shape[0] // batch
    z4a = z3.reshape(n_sec, batch, A_HEADS, A_DH)
    z4r = z3.reshape(n_sec, batch, R_HEADS, R_DK)
    z4m = z3.reshape(n_sec, batch, 1, M_HEADS * M_DH)
    k_new = k_new.reshape(batch, A_HEADS, A_DH)
    v_new = v_new.reshape(batch, A_HEADS, A_DH)
    if gates is None:
        gates = _moba_gates(z4a, cache_k, page_table)
    sel = pl.pallas_call(
        functools.partial(_moba_select_body, nb_past=nb_past),
        out_shape=jax.ShapeDtypeStruct((batch * A_HEADS, 128), I32),
        name="moba_select",
    )(gates.reshape(batch * A_HEADS, 128))
    sel = sel[:, :MOBA_TOPK].reshape(batch, A_HEADS * MOBA_TOPK)

    def sec(s, rows, width):
        return pl.BlockSpec((None, None, rows, width), lambda b, pt, sl: (s, b, 0, 0))

    def per_seq(*shape):
        return pl.BlockSpec((None,) + shape, lambda b, pt, sl: (b,) + (0,) * len(shape))

    tab = pl.BlockSpec((SUBLANES, R_DK // 2), lambda b, pt, sl: (0, 0))
    mem = pl.BlockSpec((n_mem, M_HEADS * M_DH), lambda b, pt, sl: (b, 0))
    tiles = pltpu.VMEM((2, A_HEADS, MOBA_TOPK * PAGES_PER_BLOCK, PAGE_SIZE, A_DH), F32)
    return pl.pallas_call(
        functools.partial(_sample_mixers_body, past_len=past_len),
        grid_spec=pltpu.PrefetchScalarGridSpec(
            num_scalar_prefetch=2,
            grid=(batch,),
            in_specs=[pl.BlockSpec(memory_space=pltpu.SMEM), sec(SEC_QA, A_HEADS, A_DH),
                      per_seq(A_HEADS, A_DH), per_seq(A_HEADS, A_DH),
                      pl.BlockSpec(memory_space=pl.ANY), pl.BlockSpec(memory_space=pl.ANY),
                      sec(SEC_QR, R_HEADS, R_DK), sec(SEC_KR, R_HEADS, R_DK), sec(SEC_VR, R_HEADS, R_DK),
                      sec(SEC_GR, R_HEADS, R_DK), tab, tab, per_seq(R_HEADS, R_DK, R_DV),
                      sec(SEC_QM, 1, M_HEADS * M_DH), mem, mem],
            out_specs=[per_seq(A_HEADS, A_DH), per_seq(R_HEADS, R_DV), per_seq(R_HEADS, R_DK, R_DV),
                       per_seq(1, M_HEADS * M_DH)],
            scratch_shapes=[tiles, tiles, pltpu.SemaphoreType.DMA((2,))],
        ),
        out_shape=[jax.ShapeDtypeStruct((batch, A_HEADS, A_DH), F32),
                   jax.ShapeDtypeStruct((batch, R_HEADS, R_DV), F32),
                   jax.ShapeDtypeStruct(state.shape, F32),
                   jax.ShapeDtypeStruct((batch, 1, M_HEADS * M_DH), F32)],
        compiler_params=_params("arbitrary"),
        name="sample_mixers",
    )(page_table, sel, t5_table, z4a, k_new, v_new, cache_k, cache_v, z4r, z4r, z4r, z4r, cos, sin, state,
      z4m, mem_k, mem_v)


def kernel(x_prompt, x_sample, mem_prompt, cache_k, cache_v, cache_mem_k, cache_mem_v, state_ret, page_table, t5_table, ffn1_norm, ffn1_w1, ffn1_w3, ffn1_w2, mix_norm, mem_norm, w_in, w_mem_kv, w_gate, b_gate, w_br_moba, w_br_ret, w_br_mem, w_out, ffn2_norm, ffn2_w1, ffn2_w3, ffn2_w2, final_norm):
    batch, seq, d = x_prompt.shape
    dec_batch, dec_seq, _ = x_sample.shape
    assert dec_seq == 1
    depth = w_in.shape[0]
    n_mem = mem_prompt.shape[1]
    past_len = page_table.shape[1] * PAGE_SIZE
    cos_p, sin_p = _rope_tables(0, seq)
    cos_s, sin_s = _rope_tables(past_len, dec_seq)

    xp = x_prompt.reshape(batch * seq, d)
    xs = x_sample.reshape(dec_batch, d)
    outs = [[] for _ in range(8)]
    for l in range(depth):
        last = l == depth - 1
        g_next = final_norm if last else ffn1_norm[l + 1]

        x1s, h2s, *f1_b = _ffn_half(xs, ffn1_norm[l], ffn1_w1[l], ffn1_w3[l], ffn1_w2[l], mix_norm[l],
                                    emit_x=True, post_dtype=BF16, emit_w=True)
        x1, h2, w_in_b, *f2_b = _ffn_half(xp, ffn1_norm[l], *f1_b, mix_norm[l], emit_x=True, post_dtype=BF16,
                                          cast_jobs=[w_in[l], ffn2_w1[l], ffn2_w3[l], ffn2_w2[l]])
        z3s, ka_s, va_s = _in_proj(h2s, w_in_b)
        z3, ka, va, w_gate_b, w_out_b, *branch_b = _in_proj(
            h2, w_in_b, cast_jobs=[w_gate[l], w_out[l], w_br_moba[l], w_br_ret[l], w_br_mem[l]])
        mem_k_s, mem_v_s, x1s = lax.optimization_barrier((cache_mem_k[l], cache_mem_v[l], x1s))

        z4s = z3s.reshape(z3s.shape[0], dec_batch, A_HEADS, A_DH)
        oa, gates_s = _moba_prompt(z3, ka, va, t5_table, batch, seq, paged=(page_table, z4s, cache_k[l]))

        oa_s, o_rs, s_new_s, om_s = _sample_mixers(
            z3s, ka_s, va_s, cache_k[l], cache_v[l], page_table, t5_table, gates_s, cos_s, sin_s, state_ret[l],
            mem_k_s.reshape(dec_batch * n_mem, M_HEADS * M_DH), mem_v_s.reshape(dec_batch * n_mem, M_HEADS * M_DH))
        branches_s = [o.reshape(dec_batch, -1).astype(BF16) for o in (oa_s, o_rs, om_s)]

        mkv_p = _norm_matmul(mem_prompt.reshape(batch * n_mem, d), mem_norm[l], w_mem_kv[l])
        mk_p, mv_p = mkv_p[:, :M_HEADS * M_DH], mkv_p[:, M_HEADS * M_DH:]
        o_r, s_new_p = _ret_prompt(z3, cos_p, sin_p, batch, seq)
        om = _mem_attend(z3, mk_p, mv_p, batch, seq)

        merged_s = _merge(h2s, branches_s, w_gate_b, b_gate[l], branch_b)
        merged = _merge(h2, [oa, o_r, om], w_gate_b, b_gate[l], branch_b)
        x2s = _matmul(merged_s, w_out_b, residual=x1s, name="out_proj")
        x2 = _matmul(merged, w_out_b, residual=x1, name="out_proj")
        xs = _ffn_half(x2s, ffn2_norm[l], *f2_b, g_next, emit_x=not last, post_dtype=F32)[0]
        xp = _ffn_half(x2, ffn2_norm[l], *f2_b, g_next, emit_x=not last, post_dtype=F32)[0]

        new = (ka.reshape(batch, seq, A_HEADS, A_DH), va.reshape(batch, seq, A_HEADS, A_DH),
               mk_p.reshape(batch, n_mem, M_HEADS, M_DH), mv_p.reshape(batch, n_mem, M_HEADS, M_DH), s_new_p,
               ka_s.reshape(dec_batch, dec_seq, A_HEADS, A_DH), va_s.reshape(dec_batch, dec_seq, A_HEADS, A_DH),
               s_new_s)
        for acc, val in zip(outs, new):
            acc.append(val)

    y_prompt = xp.reshape(batch, seq, d)
    y_sample = xs.reshape(dec_batch, dec_seq, d)
    return (y_prompt, y_sample) + tuple(jnp.stack(o) for o in outs)
```

```python
import functools
import math

import jax
import jax.numpy as jnp
import numpy as np
from jax import lax
from jax.experimental import pallas as pl
from jax.experimental.pallas import tpu as pltpu

F32 = jnp.float32
BF16 = jnp.bfloat16
I32 = jnp.int32

A_HEADS, A_DH = 8, 128
MOBA_BLOCK, MOBA_TOPK = 256, 3
T5_BUCKETS, T5_MAX_DIST = 32, 128
R_HEADS, R_DK, R_DV = 4, 256, 256
RET_CHUNK = 128
RET_ROPE_BASE = 10000.0
M_HEADS, M_DH = 4, 256
PAGE_SIZE = 128
EPS = 1e-6
NEG_INF = -1e30
LOG2E = math.log2(math.e)
SEC = 1024
W_IN_KA, W_IN_VA = 1, 2
(SEC_QA, SEC_QR, SEC_KR, SEC_VR, SEC_GR, SEC_QM) = range(6)

V7X_VMEM_LIMIT_BYTES = 56 * 1024 * 1024

NT_DIMS = (((1,), (1,)), ((), ()))
TN_DIMS = (((0,), (0,)), ((), ()))


def _params(*sem):
    return pltpu.CompilerParams(dimension_semantics=sem, vmem_limit_bytes=V7X_VMEM_LIMIT_BYTES)


def _tile(n, pref):
    t = min(n, pref)
    while n % t:
        t -= 1
    return t


def _rms(x, g):
    return x * lax.rsqrt(jnp.mean(x * x, axis=-1, keepdims=True) + EPS) * g


def _silu(x):
    return x * jax.nn.sigmoid(x)


def _dot(a, b):
    return jnp.dot(a, b, preferred_element_type=F32)


def _bf16(w):
    return w if w.dtype == BF16 else w.astype(BF16)


def _ffn_body(x_ref, g_ref, w1_ref, w3_ref, w2_ref, gp_ref, *refs, emit_x, emit_w, n_cast):
    refs = list(refs)
    cast_in = [refs.pop(0) for _ in range(n_cast)]
    xo_ref = refs.pop(0) if emit_x else None
    ho_ref = refs.pop(0)
    wb_refs = [refs.pop(0) for _ in range(3)] if emit_w else []
    cast_out = [refs.pop(0) for _ in range(n_cast)]
    (h_sc,) = refs
    acc_ref = xo_ref if emit_x else ho_ref
    f = pl.program_id(1)

    @pl.when(f == 0)
    def _():
        x = x_ref[...]
        h_sc[...] = _rms(x, g_ref[...]).astype(BF16)
        acc_ref[...] = x

    _cast_rows(cast_in, cast_out)

    w1, w3, w2 = _bf16(w1_ref[...]), _bf16(w3_ref[...]), _bf16(w2_ref[...])
    for wb_ref, w in zip(wb_refs, (w1, w3, w2)):
        wb_ref[...] = w
    h = h_sc[...]
    t = _silu(_dot(h, w1)) * _dot(h, w3) * 0.5
    acc_ref[...] += _dot(t.astype(BF16), w2)

    @pl.when(f == pl.num_programs(1) - 1)
    def _():
        ho_ref[...] = _rms(acc_ref[...], gp_ref[...]).astype(ho_ref.dtype)


BF16_SUBLANES = 16


def _cast_spec(a, grid):
    n_steps = grid[0] * grid[1]
    n_chunks = max(n for n in range(1, n_steps + 1)
                   if a.shape[0] % n == 0 and (a.shape[0] // n) % BF16_SUBLANES == 0)
    return pl.BlockSpec((a.shape[0] // n_chunks, a.shape[1]),
                        lambda i, j: (jnp.minimum(i * grid[1] + j, n_chunks - 1), 0))


def _cast_rows(src_refs, dst_refs):
    for src_ref, dst_ref in zip(src_refs, dst_refs):
        dst_ref[...] = src_ref[...].astype(BF16)


def _ffn_half(x, g, w1, w3, w2, g_post, *, emit_x, post_dtype, emit_w=False, cast_jobs=()):
    rows, d = x.shape
    ff = w1.shape[1]
    tm, tf = _tile(rows, 512), _tile(ff, 512)
    assert not emit_w or rows == tm
    assert emit_x or post_dtype == F32
    grid = (rows // tm, ff // tf)
    row_spec = pl.BlockSpec((tm, d), lambda i, f: (i, 0))
    vec_spec = pl.BlockSpec((1, d), lambda i, f: (0, 0))
    w13_spec = pl.BlockSpec((d, tf), lambda i, f: (0, f))
    w2_spec = pl.BlockSpec((tf, d), lambda i, f: (f, 0))
    out_shape = [jax.ShapeDtypeStruct((rows, d), post_dtype)]
    out_specs = [row_spec]
    if emit_x:
        out_shape = [jax.ShapeDtypeStruct((rows, d), F32)] + out_shape
        out_specs = [row_spec] + out_specs
    if emit_w:
        out_shape += [jax.ShapeDtypeStruct(w.shape, BF16) for w in (w1, w3, w2)]
        out_specs += [w13_spec, w13_spec, w2_spec]

    cast_specs = [_cast_spec(a, grid) for a in cast_jobs]
    out_shape += [jax.ShapeDtypeStruct(a.shape, BF16) for a in cast_jobs]
    out_specs += cast_specs
    return pl.pallas_call(
        functools.partial(_ffn_body, emit_x=emit_x, emit_w=emit_w, n_cast=len(cast_jobs)),
        grid=grid,
        in_specs=[row_spec, vec_spec, w13_spec, w13_spec, w2_spec, vec_spec] + cast_specs,
        out_specs=out_specs,
        out_shape=out_shape,
        scratch_shapes=[pltpu.VMEM((tm, d), BF16)],
        compiler_params=_params("parallel", "arbitrary"),
        name="ffn_half",
    )(x, g.reshape(1, d), w1, w3, w2, g_post.reshape(1, d), *cast_jobs)


def _norm_mm_body(x_ref, g_ref, w_ref, o_ref, h_sc):
    @pl.when(pl.program_id(1) == 0)
    def _():
        h_sc[...] = _rms(x_ref[...], g_ref[...]).astype(BF16)

    o_ref[...] = _dot(h_sc[...], _bf16(w_ref[...]))


def _norm_matmul(x, g, w):
    rows, d = x.shape
    n_cols = w.shape[1]
    tm, tn = _tile(rows, 1024), _tile(n_cols, 512)
    return pl.pallas_call(
        _norm_mm_body,
        grid=(rows // tm, n_cols // tn),
        in_specs=[pl.BlockSpec((tm, d), lambda i, j: (i, 0)), pl.BlockSpec((1, d), lambda i, j: (0, 0)),
                  pl.BlockSpec((d, tn), lambda i, j: (0, j))],
        out_specs=pl.BlockSpec((tm, tn), lambda i, j: (i, j)),
        out_shape=jax.ShapeDtypeStruct((rows, n_cols), F32),
        scratch_shapes=[pltpu.VMEM((tm, d), BF16)],
        compiler_params=_params("parallel", "arbitrary"),
        name="norm_matmul",
    )(x, g.reshape(1, d), w)


def _mm_body(a_ref, w_ref, *refs, has_residual):
    o_ref = refs[-1]
    acc = _dot(a_ref[...], _bf16(w_ref[...]))
    if has_residual:
        acc = refs[0][...] + acc
    o_ref[...] = acc.astype(o_ref.dtype)


def _in_proj_body(a_ref, w_ref, *refs):
    n_cast = (len(refs) - 3) // 2
    z_ref, k_ref, v_ref = refs[n_cast:n_cast + 3]
    _cast_rows(refs[:n_cast], refs[n_cast + 3:])
    j = pl.program_id(1)
    acc = _dot(a_ref[...], w_ref[...])

    @pl.when(j == W_IN_KA)
    def _():
        k_ref[...] = acc

    @pl.when(j == W_IN_VA)
    def _():
        v_ref[...] = acc

    @pl.when((j != W_IN_KA) & (j != W_IN_VA))
    def _():
        z_ref[...] = acc


def _in_proj(a, w, cast_jobs=()):
    rows, k = a.shape
    n_sec = w.shape[1] // SEC
    assert (W_IN_KA, W_IN_VA) == (1, 2) and n_sec == 8
    tm = _tile(rows, 1024)
    grid = (rows // tm, n_sec)
    kv_spec = pl.BlockSpec((tm, SEC), lambda i, j: (i, 0))
    cast_specs = [_cast_spec(c, grid) for c in cast_jobs]
    return pl.pallas_call(
        _in_proj_body,
        grid=grid,
        in_specs=[pl.BlockSpec((tm, k), lambda i, j: (i, 0)), pl.BlockSpec((k, SEC), lambda i, j: (0, j))]
        + cast_specs,
        out_specs=[pl.BlockSpec((None, tm, SEC), lambda i, j: (jnp.maximum(j - 2, 0), i, 0)), kv_spec, kv_spec]
        + cast_specs,
        out_shape=[jax.ShapeDtypeStruct((n_sec - 2, rows, SEC), F32)] + [jax.ShapeDtypeStruct((rows, SEC), F32)] * 2
        + [jax.ShapeDtypeStruct(c.shape, BF16) for c in cast_jobs],
        compiler_params=_params("parallel", "arbitrary"),
        name="in_proj",
    )(a, w, *cast_jobs)


def _matmul(a, w, *, residual=None, out_dtype=F32, name="matmul"):
    rows, k = a.shape
    n_cols = w.shape[1]
    tm, tn = _tile(rows, 1024), _tile(n_cols, 1024)
    in_specs = [pl.BlockSpec((tm, k), lambda i, j: (i, 0)), pl.BlockSpec((k, tn), lambda i, j: (0, j))]
    out_spec = pl.BlockSpec((tm, tn), lambda i, j: (i, j))
    args = [a, w]
    if residual is not None:
        in_specs.append(out_spec)
        args.append(residual)
    return pl.pallas_call(
        functools.partial(_mm_body, has_residual=residual is not None),
        grid=(rows // tm, n_cols // tn),
        in_specs=in_specs,
        out_specs=out_spec,
        out_shape=jax.ShapeDtypeStruct((rows, n_cols), out_dtype),
        compiler_params=_params("parallel", "arbitrary"),
        name=name,
    )(*args)


def _t5_bucket(n):
    n = jnp.maximum(n, 0)
    max_exact = T5_BUCKETS // 2
    nf = jnp.maximum(n, 1).astype(F32)
    large = max_exact + (jnp.log(nf / max_exact) / math.log(T5_MAX_DIST / max_exact)
                         * (T5_BUCKETS - max_exact)).astype(I32)
    return jnp.where(n < max_exact, n, jnp.minimum(large, T5_BUCKETS - 1))


PAGES_PER_BLOCK = MOBA_BLOCK // PAGE_SIZE
SUBLANES = 8


def _block_gates(q, page_refs, first_block, gates):
    lane = lax.broadcasted_iota(I32, gates.shape, 1)
    for i in range(len(page_refs) // PAGES_PER_BLOCK):
        pages = [page_refs[i * PAGES_PER_BLOCK + t][...] for t in range(PAGES_PER_BLOCK)]
        ksum = functools.reduce(jnp.add, [jnp.sum(x, axis=0) for x in pages])
        gate = jnp.sum(q * ksum, axis=1, keepdims=True) / MOBA_BLOCK
        gates = jnp.where(lane == first_block + i, gate, gates)
    return gates


PAGE_RING = 3


def _moba_prompt_body(*refs, nb, side_blocks, grid):
    h, b, c = pl.program_id(0), pl.program_id(1), pl.program_id(2)
    if side_blocks:
        (pt_ref, t5_ref, q_ref, k_ref, v_ref, qside_ref, cache_ref, o_ref, gates_ref,
         kb_sc, vb_sc, kdiff_sc, bias_sc, pages_sc, page_sem) = refs
        n_pages = side_blocks * PAGES_PER_BLOCK
        steps_per_seq = grid[2]
        n_steps = grid[0] * grid[1] * grid[2]
        step_idx = (h * grid[1] + b) * steps_per_seq + c

        def page_copies(s):
            seq, part, slot = s // steps_per_seq, s % steps_per_seq, s % PAGE_RING
            return [pltpu.make_async_copy(cache_ref.at[pt_ref[seq, part * n_pages + t]], pages_sc.at[slot, t],
                                          page_sem.at[slot]) for t in range(n_pages)]

        @pl.when(step_idx == 0)
        def _():
            for s in range(min(PAGE_RING - 1, n_steps)):
                for cp in page_copies(s):
                    cp.start()

        @pl.when(step_idx + PAGE_RING - 1 < n_steps)
        def _():
            for cp in page_copies(step_idx + PAGE_RING - 1):
                cp.start()

        for cp in page_copies(step_idx):
            cp.wait()
        slot = step_idx % PAGE_RING
        page_refs = [pages_sc.at[slot, t] for t in range(n_pages)]
    else:
        t5_ref, q_ref, k_ref, v_ref, o_ref, kb_sc, vb_sc, kdiff_sc, bias_sc = refs
    blk = MOBA_BLOCK
    nbp = SUBLANES
    assert nb <= nbp
    scale = A_DH ** -0.5
    ii = lax.broadcasted_iota(I32, (blk, blk), 0)
    jj = lax.broadcasted_iota(I32, (blk, blk), 1)

    @pl.when((b == 0) & (c == 0))
    def _():
        for t in range(2):
            bucket = _t5_bucket(t * blk + ii - jj)
            tile = jnp.zeros((blk, blk), F32)
            for bkt in range(T5_BUCKETS):
                tile = jnp.where(bucket == bkt, t5_ref[bkt, h], tile)
            tile = tile * LOG2E
            bias_sc[t] = jnp.where(ii >= jj, tile, NEG_INF) if t == 0 else tile
        for n in range(nb):
            kb_sc[n * blk:(n + 1) * blk, A_DH:2 * A_DH] = jnp.where(
                lax.broadcasted_iota(I32, (blk, A_DH), 1) == n, 1.0, 0.0).astype(BF16)
        vb_sc[:, A_DH:2 * A_DH] = jnp.ones((nb * blk, A_DH), BF16)

    @pl.when(c == 0)
    def _():
        km = jnp.concatenate(
            [jnp.mean(k_ref[n * blk:(n + 1) * blk, :], axis=0, keepdims=True) for n in range(nb)]
            + [jnp.zeros((1, A_DH), F32)] * (nbp - nb), axis=0)
        for n in range(nb):
            kdiff_sc[n * nbp:(n + 1) * nbp, :] = km[n:n + 1, :] - km
        kb_sc[:, 0:A_DH] = k_ref[...].astype(BF16)
        vb_sc[:, 0:A_DH] = v_ref[...].astype(BF16)
        if side_blocks:
            gates_ref[...] = jnp.zeros_like(gates_ref)

    far_bias = t5_ref[T5_BUCKETS - 1, h] * LOG2E

    def attend(cc):
        q = q_ref[cc * blk:(cc + 1) * blk, :]
        qs = (q * (scale * LOG2E)).astype(BF16)
        n_keys = (cc + 1) * blk
        if cc <= MOBA_TOPK:
            s_all = lax.dot_general(qs, kb_sc[0:n_keys, 0:A_DH], NT_DIMS, preferred_element_type=F32)
        else:
            diff_t = lax.dot_general(kdiff_sc[0:cc * nbp, :], q, NT_DIMS, precision=lax.Precision.HIGHEST,
                                     preferred_element_type=F32)
            pair = lax.broadcasted_iota(I32, diff_t.shape, 0)
            m_of, n_of = jnp.right_shift(pair, int(math.log2(nbp))), jnp.bitwise_and(pair, nbp - 1)
            beats_t = jnp.where(diff_t > 0, 1.0, jnp.where(diff_t == 0, jnp.where(n_of > m_of, 1.0, 0.0), 0.0))
            fold = jnp.where(jnp.bitwise_and(lax.broadcasted_iota(I32, (cc * nbp, 128), 0), nbp - 1)
                             == lax.broadcasted_iota(I32, (cc * nbp, 128), 1), 1.0, 0.0)
            rank = lax.dot_general(beats_t.astype(BF16), fold.astype(BF16), TN_DIMS,
                                   preferred_element_type=F32)
            lane = lax.broadcasted_iota(I32, rank.shape, 1)
            sel = jnp.where(lane < cc, jnp.where(rank < MOBA_TOPK, 0.0, NEG_INF), 0.0)
            q_sel = jnp.concatenate([qs, sel.astype(BF16)], axis=1)
            s_all = lax.dot_general(q_sel, kb_sc[0:n_keys, :], NT_DIMS, preferred_element_type=F32)
        tiles = []
        for n in range(cc + 1):
            t = s_all[:, n * blk:(n + 1) * blk]
            tiles.append(t + (bias_sc[0] if n == cc else bias_sc[1] if n == cc - 1 else far_bias))
        m = jnp.max(functools.reduce(jnp.maximum, tiles), axis=1, keepdims=True)
        p_all = jnp.concatenate([jnp.exp2(t - m).astype(BF16) for t in tiles], axis=1)
        pv = _dot(p_all, vb_sc[0:n_keys, :])
        o_ref[cc * blk:(cc + 1) * blk, :] = (pv[:, 0:A_DH] / pv[:, A_DH:2 * A_DH]).astype(o_ref.dtype)

    def step(s):
        if side_blocks:
            gates_ref[...] = _block_gates(qside_ref[...], page_refs, s * side_blocks, gates_ref[...])
        for cc in sorted({s, nb - 1 - s}):
            attend(cc)

    for s in range(_moba_steps(nb)):
        pl.when(c == s)(functools.partial(step, s))


def _moba_steps(nb):
    return (nb + 1) // 2


def _moba_prompt(z3, k, v, t5_table, batch, seq, paged=None):
    blk = MOBA_BLOCK
    nb = seq // blk
    steps = _moba_steps(nb)
    side_blocks = 0
    if paged is not None:
        page_table, q_side, cache_k = paged
        nb_past = page_table.shape[1] // PAGES_PER_BLOCK
        if page_table.shape[0] == A_HEADS * batch and nb_past % steps == 0:
            side_blocks = nb_past // steps

    seq_spec = pl.BlockSpec((seq, A_DH), lambda h, b, c, *_: (b, h))
    in_specs = [pl.BlockSpec(memory_space=pltpu.SMEM),
                pl.BlockSpec((None, seq, A_DH), lambda h, b, c, *_: (SEC_QA, b, h)), seq_spec, seq_spec]
    out_specs = [seq_spec]
    out_shape = [jax.ShapeDtypeStruct((batch * seq, A_HEADS * A_DH), BF16)]
    args = [t5_table, z3, k, v]
    scratch = [pltpu.VMEM((seq, 2 * A_DH), BF16), pltpu.VMEM((seq, 2 * A_DH), BF16),
               pltpu.VMEM((SUBLANES * SUBLANES, A_DH), F32), pltpu.VMEM((2, blk, blk), F32)]
    grid = (A_HEADS, batch, steps)
    if side_blocks:
        in_specs += [pl.BlockSpec((None, None, A_HEADS, A_DH), lambda h, b, c, pt: (SEC_QA, h * batch + b, 0, 0)),
                     pl.BlockSpec(memory_space=pl.ANY)]
        out_specs.append(pl.BlockSpec((None, A_HEADS, 128), lambda h, b, c, pt: (h * batch + b, 0, 0)))
        out_shape.append(jax.ShapeDtypeStruct((A_HEADS * batch, A_HEADS, 128), F32))
        args = [page_table] + args + [q_side, cache_k]
        scratch += [pltpu.VMEM((PAGE_RING, side_blocks * PAGES_PER_BLOCK, PAGE_SIZE, A_HEADS, A_DH), F32),
                    pltpu.SemaphoreType.DMA((PAGE_RING,))]
    out = pl.pallas_call(
        functools.partial(_moba_prompt_body, nb=nb, side_blocks=side_blocks, grid=grid),
        grid_spec=pltpu.PrefetchScalarGridSpec(
            num_scalar_prefetch=1 if side_blocks else 0,
            grid=grid,
            in_specs=in_specs,
            out_specs=out_specs,
            scratch_shapes=scratch,
        ),
        out_shape=out_shape,
        compiler_params=_params("arbitrary", "arbitrary", "arbitrary"),
        name="moba_prompt",
    )(*args)
    if paged is None:
        return out[0]
    return out[0], (out[1] if side_blocks else None)


def _rope_body(inv_ref, cos_ref, sin_ref, *, pos0):
    pos = pos0 + lax.broadcasted_iota(I32, cos_ref.shape, 0)
    ang = pos.astype(F32) * inv_ref[...]
    cos_ref[...] = jnp.cos(ang)
    sin_ref[...] = jnp.sin(ang)


def _rope_tables(pos0, n_pos):
    half = R_DK // 2
    inv = 1.0 / (RET_ROPE_BASE ** jnp.linspace(0.0, 1.0, half, dtype=F32))
    rows = -(-n_pos // 8) * 8
    return pl.pallas_call(
        functools.partial(_rope_body, pos0=pos0),
        out_shape=[jax.ShapeDtypeStruct((rows, half), F32)] * 2,
        name="rope_tables",
    )(inv.reshape(1, half))


def _rotate(x, cos, sin):
    half = x.shape[-1] // 2
    x1, x2 = x[:, :half], x[:, half:]
    return jnp.concatenate([x1 * cos - x2 * sin, x1 * sin + x2 * cos], axis=1)


def _log_decay(h, shape):
    hf = jnp.full(shape, h, I32).astype(F32)
    return jnp.log(1.0 - jnp.exp2(-5.0 - hf))


def _ret_prompt_body(q_ref, k_ref, v_ref, g_ref, cos_ref, sin_ref, o_ref, so_ref, s_sc, *, c):
    n = pl.program_id(1)

    @pl.when(n == 0)
    def _():
        s_sc[...] = jnp.zeros_like(s_sc)

    i = lax.broadcasted_iota(I32, (c, c), 0).astype(F32)
    j = lax.broadcasted_iota(I32, (c, c), 1).astype(F32)
    diff = i - j
    i_col = lax.broadcasted_iota(I32, (c, 1), 0).astype(F32)
    cos, sin = cos_ref[...], sin_ref[...]

    new_states, outs = [], []
    for h in range(R_HEADS):
        dk, dv = slice(h * R_DK, (h + 1) * R_DK), slice(h * R_DV, (h + 1) * R_DV)
        dmask = jnp.where(diff >= 0, jnp.exp(jnp.maximum(diff, 0.0) * _log_decay(h, (c, c))), 0.0)
        lg_col = _log_decay(h, (c, 1))
        q_dec = jnp.exp((i_col + 1.0) * lg_col)
        k_dec = jnp.exp((c - 1.0 - i_col) * lg_col)
        c_dec = jnp.exp(c * _log_decay(h, (1, R_DV)))
        qr = _rotate(q_ref[:, dk], cos, sin)
        kr = _rotate(k_ref[:, dk], cos, sin) * (R_DK ** -0.5)
        vb = v_ref[:, dv].astype(BF16)
        s = s_sc[h]
        att = lax.dot_general(qr.astype(BF16), kr.astype(BF16), NT_DIMS, preferred_element_type=F32) * dmask
        o = _dot(att.astype(BF16), vb) + _dot((qr * q_dec).astype(BF16), s.astype(BF16))
        new_states.append(s * c_dec + lax.dot_general((kr * k_dec).astype(BF16), vb, TN_DIMS,
                                                      preferred_element_type=F32))
        on = o * lax.rsqrt(jnp.mean(o * o, axis=-1, keepdims=True) + EPS)
        outs.append((on * _silu(g_ref[:, dv])).astype(o_ref.dtype))
    for h in range(R_HEADS):
        s_sc[h] = new_states[h]
    o_ref[...] = jnp.concatenate(outs, axis=1)

    @pl.when(n == pl.num_programs(1) - 1)
    def _():
        so_ref[...] = s_sc[...]


def _ret_prompt(z3, cos, sin, batch, seq):
    c = math.gcd(seq, RET_CHUNK)
    nc = seq // c

    def sec(s):
        return pl.BlockSpec((None, c, SEC), lambda b, n: (s, b * nc + n, 0))

    tab = pl.BlockSpec((c, R_DK // 2), lambda b, n: (n, 0))
    return pl.pallas_call(
        functools.partial(_ret_prompt_body, c=c),
        grid=(batch, nc),
        in_specs=[sec(SEC_QR), sec(SEC_KR), sec(SEC_VR), sec(SEC_GR), tab, tab],
        out_specs=[pl.BlockSpec((c, R_HEADS * R_DV), lambda b, n: (b * nc + n, 0)),
                   pl.BlockSpec((None, R_HEADS, R_DK, R_DV), lambda b, n: (b, 0, 0, 0))],
        out_shape=[jax.ShapeDtypeStruct((batch * seq, R_HEADS * R_DV), BF16),
                   jax.ShapeDtypeStruct((batch, R_HEADS, R_DK, R_DV), F32)],
        scratch_shapes=[pltpu.VMEM((R_HEADS, R_DK, R_DV), F32)],
        compiler_params=_params("parallel", "arbitrary"),
        name="retention_prompt",
    )(z3, z3, z3, z3, cos, sin)


def _ret_step_body(q_ref, k_ref, v_ref, g_ref, cos_ref, sin_ref, s_ref, o_ref, so_ref):
    cos, sin = cos_ref[0:1, :], sin_ref[0:1, :]
    rows = BF16_SUBLANES
    row0 = lax.broadcasted_iota(I32, (rows, R_DK), 0) == 0
    for h in range(R_HEADS):
        decay = jnp.exp(_log_decay(h, (1, R_DV)))
        qr = _rotate(q_ref[h:h + 1, :], cos, sin)
        kr = _rotate(k_ref[h:h + 1, :], cos, sin) * (R_DK ** -0.5)
        v = v_ref[h:h + 1, :]
        s = s_ref[h]
        q_rows = jnp.broadcast_to(qr * decay, (rows, R_DK)).astype(BF16)
        qs = _dot(q_rows, s.astype(BF16))[0:1, :]
        o = jnp.sum(qr * kr, axis=-1, keepdims=True) * v + qs
        k_rows = jnp.where(row0, jnp.broadcast_to(kr, (rows, R_DK)), 0.0).astype(BF16)
        v_rows = jnp.broadcast_to(v, (rows, R_DV)).astype(BF16)
        so_ref[h] = s * decay + lax.dot_general(k_rows, v_rows, TN_DIMS, preferred_element_type=F32)
        on = o * lax.rsqrt(jnp.mean(o * o, axis=-1, keepdims=True) + EPS)
        o_ref[h:h + 1, :] = (on * _silu(g_ref[h:h + 1, :])).astype(o_ref.dtype)


def _mem_rows(q, k_ref, v_ref):
    outs = []
    for h in range(M_HEADS):
        dh = slice(h * M_DH, (h + 1) * M_DH)
        s = lax.dot_general(q[:, dh].astype(BF16), k_ref[:, dh].astype(BF16), NT_DIMS,
                            preferred_element_type=F32) * (M_DH ** -0.5)
        m = jnp.max(s, axis=-1, keepdims=True)
        p = jnp.exp(s - m)
        l = jnp.sum(p, axis=-1, keepdims=True)
        outs.append(_dot(p.astype(BF16), v_ref[:, dh].astype(BF16)) / l)
    return jnp.concatenate(outs, axis=1)


def _mem_body(q_ref, k_ref, v_ref, o_ref):
    o_ref[...] = _mem_rows(q_ref[...], k_ref, v_ref).astype(o_ref.dtype)


def _mem_attend(z3, mk, mv, batch, seq):
    n_mem = mk.shape[0] // batch
    width = M_HEADS * M_DH
    ts = _tile(seq, 512)
    nt = seq // ts
    kv = pl.BlockSpec((n_mem, width), lambda b, t: (b, 0))
    return pl.pallas_call(
        _mem_body,
        grid=(batch, nt),
        in_specs=[pl.BlockSpec((None, ts, width), lambda b, t: (SEC_QM, b * nt + t, 0)), kv, kv],
        out_specs=pl.BlockSpec((ts, width), lambda b, t: (b * nt + t, 0)),
        out_shape=jax.ShapeDtypeStruct((batch * seq, width), BF16),
        compiler_params=_params("parallel", "arbitrary"),
        name="mem_attend",
    )(z3, mk, mv)


def _merge_body(h_ref, *refs):
    x_refs, wg_refs, bg_refs, wb_refs, o_ref = refs[0:3], refs[3:6], refs[6:9], refs[9:12], refs[12]
    h = h_ref[...]
    merged = None
    for g in range(3):
        term = jax.nn.sigmoid(_dot(h, wg_refs[g][...]) + bg_refs[g][...]) * _dot(x_refs[g][...], wb_refs[g][...])
        merged = term if merged is None else merged + term
    o_ref[...] = merged.astype(o_ref.dtype)


def _merge(h, branches, w_gate, b_gate, branch_ws):
    rows, d = h.shape
    tm, tn = _tile(rows, 1024), _tile(d, 512)
    nj = d // tn
    b_gate = b_gate.reshape(1, 3 * d)

    def row(width):
        return pl.BlockSpec((tm, width), lambda i, j: (i, 0))

    def cols(n_rows, group=0):
        return pl.BlockSpec((n_rows, tn), lambda i, j: (0, group * nj + j))

    return pl.pallas_call(
        _merge_body,
        grid=(rows // tm, nj),
        in_specs=([row(d)] + [row(x.shape[1]) for x in branches]
                  + [cols(d, g) for g in range(3)] + [cols(1, g) for g in range(3)]
                  + [cols(w.shape[0]) for w in branch_ws]),
        out_specs=pl.BlockSpec((tm, tn), lambda i, j: (i, j)),
        out_shape=jax.ShapeDtypeStruct((rows, d), BF16),
        compiler_params=_params("parallel", "arbitrary"),
        name="gated_merge",
    )(h, *branches, w_gate, w_gate, w_gate, b_gate, b_gate, b_gate, *branch_ws)


def _moba_gate_body(pt_ref, q_ref, *refs, bps):
    del pt_ref
    k_refs, gates_ref = refs[:-1], refs[-1]
    n = pl.program_id(1)

    @pl.when(n == 0)
    def _():
        gates_ref[...] = jnp.zeros_like(gates_ref)

    gates_ref[...] = _block_gates(q_ref[...], k_refs, n * bps, gates_ref[...])


def _moba_select_body(g_ref, sel_ref, *, nb_past):
    lane = lax.broadcasted_iota(I32, g_ref.shape, 1)
    gw = jnp.where(lane < nb_past, g_ref[...], NEG_INF)
    sel = jnp.zeros(gw.shape, I32)
    for r in range(MOBA_TOPK):
        mx = jnp.max(gw, axis=1, keepdims=True)
        idx = jnp.min(jnp.where(gw == mx, lane, 128), axis=1, keepdims=True)
        sel = jnp.where(lane == r, idx, sel)
        gw = jnp.where(lane == idx, -jnp.inf, gw)
    sel_ref[...] = sel


def _moba_attend_body(pt_ref, sel_ref, t5_ref, q_ref, kn_ref, vn_ref, ck_ref, cv_ref, o_ref, kbuf, vbuf, sem,
                      *, past_len):
    b = pl.program_id(0)
    slot = lax.rem(b, 2)
    scale = A_DH ** -0.5
    n_tiles = MOBA_TOPK * PAGES_PER_BLOCK

    def tile_copies(bb, to_slot):
        copies = []
        for h in range(A_HEADS):
            for r in range(MOBA_TOPK):
                block = sel_ref[bb, h * MOBA_TOPK + r]
                for t in range(PAGES_PER_BLOCK):
                    page = pt_ref[bb, block * PAGES_PER_BLOCK + t]
                    j = r * PAGES_PER_BLOCK + t
                    copies.append(pltpu.make_async_copy(ck_ref.at[page, :, h, :], kbuf.at[to_slot, h, j],
                                                        sem.at[to_slot]))
                    copies.append(pltpu.make_async_copy(cv_ref.at[page, :, h, :], vbuf.at[to_slot, h, j],
                                                        sem.at[to_slot]))
        return copies

    def start_all(copies):
        for i, cp in enumerate(copies):
            cp.start(priority=i % 2)

    @pl.when(b == 0)
    def _():
        start_all(tile_copies(0, 0))

    @pl.when(b + 1 < pl.num_programs(0))
    def _():
        start_all(tile_copies(b + 1, 1 - slot))

    for cp in tile_copies(b, slot):
        cp.wait()

    q, kn, vn = q_ref[...], kn_ref[...], vn_ref[...]
    outs = []
    assert n_tiles <= SUBLANES
    lane = lax.broadcasted_iota(I32, (SUBLANES, PAGE_SIZE), 1)
    row = lax.broadcasted_iota(I32, (SUBLANES, PAGE_SIZE), 0)
    for h in range(A_HEADS):
        qh = q[h:h + 1, :]
        q8 = jnp.broadcast_to(qh, (8, A_DH)).astype(BF16)
        first_key = jnp.zeros((SUBLANES, PAGE_SIZE), I32)
        for r in range(MOBA_TOPK):
            block = sel_ref[b, h * MOBA_TOPK + r]
            for t in range(PAGES_PER_BLOCK):
                first_key = jnp.where(row == r * PAGES_PER_BLOCK + t, block * MOBA_BLOCK + t * PAGE_SIZE, first_key)
        bucket = _t5_bucket(past_len - (first_key + lane))
        bias = jnp.zeros((SUBLANES, PAGE_SIZE), F32)
        for bkt in range(T5_BUCKETS):
            bias = jnp.where(bucket == bkt, t5_ref[bkt, h], bias)
        s = []
        for j in range(n_tiles):
            kt = kbuf[slot, h, j].astype(BF16)
            raw = lax.dot_general(q8, kt, NT_DIMS, preferred_element_type=F32)[0:1, :]
            s.append(raw * scale + bias[j:j + 1, :])
        s_own = jnp.sum(qh * kn[h:h + 1, :], axis=1, keepdims=True) * scale + t5_ref[0, h]
        m = jnp.maximum(jnp.max(functools.reduce(jnp.maximum, s), axis=1, keepdims=True), s_own)
        p = [jnp.exp(x - m) for x in s]
        p_own = jnp.exp(s_own - m)
        l = jnp.sum(functools.reduce(jnp.add, p), axis=1, keepdims=True) + p_own
        acc = p_own * vn[h:h + 1, :]
        for j in range(n_tiles):
            p8 = jnp.broadcast_to(p[j], (8, PAGE_SIZE)).astype(BF16)
            acc = acc + _dot(p8, vbuf[slot, h, j].astype(BF16))[0:1, :]
        outs.append(acc / l)
    o_ref[...] = jnp.concatenate(outs, axis=0)


def _moba_gates(q4, cache_k, page_table):
    batch, n_pages = page_table.shape
    nb_past = n_pages // PAGES_PER_BLOCK
    bps = _tile(nb_past, 8)

    def page(t):
        return pl.BlockSpec((None, PAGE_SIZE, A_HEADS, A_DH),
                            lambda b, n, pt: (pt[b, n * bps * PAGES_PER_BLOCK + t], 0, 0, 0))

    pages = [page(t) for t in range(bps * PAGES_PER_BLOCK)]
    return pl.pallas_call(
        functools.partial(_moba_gate_body, bps=bps),
        grid_spec=pltpu.PrefetchScalarGridSpec(
            num_scalar_prefetch=1,
            grid=(batch, nb_past // bps),
            in_specs=[pl.BlockSpec((None, None, A_HEADS, A_DH), lambda b, n, pt: (SEC_QA, b, 0, 0))] + pages,
            out_specs=pl.BlockSpec((None, A_HEADS, 128), lambda b, n, pt: (b, 0, 0)),
        ),
        out_shape=jax.ShapeDtypeStruct((batch, A_HEADS, 128), F32),
        compiler_params=_params("parallel", "arbitrary"),
        name="moba_gate",
    )(page_table, q4, *([cache_k] * len(pages)))


def _sample_mixers_body(pt_ref, sel_ref, t5_ref, qa_ref, kn_ref, vn_ref, ck_ref, cv_ref,
                        qr_ref, kr_ref, vr_ref, gr_ref, cos_ref, sin_ref, state_ref, qm_ref, mk_ref, mv_ref,
                        oa_ref, or_ref, state_out_ref, om_ref, kbuf, vbuf, sem, *, past_len):
    _moba_attend_body(pt_ref, sel_ref, t5_ref, qa_ref, kn_ref, vn_ref, ck_ref, cv_ref, oa_ref, kbuf, vbuf, sem,
                      past_len=past_len)
    _ret_step_body(qr_ref, kr_ref, vr_ref, gr_ref, cos_ref, sin_ref, state_ref, or_ref, state_out_ref)
    q_rows = jnp.broadcast_to(qm_ref[...], (BF16_SUBLANES, qm_ref.shape[-1]))
    om_ref[...] = _mem_rows(q_rows, mk_ref, mv_ref)[0:1, :]


def _sample_mixers(z3, k_new, v_new, cache_k, cache_v, page_table, t5_table, gates, cos, sin, state, mem_k, mem_v):
    batch, n_pages = page_table.shape
    past_len = n_pages * PAGE_SIZE
    assert past_len % MOBA_BLOCK == 0 and MOBA_BLOCK % PAGE_SIZE == 0
    nb_past = past_len // MOBA_BLOCK
    assert MOBA_TOPK <= nb_past <= 128
    n_sec = z3.shape[0]
    n_mem = mem_k.shape[0] // batch
    z4a = z3.reshape(n_sec, batch, A_HEADS, A_DH)
    z4r = z3.reshape(n_sec, batch, R_HEADS, R_DK)
    z4m = z3.reshape(n_sec, batch, 1, M_HEADS * M_DH)
    k_new = k_new.reshape(batch, A_HEADS, A_DH)
    v_new = v_new.reshape(batch, A_HEADS, A_DH)
    if gates is None:
        gates = _moba_gates(z4a, cache_k, page_table)
    sel = pl.pallas_call(
        functools.partial(_moba_select_body, nb_past=nb_past),
        out_shape=jax.ShapeDtypeStruct((batch * A_HEADS, 128), I32),
        name="moba_select",
    )(gates.reshape(batch * A_HEADS, 128))
    sel = sel[:, :MOBA_TOPK].reshape(batch, A_HEADS * MOBA_TOPK)

    def sec(s, rows, width):
        return pl.BlockSpec((None, None, rows, width), lambda b, pt, sl: (s, b, 0, 0))

    def per_seq(*shape):
        return pl.BlockSpec((None,) + shape, lambda b, pt, sl: (b,) + (0,) * len(shape))

    tab = pl.BlockSpec((SUBLANES, R_DK // 2), lambda b, pt, sl: (0, 0))
    mem = pl.BlockSpec((n_mem, M_HEADS * M_DH), lambda b, pt, sl: (b, 0))
    tiles = pltpu.VMEM((2, A_HEADS, MOBA_TOPK * PAGES_PER_BLOCK, PAGE_SIZE, A_DH), F32)
    return pl.pallas_call(
        functools.partial(_sample_mixers_body, past_len=past_len),
        grid_spec=pltpu.PrefetchScalarGridSpec(
            num_scalar_prefetch=2,
            grid=(batch,),
            in_specs=[pl.BlockSpec(memory_space=pltpu.SMEM), sec(SEC_QA, A_HEADS, A_DH),
                      per_seq(A_HEADS, A_DH), per_seq(A_HEADS, A_DH),
                      pl.BlockSpec(memory_space=pl.ANY), pl.BlockSpec(memory_space=pl.ANY),
                      sec(SEC_QR, R_HEADS, R_DK), sec(SEC_KR, R_HEADS, R_DK), sec(SEC_VR, R_HEADS, R_DK),
                      sec(SEC_GR, R_HEADS, R_DK), tab, tab, per_seq(R_HEADS, R_DK, R_DV),
                      sec(SEC_QM, 1, M_HEADS * M_DH), mem, mem],
            out_specs=[per_seq(A_HEADS, A_DH), per_seq(R_HEADS, R_DV), per_seq(R_HEADS, R_DK, R_DV),
                       per_seq(1, M_HEADS * M_DH)],
            scratch_shapes=[tiles, tiles, pltpu.SemaphoreType.DMA((2,))],
        ),
        out_shape=[jax.ShapeDtypeStruct((batch, A_HEADS, A_DH), F32),
                   jax.ShapeDtypeStruct((batch, R_HEADS, R_DV), F32),
                   jax.ShapeDtypeStruct(state.shape, F32),
                   jax.ShapeDtypeStruct((batch, 1, M_HEADS * M_DH), F32)],
        compiler_params=_params("arbitrary"),
        name="sample_mixers",
    )(page_table, sel, t5_table, z4a, k_new, v_new, cache_k, cache_v, z4r, z4r, z4r, z4r, cos, sin, state,
      z4m, mem_k, mem_v)


def kernel(x_prompt, x_sample, mem_prompt, cache_k, cache_v, cache_mem_k, cache_mem_v, state_ret, page_table, t5_table, ffn1_norm, ffn1_w1, ffn1_w3, ffn1_w2, mix_norm, mem_norm, w_in, w_mem_kv, w_gate, b_gate, w_br_moba, w_br_ret, w_br_mem, w_out, ffn2_norm, ffn2_w1, ffn2_w3, ffn2_w2, final_norm):
    batch, seq, d = x_prompt.shape
    dec_batch, dec_seq, _ = x_sample.shape
    assert dec_seq == 1
    depth = w_in.shape[0]
    n_mem = mem_prompt.shape[1]
    past_len = page_table.shape[1] * PAGE_SIZE
    cos_p, sin_p = _rope_tables(0, seq)
    cos_s, sin_s = _rope_tables(past_len, dec_seq)

    xp = x_prompt.reshape(batch * seq, d)
    xs = x_sample.reshape(dec_batch, d)
    outs = [[] for _ in range(8)]
    for l in range(depth):
        last = l == depth - 1
        g_next = final_norm if last else ffn1_norm[l + 1]

        x1s, h2s, *f1_b = _ffn_half(xs, ffn1_norm[l], ffn1_w1[l], ffn1_w3[l], ffn1_w2[l], mix_norm[l],
                                    emit_x=True, post_dtype=BF16, emit_w=True)
        x1, h2, w_in_b, *f2_b = _ffn_half(xp, ffn1_norm[l], *f1_b, mix_norm[l], emit_x=True, post_dtype=BF16,
                                          cast_jobs=[w_in[l], ffn2_w1[l], ffn2_w3[l], ffn2_w2[l]])
        z3s, ka_s, va_s = _in_proj(h2s, w_in_b)
        z3, ka, va, w_gate_b, w_out_b, *branch_b = _in_proj(
            h2, w_in_b, cast_jobs=[w_gate[l], w_out[l], w_br_moba[l], w_br_ret[l], w_br_mem[l]])
        mem_k_s, mem_v_s, x1s = lax.optimization_barrier((cache_mem_k[l], cache_mem_v[l], x1s))

        z4s = z3s.reshape(z3s.shape[0], dec_batch, A_HEADS, A_DH)
        oa, gates_s = _moba_prompt(z3, ka, va, t5_table, batch, seq, paged=(page_table, z4s, cache_k[l]))

        oa_s, o_rs, s_new_s, om_s = _sample_mixers(
            z3s, ka_s, va_s, cache_k[l], cache_v[l], page_table, t5_table, gates_s, cos_s, sin_s, state_ret[l],
            mem_k_s.reshape(dec_batch * n_mem, M_HEADS * M_DH), mem_v_s.reshape(dec_batch * n_mem, M_HEADS * M_DH))
        branches_s = [o.reshape(dec_batch, -1).astype(BF16) for o in (oa_s, o_rs, om_s)]

        mkv_p = _norm_matmul(mem_prompt.reshape(batch * n_mem, d), mem_norm[l], w_mem_kv[l])
        mk_p, mv_p = mkv_p[:, :M_HEADS * M_DH], mkv_p[:, M_HEADS * M_DH:]
        o_r, s_new_p = _ret_prompt(z3, cos_p, sin_p, batch, seq)
        om = _mem_attend(z3, mk_p, mv_p, batch, seq)

        merged_s = _merge(h2s, branches_s, w_gate_b, b_gate[l], branch_b)
        merged = _merge(h2, [oa, o_r, om], w_gate_b, b_gate[l], branch_b)
        x2s = _matmul(merged_s, w_out_b, residual=x1s, name="out_proj")
        x2 = _matmul(merged, w_out_b, residual=x1, name="out_proj")
        xs = _ffn_half(x2s, ffn2_norm[l], *f2_b, g_next, emit_x=not last, post_dtype=F32)[0]
        xp = _ffn_half(x2, ffn2_norm[l], *f2_b, g_next, emit_x=not last, post_dtype=F32)[0]

        new = (ka.reshape(batch, seq, A_HEADS, A_DH), va.reshape(batch, seq, A_HEADS, A_DH),
               mk_p.reshape(batch, n_mem, M_HEADS, M_DH), mv_p.reshape(batch, n_mem, M_HEADS, M_DH), s_new_p,
               ka_s.reshape(dec_batch, dec_seq, A_HEADS, A_DH), va_s.reshape(dec_batch, dec_seq, A_HEADS, A_DH),
               s_new_s)
        for acc, val in zip(outs, new):
            acc.append(val)

    y_prompt = xp.reshape(batch, seq, d)
    y_sample = xs.reshape(dec_batch, dec_seq, d)
    return (y_prompt, y_sample) + tuple(jnp.stack(o) for o in outs)
```

```python
import functools
import math

import jax
import jax.numpy as jnp
import numpy as np
from jax import lax
from jax.experimental import pallas as pl
from jax.experimental.pallas import tpu as pltpu

F32 = jnp.float32
BF16 = jnp.bfloat16
I32 = jnp.int32

A_HEADS, A_DH = 8, 128
MOBA_BLOCK, MOBA_TOPK = 256, 3
T5_BUCKETS, T5_MAX_DIST = 32, 128
R_HEADS, R_DK, R_DV = 4, 256, 256
RET_CHUNK = 128
RET_ROPE_BASE = 10000.0
M_HEADS, M_DH = 4, 256
PAGE_SIZE = 128
EPS = 1e-6
NEG_INF = -1e30
LOG2E = math.log2(math.e)
SEC = 1024
W_IN_KA, W_IN_VA = 1, 2
(SEC_QA, SEC_QR, SEC_KR, SEC_VR, SEC_GR, SEC_QM) = range(6)

V7X_VMEM_LIMIT_BYTES = 56 * 1024 * 1024

NT_DIMS = (((1,), (1,)), ((), ()))
TN_DIMS = (((0,), (0,)), ((), ()))


def _params(*sem):
    return pltpu.CompilerParams(dimension_semantics=sem, vmem_limit_bytes=V7X_VMEM_LIMIT_BYTES)


def _tile(n, pref):
    t = min(n, pref)
    while n % t:
        t -= 1
    return t


def _rms(x, g):
    return x * lax.rsqrt(jnp.mean(x * x, axis=-1, keepdims=True) + EPS) * g


def _silu(x):
    return x * jax.nn.sigmoid(x)


def _dot(a, b):
    return jnp.dot(a, b, preferred_element_type=F32)


def _bf16(w):
    return w if w.dtype == BF16 else w.astype(BF16)


def _ffn_body(x_ref, g_ref, w1_ref, w3_ref, w2_ref, gp_ref, *refs, emit_x, emit_w, n_cast):
    refs = list(refs)
    cast_in = [refs.pop(0) for _ in range(n_cast)]
    xo_ref = refs.pop(0) if emit_x else None
    ho_ref = refs.pop(0)
    wb_refs = [refs.pop(0) for _ in range(3)] if emit_w else []
    cast_out = [refs.pop(0) for _ in range(n_cast)]
    (h_sc,) = refs
    acc_ref = xo_ref if emit_x else ho_ref
    f = pl.program_id(1)

    @pl.when(f == 0)
    def _():
        x = x_ref[...]
        h_sc[...] = _rms(x, g_ref[...]).astype(BF16)
        acc_ref[...] = x

    _cast_rows(cast_in, cast_out)

    w1, w3, w2 = _bf16(w1_ref[...]), _bf16(w3_ref[...]), _bf16(w2_ref[...])
    for wb_ref, w in zip(wb_refs, (w1, w3, w2)):
        wb_ref[...] = w
    h = h_sc[...]
    t = _silu(_dot(h, w1)) * _dot(h, w3) * 0.5
    acc_ref[...] += _dot(t.astype(BF16), w2)

    @pl.when(f == pl.num_programs(1) - 1)
    def _():
        ho_ref[...] = _rms(acc_ref[...], gp_ref[...]).astype(ho_ref.dtype)


BF16_SUBLANES = 16


def _cast_spec(a, grid):
    n_steps = grid[0] * grid[1]
    n_chunks = max(n for n in range(1, n_steps + 1)
                   if a.shape[0] % n == 0 and (a.shape[0] // n) % BF16_SUBLANES == 0)
    return pl.BlockSpec((a.shape[0] // n_chunks, a.shape[1]),
                        lambda i, j: (jnp.minimum(i * grid[1] + j, n_chunks - 1), 0))


def _cast_rows(src_refs, dst_refs):
    for src_ref, dst_ref in zip(src_refs, dst_refs):
        dst_ref[...] = src_ref[...].astype(BF16)


def _ffn_half(x, g, w1, w3, w2, g_post, *, emit_x, post_dtype, emit_w=False, cast_jobs=()):
    rows, d = x.shape
    ff = w1.shape[1]
    tm = _tile(rows, 512 if (emit_x or cast_jobs) else 1024)
    tf = _tile(ff, 512)
    assert not emit_w or rows == tm
    assert emit_x or post_dtype == F32
    grid = (rows // tm, ff // tf)
    row_spec = pl.BlockSpec((tm, d), lambda i, f: (i, 0))
    vec_spec = pl.BlockSpec((1, d), lambda i, f: (0, 0))
    w13_spec = pl.BlockSpec((d, tf), lambda i, f: (0, f))
    w2_spec = pl.BlockSpec((tf, d), lambda i, f: (f, 0))
    out_shape = [jax.ShapeDtypeStruct((rows, d), post_dtype)]
    out_specs = [row_spec]
    if emit_x:
        out_shape = [jax.ShapeDtypeStruct((rows, d), F32)] + out_shape
        out_specs = [row_spec] + out_specs
    if emit_w:
        out_shape += [jax.ShapeDtypeStruct(w.shape, BF16) for w in (w1, w3, w2)]
        out_specs += [w13_spec, w13_spec, w2_spec]

    cast_specs = [_cast_spec(a, grid) for a in cast_jobs]
    out_shape += [jax.ShapeDtypeStruct(a.shape, BF16) for a in cast_jobs]
    out_specs += cast_specs
    return pl.pallas_call(
        functools.partial(_ffn_body, emit_x=emit_x, emit_w=emit_w, n_cast=len(cast_jobs)),
        grid=grid,
        in_specs=[row_spec, vec_spec, w13_spec, w13_spec, w2_spec, vec_spec] + cast_specs,
        out_specs=out_specs,
        out_shape=out_shape,
        scratch_shapes=[pltpu.VMEM((tm, d), BF16)],
        compiler_params=_params("parallel", "arbitrary"),
        name="ffn_half",
    )(x, g.reshape(1, d), w1, w3, w2, g_post.reshape(1, d), *cast_jobs)


def _norm_mm_body(x_ref, g_ref, w_ref, o_ref, h_sc):
    @pl.when(pl.program_id(1) == 0)
    def _():
        h_sc[...] = _rms(x_ref[...], g_ref[...]).astype(BF16)

    o_ref[...] = _dot(h_sc[...], _bf16(w_ref[...]))


def _norm_matmul(x, g, w):
    rows, d = x.shape
    n_cols = w.shape[1]
    tm, tn = _tile(rows, 1024), _tile(n_cols, 512)
    return pl.pallas_call(
        _norm_mm_body,
        grid=(rows // tm, n_cols // tn),
        in_specs=[pl.BlockSpec((tm, d), lambda i, j: (i, 0)), pl.BlockSpec((1, d), lambda i, j: (0, 0)),
                  pl.BlockSpec((d, tn), lambda i, j: (0, j))],
        out_specs=pl.BlockSpec((tm, tn), lambda i, j: (i, j)),
        out_shape=jax.ShapeDtypeStruct((rows, n_cols), F32),
        scratch_shapes=[pltpu.VMEM((tm, d), BF16)],
        compiler_params=_params("parallel", "arbitrary"),
        name="norm_matmul",
    )(x, g.reshape(1, d), w)


def _mm_body(a_ref, w_ref, *refs, has_residual):
    o_ref = refs[-1]
    acc = _dot(a_ref[...], _bf16(w_ref[...]))
    if has_residual:
        acc = refs[0][...] + acc
    o_ref[...] = acc.astype(o_ref.dtype)


def _in_proj_body(a_ref, w_ref, *refs):
    n_cast = (len(refs) - 3) // 2
    z_ref, k_ref, v_ref = refs[n_cast:n_cast + 3]
    _cast_rows(refs[:n_cast], refs[n_cast + 3:])
    j = pl.program_id(1)
    acc = _dot(a_ref[...], w_ref[...])

    @pl.when(j == W_IN_KA)
    def _():
        k_ref[...] = acc

    @pl.when(j == W_IN_VA)
    def _():
        v_ref[...] = acc

    @pl.when((j != W_IN_KA) & (j != W_IN_VA))
    def _():
        z_ref[...] = acc


def _in_proj(a, w, cast_jobs=()):
    rows, k = a.shape
    n_sec = w.shape[1] // SEC
    assert (W_IN_KA, W_IN_VA) == (1, 2) and n_sec == 8
    tm = _tile(rows, 1024)
    grid = (rows // tm, n_sec)
    kv_spec = pl.BlockSpec((tm, SEC), lambda i, j: (i, 0))
    cast_specs = [_cast_spec(c, grid) for c in cast_jobs]
    return pl.pallas_call(
        _in_proj_body,
        grid=grid,
        in_specs=[pl.BlockSpec((tm, k), lambda i, j: (i, 0)), pl.BlockSpec((k, SEC), lambda i, j: (0, j))]
        + cast_specs,
        out_specs=[pl.BlockSpec((None, tm, SEC), lambda i, j: (jnp.maximum(j - 2, 0), i, 0)), kv_spec, kv_spec]
        + cast_specs,
        out_shape=[jax.ShapeDtypeStruct((n_sec - 2, rows, SEC), F32)] + [jax.ShapeDtypeStruct((rows, SEC), F32)] * 2
        + [jax.ShapeDtypeStruct(c.shape, BF16) for c in cast_jobs],
        compiler_params=_params("parallel", "arbitrary"),
        name="in_proj",
    )(a, w, *cast_jobs)


def _matmul(a, w, *, residual=None, out_dtype=F32, name="matmul"):
    rows, k = a.shape
    n_cols = w.shape[1]
    tm, tn = _tile(rows, 1024), _tile(n_cols, 1024)
    in_specs = [pl.BlockSpec((tm, k), lambda i, j: (i, 0)), pl.BlockSpec((k, tn), lambda i, j: (0, j))]
    out_spec = pl.BlockSpec((tm, tn), lambda i, j: (i, j))
    args = [a, w]
    if residual is not None:
        in_specs.append(out_spec)
        args.append(residual)
    return pl.pallas_call(
        functools.partial(_mm_body, has_residual=residual is not None),
        grid=(rows // tm, n_cols // tn),
        in_specs=in_specs,
        out_specs=out_spec,
        out_shape=jax.ShapeDtypeStruct((rows, n_cols), out_dtype),
        compiler_params=_params("parallel", "arbitrary"),
        name=name,
    )(*args)


def _t5_bucket(n):
    n = jnp.maximum(n, 0)
    max_exact = T5_BUCKETS // 2
    nf = jnp.maximum(n, 1).astype(F32)
    large = max_exact + (jnp.log(nf / max_exact) / math.log(T5_MAX_DIST / max_exact)
                         * (T5_BUCKETS - max_exact)).astype(I32)
    return jnp.where(n < max_exact, n, jnp.minimum(large, T5_BUCKETS - 1))


PAGES_PER_BLOCK = MOBA_BLOCK // PAGE_SIZE
SUBLANES = 8


def _block_gates(q, page_refs, first_block, gates):
    lane = lax.broadcasted_iota(I32, gates.shape, 1)
    for i in range(len(page_refs) // PAGES_PER_BLOCK):
        pages = [page_refs[i * PAGES_PER_BLOCK + t][...] for t in range(PAGES_PER_BLOCK)]
        ksum = functools.reduce(jnp.add, [jnp.sum(x, axis=0) for x in pages])
        gate = jnp.sum(q * ksum, axis=1, keepdims=True) / MOBA_BLOCK
        gates = jnp.where(lane == first_block + i, gate, gates)
    return gates


PAGE_RING = 3


def _moba_prompt_body(*refs, nb, side_blocks, grid):
    h, b, c = pl.program_id(0), pl.program_id(1), pl.program_id(2)
    if side_blocks:
        (pt_ref, t5_ref, q_ref, k_ref, v_ref, qside_ref, cache_ref, o_ref, gates_ref,
         kb_sc, vb_sc, kdiff_sc, bias_sc, pages_sc, page_sem) = refs
        n_pages = side_blocks * PAGES_PER_BLOCK
        steps_per_seq = grid[2]
        n_steps = grid[0] * grid[1] * grid[2]
        step_idx = (h * grid[1] + b) * steps_per_seq + c

        def page_copies(s):
            seq, part, slot = s // steps_per_seq, s % steps_per_seq, s % PAGE_RING
            return [pltpu.make_async_copy(cache_ref.at[pt_ref[seq, part * n_pages + t]], pages_sc.at[slot, t],
                                          page_sem.at[slot]) for t in range(n_pages)]

        @pl.when(step_idx == 0)
        def _():
            for s in range(min(PAGE_RING - 1, n_steps)):
                for cp in page_copies(s):
                    cp.start()

        @pl.when(step_idx + PAGE_RING - 1 < n_steps)
        def _():
            for cp in page_copies(step_idx + PAGE_RING - 1):
                cp.start()

        for cp in page_copies(step_idx):
            cp.wait()
        slot = step_idx % PAGE_RING
        page_refs = [pages_sc.at[slot, t] for t in range(n_pages)]
    else:
        t5_ref, q_ref, k_ref, v_ref, o_ref, kb_sc, vb_sc, kdiff_sc, bias_sc = refs
    blk = MOBA_BLOCK
    nbp = SUBLANES
    assert nb <= nbp
    scale = A_DH ** -0.5
    ii = lax.broadcasted_iota(I32, (blk, blk), 0)
    jj = lax.broadcasted_iota(I32, (blk, blk), 1)

    @pl.when((b == 0) & (c == 0))
    def _():
        for t in range(2):
            bucket = _t5_bucket(t * blk + ii - jj)
            tile = jnp.zeros((blk, blk), F32)
            for bkt in range(T5_BUCKETS):
                tile = jnp.where(bucket == bkt, t5_ref[bkt, h], tile)
            tile = tile * LOG2E
            bias_sc[t] = jnp.where(ii >= jj, tile, NEG_INF) if t == 0 else tile
        for n in range(nb):
            kb_sc[n * blk:(n + 1) * blk, A_DH:2 * A_DH] = jnp.where(
                lax.broadcasted_iota(I32, (blk, A_DH), 1) == n, 1.0, 0.0).astype(BF16)
        vb_sc[:, A_DH:2 * A_DH] = jnp.ones((nb * blk, A_DH), BF16)

    @pl.when(c == 0)
    def _():
        km = jnp.concatenate(
            [jnp.mean(k_ref[n * blk:(n + 1) * blk, :], axis=0, keepdims=True) for n in range(nb)]
            + [jnp.zeros((1, A_DH), F32)] * (nbp - nb), axis=0)
        for n in range(nb):
            kdiff_sc[n * nbp:(n + 1) * nbp, :] = km[n:n + 1, :] - km
        kb_sc[:, 0:A_DH] = k_ref[...].astype(BF16)
        vb_sc[:, 0:A_DH] = v_ref[...].astype(BF16)
        if side_blocks:
            gates_ref[...] = jnp.zeros_like(gates_ref)

    far_bias = t5_ref[T5_BUCKETS - 1, h] * LOG2E

    def attend(cc):
        q = q_ref[cc * blk:(cc + 1) * blk, :]
        qs = (q * (scale * LOG2E)).astype(BF16)
        n_keys = (cc + 1) * blk
        if cc <= MOBA_TOPK:
            s_all = lax.dot_general(qs, kb_sc[0:n_keys, 0:A_DH], NT_DIMS, preferred_element_type=F32)
        else:
            diff_t = lax.dot_general(kdiff_sc[0:cc * nbp, :], q, NT_DIMS, precision=lax.Precision.HIGHEST,
                                     preferred_element_type=F32)
            pair = lax.broadcasted_iota(I32, diff_t.shape, 0)
            m_of, n_of = jnp.right_shift(pair, int(math.log2(nbp))), jnp.bitwise_and(pair, nbp - 1)
            beats_t = jnp.where(diff_t > 0, 1.0, jnp.where(diff_t == 0, jnp.where(n_of > m_of, 1.0, 0.0), 0.0))
            fold = jnp.where(jnp.bitwise_and(lax.broadcasted_iota(I32, (cc * nbp, 128), 0), nbp - 1)
                             == lax.broadcasted_iota(I32, (cc * nbp, 128), 1), 1.0, 0.0)
            rank = lax.dot_general(beats_t.astype(BF16), fold.astype(BF16), TN_DIMS,
                                   preferred_element_type=F32)
            lane = lax.broadcasted_iota(I32, rank.shape, 1)
            sel = jnp.where(lane < cc, jnp.where(rank < MOBA_TOPK, 0.0, NEG_INF), 0.0)
            q_sel = jnp.concatenate([qs, sel.astype(BF16)], axis=1)
            s_all = lax.dot_general(q_sel, kb_sc[0:n_keys, :], NT_DIMS, preferred_element_type=F32)
        tiles = []
        for n in range(cc + 1):
            t = s_all[:, n * blk:(n + 1) * blk]
            tiles.append(t + (bias_sc[0] if n == cc else bias_sc[1] if n == cc - 1 else far_bias))
        m = jnp.max(functools.reduce(jnp.maximum, tiles), axis=1, keepdims=True)
        p_all = jnp.concatenate([jnp.exp2(t - m).astype(BF16) for t in tiles], axis=1)
        pv = _dot(p_all, vb_sc[0:n_keys, :])
        o_ref[cc * blk:(cc + 1) * blk, :] = (pv[:, 0:A_DH] / pv[:, A_DH:2 * A_DH]).astype(o_ref.dtype)

    def step(s):
        if side_blocks:
            gates_ref[...] = _block_gates(qside_ref[...], page_refs, s * side_blocks, gates_ref[...])
        for cc in sorted({s, nb - 1 - s}):
            attend(cc)

    for s in range(_moba_steps(nb)):
        pl.when(c == s)(functools.partial(step, s))


def _moba_steps(nb):
    return (nb + 1) // 2


def _moba_prompt(z3, k, v, t5_table, batch, seq, paged=None):
    blk = MOBA_BLOCK
    nb = seq // blk
    steps = _moba_steps(nb)
    side_blocks = 0
    if paged is not None:
        page_table, q_side, cache_k = paged
        nb_past = page_table.shape[1] // PAGES_PER_BLOCK
        if page_table.shape[0] == A_HEADS * batch and nb_past % steps == 0:
            side_blocks = nb_past // steps

    seq_spec = pl.BlockSpec((seq, A_DH), lambda h, b, c, *_: (b, h))
    in_specs = [pl.BlockSpec(memory_space=pltpu.SMEM),
                pl.BlockSpec((None, seq, A_DH), lambda h, b, c, *_: (SEC_QA, b, h)), seq_spec, seq_spec]
    out_specs = [seq_spec]
    out_shape = [jax.ShapeDtypeStruct((batch * seq, A_HEADS * A_DH), BF16)]
    args = [t5_table, z3, k, v]
    scratch = [pltpu.VMEM((seq, 2 * A_DH), BF16), pltpu.VMEM((seq, 2 * A_DH), BF16),
               pltpu.VMEM((SUBLANES * SUBLANES, A_DH), F32), pltpu.VMEM((2, blk, blk), F32)]
    grid = (A_HEADS, batch, steps)
    if side_blocks:
        in_specs += [pl.BlockSpec((None, None, A_HEADS, A_DH), lambda h, b, c, pt: (SEC_QA, h * batch + b, 0, 0)),
                     pl.BlockSpec(memory_space=pl.ANY)]
        out_specs.append(pl.BlockSpec((None, A_HEADS, 128), lambda h, b, c, pt: (h * batch + b, 0, 0)))
        out_shape.append(jax.ShapeDtypeStruct((A_HEADS * batch, A_HEADS, 128), F32))
        args = [page_table] + args + [q_side, cache_k]
        scratch += [pltpu.VMEM((PAGE_RING, side_blocks * PAGES_PER_BLOCK, PAGE_SIZE, A_HEADS, A_DH), F32),
                    pltpu.SemaphoreType.DMA((PAGE_RING,))]
    out = pl.pallas_call(
        functools.partial(_moba_prompt_body, nb=nb, side_blocks=side_blocks, grid=grid),
        grid_spec=pltpu.PrefetchScalarGridSpec(
            num_scalar_prefetch=1 if side_blocks else 0,
            grid=grid,
            in_specs=in_specs,
            out_specs=out_specs,
            scratch_shapes=scratch,
        ),
        out_shape=out_shape,
        compiler_params=_params("arbitrary", "arbitrary", "arbitrary"),
        name="moba_prompt",
    )(*args)
    if paged is None:
        return out[0]
    return out[0], (out[1] if side_blocks else None)


def _rope_body(inv_ref, cos_ref, sin_ref, *, pos0):
    pos = pos0 + lax.broadcasted_iota(I32, cos_ref.shape, 0)
    ang = pos.astype(F32) * inv_ref[...]
    cos_ref[...] = jnp.cos(ang)
    sin_ref[...] = jnp.sin(ang)


def _rope_tables(pos0, n_pos):
    half = R_DK // 2
    inv = 1.0 / (RET_ROPE_BASE ** jnp.linspace(0.0, 1.0, half, dtype=F32))
    rows = -(-n_pos // 8) * 8
    return pl.pallas_call(
        functools.partial(_rope_body, pos0=pos0),
        out_shape=[jax.ShapeDtypeStruct((rows, half), F32)] * 2,
        name="rope_tables",
    )(inv.reshape(1, half))


def _rotate(x, cos, sin):
    half = x.shape[-1] // 2
    x1, x2 = x[:, :half], x[:, half:]
    return jnp.concatenate([x1 * cos - x2 * sin, x1 * sin + x2 * cos], axis=1)


def _log_decay(h, shape):
    hf = jnp.full(shape, h, I32).astype(F32)
    return jnp.log(1.0 - jnp.exp2(-5.0 - hf))


def _ret_prompt_body(q_ref, k_ref, v_ref, g_ref, cos_ref, sin_ref, o_ref, so_ref, s_sc, *, c):
    n = pl.program_id(1)

    @pl.when(n == 0)
    def _():
        s_sc[...] = jnp.zeros_like(s_sc)

    i = lax.broadcasted_iota(I32, (c, c), 0).astype(F32)
    j = lax.broadcasted_iota(I32, (c, c), 1).astype(F32)
    diff = i - j
    i_col = lax.broadcasted_iota(I32, (c, 1), 0).astype(F32)
    cos, sin = cos_ref[...], sin_ref[...]

    new_states, outs = [], []
    for h in range(R_HEADS):
        dk, dv = slice(h * R_DK, (h + 1) * R_DK), slice(h * R_DV, (h + 1) * R_DV)
        dmask = jnp.where(diff >= 0, jnp.exp(jnp.maximum(diff, 0.0) * _log_decay(h, (c, c))), 0.0)
        lg_col = _log_decay(h, (c, 1))
        q_dec = jnp.exp((i_col + 1.0) * lg_col)
        k_dec = jnp.exp((c - 1.0 - i_col) * lg_col)
        c_dec = jnp.exp(c * _log_decay(h, (1, R_DV)))
        qr = _rotate(q_ref[:, dk], cos, sin)
        kr = _rotate(k_ref[:, dk], cos, sin) * (R_DK ** -0.5)
        vb = v_ref[:, dv].astype(BF16)
        s = s_sc[h]
        att = lax.dot_general(qr.astype(BF16), kr.astype(BF16), NT_DIMS, preferred_element_type=F32) * dmask
        o = _dot(att.astype(BF16), vb) + _dot((qr * q_dec).astype(BF16), s.astype(BF16))
        new_states.append(s * c_dec + lax.dot_general((kr * k_dec).astype(BF16), vb, TN_DIMS,
                                                      preferred_element_type=F32))
        on = o * lax.rsqrt(jnp.mean(o * o, axis=-1, keepdims=True) + EPS)
        outs.append((on * _silu(g_ref[:, dv])).astype(o_ref.dtype))
    for h in range(R_HEADS):
        s_sc[h] = new_states[h]
    o_ref[...] = jnp.concatenate(outs, axis=1)

    @pl.when(n == pl.num_programs(1) - 1)
    def _():
        so_ref[...] = s_sc[...]


def _ret_prompt(z3, cos, sin, batch, seq):
    c = math.gcd(seq, RET_CHUNK)
    nc = seq // c

    def sec(s):
        return pl.BlockSpec((None, c, SEC), lambda b, n: (s, b * nc + n, 0))

    tab = pl.BlockSpec((c, R_DK // 2), lambda b, n: (n, 0))
    return pl.pallas_call(
        functools.partial(_ret_prompt_body, c=c),
        grid=(batch, nc),
        in_specs=[sec(SEC_QR), sec(SEC_KR), sec(SEC_VR), sec(SEC_GR), tab, tab],
        out_specs=[pl.BlockSpec((c, R_HEADS * R_DV), lambda b, n: (b * nc + n, 0)),
                   pl.BlockSpec((None, R_HEADS, R_DK, R_DV), lambda b, n: (b, 0, 0, 0))],
        out_shape=[jax.ShapeDtypeStruct((batch * seq, R_HEADS * R_DV), BF16),
                   jax.ShapeDtypeStruct((batch, R_HEADS, R_DK, R_DV), F32)],
        scratch_shapes=[pltpu.VMEM((R_HEADS, R_DK, R_DV), F32)],
        compiler_params=_params("parallel", "arbitrary"),
        name="retention_prompt",
    )(z3, z3, z3, z3, cos, sin)


def _ret_step_body(q_ref, k_ref, v_ref, g_ref, cos_ref, sin_ref, s_ref, o_ref, so_ref):
    cos, sin = cos_ref[0:1, :], sin_ref[0:1, :]
    rows = BF16_SUBLANES
    row0 = lax.broadcasted_iota(I32, (rows, R_DK), 0) == 0
    for h in range(R_HEADS):
        decay = jnp.exp(_log_decay(h, (1, R_DV)))
        qr = _rotate(q_ref[h:h + 1, :], cos, sin)
        kr = _rotate(k_ref[h:h + 1, :], cos, sin) * (R_DK ** -0.5)
        v = v_ref[h:h + 1, :]
        s = s_ref[h]
        q_rows = jnp.broadcast_to(qr * decay, (rows, R_DK)).astype(BF16)
        qs = _dot(q_rows, s.astype(BF16))[0:1, :]
        o = jnp.sum(qr * kr, axis=-1, keepdims=True) * v + qs
        k_rows = jnp.where(row0, jnp.broadcast_to(kr, (rows, R_DK)), 0.0).astype(BF16)
        v_rows = jnp.broadcast_to(v, (rows, R_DV)).astype(BF16)
        so_ref[h] = s * decay + lax.dot_general(k_rows, v_rows, TN_DIMS, preferred_element_type=F32)
        on = o * lax.rsqrt(jnp.mean(o * o, axis=-1, keepdims=True) + EPS)
        o_ref[h:h + 1, :] = (on * _silu(g_ref[h:h + 1, :])).astype(o_ref.dtype)


def _mem_rows(q, k_ref, v_ref):
    outs = []
    for h in range(M_HEADS):
        dh = slice(h * M_DH, (h + 1) * M_DH)
        s = lax.dot_general(q[:, dh].astype(BF16), k_ref[:, dh].astype(BF16), NT_DIMS,
                            preferred_element_type=F32) * (M_DH ** -0.5)
        m = jnp.max(s, axis=-1, keepdims=True)
        p = jnp.exp(s - m)
        l = jnp.sum(p, axis=-1, keepdims=True)
        outs.append(_dot(p.astype(BF16), v_ref[:, dh].astype(BF16)) / l)
    return jnp.concatenate(outs, axis=1)


def _mem_body(q_ref, k_ref, v_ref, o_ref):
    o_ref[...] = _mem_rows(q_ref[...], k_ref, v_ref).astype(o_ref.dtype)


def _mem_attend(z3, mk, mv, batch, seq):
    n_mem = mk.shape[0] // batch
    width = M_HEADS * M_DH
    ts = _tile(seq, 512)
    nt = seq // ts
    kv = pl.BlockSpec((n_mem, width), lambda b, t: (b, 0))
    return pl.pallas_call(
        _mem_body,
        grid=(batch, nt),
        in_specs=[pl.BlockSpec((None, ts, width), lambda b, t: (SEC_QM, b * nt + t, 0)), kv, kv],
        out_specs=pl.BlockSpec((ts, width), lambda b, t: (b * nt + t, 0)),
        out_shape=jax.ShapeDtypeStruct((batch * seq, width), BF16),
        compiler_params=_params("parallel", "arbitrary"),
        name="mem_attend",
    )(z3, mk, mv)


def _merge_body(h_ref, *refs):
    x_refs, wg_refs, bg_refs, wb_refs, o_ref = refs[0:3], refs[3:6], refs[6:9], refs[9:12], refs[12]
    h = h_ref[...]
    merged = None
    for g in range(3):
        term = jax.nn.sigmoid(_dot(h, wg_refs[g][...]) + bg_refs[g][...]) * _dot(x_refs[g][...], wb_refs[g][...])
        merged = term if merged is None else merged + term
    o_ref[...] = merged.astype(o_ref.dtype)


def _merge(h, branches, w_gate, b_gate, branch_ws):
    rows, d = h.shape
    tm, tn = _tile(rows, 1024), _tile(d, 512)
    nj = d // tn
    b_gate = b_gate.reshape(1, 3 * d)

    def row(width):
        return pl.BlockSpec((tm, width), lambda i, j: (i, 0))

    def cols(n_rows, group=0):
        return pl.BlockSpec((n_rows, tn), lambda i, j: (0, group * nj + j))

    return pl.pallas_call(
        _merge_body,
        grid=(rows // tm, nj),
        in_specs=([row(d)] + [row(x.shape[1]) for x in branches]
                  + [cols(d, g) for g in range(3)] + [cols(1, g) for g in range(3)]
                  + [cols(w.shape[0]) for w in branch_ws]),
        out_specs=pl.BlockSpec((tm, tn), lambda i, j: (i, j)),
        out_shape=jax.ShapeDtypeStruct((rows, d), BF16),
        compiler_params=_params("parallel", "arbitrary"),
        name="gated_merge",
    )(h, *branches, w_gate, w_gate, w_gate, b_gate, b_gate, b_gate, *branch_ws)


def _moba_gate_body(pt_ref, q_ref, *refs, bps):
    del pt_ref
    k_refs, gates_ref = refs[:-1], refs[-1]
    n = pl.program_id(1)

    @pl.when(n == 0)
    def _():
        gates_ref[...] = jnp.zeros_like(gates_ref)

    gates_ref[...] = _block_gates(q_ref[...], k_refs, n * bps, gates_ref[...])


def _moba_select_body(g_ref, sel_ref, *, nb_past):
    lane = lax.broadcasted_iota(I32, g_ref.shape, 1)
    gw = jnp.where(lane < nb_past, g_ref[...], NEG_INF)
    sel = jnp.zeros(gw.shape, I32)
    for r in range(MOBA_TOPK):
        mx = jnp.max(gw, axis=1, keepdims=True)
        idx = jnp.min(jnp.where(gw == mx, lane, 128), axis=1, keepdims=True)
        sel = jnp.where(lane == r, idx, sel)
        gw = jnp.where(lane == idx, -jnp.inf, gw)
    sel_ref[...] = sel


def _moba_attend_body(pt_ref, sel_ref, t5_ref, q_ref, kn_ref, vn_ref, ck_ref, cv_ref, o_ref, kbuf, vbuf, sem,
                      *, past_len):
    b = pl.program_id(0)
    slot = lax.rem(b, 2)
    scale = A_DH ** -0.5
    n_tiles = MOBA_TOPK * PAGES_PER_BLOCK

    def tile_copies(bb, to_slot):
        copies = []
        for h in range(A_HEADS):
            for r in range(MOBA_TOPK):
                block = sel_ref[bb, h * MOBA_TOPK + r]
                for t in range(PAGES_PER_BLOCK):
                    page = pt_ref[bb, block * PAGES_PER_BLOCK + t]
                    j = r * PAGES_PER_BLOCK + t
                    copies.append(pltpu.make_async_copy(ck_ref.at[page, :, h, :], kbuf.at[to_slot, h, j],
                                                        sem.at[to_slot]))
                    copies.append(pltpu.make_async_copy(cv_ref.at[page, :, h, :], vbuf.at[to_slot, h, j],
                                                        sem.at[to_slot]))
        return copies

    def start_all(copies):
        for i, cp in enumerate(copies):
            cp.start(priority=i % 2)

    @pl.when(b == 0)
    def _():
        start_all(tile_copies(0, 0))

    @pl.when(b + 1 < pl.num_programs(0))
    def _():
        start_all(tile_copies(b + 1, 1 - slot))

    for cp in tile_copies(b, slot):
        cp.wait()

    q, kn, vn = q_ref[...], kn_ref[...], vn_ref[...]
    outs = []
    assert n_tiles <= SUBLANES
    lane = lax.broadcasted_iota(I32, (SUBLANES, PAGE_SIZE), 1)
    row = lax.broadcasted_iota(I32, (SUBLANES, PAGE_SIZE), 0)
    for h in range(A_HEADS):
        qh = q[h:h + 1, :]
        q8 = jnp.broadcast_to(qh, (8, A_DH)).astype(BF16)
        first_key = jnp.zeros((SUBLANES, PAGE_SIZE), I32)
        for r in range(MOBA_TOPK):
            block = sel_ref[b, h * MOBA_TOPK + r]
            for t in range(PAGES_PER_BLOCK):
                first_key = jnp.where(row == r * PAGES_PER_BLOCK + t, block * MOBA_BLOCK + t * PAGE_SIZE, first_key)
        bucket = _t5_bucket(past_len - (first_key + lane))
        bias = jnp.zeros((SUBLANES, PAGE_SIZE), F32)
        for bkt in range(T5_BUCKETS):
            bias = jnp.where(bucket == bkt, t5_ref[bkt, h], bias)
        s = []
        for j in range(n_tiles):
            kt = kbuf[slot, h, j].astype(BF16)
            raw = lax.dot_general(q8, kt, NT_DIMS, preferred_element_type=F32)[0:1, :]
            s.append(raw * scale + bias[j:j + 1, :])
        s_own = jnp.sum(qh * kn[h:h + 1, :], axis=1, keepdims=True) * scale + t5_ref[0, h]
        m = jnp.maximum(jnp.max(functools.reduce(jnp.maximum, s), axis=1, keepdims=True), s_own)
        p = [jnp.exp(x - m) for x in s]
        p_own = jnp.exp(s_own - m)
        l = jnp.sum(functools.reduce(jnp.add, p), axis=1, keepdims=True) + p_own
        acc = p_own * vn[h:h + 1, :]
        for j in range(n_tiles):
            p8 = jnp.broadcast_to(p[j], (8, PAGE_SIZE)).astype(BF16)
            acc = acc + _dot(p8, vbuf[slot, h, j].astype(BF16))[0:1, :]
        outs.append(acc / l)
    o_ref[...] = jnp.concatenate(outs, axis=0)


def _moba_gates(q4, cache_k, page_table):
    batch, n_pages = page_table.shape
    nb_past = n_pages // PAGES_PER_BLOCK
    bps = _tile(nb_past, 8)

    def page(t):
        return pl.BlockSpec((None, PAGE_SIZE, A_HEADS, A_DH),
                            lambda b, n, pt: (pt[b, n * bps * PAGES_PER_BLOCK + t], 0, 0, 0))

    pages = [page(t) for t in range(bps * PAGES_PER_BLOCK)]
    return pl.pallas_call(
        functools.partial(_moba_gate_body, bps=bps),
        grid_spec=pltpu.PrefetchScalarGridSpec(
            num_scalar_prefetch=1,
            grid=(batch, nb_past // bps),
            in_specs=[pl.BlockSpec((None, None, A_HEADS, A_DH), lambda b, n, pt: (SEC_QA, b, 0, 0))] + pages,
            out_specs=pl.BlockSpec((None, A_HEADS, 128), lambda b, n, pt: (b, 0, 0)),
        ),
        out_shape=jax.ShapeDtypeStruct((batch, A_HEADS, 128), F32),
        compiler_params=_params("parallel", "arbitrary"),
        name="moba_gate",
    )(page_table, q4, *([cache_k] * len(pages)))


def _sample_mixers_body(pt_ref, sel_ref, t5_ref, qa_ref, kn_ref, vn_ref, ck_ref, cv_ref,
                        qr_ref, kr_ref, vr_ref, gr_ref, cos_ref, sin_ref, state_ref, qm_ref, mk_ref, mv_ref,
                        oa_ref, or_ref, state_out_ref, om_ref, kbuf, vbuf, sem, *, past_len):
    _moba_attend_body(pt_ref, sel_ref, t5_ref, qa_ref, kn_ref, vn_ref, ck_ref, cv_ref, oa_ref, kbuf, vbuf, sem,
                      past_len=past_len)
    _ret_step_body(qr_ref, kr_ref, vr_ref, gr_ref, cos_ref, sin_ref, state_ref, or_ref, state_out_ref)
    q_rows = jnp.broadcast_to(qm_ref[...], (BF16_SUBLANES, qm_ref.shape[-1]))
    om_ref[...] = _mem_rows(q_rows, mk_ref, mv_ref)[0:1, :]


def _sample_mixers(z3, k_new, v_new, cache_k, cache_v, page_table, t5_table, gates, cos, sin, state, mem_k, mem_v):
    batch, n_pages = page_table.shape
    past_len = n_pages * PAGE_SIZE
    assert past_len % MOBA_BLOCK == 0 and MOBA_BLOCK % PAGE_SIZE == 0
    nb_past = past_len // MOBA_BLOCK
    assert MOBA_TOPK <= nb_past <= 128
    n_sec = z3.shape[0]
    n_mem = mem_k.shape[0] // batch
    z4a = z3.reshape(n_sec, batch, A_HEADS, A_DH)
    z4r = z3.reshape(n_sec, batch, R_HEADS, R_DK)
    z4m = z3.reshape(n_sec, batch, 1, M_HEADS * M_DH)
    k_new = k_new.reshape(batch, A_HEADS, A_DH)
    v_new = v_new.reshape(batch, A_HEADS, A_DH)
    if gates is None:
        gates = _moba_gates(z4a, cache_k, page_table)
    sel = pl.pallas_call(
        functools.partial(_moba_select_body, nb_past=nb_past),
        out_shape=jax.ShapeDtypeStruct((batch * A_HEADS, 128), I32),
        name="moba_select",
    )(gates.reshape(batch * A_HEADS, 128))
    sel = sel[:, :MOBA_TOPK].reshape(batch, A_HEADS * MOBA_TOPK)

    def sec(s, rows, width):
        return pl.BlockSpec((None, None, rows, width), lambda b, pt, sl: (s, b, 0, 0))

    def per_seq(*shape):
        return pl.BlockSpec((None,) + shape, lambda b, pt, sl: (b,) + (0,) * len(shape))

    tab = pl.BlockSpec((SUBLANES, R_DK // 2), lambda b, pt, sl: (0, 0))
    mem = pl.BlockSpec((n_mem, M_HEADS * M_DH), lambda b, pt, sl: (b, 0))
    tiles = pltpu.VMEM((2, A_HEADS, MOBA_TOPK * PAGES_PER_BLOCK, PAGE_SIZE, A_DH), F32)
    return pl.pallas_call(
        functools.partial(_sample_mixers_body, past_len=past_len),
        grid_spec=pltpu.PrefetchScalarGridSpec(
            num_scalar_prefetch=2,
            grid=(batch,),
            in_specs=[pl.BlockSpec(memory_space=pltpu.SMEM), sec(SEC_QA, A_HEADS, A_DH),
                      per_seq(A_HEADS, A_DH), per_seq(A_HEADS, A_DH),
                      pl.BlockSpec(memory_space=pl.ANY), pl.BlockSpec(memory_space=pl.ANY),
                      sec(SEC_QR, R_HEADS, R_DK), sec(SEC_KR, R_HEADS, R_DK), sec(SEC_VR, R_HEADS, R_DK),
                      sec(SEC_GR, R_HEADS, R_DK), tab, tab, per_seq(R_HEADS, R_DK, R_DV),
                      sec(SEC_QM, 1, M_HEADS * M_DH), mem, mem],
            out_specs=[per_seq(A_HEADS, A_DH), per_seq(R_HEADS, R_DV), per_seq(R_HEADS, R_DK, R_DV),
                       per_seq(1, M_HEADS * M_DH)],
            scratch_shapes=[tiles, tiles, pltpu.SemaphoreType.DMA((2,))],
        ),
        out_shape=[jax.ShapeDtypeStruct((batch, A_HEADS, A_DH), F32),
                   jax.ShapeDtypeStruct((batch, R_HEADS, R_DV), F32),
                   jax.ShapeDtypeStruct(state.shape, F32),
                   jax.ShapeDtypeStruct((batch, 1, M_HEADS * M_DH), F32)],
        compiler_params=_params("arbitrary"),
        name="sample_mixers",
    )(page_table, sel, t5_table, z4a, k_new, v_new, cache_k, cache_v, z4r, z4r, z4r, z4r, cos, sin, state,
      z4m, mem_k, mem_v)


def kernel(x_prompt, x_sample, mem_prompt, cache_k, cache_v, cache_mem_k, cache_mem_v, state_ret, page_table, t5_table, ffn1_norm, ffn1_w1, ffn1_w3, ffn1_w2, mix_norm, mem_norm, w_in, w_mem_kv, w_gate, b_gate, w_br_moba, w_br_ret, w_br_mem, w_out, ffn2_norm, ffn2_w1, ffn2_w3, ffn2_w2, final_norm):
    batch, seq, d = x_prompt.shape
    dec_batch, dec_seq, _ = x_sample.shape
    assert dec_seq == 1
    depth = w_in.shape[0]
    n_mem = mem_prompt.shape[1]
    past_len = page_table.shape[1] * PAGE_SIZE
    cos_p, sin_p = _rope_tables(0, seq)
    cos_s, sin_s = _rope_tables(past_len, dec_seq)

    xp = x_prompt.reshape(batch * seq, d)
    xs = x_sample.reshape(dec_batch, d)
    outs = [[] for _ in range(8)]
    for l in range(depth):
        last = l == depth - 1
        g_next = final_norm if last else ffn1_norm[l + 1]

        x1s, h2s, *f1_b = _ffn_half(xs, ffn1_norm[l], ffn1_w1[l], ffn1_w3[l], ffn1_w2[l], mix_norm[l],
                                    emit_x=True, post_dtype=BF16, emit_w=True)
        x1, h2, w_in_b, *f2_b = _ffn_half(xp, ffn1_norm[l], *f1_b, mix_norm[l], emit_x=True, post_dtype=BF16,
                                          cast_jobs=[w_in[l], ffn2_w1[l], ffn2_w3[l], ffn2_w2[l]])
        z3s, ka_s, va_s = _in_proj(h2s, w_in_b)
        z3, ka, va, w_gate_b, w_out_b, *branch_b = _in_proj(
            h2, w_in_b, cast_jobs=[w_gate[l], w_out[l], w_br_moba[l], w_br_ret[l], w_br_mem[l]])
        mem_k_s, mem_v_s, x1s = lax.optimization_barrier((cache_mem_k[l], cache_mem_v[l], x1s))

        z4s = z3s.reshape(z3s.shape[0], dec_batch, A_HEADS, A_DH)
        oa, gates_s = _moba_prompt(z3, ka, va, t5_table, batch, seq, paged=(page_table, z4s, cache_k[l]))

        oa_s, o_rs, s_new_s, om_s = _sample_mixers(
            z3s, ka_s, va_s, cache_k[l], cache_v[l], page_table, t5_table, gates_s, cos_s, sin_s, state_ret[l],
            mem_k_s.reshape(dec_batch * n_mem, M_HEADS * M_DH), mem_v_s.reshape(dec_batch * n_mem, M_HEADS * M_DH))
        branches_s = [o.reshape(dec_batch, -1).astype(BF16) for o in (oa_s, o_rs, om_s)]

        mkv_p = _norm_matmul(mem_prompt.reshape(batch * n_mem, d), mem_norm[l], w_mem_kv[l])
        mk_p, mv_p = mkv_p[:, :M_HEADS * M_DH], mkv_p[:, M_HEADS * M_DH:]
        o_r, s_new_p = _ret_prompt(z3, cos_p, sin_p, batch, seq)
        om = _mem_attend(z3, mk_p, mv_p, batch, seq)

        merged_s = _merge(h2s, branches_s, w_gate_b, b_gate[l], branch_b)
        merged = _merge(h2, [oa, o_r, om], w_gate_b, b_gate[l], branch_b)
        x2s = _matmul(merged_s, w_out_b, residual=x1s, name="out_proj")
        x2 = _matmul(merged, w_out_b, residual=x1, name="out_proj")
        xs = _ffn_half(x2s, ffn2_norm[l], *f2_b, g_next, emit_x=not last, post_dtype=F32)[0]
        xp = _ffn_half(x2, ffn2_norm[l], *f2_b, g_next, emit_x=not last, post_dtype=F32)[0]

        new = (ka.reshape(batch, seq, A_HEADS, A_DH), va.reshape(batch, seq, A_HEADS, A_DH),
               mk_p.reshape(batch, n_mem, M_HEADS, M_DH), mv_p.reshape(batch, n_mem, M_HEADS, M_DH), s_new_p,
               ka_s.reshape(dec_batch, dec_seq, A_HEADS, A_DH), va_s.reshape(dec_batch, dec_seq, A_HEADS, A_DH),
               s_new_s)
        for acc, val in zip(outs, new):
            acc.append(val)

    y_prompt = xp.reshape(batch, seq, d)
    y_sample = xs.reshape(dec_batch, dec_seq, d)
    return (y_prompt, y_sample) + tuple(jnp.stack(o) for o in outs)
```

```python
import functools
import math

import jax
import jax.numpy as jnp
import numpy as np
from jax import lax
from jax.experimental import pallas as pl
from jax.experimental.pallas import tpu as pltpu

F32 = jnp.float32
BF16 = jnp.bfloat16
I32 = jnp.int32

A_HEADS, A_DH = 8, 128
MOBA_BLOCK, MOBA_TOPK = 256, 3
T5_BUCKETS, T5_MAX_DIST = 32, 128
R_HEADS, R_DK, R_DV = 4, 256, 256
RET_CHUNK = 128
RET_ROPE_BASE = 10000.0
M_HEADS, M_DH = 4, 256
PAGE_SIZE = 128
EPS = 1e-6
NEG_INF = -1e30
LOG2E = math.log2(math.e)
SEC = 1024
W_IN_KA, W_IN_VA = 1, 2
(SEC_QA, SEC_QR, SEC_KR, SEC_VR, SEC_GR, SEC_QM) = range(6)

V7X_VMEM_LIMIT_BYTES = 56 * 1024 * 1024

NT_DIMS = (((1,), (1,)), ((), ()))
TN_DIMS = (((0,), (0,)), ((), ()))


def _params(*sem):
    return pltpu.CompilerParams(dimension_semantics=sem, vmem_limit_bytes=V7X_VMEM_LIMIT_BYTES)


def _tile(n, pref):
    t = min(n, pref)
    while n % t:
        t -= 1
    return t


def _rms(x, g):
    return x * lax.rsqrt(jnp.mean(x * x, axis=-1, keepdims=True) + EPS) * g


def _silu(x):
    return x * jax.nn.sigmoid(x)


def _dot(a, b):
    return jnp.dot(a, b, preferred_element_type=F32)


def _bf16(w):
    return w if w.dtype == BF16 else w.astype(BF16)


def _ffn_body(x_ref, g_ref, w1_ref, w3_ref, w2_ref, gp_ref, *refs, emit_x, emit_w, n_cast):
    refs = list(refs)
    cast_in = [refs.pop(0) for _ in range(n_cast)]
    xo_ref = refs.pop(0) if emit_x else None
    ho_ref = refs.pop(0)
    wb_refs = [refs.pop(0) for _ in range(3)] if emit_w else []
    cast_out = [refs.pop(0) for _ in range(n_cast)]
    (h_sc,) = refs
    acc_ref = xo_ref if emit_x else ho_ref
    f = pl.program_id(1)

    @pl.when(f == 0)
    def _():
        x = x_ref[...]
        h_sc[...] = _rms(x, g_ref[...]).astype(BF16)
        acc_ref[...] = x

    _cast_rows(cast_in, cast_out)

    w1, w3, w2 = _bf16(w1_ref[...]), _bf16(w3_ref[...]), _bf16(w2_ref[...])
    for wb_ref, w in zip(wb_refs, (w1, w3, w2)):
        wb_ref[...] = w
    h = h_sc[...]
    t = _silu(_dot(h, w1)) * _dot(h, w3) * 0.5
    acc_ref[...] += _dot(t.astype(BF16), w2)

    @pl.when(f == pl.num_programs(1) - 1)
    def _():
        ho_ref[...] = _rms(acc_ref[...], gp_ref[...]).astype(ho_ref.dtype)


BF16_SUBLANES = 16


def _cast_spec(a, grid):
    n_steps = grid[0] * grid[1]
    n_chunks = max(n for n in range(1, n_steps + 1)
                   if a.shape[0] % n == 0 and (a.shape[0] // n) % BF16_SUBLANES == 0)
    return pl.BlockSpec((a.shape[0] // n_chunks, a.shape[1]),
                        lambda i, j: (jnp.minimum(i * grid[1] + j, n_chunks - 1), 0))


def _cast_rows(src_refs, dst_refs):
    for src_ref, dst_ref in zip(src_refs, dst_refs):
        dst_ref[...] = src_ref[...].astype(BF16)


def _ffn_half(x, g, w1, w3, w2, g_post, *, emit_x, post_dtype, emit_w=False, cast_jobs=()):
    rows, d = x.shape
    ff = w1.shape[1]
    tm = _tile(rows, 512 if (emit_x or cast_jobs) else 1024)
    tf = _tile(ff, 512)
    assert not emit_w or rows == tm
    assert emit_x or post_dtype == F32
    grid = (rows // tm, ff // tf)
    row_spec = pl.BlockSpec((tm, d), lambda i, f: (i, 0))
    vec_spec = pl.BlockSpec((1, d), lambda i, f: (0, 0))
    w13_spec = pl.BlockSpec((d, tf), lambda i, f: (0, f))
    w2_spec = pl.BlockSpec((tf, d), lambda i, f: (f, 0))
    out_shape = [jax.ShapeDtypeStruct((rows, d), post_dtype)]
    out_specs = [row_spec]
    if emit_x:
        out_shape = [jax.ShapeDtypeStruct((rows, d), F32)] + out_shape
        out_specs = [row_spec] + out_specs
    if emit_w:
        out_shape += [jax.ShapeDtypeStruct(w.shape, BF16) for w in (w1, w3, w2)]
        out_specs += [w13_spec, w13_spec, w2_spec]

    cast_specs = [_cast_spec(a, grid) for a in cast_jobs]
    out_shape += [jax.ShapeDtypeStruct(a.shape, BF16) for a in cast_jobs]
    out_specs += cast_specs
    return pl.pallas_call(
        functools.partial(_ffn_body, emit_x=emit_x, emit_w=emit_w, n_cast=len(cast_jobs)),
        grid=grid,
        in_specs=[row_spec, vec_spec, w13_spec, w13_spec, w2_spec, vec_spec] + cast_specs,
        out_specs=out_specs,
        out_shape=out_shape,
        scratch_shapes=[pltpu.VMEM((tm, d), BF16)],
        compiler_params=_params("parallel", "arbitrary"),
        name="ffn_half",
    )(x, g.reshape(1, d), w1, w3, w2, g_post.reshape(1, d), *cast_jobs)


def _norm_mm_body(x_ref, g_ref, w_ref, o_ref, h_sc):
    @pl.when(pl.program_id(1) == 0)
    def _():
        h_sc[...] = _rms(x_ref[...], g_ref[...]).astype(BF16)

    o_ref[...] = _dot(h_sc[...], _bf16(w_ref[...]))


def _norm_matmul(x, g, w):
    rows, d = x.shape
    n_cols = w.shape[1]
    tm, tn = _tile(rows, 1024), _tile(n_cols, 512)
    return pl.pallas_call(
        _norm_mm_body,
        grid=(rows // tm, n_cols // tn),
        in_specs=[pl.BlockSpec((tm, d), lambda i, j: (i, 0)), pl.BlockSpec((1, d), lambda i, j: (0, 0)),
                  pl.BlockSpec((d, tn), lambda i, j: (0, j))],
        out_specs=pl.BlockSpec((tm, tn), lambda i, j: (i, j)),
        out_shape=jax.ShapeDtypeStruct((rows, n_cols), F32),
        scratch_shapes=[pltpu.VMEM((tm, d), BF16)],
        compiler_params=_params("parallel", "arbitrary"),
        name="norm_matmul",
    )(x, g.reshape(1, d), w)


def _mm_body(a_ref, w_ref, *refs, has_residual):
    o_ref = refs[-1]
    acc = _dot(a_ref[...], _bf16(w_ref[...]))
    if has_residual:
        acc = refs[0][...] + acc
    o_ref[...] = acc.astype(o_ref.dtype)


def _in_proj_body(a_ref, w_ref, *refs):
    n_cast = (len(refs) - 3) // 2
    z_ref, k_ref, v_ref = refs[n_cast:n_cast + 3]
    _cast_rows(refs[:n_cast], refs[n_cast + 3:])
    j = pl.program_id(1)
    acc = _dot(a_ref[...], w_ref[...])

    @pl.when(j == W_IN_KA)
    def _():
        k_ref[...] = acc

    @pl.when(j == W_IN_VA)
    def _():
        v_ref[...] = acc

    @pl.when((j != W_IN_KA) & (j != W_IN_VA))
    def _():
        z_ref[...] = acc


def _in_proj(a, w, cast_jobs=()):
    rows, k = a.shape
    n_sec = w.shape[1] // SEC
    assert (W_IN_KA, W_IN_VA) == (1, 2) and n_sec == 8
    tm = _tile(rows, 1024)
    grid = (rows // tm, n_sec)
    kv_spec = pl.BlockSpec((tm, SEC), lambda i, j: (i, 0))
    cast_specs = [_cast_spec(c, grid) for c in cast_jobs]
    return pl.pallas_call(
        _in_proj_body,
        grid=grid,
        in_specs=[pl.BlockSpec((tm, k), lambda i, j: (i, 0)), pl.BlockSpec((k, SEC), lambda i, j: (0, j))]
        + cast_specs,
        out_specs=[pl.BlockSpec((None, tm, SEC), lambda i, j: (jnp.maximum(j - 2, 0), i, 0)), kv_spec, kv_spec]
        + cast_specs,
        out_shape=[jax.ShapeDtypeStruct((n_sec - 2, rows, SEC), F32)] + [jax.ShapeDtypeStruct((rows, SEC), F32)] * 2
        + [jax.ShapeDtypeStruct(c.shape, BF16) for c in cast_jobs],
        compiler_params=_params("parallel", "arbitrary"),
        name="in_proj",
    )(a, w, *cast_jobs)


def _matmul(a, w, *, residual=None, out_dtype=F32, name="matmul"):
    rows, k = a.shape
    n_cols = w.shape[1]
    tm, tn = _tile(rows, 512), n_cols
    in_specs = [pl.BlockSpec((tm, k), lambda i, j: (i, 0)), pl.BlockSpec((k, tn), lambda i, j: (0, j))]
    out_spec = pl.BlockSpec((tm, tn), lambda i, j: (i, j))
    args = [a, w]
    if residual is not None:
        in_specs.append(out_spec)
        args.append(residual)
    return pl.pallas_call(
        functools.partial(_mm_body, has_residual=residual is not None),
        grid=(rows // tm, n_cols // tn),
        in_specs=in_specs,
        out_specs=out_spec,
        out_shape=jax.ShapeDtypeStruct((rows, n_cols), out_dtype),
        compiler_params=_params("parallel", "arbitrary"),
        name=name,
    )(*args)


def _t5_bucket(n):
    n = jnp.maximum(n, 0)
    max_exact = T5_BUCKETS // 2
    nf = jnp.maximum(n, 1).astype(F32)
    large = max_exact + (jnp.log(nf / max_exact) / math.log(T5_MAX_DIST / max_exact)
                         * (T5_BUCKETS - max_exact)).astype(I32)
    return jnp.where(n < max_exact, n, jnp.minimum(large, T5_BUCKETS - 1))


PAGES_PER_BLOCK = MOBA_BLOCK // PAGE_SIZE
SUBLANES = 8


def _block_gates(q, page_refs, first_block, gates):
    lane = lax.broadcasted_iota(I32, gates.shape, 1)
    for i in range(len(page_refs) // PAGES_PER_BLOCK):
        pages = [page_refs[i * PAGES_PER_BLOCK + t][...] for t in range(PAGES_PER_BLOCK)]
        ksum = functools.reduce(jnp.add, [jnp.sum(x, axis=0) for x in pages])
        gate = jnp.sum(q * ksum, axis=1, keepdims=True) / MOBA_BLOCK
        gates = jnp.where(lane == first_block + i, gate, gates)
    return gates


PAGE_RING = 3


def _moba_prompt_body(*refs, nb, side_blocks, grid):
    h, b, c = pl.program_id(0), pl.program_id(1), pl.program_id(2)
    if side_blocks:
        (pt_ref, t5_ref, q_ref, k_ref, v_ref, qside_ref, cache_ref, o_ref, gates_ref,
         kb_sc, vb_sc, kdiff_sc, bias_sc, pages_sc, page_sem) = refs
        n_pages = side_blocks * PAGES_PER_BLOCK
        steps_per_seq = grid[2]
        n_steps = grid[0] * grid[1] * grid[2]
        step_idx = (h * grid[1] + b) * steps_per_seq + c

        def page_copies(s):
            seq, part, slot = s // steps_per_seq, s % steps_per_seq, s % PAGE_RING
            return [pltpu.make_async_copy(cache_ref.at[pt_ref[seq, part * n_pages + t]], pages_sc.at[slot, t],
                                          page_sem.at[slot]) for t in range(n_pages)]

        @pl.when(step_idx == 0)
        def _():
            for s in range(min(PAGE_RING - 1, n_steps)):
                for cp in page_copies(s):
                    cp.start()

        @pl.when(step_idx + PAGE_RING - 1 < n_steps)
        def _():
            for cp in page_copies(step_idx + PAGE_RING - 1):
                cp.start()

        for cp in page_copies(step_idx):
            cp.wait()
        slot = step_idx % PAGE_RING
        page_refs = [pages_sc.at[slot, t] for t in range(n_pages)]
    else:
        t5_ref, q_ref, k_ref, v_ref, o_ref, kb_sc, vb_sc, kdiff_sc, bias_sc = refs
    blk = MOBA_BLOCK
    nbp = SUBLANES
    assert nb <= nbp
    scale = A_DH ** -0.5
    ii = lax.broadcasted_iota(I32, (blk, blk), 0)
    jj = lax.broadcasted_iota(I32, (blk, blk), 1)

    @pl.when((b == 0) & (c == 0))
    def _():
        for t in range(2):
            bucket = _t5_bucket(t * blk + ii - jj)
            tile = jnp.zeros((blk, blk), F32)
            for bkt in range(T5_BUCKETS):
                tile = jnp.where(bucket == bkt, t5_ref[bkt, h], tile)
            tile = tile * LOG2E
            bias_sc[t] = jnp.where(ii >= jj, tile, NEG_INF) if t == 0 else tile
        for n in range(nb):
            kb_sc[n * blk:(n + 1) * blk, A_DH:2 * A_DH] = jnp.where(
                lax.broadcasted_iota(I32, (blk, A_DH), 1) == n, 1.0, 0.0).astype(BF16)
        vb_sc[:, A_DH:2 * A_DH] = jnp.ones((nb * blk, A_DH), BF16)

    @pl.when(c == 0)
    def _():
        km = jnp.concatenate(
            [jnp.mean(k_ref[n * blk:(n + 1) * blk, :], axis=0, keepdims=True) for n in range(nb)]
            + [jnp.zeros((1, A_DH), F32)] * (nbp - nb), axis=0)
        for n in range(nb):
            kdiff_sc[n * nbp:(n + 1) * nbp, :] = km[n:n + 1, :] - km
        kb_sc[:, 0:A_DH] = k_ref[...].astype(BF16)
        vb_sc[:, 0:A_DH] = v_ref[...].astype(BF16)
        if side_blocks:
            gates_ref[...] = jnp.zeros_like(gates_ref)

    far_bias = t5_ref[T5_BUCKETS - 1, h] * LOG2E

    def attend(cc):
        q = q_ref[cc * blk:(cc + 1) * blk, :]
        qs = (q * (scale * LOG2E)).astype(BF16)
        n_keys = (cc + 1) * blk
        if cc <= MOBA_TOPK:
            s_all = lax.dot_general(qs, kb_sc[0:n_keys, 0:A_DH], NT_DIMS, preferred_element_type=F32)
        else:
            diff_t = lax.dot_general(kdiff_sc[0:cc * nbp, :], q, NT_DIMS, precision=lax.Precision.HIGHEST,
                                     preferred_element_type=F32)
            pair = lax.broadcasted_iota(I32, diff_t.shape, 0)
            m_of, n_of = jnp.right_shift(pair, int(math.log2(nbp))), jnp.bitwise_and(pair, nbp - 1)
            beats_t = jnp.where(diff_t > 0, 1.0, jnp.where(diff_t == 0, jnp.where(n_of > m_of, 1.0, 0.0), 0.0))
            fold = jnp.where(jnp.bitwise_and(lax.broadcasted_iota(I32, (cc * nbp, 128), 0), nbp - 1)
                             == lax.broadcasted_iota(I32, (cc * nbp, 128), 1), 1.0, 0.0)
            rank = lax.dot_general(beats_t.astype(BF16), fold.astype(BF16), TN_DIMS,
                                   preferred_element_type=F32)
            lane = lax.broadcasted_iota(I32, rank.shape, 1)
            sel = jnp.where(lane < cc, jnp.where(rank < MOBA_TOPK, 0.0, NEG_INF), 0.0)
            q_sel = jnp.concatenate([qs, sel.astype(BF16)], axis=1)
            s_all = lax.dot_general(q_sel, kb_sc[0:n_keys, :], NT_DIMS, preferred_element_type=F32)
        tiles = []
        for n in range(cc + 1):
            t = s_all[:, n * blk:(n + 1) * blk]
            tiles.append(t + (bias_sc[0] if n == cc else bias_sc[1] if n == cc - 1 else far_bias))
        m = jnp.max(functools.reduce(jnp.maximum, tiles), axis=1, keepdims=True)
        p_all = jnp.concatenate([jnp.exp2(t - m).astype(BF16) for t in tiles], axis=1)
        pv = _dot(p_all, vb_sc[0:n_keys, :])
        o_ref[cc * blk:(cc + 1) * blk, :] = (pv[:, 0:A_DH] / pv[:, A_DH:2 * A_DH]).astype(o_ref.dtype)

    def step(s):
        if side_blocks:
            gates_ref[...] = _block_gates(qside_ref[...], page_refs, s * side_blocks, gates_ref[...])
        for cc in sorted({s, nb - 1 - s}):
            attend(cc)

    for s in range(_moba_steps(nb)):
        pl.when(c == s)(functools.partial(step, s))


def _moba_steps(nb):
    return (nb + 1) // 2


def _moba_prompt(z3, k, v, t5_table, batch, seq, paged=None):
    blk = MOBA_BLOCK
    nb = seq // blk
    steps = _moba_steps(nb)
    side_blocks = 0
    if paged is not None:
        page_table, q_side, cache_k = paged
        nb_past = page_table.shape[1] // PAGES_PER_BLOCK
        if page_table.shape[0] == A_HEADS * batch and nb_past % steps == 0:
            side_blocks = nb_past // steps

    seq_spec = pl.BlockSpec((seq, A_DH), lambda h, b, c, *_: (b, h))
    in_specs = [pl.BlockSpec(memory_space=pltpu.SMEM),
                pl.BlockSpec((None, seq, A_DH), lambda h, b, c, *_: (SEC_QA, b, h)), seq_spec, seq_spec]
    out_specs = [seq_spec]
    out_shape = [jax.ShapeDtypeStruct((batch * seq, A_HEADS * A_DH), BF16)]
    args = [t5_table, z3, k, v]
    scratch = [pltpu.VMEM((seq, 2 * A_DH), BF16), pltpu.VMEM((seq, 2 * A_DH), BF16),
               pltpu.VMEM((SUBLANES * SUBLANES, A_DH), F32), pltpu.VMEM((2, blk, blk), F32)]
    grid = (A_HEADS, batch, steps)
    if side_blocks:
        in_specs += [pl.BlockSpec((None, None, A_HEADS, A_DH), lambda h, b, c, pt: (SEC_QA, h * batch + b, 0, 0)),
                     pl.BlockSpec(memory_space=pl.ANY)]
        out_specs.append(pl.BlockSpec((None, A_HEADS, 128), lambda h, b, c, pt: (h * batch + b, 0, 0)))
        out_shape.append(jax.ShapeDtypeStruct((A_HEADS * batch, A_HEADS, 128), F32))
        args = [page_table] + args + [q_side, cache_k]
        scratch += [pltpu.VMEM((PAGE_RING, side_blocks * PAGES_PER_BLOCK, PAGE_SIZE, A_HEADS, A_DH), F32),
                    pltpu.SemaphoreType.DMA((PAGE_RING,))]
    out = pl.pallas_call(
        functools.partial(_moba_prompt_body, nb=nb, side_blocks=side_blocks, grid=grid),
        grid_spec=pltpu.PrefetchScalarGridSpec(
            num_scalar_prefetch=1 if side_blocks else 0,
            grid=grid,
            in_specs=in_specs,
            out_specs=out_specs,
            scratch_shapes=scratch,
        ),
        out_shape=out_shape,
        compiler_params=_params("arbitrary", "arbitrary", "arbitrary"),
        name="moba_prompt",
    )(*args)
    if paged is None:
        return out[0]
    return out[0], (out[1] if side_blocks else None)


def _rope_body(inv_ref, cos_ref, sin_ref, *, pos0):
    pos = pos0 + lax.broadcasted_iota(I32, cos_ref.shape, 0)
    ang = pos.astype(F32) * inv_ref[...]
    cos_ref[...] = jnp.cos(ang)
    sin_ref[...] = jnp.sin(ang)


def _rope_tables(pos0, n_pos):
    half = R_DK // 2
    inv = 1.0 / (RET_ROPE_BASE ** jnp.linspace(0.0, 1.0, half, dtype=F32))
    rows = -(-n_pos // 8) * 8
    return pl.pallas_call(
        functools.partial(_rope_body, pos0=pos0),
        out_shape=[jax.ShapeDtypeStruct((rows, half), F32)] * 2,
        name="rope_tables",
    )(inv.reshape(1, half))


def _rotate(x, cos, sin):
    half = x.shape[-1] // 2
    x1, x2 = x[:, :half], x[:, half:]
    return jnp.concatenate([x1 * cos - x2 * sin, x1 * sin + x2 * cos], axis=1)


def _log_decay(h, shape):
    hf = jnp.full(shape, h, I32).astype(F32)
    return jnp.log(1.0 - jnp.exp2(-5.0 - hf))


def _ret_prompt_body(q_ref, k_ref, v_ref, g_ref, cos_ref, sin_ref, o_ref, so_ref, s_sc, *, c):
    n = pl.program_id(1)

    @pl.when(n == 0)
    def _():
        s_sc[...] = jnp.zeros_like(s_sc)

    i = lax.broadcasted_iota(I32, (c, c), 0).astype(F32)
    j = lax.broadcasted_iota(I32, (c, c), 1).astype(F32)
    diff = i - j
    i_col = lax.broadcasted_iota(I32, (c, 1), 0).astype(F32)
    cos, sin = cos_ref[...], sin_ref[...]

    new_states, outs = [], []
    for h in range(R_HEADS):
        dk, dv = slice(h * R_DK, (h + 1) * R_DK), slice(h * R_DV, (h + 1) * R_DV)
        dmask = jnp.where(diff >= 0, jnp.exp(jnp.maximum(diff, 0.0) * _log_decay(h, (c, c))), 0.0)
        lg_col = _log_decay(h, (c, 1))
        q_dec = jnp.exp((i_col + 1.0) * lg_col)
        k_dec = jnp.exp((c - 1.0 - i_col) * lg_col)
        c_dec = jnp.exp(c * _log_decay(h, (1, R_DV)))
        qr = _rotate(q_ref[:, dk], cos, sin)
        kr = _rotate(k_ref[:, dk], cos, sin) * (R_DK ** -0.5)
        vb = v_ref[:, dv].astype(BF16)
        s = s_sc[h]
        att = lax.dot_general(qr.astype(BF16), kr.astype(BF16), NT_DIMS, preferred_element_type=F32) * dmask
        o = _dot(att.astype(BF16), vb) + _dot((qr * q_dec).astype(BF16), s.astype(BF16))
        new_states.append(s * c_dec + lax.dot_general((kr * k_dec).astype(BF16), vb, TN_DIMS,
                                                      preferred_element_type=F32))
        on = o * lax.rsqrt(jnp.mean(o * o, axis=-1, keepdims=True) + EPS)
        outs.append((on * _silu(g_ref[:, dv])).astype(o_ref.dtype))
    for h in range(R_HEADS):
        s_sc[h] = new_states[h]
    o_ref[...] = jnp.concatenate(outs, axis=1)

    @pl.when(n == pl.num_programs(1) - 1)
    def _():
        so_ref[...] = s_sc[...]


def _ret_prompt(z3, cos, sin, batch, seq):
    c = math.gcd(seq, RET_CHUNK)
    nc = seq // c

    def sec(s):
        return pl.BlockSpec((None, c, SEC), lambda b, n: (s, b * nc + n, 0))

    tab = pl.BlockSpec((c, R_DK // 2), lambda b, n: (n, 0))
    return pl.pallas_call(
        functools.partial(_ret_prompt_body, c=c),
        grid=(batch, nc),
        in_specs=[sec(SEC_QR), sec(SEC_KR), sec(SEC_VR), sec(SEC_GR), tab, tab],
        out_specs=[pl.BlockSpec((c, R_HEADS * R_DV), lambda b, n: (b * nc + n, 0)),
                   pl.BlockSpec((None, R_HEADS, R_DK, R_DV), lambda b, n: (b, 0, 0, 0))],
        out_shape=[jax.ShapeDtypeStruct((batch * seq, R_HEADS * R_DV), BF16),
                   jax.ShapeDtypeStruct((batch, R_HEADS, R_DK, R_DV), F32)],
        scratch_shapes=[pltpu.VMEM((R_HEADS, R_DK, R_DV), F32)],
        compiler_params=_params("parallel", "arbitrary"),
        name="retention_prompt",
    )(z3, z3, z3, z3, cos, sin)


def _ret_step_body(q_ref, k_ref, v_ref, g_ref, cos_ref, sin_ref, s_ref, o_ref, so_ref):
    cos, sin = cos_ref[0:1, :], sin_ref[0:1, :]
    rows = BF16_SUBLANES
    row0 = lax.broadcasted_iota(I32, (rows, R_DK), 0) == 0
    for h in range(R_HEADS):
        decay = jnp.exp(_log_decay(h, (1, R_DV)))
        qr = _rotate(q_ref[h:h + 1, :], cos, sin)
        kr = _rotate(k_ref[h:h + 1, :], cos, sin) * (R_DK ** -0.5)
        v = v_ref[h:h + 1, :]
        s = s_ref[h]
        q_rows = jnp.broadcast_to(qr * decay, (rows, R_DK)).astype(BF16)
        qs = _dot(q_rows, s.astype(BF16))[0:1, :]
        o = jnp.sum(qr * kr, axis=-1, keepdims=True) * v + qs
        k_rows = jnp.where(row0, jnp.broadcast_to(kr, (rows, R_DK)), 0.0).astype(BF16)
        v_rows = jnp.broadcast_to(v, (rows, R_DV)).astype(BF16)
        so_ref[h] = s * decay + lax.dot_general(k_rows, v_rows, TN_DIMS, preferred_element_type=F32)
        on = o * lax.rsqrt(jnp.mean(o * o, axis=-1, keepdims=True) + EPS)
        o_ref[h:h + 1, :] = (on * _silu(g_ref[h:h + 1, :])).astype(o_ref.dtype)


def _mem_rows(q, k_ref, v_ref):
    outs = []
    for h in range(M_HEADS):
        dh = slice(h * M_DH, (h + 1) * M_DH)
        s = lax.dot_general(q[:, dh].astype(BF16), k_ref[:, dh].astype(BF16), NT_DIMS,
                            preferred_element_type=F32) * (M_DH ** -0.5)
        m = jnp.max(s, axis=-1, keepdims=True)
        p = jnp.exp(s - m)
        l = jnp.sum(p, axis=-1, keepdims=True)
        outs.append(_dot(p.astype(BF16), v_ref[:, dh].astype(BF16)) / l)
    return jnp.concatenate(outs, axis=1)


def _mem_body(q_ref, k_ref, v_ref, o_ref):
    o_ref[...] = _mem_rows(q_ref[...], k_ref, v_ref).astype(o_ref.dtype)


def _mem_attend(z3, mk, mv, batch, seq):
    n_mem = mk.shape[0] // batch
    width = M_HEADS * M_DH
    ts = _tile(seq, 512)
    nt = seq // ts
    kv = pl.BlockSpec((n_mem, width), lambda b, t: (b, 0))
    return pl.pallas_call(
        _mem_body,
        grid=(batch, nt),
        in_specs=[pl.BlockSpec((None, ts, width), lambda b, t: (SEC_QM, b * nt + t, 0)), kv, kv],
        out_specs=pl.BlockSpec((ts, width), lambda b, t: (b * nt + t, 0)),
        out_shape=jax.ShapeDtypeStruct((batch * seq, width), BF16),
        compiler_params=_params("parallel", "arbitrary"),
        name="mem_attend",
    )(z3, mk, mv)


def _merge_body(h_ref, *refs):
    x_refs, wg_refs, bg_refs, wb_refs, o_ref = refs[0:3], refs[3:6], refs[6:9], refs[9:12], refs[12]
    h = h_ref[...]
    merged = None
    for g in range(3):
        term = jax.nn.sigmoid(_dot(h, wg_refs[g][...]) + bg_refs[g][...]) * _dot(x_refs[g][...], wb_refs[g][...])
        merged = term if merged is None else merged + term
    o_ref[...] = merged.astype(o_ref.dtype)


def _merge(h, branches, w_gate, b_gate, branch_ws):
    rows, d = h.shape
    tm, tn = _tile(rows, 1024), _tile(d, 512)
    nj = d // tn
    b_gate = b_gate.reshape(1, 3 * d)

    def row(width):
        return pl.BlockSpec((tm, width), lambda i, j: (i, 0))

    def cols(n_rows, group=0):
        return pl.BlockSpec((n_rows, tn), lambda i, j: (0, group * nj + j))

    return pl.pallas_call(
        _merge_body,
        grid=(rows // tm, nj),
        in_specs=([row(d)] + [row(x.shape[1]) for x in branches]
                  + [cols(d, g) for g in range(3)] + [cols(1, g) for g in range(3)]
                  + [cols(w.shape[0]) for w in branch_ws]),
        out_specs=pl.BlockSpec((tm, tn), lambda i, j: (i, j)),
        out_shape=jax.ShapeDtypeStruct((rows, d), BF16),
        compiler_params=_params("parallel", "arbitrary"),
        name="gated_merge",
    )(h, *branches, w_gate, w_gate, w_gate, b_gate, b_gate, b_gate, *branch_ws)


def _moba_gate_body(pt_ref, q_ref, *refs, bps):
    del pt_ref
    k_refs, gates_ref = refs[:-1], refs[-1]
    n = pl.program_id(1)

    @pl.when(n == 0)
    def _():
        gates_ref[...] = jnp.zeros_like(gates_ref)

    gates_ref[...] = _block_gates(q_ref[...], k_refs, n * bps, gates_ref[...])


def _moba_select_body(g_ref, sel_ref, *, nb_past):
    lane = lax.broadcasted_iota(I32, g_ref.shape, 1)
    gw = jnp.where(lane < nb_past, g_ref[...], NEG_INF)
    sel = jnp.zeros(gw.shape, I32)
    for r in range(MOBA_TOPK):
        mx = jnp.max(gw, axis=1, keepdims=True)
        idx = jnp.min(jnp.where(gw == mx, lane, 128), axis=1, keepdims=True)
        sel = jnp.where(lane == r, idx, sel)
        gw = jnp.where(lane == idx, -jnp.inf, gw)
    sel_ref[...] = sel


def _moba_attend_body(pt_ref, sel_ref, t5_ref, q_ref, kn_ref, vn_ref, ck_ref, cv_ref, o_ref, kbuf, vbuf, sem,
                      *, past_len):
    b = pl.program_id(0)
    slot = lax.rem(b, 2)
    scale = A_DH ** -0.5
    n_tiles = MOBA_TOPK * PAGES_PER_BLOCK

    def tile_copies(bb, to_slot):
        copies = []
        for h in range(A_HEADS):
            for r in range(MOBA_TOPK):
                block = sel_ref[bb, h * MOBA_TOPK + r]
                for t in range(PAGES_PER_BLOCK):
                    page = pt_ref[bb, block * PAGES_PER_BLOCK + t]
                    j = r * PAGES_PER_BLOCK + t
                    copies.append(pltpu.make_async_copy(ck_ref.at[page, :, h, :], kbuf.at[to_slot, h, j],
                                                        sem.at[to_slot]))
                    copies.append(pltpu.make_async_copy(cv_ref.at[page, :, h, :], vbuf.at[to_slot, h, j],
                                                        sem.at[to_slot]))
        return copies

    def start_all(copies):
        for i, cp in enumerate(copies):
            cp.start(priority=i % 2)

    @pl.when(b == 0)
    def _():
        start_all(tile_copies(0, 0))

    @pl.when(b + 1 < pl.num_programs(0))
    def _():
        start_all(tile_copies(b + 1, 1 - slot))

    for cp in tile_copies(b, slot):
        cp.wait()

    q, kn, vn = q_ref[...], kn_ref[...], vn_ref[...]
    outs = []
    assert n_tiles <= SUBLANES
    lane = lax.broadcasted_iota(I32, (SUBLANES, PAGE_SIZE), 1)
    row = lax.broadcasted_iota(I32, (SUBLANES, PAGE_SIZE), 0)
    for h in range(A_HEADS):
        qh = q[h:h + 1, :]
        q8 = jnp.broadcast_to(qh, (8, A_DH)).astype(BF16)
        first_key = jnp.zeros((SUBLANES, PAGE_SIZE), I32)
        for r in range(MOBA_TOPK):
            block = sel_ref[b, h * MOBA_TOPK + r]
            for t in range(PAGES_PER_BLOCK):
                first_key = jnp.where(row == r * PAGES_PER_BLOCK + t, block * MOBA_BLOCK + t * PAGE_SIZE, first_key)
        bucket = _t5_bucket(past_len - (first_key + lane))
        bias = jnp.zeros((SUBLANES, PAGE_SIZE), F32)
        for bkt in range(T5_BUCKETS):
            bias = jnp.where(bucket == bkt, t5_ref[bkt, h], bias)
        s = []
        for j in range(n_tiles):
            kt = kbuf[slot, h, j].astype(BF16)
            raw = lax.dot_general(q8, kt, NT_DIMS, preferred_element_type=F32)[0:1, :]
            s.append(raw * scale + bias[j:j + 1, :])
        s_own = jnp.sum(qh * kn[h:h + 1, :], axis=1, keepdims=True) * scale + t5_ref[0, h]
        m = jnp.maximum(jnp.max(functools.reduce(jnp.maximum, s), axis=1, keepdims=True), s_own)
        p = [jnp.exp(x - m) for x in s]
        p_own = jnp.exp(s_own - m)
        l = jnp.sum(functools.reduce(jnp.add, p), axis=1, keepdims=True) + p_own
        acc = p_own * vn[h:h + 1, :]
        for j in range(n_tiles):
            p8 = jnp.broadcast_to(p[j], (8, PAGE_SIZE)).astype(BF16)
            acc = acc + _dot(p8, vbuf[slot, h, j].astype(BF16))[0:1, :]
        outs.append(acc / l)
    o_ref[...] = jnp.concatenate(outs, axis=0)


def _moba_gates(q4, cache_k, page_table):
    batch, n_pages = page_table.shape
    nb_past = n_pages // PAGES_PER_BLOCK
    bps = _tile(nb_past, 8)

    def page(t):
        return pl.BlockSpec((None, PAGE_SIZE, A_HEADS, A_DH),
                            lambda b, n, pt: (pt[b, n * bps * PAGES_PER_BLOCK + t], 0, 0, 0))

    pages = [page(t) for t in range(bps * PAGES_PER_BLOCK)]
    return pl.pallas_call(
        functools.partial(_moba_gate_body, bps=bps),
        grid_spec=pltpu.PrefetchScalarGridSpec(
            num_scalar_prefetch=1,
            grid=(batch, nb_past // bps),
            in_specs=[pl.BlockSpec((None, None, A_HEADS, A_DH), lambda b, n, pt: (SEC_QA, b, 0, 0))] + pages,
            out_specs=pl.BlockSpec((None, A_HEADS, 128), lambda b, n, pt: (b, 0, 0)),
        ),
        out_shape=jax.ShapeDtypeStruct((batch, A_HEADS, 128), F32),
        compiler_params=_params("parallel", "arbitrary"),
        name="moba_gate",
    )(page_table, q4, *([cache_k] * len(pages)))


def _sample_mixers_body(pt_ref, sel_ref, t5_ref, qa_ref, kn_ref, vn_ref, ck_ref, cv_ref,
                        qr_ref, kr_ref, vr_ref, gr_ref, cos_ref, sin_ref, state_ref, qm_ref, mk_ref, mv_ref,
                        oa_ref, or_ref, state_out_ref, om_ref, kbuf, vbuf, sem, *, past_len):
    _moba_attend_body(pt_ref, sel_ref, t5_ref, qa_ref, kn_ref, vn_ref, ck_ref, cv_ref, oa_ref, kbuf, vbuf, sem,
                      past_len=past_len)
    _ret_step_body(qr_ref, kr_ref, vr_ref, gr_ref, cos_ref, sin_ref, state_ref, or_ref, state_out_ref)
    q_rows = jnp.broadcast_to(qm_ref[...], (BF16_SUBLANES, qm_ref.shape[-1]))
    om_ref[...] = _mem_rows(q_rows, mk_ref, mv_ref)[0:1, :]


def _sample_mixers(z3, k_new, v_new, cache_k, cache_v, page_table, t5_table, gates, cos, sin, state, mem_k, mem_v):
    batch, n_pages = page_table.shape
    past_len = n_pages * PAGE_SIZE
    assert past_len % MOBA_BLOCK == 0 and MOBA_BLOCK % PAGE_SIZE == 0
    nb_past = past_len // MOBA_BLOCK
    assert MOBA_TOPK <= nb_past <= 128
    n_sec = z3.shape[0]
    n_mem = mem_k.shape[0] // batch
    z4a = z3.reshape(n_sec, batch, A_HEADS, A_DH)
    z4r = z3.reshape(n_sec, batch, R_HEADS, R_DK)
    z4m = z3.reshape(n_sec, batch, 1, M_HEADS * M_DH)
    k_new = k_new.reshape(batch, A_HEADS, A_DH)
    v_new = v_new.reshape(batch, A_HEADS, A_DH)
    if gates is None:
        gates = _moba_gates(z4a, cache_k, page_table)
    sel = pl.pallas_call(
        functools.partial(_moba_select_body, nb_past=nb_past),
        out_shape=jax.ShapeDtypeStruct((batch * A_HEADS, 128), I32),
        name="moba_select",
    )(gates.reshape(batch * A_HEADS, 128))
    sel = sel[:, :MOBA_TOPK].reshape(batch, A_HEADS * MOBA_TOPK)

    def sec(s, rows, width):
        return pl.BlockSpec((None, None, rows, width), lambda b, pt, sl: (s, b, 0, 0))

    def per_seq(*shape):
        return pl.BlockSpec((None,) + shape, lambda b, pt, sl: (b,) + (0,) * len(shape))

    tab = pl.BlockSpec((SUBLANES, R_DK // 2), lambda b, pt, sl: (0, 0))
    mem = pl.BlockSpec((n_mem, M_HEADS * M_DH), lambda b, pt, sl: (b, 0))
    tiles = pltpu.VMEM((2, A_HEADS, MOBA_TOPK * PAGES_PER_BLOCK, PAGE_SIZE, A_DH), F32)
    return pl.pallas_call(
        functools.partial(_sample_mixers_body, past_len=past_len),
        grid_spec=pltpu.PrefetchScalarGridSpec(
            num_scalar_prefetch=2,
            grid=(batch,),
            in_specs=[pl.BlockSpec(memory_space=pltpu.SMEM), sec(SEC_QA, A_HEADS, A_DH),
                      per_seq(A_HEADS, A_DH), per_seq(A_HEADS, A_DH),
                      pl.BlockSpec(memory_space=pl.ANY), pl.BlockSpec(memory_space=pl.ANY),
                      sec(SEC_QR, R_HEADS, R_DK), sec(SEC_KR, R_HEADS, R_DK), sec(SEC_VR, R_HEADS, R_DK),
                      sec(SEC_GR, R_HEADS, R_DK), tab, tab, per_seq(R_HEADS, R_DK, R_DV),
                      sec(SEC_QM, 1, M_HEADS * M_DH), mem, mem],
            out_specs=[per_seq(A_HEADS, A_DH), per_seq(R_HEADS, R_DV), per_seq(R_HEADS, R_DK, R_DV),
                       per_seq(1, M_HEADS * M_DH)],
            scratch_shapes=[tiles, tiles, pltpu.SemaphoreType.DMA((2,))],
        ),
        out_shape=[jax.ShapeDtypeStruct((batch, A_HEADS, A_DH), F32),
                   jax.ShapeDtypeStruct((batch, R_HEADS, R_DV), F32),
                   jax.ShapeDtypeStruct(state.shape, F32),
                   jax.ShapeDtypeStruct((batch, 1, M_HEADS * M_DH), F32)],
        compiler_params=_params("arbitrary"),
        name="sample_mixers",
    )(page_table, sel, t5_table, z4a, k_new, v_new, cache_k, cache_v, z4r, z4r, z4r, z4r, cos, sin, state,
      z4m, mem_k, mem_v)


def kernel(x_prompt, x_sample, mem_prompt, cache_k, cache_v, cache_mem_k, cache_mem_v, state_ret, page_table, t5_table, ffn1_norm, ffn1_w1, ffn1_w3, ffn1_w2, mix_norm, mem_norm, w_in, w_mem_kv, w_gate, b_gate, w_br_moba, w_br_ret, w_br_mem, w_out, ffn2_norm, ffn2_w1, ffn2_w3, ffn2_w2, final_norm):
    batch, seq, d = x_prompt.shape
    dec_batch, dec_seq, _ = x_sample.shape
    assert dec_seq == 1
    depth = w_in.shape[0]
    n_mem = mem_prompt.shape[1]
    past_len = page_table.shape[1] * PAGE_SIZE
    cos_p, sin_p = _rope_tables(0, seq)
    cos_s, sin_s = _rope_tables(past_len, dec_seq)

    xp = x_prompt.reshape(batch * seq, d)
    xs = x_sample.reshape(dec_batch, d)
    outs = [[] for _ in range(8)]
    for l in range(depth):
        last = l == depth - 1
        g_next = final_norm if last else ffn1_norm[l + 1]

        x1s, h2s, *f1_b = _ffn_half(xs, ffn1_norm[l], ffn1_w1[l], ffn1_w3[l], ffn1_w2[l], mix_norm[l],
                                    emit_x=True, post_dtype=BF16, emit_w=True)
        x1, h2, w_in_b, *f2_b = _ffn_half(xp, ffn1_norm[l], *f1_b, mix_norm[l], emit_x=True, post_dtype=BF16,
                                          cast_jobs=[w_in[l], ffn2_w1[l], ffn2_w3[l], ffn2_w2[l]])
        z3s, ka_s, va_s = _in_proj(h2s, w_in_b)
        z3, ka, va, w_gate_b, w_out_b, *branch_b = _in_proj(
            h2, w_in_b, cast_jobs=[w_gate[l], w_out[l], w_br_moba[l], w_br_ret[l], w_br_mem[l]])
        mem_k_s, mem_v_s, x1s = lax.optimization_barrier((cache_mem_k[l], cache_mem_v[l], x1s))

        z4s = z3s.reshape(z3s.shape[0], dec_batch, A_HEADS, A_DH)
        oa, gates_s = _moba_prompt(z3, ka, va, t5_table, batch, seq, paged=(page_table, z4s, cache_k[l]))

        oa_s, o_rs, s_new_s, om_s = _sample_mixers(
            z3s, ka_s, va_s, cache_k[l], cache_v[l], page_table, t5_table, gates_s, cos_s, sin_s, state_ret[l],
            mem_k_s.reshape(dec_batch * n_mem, M_HEADS * M_DH), mem_v_s.reshape(dec_batch * n_mem, M_HEADS * M_DH))
        branches_s = [o.reshape(dec_batch, -1).astype(BF16) for o in (oa_s, o_rs, om_s)]

        mkv_p = _norm_matmul(mem_prompt.reshape(batch * n_mem, d), mem_norm[l], w_mem_kv[l])
        mk_p, mv_p = mkv_p[:, :M_HEADS * M_DH], mkv_p[:, M_HEADS * M_DH:]
        o_r, s_new_p = _ret_prompt(z3, cos_p, sin_p, batch, seq)
        om = _mem_attend(z3, mk_p, mv_p, batch, seq)

        merged_s = _merge(h2s, branches_s, w_gate_b, b_gate[l], branch_b)
        merged = _merge(h2, [oa, o_r, om], w_gate_b, b_gate[l], branch_b)
        x2s = _matmul(merged_s, w_out_b, residual=x1s, name="out_proj")
        x2 = _matmul(merged, w_out_b, residual=x1, name="out_proj")
        xs = _ffn_half(x2s, ffn2_norm[l], *f2_b, g_next, emit_x=not last, post_dtype=F32)[0]
        xp = _ffn_half(x2, ffn2_norm[l], *f2_b, g_next, emit_x=not last, post_dtype=F32)[0]

        new = (ka.reshape(batch, seq, A_HEADS, A_DH), va.reshape(batch, seq, A_HEADS, A_DH),
               mk_p.reshape(batch, n_mem, M_HEADS, M_DH), mv_p.reshape(batch, n_mem, M_HEADS, M_DH), s_new_p,
               ka_s.reshape(dec_batch, dec_seq, A_HEADS, A_DH), va_s.reshape(dec_batch, dec_seq, A_HEADS, A_DH),
               s_new_s)
        for acc, val in zip(outs, new):
            acc.append(val)

    y_prompt = xp.reshape(batch, seq, d)
    y_sample = xs.reshape(dec_batch, dec_seq, d)
    return (y_prompt, y_sample) + tuple(jnp.stack(o) for o in outs)
```

```python
import functools
import math

import jax
import jax.numpy as jnp
import numpy as np
from jax import lax
from jax.experimental import pallas as pl
from jax.experimental.pallas import tpu as pltpu

F32 = jnp.float32
BF16 = jnp.bfloat16
I32 = jnp.int32

A_HEADS, A_DH = 8, 128
MOBA_BLOCK, MOBA_TOPK = 256, 3
T5_BUCKETS, T5_MAX_DIST = 32, 128
R_HEADS, R_DK, R_DV = 4, 256, 256
RET_CHUNK = 128
RET_ROPE_BASE = 10000.0
M_HEADS, M_DH = 4, 256
PAGE_SIZE = 128
EPS = 1e-6
NEG_INF = -1e30
LOG2E = math.log2(math.e)
SEC = 1024
W_IN_KA, W_IN_VA = 1, 2
(SEC_QA, SEC_QR, SEC_KR, SEC_VR, SEC_GR, SEC_QM) = range(6)

V7X_VMEM_LIMIT_BYTES = 56 * 1024 * 1024

NT_DIMS = (((1,), (1,)), ((), ()))
TN_DIMS = (((0,), (0,)), ((), ()))


def _params(*sem):
    return pltpu.CompilerParams(dimension_semantics=sem, vmem_limit_bytes=V7X_VMEM_LIMIT_BYTES)


def _tile(n, pref):
    t = min(n, pref)
    while n % t:
        t -= 1
    return t


def _rms(x, g):
    return x * lax.rsqrt(jnp.mean(x * x, axis=-1, keepdims=True) + EPS) * g


def _silu(x):
    return x * jax.nn.sigmoid(x)


def _dot(a, b):
    return jnp.dot(a, b, preferred_element_type=F32)


def _bf16(w):
    return w if w.dtype == BF16 else w.astype(BF16)


def _ffn_body(x_ref, g_ref, w1_ref, w3_ref, w2_ref, gp_ref, *refs, emit_x, emit_w, n_cast):
    refs = list(refs)
    cast_in = [refs.pop(0) for _ in range(n_cast)]
    xo_ref = refs.pop(0) if emit_x else None
    ho_ref = refs.pop(0)
    wb_refs = [refs.pop(0) for _ in range(3)] if emit_w else []
    cast_out = [refs.pop(0) for _ in range(n_cast)]
    (h_sc,) = refs
    acc_ref = xo_ref if emit_x else ho_ref
    f = pl.program_id(1)

    @pl.when(f == 0)
    def _():
        x = x_ref[...]
        h_sc[...] = _rms(x, g_ref[...]).astype(BF16)
        acc_ref[...] = x

    _cast_rows(cast_in, cast_out)

    w1, w3, w2 = _bf16(w1_ref[...]), _bf16(w3_ref[...]), _bf16(w2_ref[...])
    for wb_ref, w in zip(wb_refs, (w1, w3, w2)):
        wb_ref[...] = w
    h = h_sc[...]
    t = _silu(_dot(h, w1)) * _dot(h, w3) * 0.5
    acc_ref[...] += _dot(t.astype(BF16), w2)

    @pl.when(f == pl.num_programs(1) - 1)
    def _():
        ho_ref[...] = _rms(acc_ref[...], gp_ref[...]).astype(ho_ref.dtype)


BF16_SUBLANES = 16


def _cast_spec(a, grid):
    n_steps = grid[0] * grid[1]
    n_chunks = max(n for n in range(1, n_steps + 1)
                   if a.shape[0] % n == 0 and (a.shape[0] // n) % BF16_SUBLANES == 0)
    return pl.BlockSpec((a.shape[0] // n_chunks, a.shape[1]),
                        lambda i, j: (jnp.minimum(i * grid[1] + j, n_chunks - 1), 0))


def _cast_rows(src_refs, dst_refs):
    for src_ref, dst_ref in zip(src_refs, dst_refs):
        dst_ref[...] = src_ref[...].astype(BF16)


def _ffn_half(x, g, w1, w3, w2, g_post, *, emit_x, post_dtype, emit_w=False, cast_jobs=()):
    rows, d = x.shape
    ff = w1.shape[1]
    tm = _tile(rows, 512 if (emit_x or cast_jobs) else 1024)
    tf = _tile(ff, 512)
    assert not emit_w or rows == tm
    assert emit_x or post_dtype == F32
    grid = (rows // tm, ff // tf)
    row_spec = pl.BlockSpec((tm, d), lambda i, f: (i, 0))
    vec_spec = pl.BlockSpec((1, d), lambda i, f: (0, 0))
    w13_spec = pl.BlockSpec((d, tf), lambda i, f: (0, f))
    w2_spec = pl.BlockSpec((tf, d), lambda i, f: (f, 0))
    out_shape = [jax.ShapeDtypeStruct((rows, d), post_dtype)]
    out_specs = [row_spec]
    if emit_x:
        out_shape = [jax.ShapeDtypeStruct((rows, d), F32)] + out_shape
        out_specs = [row_spec] + out_specs
    if emit_w:
        out_shape += [jax.ShapeDtypeStruct(w.shape, BF16) for w in (w1, w3, w2)]
        out_specs += [w13_spec, w13_spec, w2_spec]

    cast_specs = [_cast_spec(a, grid) for a in cast_jobs]
    out_shape += [jax.ShapeDtypeStruct(a.shape, BF16) for a in cast_jobs]
    out_specs += cast_specs
    return pl.pallas_call(
        functools.partial(_ffn_body, emit_x=emit_x, emit_w=emit_w, n_cast=len(cast_jobs)),
        grid=grid,
        in_specs=[row_spec, vec_spec, w13_spec, w13_spec, w2_spec, vec_spec] + cast_specs,
        out_specs=out_specs,
        out_shape=out_shape,
        scratch_shapes=[pltpu.VMEM((tm, d), BF16)],
        compiler_params=_params("parallel", "arbitrary"),
        name="ffn_half",
    )(x, g.reshape(1, d), w1, w3, w2, g_post.reshape(1, d), *cast_jobs)


def _norm_mm_body(x_ref, g_ref, w_ref, o_ref, h_sc):
    @pl.when(pl.program_id(1) == 0)
    def _():
        h_sc[...] = _rms(x_ref[...], g_ref[...]).astype(BF16)

    o_ref[...] = _dot(h_sc[...], _bf16(w_ref[...]))


def _norm_matmul(x, g, w):
    rows, d = x.shape
    n_cols = w.shape[1]
    tm, tn = _tile(rows, 1024), _tile(n_cols, 512)
    return pl.pallas_call(
        _norm_mm_body,
        grid=(rows // tm, n_cols // tn),
        in_specs=[pl.BlockSpec((tm, d), lambda i, j: (i, 0)), pl.BlockSpec((1, d), lambda i, j: (0, 0)),
                  pl.BlockSpec((d, tn), lambda i, j: (0, j))],
        out_specs=pl.BlockSpec((tm, tn), lambda i, j: (i, j)),
        out_shape=jax.ShapeDtypeStruct((rows, n_cols), F32),
        scratch_shapes=[pltpu.VMEM((tm, d), BF16)],
        compiler_params=_params("parallel", "arbitrary"),
        name="norm_matmul",
    )(x, g.reshape(1, d), w)


def _mm_body(a_ref, w_ref, *refs, has_residual):
    o_ref = refs[-1]
    acc = _dot(a_ref[...], _bf16(w_ref[...]))
    if has_residual:
        acc = refs[0][...] + acc
    o_ref[...] = acc.astype(o_ref.dtype)


def _in_proj_body(a_ref, w_ref, *refs):
    n_cast = (len(refs) - 3) // 2
    z_ref, k_ref, v_ref = refs[n_cast:n_cast + 3]
    _cast_rows(refs[:n_cast], refs[n_cast + 3:])
    j = pl.program_id(1)
    acc = _dot(a_ref[...], w_ref[...])

    @pl.when(j == W_IN_KA)
    def _():
        k_ref[...] = acc

    @pl.when(j == W_IN_VA)
    def _():
        v_ref[...] = acc

    @pl.when((j != W_IN_KA) & (j != W_IN_VA))
    def _():
        z_ref[...] = acc


def _in_proj(a, w, cast_jobs=()):
    rows, k = a.shape
    n_sec = w.shape[1] // SEC
    assert (W_IN_KA, W_IN_VA) == (1, 2) and n_sec == 8
    tm = _tile(rows, 1024)
    grid = (rows // tm, n_sec)
    kv_spec = pl.BlockSpec((tm, SEC), lambda i, j: (i, 0))
    cast_specs = [_cast_spec(c, grid) for c in cast_jobs]
    return pl.pallas_call(
        _in_proj_body,
        grid=grid,
        in_specs=[pl.BlockSpec((tm, k), lambda i, j: (i, 0)), pl.BlockSpec((k, SEC), lambda i, j: (0, j))]
        + cast_specs,
        out_specs=[pl.BlockSpec((None, tm, SEC), lambda i, j: (jnp.maximum(j - 2, 0), i, 0)), kv_spec, kv_spec]
        + cast_specs,
        out_shape=[jax.ShapeDtypeStruct((n_sec - 2, rows, SEC), F32)] + [jax.ShapeDtypeStruct((rows, SEC), F32)] * 2
        + [jax.ShapeDtypeStruct(c.shape, BF16) for c in cast_jobs],
        compiler_params=_params("parallel", "arbitrary"),
        name="in_proj",
    )(a, w, *cast_jobs)


def _matmul(a, w, *, residual=None, out_dtype=F32, name="matmul"):
    rows, k = a.shape
    n_cols = w.shape[1]
    tm, tn = _tile(rows, 512), n_cols
    in_specs = [pl.BlockSpec((tm, k), lambda i, j: (i, 0)), pl.BlockSpec((k, tn), lambda i, j: (0, j))]
    out_spec = pl.BlockSpec((tm, tn), lambda i, j: (i, j))
    args = [a, w]
    if residual is not None:
        in_specs.append(out_spec)
        args.append(residual)
    return pl.pallas_call(
        functools.partial(_mm_body, has_residual=residual is not None),
        grid=(rows // tm, n_cols // tn),
        in_specs=in_specs,
        out_specs=out_spec,
        out_shape=jax.ShapeDtypeStruct((rows, n_cols), out_dtype),
        compiler_params=_params("parallel", "arbitrary"),
        name=name,
    )(*args)


def _t5_bucket(n):
    n = jnp.maximum(n, 0)
    max_exact = T5_BUCKETS // 2
    nf = jnp.maximum(n, 1).astype(F32)
    large = max_exact + (jnp.log(nf / max_exact) / math.log(T5_MAX_DIST / max_exact)
                         * (T5_BUCKETS - max_exact)).astype(I32)
    return jnp.where(n < max_exact, n, jnp.minimum(large, T5_BUCKETS - 1))


PAGES_PER_BLOCK = MOBA_BLOCK // PAGE_SIZE
SUBLANES = 8


def _block_gates(q, page_refs, first_block, gates):
    lane = lax.broadcasted_iota(I32, gates.shape, 1)
    for i in range(len(page_refs) // PAGES_PER_BLOCK):
        pages = [page_refs[i * PAGES_PER_BLOCK + t][...] for t in range(PAGES_PER_BLOCK)]
        ksum = functools.reduce(jnp.add, [jnp.sum(x, axis=0) for x in pages])
        gate = jnp.sum(q * ksum, axis=1, keepdims=True) / MOBA_BLOCK
        gates = jnp.where(lane == first_block + i, gate, gates)
    return gates


PAGE_RING = 3


def _moba_prompt_body(*refs, nb, side_blocks, grid):
    h, b, c = pl.program_id(0), pl.program_id(1), pl.program_id(2)
    if side_blocks:
        (pt_ref, t5_ref, q_ref, k_ref, v_ref, qside_ref, cache_ref, o_ref, gates_ref,
         kb_sc, vb_sc, kdiff_sc, bias_sc, pages_sc, page_sem) = refs
        n_pages = side_blocks * PAGES_PER_BLOCK
        steps_per_seq = grid[2]
        n_steps = grid[0] * grid[1] * grid[2]
        step_idx = (h * grid[1] + b) * steps_per_seq + c

        def page_copies(s):
            seq, part, slot = s // steps_per_seq, s % steps_per_seq, s % PAGE_RING
            return [pltpu.make_async_copy(cache_ref.at[pt_ref[seq, part * n_pages + t]], pages_sc.at[slot, t],
                                          page_sem.at[slot]) for t in range(n_pages)]

        @pl.when(step_idx == 0)
        def _():
            for s in range(min(PAGE_RING - 1, n_steps)):
                for cp in page_copies(s):
                    cp.start()

        @pl.when(step_idx + PAGE_RING - 1 < n_steps)
        def _():
            for cp in page_copies(step_idx + PAGE_RING - 1):
                cp.start()

        for cp in page_copies(step_idx):
            cp.wait()
        slot = step_idx % PAGE_RING
        page_refs = [pages_sc.at[slot, t] for t in range(n_pages)]
    else:
        t5_ref, q_ref, k_ref, v_ref, o_ref, kb_sc, vb_sc, kdiff_sc, bias_sc = refs
    blk = MOBA_BLOCK
    nbp = SUBLANES
    assert nb <= nbp
    scale = A_DH ** -0.5
    ii = lax.broadcasted_iota(I32, (blk, blk), 0)
    jj = lax.broadcasted_iota(I32, (blk, blk), 1)

    @pl.when((b == 0) & (c == 0))
    def _():
        for t in range(2):
            bucket = _t5_bucket(t * blk + ii - jj)
            tile = jnp.zeros((blk, blk), F32)
            for bkt in range(T5_BUCKETS):
                tile = jnp.where(bucket == bkt, t5_ref[bkt, h], tile)
            tile = tile * LOG2E
            bias_sc[t] = jnp.where(ii >= jj, tile, NEG_INF) if t == 0 else tile
        for n in range(nb):
            kb_sc[n * blk:(n + 1) * blk, A_DH:2 * A_DH] = jnp.where(
                lax.broadcasted_iota(I32, (blk, A_DH), 1) == n, 1.0, 0.0).astype(BF16)
        vb_sc[:, A_DH:2 * A_DH] = jnp.ones((nb * blk, A_DH), BF16)

    @pl.when(c == 0)
    def _():
        km = jnp.concatenate(
            [jnp.mean(k_ref[n * blk:(n + 1) * blk, :], axis=0, keepdims=True) for n in range(nb)]
            + [jnp.zeros((1, A_DH), F32)] * (nbp - nb), axis=0)
        for n in range(nb):
            kdiff_sc[n * nbp:(n + 1) * nbp, :] = km[n:n + 1, :] - km
        kb_sc[:, 0:A_DH] = k_ref[...].astype(BF16)
        vb_sc[:, 0:A_DH] = v_ref[...].astype(BF16)
        if side_blocks:
            gates_ref[...] = jnp.zeros_like(gates_ref)

    far_bias = t5_ref[T5_BUCKETS - 1, h] * LOG2E

    def attend(cc):
        q = q_ref[cc * blk:(cc + 1) * blk, :]
        qs = (q * (scale * LOG2E)).astype(BF16)
        n_keys = (cc + 1) * blk
        if cc <= MOBA_TOPK:
            s_all = lax.dot_general(qs, kb_sc[0:n_keys, 0:A_DH], NT_DIMS, preferred_element_type=F32)
        else:
            diff_t = lax.dot_general(kdiff_sc[0:cc * nbp, :], q, NT_DIMS, precision=lax.Precision.HIGHEST,
                                     preferred_element_type=F32)
            pair = lax.broadcasted_iota(I32, diff_t.shape, 0)
            m_of, n_of = jnp.right_shift(pair, int(math.log2(nbp))), jnp.bitwise_and(pair, nbp - 1)
            beats_t = jnp.where(diff_t > 0, 1.0, jnp.where(diff_t == 0, jnp.where(n_of > m_of, 1.0, 0.0), 0.0))
            fold = jnp.where(jnp.bitwise_and(lax.broadcasted_iota(I32, (cc * nbp, 128), 0), nbp - 1)
                             == lax.broadcasted_iota(I32, (cc * nbp, 128), 1), 1.0, 0.0)
            rank = lax.dot_general(beats_t.astype(BF16), fold.astype(BF16), TN_DIMS,
                                   preferred_element_type=F32)
            lane = lax.broadcasted_iota(I32, rank.shape, 1)
            sel = jnp.where(lane < cc, jnp.where(rank < MOBA_TOPK, 0.0, NEG_INF), 0.0)
            q_sel = jnp.concatenate([qs, sel.astype(BF16)], axis=1)
            s_all = lax.dot_general(q_sel, kb_sc[0:n_keys, :], NT_DIMS, preferred_element_type=F32)
        tiles = []
        for n in range(cc + 1):
            t = s_all[:, n * blk:(n + 1) * blk]
            tiles.append(t + (bias_sc[0] if n == cc else bias_sc[1] if n == cc - 1 else far_bias))
        m = jnp.max(functools.reduce(jnp.maximum, tiles), axis=1, keepdims=True)
        p_all = jnp.concatenate([jnp.exp2(t - m).astype(BF16) for t in tiles], axis=1)
        pv = _dot(p_all, vb_sc[0:n_keys, :])
        o_ref[cc * blk:(cc + 1) * blk, :] = (pv[:, 0:A_DH] / pv[:, A_DH:2 * A_DH]).astype(o_ref.dtype)

    def step(s):
        if side_blocks:
            gates_ref[...] = _block_gates(qside_ref[...], page_refs, s * side_blocks, gates_ref[...])
        for cc in sorted({s, nb - 1 - s}):
            attend(cc)

    for s in range(_moba_steps(nb)):
        pl.when(c == s)(functools.partial(step, s))


def _moba_steps(nb):
    return (nb + 1) // 2


def _moba_prompt(z3, k, v, t5_table, batch, seq, paged=None):
    blk = MOBA_BLOCK
    nb = seq // blk
    steps = _moba_steps(nb)
    side_blocks = 0
    if paged is not None:
        page_table, q_side, cache_k = paged
        nb_past = page_table.shape[1] // PAGES_PER_BLOCK
        if page_table.shape[0] == A_HEADS * batch and nb_past % steps == 0:
            side_blocks = nb_past // steps

    seq_spec = pl.BlockSpec((seq, A_DH), lambda h, b, c, *_: (b, h))
    in_specs = [pl.BlockSpec(memory_space=pltpu.SMEM),
                pl.BlockSpec((None, seq, A_DH), lambda h, b, c, *_: (SEC_QA, b, h)), seq_spec, seq_spec]
    out_specs = [seq_spec]
    out_shape = [jax.ShapeDtypeStruct((batch * seq, A_HEADS * A_DH), BF16)]
    args = [t5_table, z3, k, v]
    scratch = [pltpu.VMEM((seq, 2 * A_DH), BF16), pltpu.VMEM((seq, 2 * A_DH), BF16),
               pltpu.VMEM((SUBLANES * SUBLANES, A_DH), F32), pltpu.VMEM((2, blk, blk), F32)]
    grid = (A_HEADS, batch, steps)
    if side_blocks:
        in_specs += [pl.BlockSpec((None, None, A_HEADS, A_DH), lambda h, b, c, pt: (SEC_QA, h * batch + b, 0, 0)),
                     pl.BlockSpec(memory_space=pl.ANY)]
        out_specs.append(pl.BlockSpec((None, A_HEADS, 128), lambda h, b, c, pt: (h * batch + b, 0, 0)))
        out_shape.append(jax.ShapeDtypeStruct((A_HEADS * batch, A_HEADS, 128), F32))
        args = [page_table] + args + [q_side, cache_k]
        scratch += [pltpu.VMEM((PAGE_RING, side_blocks * PAGES_PER_BLOCK, PAGE_SIZE, A_HEADS, A_DH), F32),
                    pltpu.SemaphoreType.DMA((PAGE_RING,))]
    out = pl.pallas_call(
        functools.partial(_moba_prompt_body, nb=nb, side_blocks=side_blocks, grid=grid),
        grid_spec=pltpu.PrefetchScalarGridSpec(
            num_scalar_prefetch=1 if side_blocks else 0,
            grid=grid,
            in_specs=in_specs,
            out_specs=out_specs,
            scratch_shapes=scratch,
        ),
        out_shape=out_shape,
        compiler_params=_params("arbitrary", "arbitrary", "arbitrary"),
        name="moba_prompt",
    )(*args)
    if paged is None:
        return out[0]
    return out[0], (out[1] if side_blocks else None)


def _rope_body(inv_ref, cos_ref, sin_ref, *, pos0):
    pos = pos0 + lax.broadcasted_iota(I32, cos_ref.shape, 0)
    ang = pos.astype(F32) * inv_ref[...]
    cos_ref[...] = jnp.cos(ang)
    sin_ref[...] = jnp.sin(ang)


def _rope_tables(pos0, n_pos):
    half = R_DK // 2
    inv = 1.0 / (RET_ROPE_BASE ** jnp.linspace(0.0, 1.0, half, dtype=F32))
    rows = -(-n_pos // 8) * 8
    return pl.pallas_call(
        functools.partial(_rope_body, pos0=pos0),
        out_shape=[jax.ShapeDtypeStruct((rows, half), F32)] * 2,
        name="rope_tables",
    )(inv.reshape(1, half))


def _rotate(x, cos, sin):
    half = x.shape[-1] // 2
    x1, x2 = x[:, :half], x[:, half:]
    return jnp.concatenate([x1 * cos - x2 * sin, x1 * sin + x2 * cos], axis=1)


def _log_decay(h, shape):
    hf = jnp.full(shape, h, I32).astype(F32)
    return jnp.log(1.0 - jnp.exp2(-5.0 - hf))


def _ret_prompt_body(q_ref, k_ref, v_ref, g_ref, cos_ref, sin_ref, o_ref, so_ref, s_sc, *, c):
    n = pl.program_id(1)

    @pl.when(n == 0)
    def _():
        s_sc[...] = jnp.zeros_like(s_sc)

    i = lax.broadcasted_iota(I32, (c, c), 0).astype(F32)
    j = lax.broadcasted_iota(I32, (c, c), 1).astype(F32)
    diff = i - j
    i_col = lax.broadcasted_iota(I32, (c, 1), 0).astype(F32)
    cos, sin = cos_ref[...], sin_ref[...]

    new_states, outs = [], []
    for h in range(R_HEADS):
        dk, dv = slice(h * R_DK, (h + 1) * R_DK), slice(h * R_DV, (h + 1) * R_DV)
        dmask = jnp.where(diff >= 0, jnp.exp(jnp.maximum(diff, 0.0) * _log_decay(h, (c, c))), 0.0)
        lg_col = _log_decay(h, (c, 1))
        q_dec = jnp.exp((i_col + 1.0) * lg_col)
        k_dec = jnp.exp((c - 1.0 - i_col) * lg_col)
        c_dec = jnp.exp(c * _log_decay(h, (1, R_DV)))
        qr = _rotate(q_ref[:, dk], cos, sin)
        kr = _rotate(k_ref[:, dk], cos, sin) * (R_DK ** -0.5)
        vb = v_ref[:, dv].astype(BF16)
        s = s_sc[h]
        att = lax.dot_general(qr.astype(BF16), kr.astype(BF16), NT_DIMS, preferred_element_type=F32) * dmask
        o = _dot(att.astype(BF16), vb) + _dot((qr * q_dec).astype(BF16), s.astype(BF16))
        new_states.append(s * c_dec + lax.dot_general((kr * k_dec).astype(BF16), vb, TN_DIMS,
                                                      preferred_element_type=F32))
        on = o * lax.rsqrt(jnp.mean(o * o, axis=-1, keepdims=True) + EPS)
        outs.append((on * _silu(g_ref[:, dv])).astype(o_ref.dtype))
    for h in range(R_HEADS):
        s_sc[h] = new_states[h]
    o_ref[...] = jnp.concatenate(outs, axis=1)

    @pl.when(n == pl.num_programs(1) - 1)
    def _():
        so_ref[...] = s_sc[...]


def _ret_prompt(z3, cos, sin, batch, seq):
    c = math.gcd(seq, RET_CHUNK)
    nc = seq // c

    def sec(s):
        return pl.BlockSpec((None, c, SEC), lambda b, n: (s, b * nc + n, 0))

    tab = pl.BlockSpec((c, R_DK // 2), lambda b, n: (n, 0))
    return pl.pallas_call(
        functools.partial(_ret_prompt_body, c=c),
        grid=(batch, nc),
        in_specs=[sec(SEC_QR), sec(SEC_KR), sec(SEC_VR), sec(SEC_GR), tab, tab],
        out_specs=[pl.BlockSpec((c, R_HEADS * R_DV), lambda b, n: (b * nc + n, 0)),
                   pl.BlockSpec((None, R_HEADS, R_DK, R_DV), lambda b, n: (b, 0, 0, 0))],
        out_shape=[jax.ShapeDtypeStruct((batch * seq, R_HEADS * R_DV), BF16),
                   jax.ShapeDtypeStruct((batch, R_HEADS, R_DK, R_DV), F32)],
        scratch_shapes=[pltpu.VMEM((R_HEADS, R_DK, R_DV), F32)],
        compiler_params=_params("parallel", "arbitrary"),
        name="retention_prompt",
    )(z3, z3, z3, z3, cos, sin)


def _ret_step_body(q_ref, k_ref, v_ref, g_ref, cos_ref, sin_ref, s_ref, o_ref, so_ref):
    cos, sin = cos_ref[0:1, :], sin_ref[0:1, :]
    rows = BF16_SUBLANES
    row0 = lax.broadcasted_iota(I32, (rows, R_DK), 0) == 0
    for h in range(R_HEADS):
        decay = jnp.exp(_log_decay(h, (1, R_DV)))
        qr = _rotate(q_ref[h:h + 1, :], cos, sin)
        kr = _rotate(k_ref[h:h + 1, :], cos, sin) * (R_DK ** -0.5)
        v = v_ref[h:h + 1, :]
        s = s_ref[h]
        q_rows = jnp.broadcast_to(qr * decay, (rows, R_DK)).astype(BF16)
        qs = _dot(q_rows, s.astype(BF16))[0:1, :]
        o = jnp.sum(qr * kr, axis=-1, keepdims=True) * v + qs
        k_rows = jnp.where(row0, jnp.broadcast_to(kr, (rows, R_DK)), 0.0).astype(BF16)
        v_rows = jnp.broadcast_to(v, (rows, R_DV)).astype(BF16)
        so_ref[h] = s * decay + lax.dot_general(k_rows, v_rows, TN_DIMS, preferred_element_type=F32)
        on = o * lax.rsqrt(jnp.mean(o * o, axis=-1, keepdims=True) + EPS)
        o_ref[h:h + 1, :] = (on * _silu(g_ref[h:h + 1, :])).astype(o_ref.dtype)


def _mem_rows(q, k_ref, v_ref):
    outs = []
    for h in range(M_HEADS):
        dh = slice(h * M_DH, (h + 1) * M_DH)
        s = lax.dot_general(q[:, dh].astype(BF16), k_ref[:, dh].astype(BF16), NT_DIMS,
                            preferred_element_type=F32) * (M_DH ** -0.5)
        m = jnp.max(s, axis=-1, keepdims=True)
        p = jnp.exp(s - m)
        l = jnp.sum(p, axis=-1, keepdims=True)
        outs.append(_dot(p.astype(BF16), v_ref[:, dh].astype(BF16)) / l)
    return jnp.concatenate(outs, axis=1)


def _mem_body(q_ref, k_ref, v_ref, o_ref):
    o_ref[...] = _mem_rows(q_ref[...], k_ref, v_ref).astype(o_ref.dtype)


def _mem_attend(z3, mk, mv, batch, seq):
    n_mem = mk.shape[0] // batch
    width = M_HEADS * M_DH
    ts = _tile(seq, 512)
    nt = seq // ts
    kv = pl.BlockSpec((n_mem, width), lambda b, t: (b, 0))
    return pl.pallas_call(
        _mem_body,
        grid=(batch, nt),
        in_specs=[pl.BlockSpec((None, ts, width), lambda b, t: (SEC_QM, b * nt + t, 0)), kv, kv],
        out_specs=pl.BlockSpec((ts, width), lambda b, t: (b * nt + t, 0)),
        out_shape=jax.ShapeDtypeStruct((batch * seq, width), BF16),
        compiler_params=_params("parallel", "arbitrary"),
        name="mem_attend",
    )(z3, mk, mv)


def _merge_body(h_ref, *refs):
    x_refs, wg_refs, bg_refs, wb_refs, o_ref = refs[0:3], refs[3:6], refs[6:9], refs[9:12], refs[12]
    h = h_ref[...]
    merged = None
    for g in range(3):
        term = jax.nn.sigmoid(_dot(h, wg_refs[g][...]) + bg_refs[g][...]) * _dot(x_refs[g][...], wb_refs[g][...])
        merged = term if merged is None else merged + term
    o_ref[...] = merged.astype(o_ref.dtype)


def _merge(h, branches, w_gate, b_gate, branch_ws):
    rows, d = h.shape
    tm, tn = _tile(rows, 1024), _tile(d, 512)
    nj = d // tn
    b_gate = b_gate.reshape(1, 3 * d)

    def row(width):
        return pl.BlockSpec((tm, width), lambda i, j: (i, 0))

    def cols(n_rows, group=0):
        return pl.BlockSpec((n_rows, tn), lambda i, j: (0, group * nj + j))

    return pl.pallas_call(
        _merge_body,
        grid=(rows // tm, nj),
        in_specs=([row(d)] + [row(x.shape[1]) for x in branches]
                  + [cols(d, g) for g in range(3)] + [cols(1, g) for g in range(3)]
                  + [cols(w.shape[0]) for w in branch_ws]),
        out_specs=pl.BlockSpec((tm, tn), lambda i, j: (i, j)),
        out_shape=jax.ShapeDtypeStruct((rows, d), BF16),
        compiler_params=_params("parallel", "arbitrary"),
        name="gated_merge",
    )(h, *branches, w_gate, w_gate, w_gate, b_gate, b_gate, b_gate, *branch_ws)


def _moba_gate_body(pt_ref, q_ref, *refs, bps):
    del pt_ref
    k_refs, gates_ref = refs[:-1], refs[-1]
    n = pl.program_id(1)

    @pl.when(n == 0)
    def _():
        gates_ref[...] = jnp.zeros_like(gates_ref)

    gates_ref[...] = _block_gates(q_ref[...], k_refs, n * bps, gates_ref[...])


def _moba_select_body(g_ref, sel_ref, *, nb_past):
    lane = lax.broadcasted_iota(I32, g_ref.shape, 1)
    gw = jnp.where(lane < nb_past, g_ref[...], NEG_INF)
    sel = jnp.zeros(gw.shape, I32)
    for r in range(MOBA_TOPK):
        mx = jnp.max(gw, axis=1, keepdims=True)
        idx = jnp.min(jnp.where(gw == mx, lane, 128), axis=1, keepdims=True)
        sel = jnp.where(lane == r, idx, sel)
        gw = jnp.where(lane == idx, -jnp.inf, gw)
    sel_ref[...] = sel


def _moba_attend_body(pt_ref, sel_ref, t5_ref, q_ref, kn_ref, vn_ref, ck_ref, cv_ref, o_ref, kbuf, vbuf, sem,
                      *, past_len):
    b = pl.program_id(0)
    slot = lax.rem(b, 2)
    scale = A_DH ** -0.5
    n_tiles = MOBA_TOPK * PAGES_PER_BLOCK

    def tile_copies(bb, to_slot):
        copies = []
        for h in range(A_HEADS):
            for r in range(MOBA_TOPK):
                block = sel_ref[bb, h * MOBA_TOPK + r]
                for t in range(PAGES_PER_BLOCK):
                    page = pt_ref[bb, block * PAGES_PER_BLOCK + t]
                    j = r * PAGES_PER_BLOCK + t
                    copies.append(pltpu.make_async_copy(ck_ref.at[page, :, h, :], kbuf.at[to_slot, h, j],
                                                        sem.at[to_slot]))
                    copies.append(pltpu.make_async_copy(cv_ref.at[page, :, h, :], vbuf.at[to_slot, h, j],
                                                        sem.at[to_slot]))
        return copies

    def start_all(copies):
        for i, cp in enumerate(copies):
            cp.start(priority=i % 2)

    @pl.when(b == 0)
    def _():
        start_all(tile_copies(0, 0))

    @pl.when(b + 1 < pl.num_programs(0))
    def _():
        start_all(tile_copies(b + 1, 1 - slot))

    for cp in tile_copies(b, slot):
        cp.wait()

    q, kn, vn = q_ref[...], kn_ref[...], vn_ref[...]
    outs = []
    assert n_tiles <= SUBLANES
    lane = lax.broadcasted_iota(I32, (SUBLANES, PAGE_SIZE), 1)
    row = lax.broadcasted_iota(I32, (SUBLANES, PAGE_SIZE), 0)
    for h in range(A_HEADS):
        qh = q[h:h + 1, :]
        q8 = jnp.broadcast_to(qh, (8, A_DH)).astype(BF16)
        first_key = jnp.zeros((SUBLANES, PAGE_SIZE), I32)
        for r in range(MOBA_TOPK):
            block = sel_ref[b, h * MOBA_TOPK + r]
            for t in range(PAGES_PER_BLOCK):
                first_key = jnp.where(row == r * PAGES_PER_BLOCK + t, block * MOBA_BLOCK + t * PAGE_SIZE, first_key)
        bucket = _t5_bucket(past_len - (first_key + lane))
        bias = jnp.zeros((SUBLANES, PAGE_SIZE), F32)
        for bkt in range(T5_BUCKETS):
            bias = jnp.where(bucket == bkt, t5_ref[bkt, h], bias)
        s = []
        for j in range(n_tiles):
            kt = kbuf[slot, h, j].astype(BF16)
            raw = lax.dot_general(q8, kt, NT_DIMS, preferred_element_type=F32)[0:1, :]
            s.append(raw * scale + bias[j:j + 1, :])
        s_own = jnp.sum(qh * kn[h:h + 1, :], axis=1, keepdims=True) * scale + t5_ref[0, h]
        m = jnp.maximum(jnp.max(functools.reduce(jnp.maximum, s), axis=1, keepdims=True), s_own)
        p = [jnp.exp(x - m) for x in s]
        p_own = jnp.exp(s_own - m)
        l = jnp.sum(functools.reduce(jnp.add, p), axis=1, keepdims=True) + p_own
        acc = p_own * vn[h:h + 1, :]
        for j in range(n_tiles):
            p8 = jnp.broadcast_to(p[j], (8, PAGE_SIZE)).astype(BF16)
            acc = acc + _dot(p8, vbuf[slot, h, j].astype(BF16))[0:1, :]
        outs.append(acc / l)
    o_ref[...] = jnp.concatenate(outs, axis=0)


def _moba_gates(q4, cache_k, page_table):
    batch, n_pages = page_table.shape
    nb_past = n_pages // PAGES_PER_BLOCK
    bps = _tile(nb_past, 8)

    def page(t):
        return pl.BlockSpec((None, PAGE_SIZE, A_HEADS, A_DH),
                            lambda b, n, pt: (pt[b, n * bps * PAGES_PER_BLOCK + t], 0, 0, 0))

    pages = [page(t) for t in range(bps * PAGES_PER_BLOCK)]
    return pl.pallas_call(
        functools.partial(_moba_gate_body, bps=bps),
        grid_spec=pltpu.PrefetchScalarGridSpec(
            num_scalar_prefetch=1,
            grid=(batch, nb_past // bps),
            in_specs=[pl.BlockSpec((None, None, A_HEADS, A_DH), lambda b, n, pt: (SEC_QA, b, 0, 0))] + pages,
            out_specs=pl.BlockSpec((None, A_HEADS, 128), lambda b, n, pt: (b, 0, 0)),
        ),
        out_shape=jax.ShapeDtypeStruct((batch, A_HEADS, 128), F32),
        compiler_params=_params("parallel", "arbitrary"),
        name="moba_gate",
    )(page_table, q4, *([cache_k] * len(pages)))


def _sample_mixers_body(pt_ref, sel_ref, t5_ref, qa_ref, kn_ref, vn_ref, ck_ref, cv_ref,
                        qr_ref, kr_ref, vr_ref, gr_ref, cos_ref, sin_ref, state_ref, qm_ref, mk_ref, mv_ref,
                        oa_ref, or_ref, state_out_ref, om_ref, kbuf, vbuf, sem, *, past_len):
    _moba_attend_body(pt_ref, sel_ref, t5_ref, qa_ref, kn_ref, vn_ref, ck_ref, cv_ref, oa_ref, kbuf, vbuf, sem,
                      past_len=past_len)
    _ret_step_body(qr_ref, kr_ref, vr_ref, gr_ref, cos_ref, sin_ref, state_ref, or_ref, state_out_ref)
    q_rows = jnp.broadcast_to(qm_ref[...], (BF16_SUBLANES, qm_ref.shape[-1]))
    om_ref[...] = _mem_rows(q_rows, mk_ref, mv_ref)[0:1, :]


def _sample_mixers(z3, k_new, v_new, cache_k, cache_v, page_table, t5_table, gates, cos, sin, state, mem_k, mem_v):
    batch, n_pages = page_table.shape
    past_len = n_pages * PAGE_SIZE
    assert past_len % MOBA_BLOCK == 0 and MOBA_BLOCK % PAGE_SIZE == 0
    nb_past = past_len // MOBA_BLOCK
    assert MOBA_TOPK <= nb_past <= 128
    n_sec = z3.shape[0]
    n_mem = mem_k.shape[0] // batch
    z4a = z3.reshape(n_sec, batch, A_HEADS, A_DH)
    z4r = z3.reshape(n_sec, batch, R_HEADS, R_DK)
    z4m = z3.reshape(n_sec, batch, 1, M_HEADS * M_DH)
    k_new = k_new.reshape(batch, A_HEADS, A_DH)
    v_new = v_new.reshape(batch, A_HEADS, A_DH)
    if gates is None:
        gates = _moba_gates(z4a, cache_k, page_table)
    sel = pl.pallas_call(
        functools.partial(_moba_select_body, nb_past=nb_past),
        out_shape=jax.ShapeDtypeStruct((batch * A_HEADS, 128), I32),
        name="moba_select",
    )(gates.reshape(batch * A_HEADS, 128))
    sel = sel[:, :MOBA_TOPK].reshape(batch, A_HEADS * MOBA_TOPK)

    def sec(s, rows, width):
        return pl.BlockSpec((None, None, rows, width), lambda b, pt, sl: (s, b, 0, 0))

    def per_seq(*shape):
        return pl.BlockSpec((None,) + shape, lambda b, pt, sl: (b,) + (0,) * len(shape))

    tab = pl.BlockSpec((SUBLANES, R_DK // 2), lambda b, pt, sl: (0, 0))
    mem = pl.BlockSpec((n_mem, M_HEADS * M_DH), lambda b, pt, sl: (b, 0))
    tiles = pltpu.VMEM((2, A_HEADS, MOBA_TOPK * PAGES_PER_BLOCK, PAGE_SIZE, A_DH), F32)
    return pl.pallas_call(
        functools.partial(_sample_mixers_body, past_len=past_len),
        grid_spec=pltpu.PrefetchScalarGridSpec(
            num_scalar_prefetch=2,
            grid=(batch,),
            in_specs=[pl.BlockSpec(memory_space=pltpu.SMEM), sec(SEC_QA, A_HEADS, A_DH),
                      per_seq(A_HEADS, A_DH), per_seq(A_HEADS, A_DH),
                      pl.BlockSpec(memory_space=pl.ANY), pl.BlockSpec(memory_space=pl.ANY),
                      sec(SEC_QR, R_HEADS, R_DK), sec(SEC_KR, R_HEADS, R_DK), sec(SEC_VR, R_HEADS, R_DK),
                      sec(SEC_GR, R_HEADS, R_DK), tab, tab, per_seq(R_HEADS, R_DK, R_DV),
                      sec(SEC_QM, 1, M_HEADS * M_DH), mem, mem],
            out_specs=[per_seq(A_HEADS, A_DH), per_seq(R_HEADS, R_DV), per_seq(R_HEADS, R_DK, R_DV),
                       per_seq(1, M_HEADS * M_DH)],
            scratch_shapes=[tiles, tiles, pltpu.SemaphoreType.DMA((2,))],
        ),
        out_shape=[jax.ShapeDtypeStruct((batch, A_HEADS, A_DH), F32),
                   jax.ShapeDtypeStruct((batch, R_HEADS, R_DV), F32),
                   jax.ShapeDtypeStruct(state.shape, F32),
                   jax.ShapeDtypeStruct((batch, 1, M_HEADS * M_DH), F32)],
        compiler_params=_params("arbitrary"),
        name="sample_mixers",
    )(page_table, sel, t5_table, z4a, k_new, v_new, cache_k, cache_v, z4r, z4r, z4r, z4r, cos, sin, state,
      z4m, mem_k, mem_v)


def kernel(x_prompt, x_sample, mem_prompt, cache_k, cache_v, cache_mem_k, cache_mem_v, state_ret, page_table, t5_table, ffn1_norm, ffn1_w1, ffn1_w3, ffn1_w2, mix_norm, mem_norm, w_in, w_mem_kv, w_gate, b_gate, w_br_moba, w_br_ret, w_br_mem, w_out, ffn2_norm, ffn2_w1, ffn2_w3, ffn2_w2, final_norm):
    batch, seq, d = x_prompt.shape
    dec_batch, dec_seq, _ = x_sample.shape
    assert dec_seq == 1
    depth = w_in.shape[0]
    n_mem = mem_prompt.shape[1]
    past_len = page_table.shape[1] * PAGE_SIZE
    cos_p, sin_p = _rope_tables(0, seq)
    cos_s, sin_s = _rope_tables(past_len, dec_seq)

    xp = x_prompt.reshape(batch * seq, d)
    xs = x_sample.reshape(dec_batch, d)
    outs = [[] for _ in range(8)]
    for l in range(depth):
        last = l == depth - 1
        g_next = final_norm if last else ffn1_norm[l + 1]

        x1s, h2s, *f1_b = _ffn_half(xs, ffn1_norm[l], ffn1_w1[l], ffn1_w3[l], ffn1_w2[l], mix_norm[l],
                                    emit_x=True, post_dtype=BF16, emit_w=True)
        later_ws = [w_in[l], w_gate[l], w_out[l], ffn2_w1[l], ffn2_w3[l], ffn2_w2[l],
                    w_br_moba[l], w_br_ret[l], w_br_mem[l]]
        x1, h2, *later_b = _ffn_half(xp, ffn1_norm[l], *f1_b, mix_norm[l], emit_x=True, post_dtype=BF16,
                                     cast_jobs=later_ws)
        w_in_b, w_gate_b, w_out_b = later_b[:3]
        f2_b, branch_b = later_b[3:6], later_b[6:9]
        z3s, ka_s, va_s = _in_proj(h2s, w_in_b)
        z3, ka, va = _in_proj(h2, w_in_b)
        mem_k_s, mem_v_s, x1s = lax.optimization_barrier((cache_mem_k[l], cache_mem_v[l], x1s))

        z4s = z3s.reshape(z3s.shape[0], dec_batch, A_HEADS, A_DH)
        oa, gates_s = _moba_prompt(z3, ka, va, t5_table, batch, seq, paged=(page_table, z4s, cache_k[l]))

        oa_s, o_rs, s_new_s, om_s = _sample_mixers(
            z3s, ka_s, va_s, cache_k[l], cache_v[l], page_table, t5_table, gates_s, cos_s, sin_s, state_ret[l],
            mem_k_s.reshape(dec_batch * n_mem, M_HEADS * M_DH), mem_v_s.reshape(dec_batch * n_mem, M_HEADS * M_DH))
        branches_s = [o.reshape(dec_batch, -1).astype(BF16) for o in (oa_s, o_rs, om_s)]

        mkv_p = _norm_matmul(mem_prompt.reshape(batch * n_mem, d), mem_norm[l], w_mem_kv[l])
        mk_p, mv_p = mkv_p[:, :M_HEADS * M_DH], mkv_p[:, M_HEADS * M_DH:]
        o_r, s_new_p = _ret_prompt(z3, cos_p, sin_p, batch, seq)
        om = _mem_attend(z3, mk_p, mv_p, batch, seq)

        merged_s = _merge(h2s, branches_s, w_gate_b, b_gate[l], branch_b)
        merged = _merge(h2, [oa, o_r, om], w_gate_b, b_gate[l], branch_b)
        x2s = _matmul(merged_s, w_out_b, residual=x1s, name="out_proj")
        x2 = _matmul(merged, w_out_b, residual=x1, name="out_proj")
        xs = _ffn_half(x2s, ffn2_norm[l], *f2_b, g_next, emit_x=not last, post_dtype=F32)[0]
        xp = _ffn_half(x2, ffn2_norm[l], *f2_b, g_next, emit_x=not last, post_dtype=F32)[0]

        new = (ka.reshape(batch, seq, A_HEADS, A_DH), va.reshape(batch, seq, A_HEADS, A_DH),
               mk_p.reshape(batch, n_mem, M_HEADS, M_DH), mv_p.reshape(batch, n_mem, M_HEADS, M_DH), s_new_p,
               ka_s.reshape(dec_batch, dec_seq, A_HEADS, A_DH), va_s.reshape(dec_batch, dec_seq, A_HEADS, A_DH),
               s_new_s)
        for acc, val in zip(outs, new):
            acc.append(val)

    y_prompt = xp.reshape(batch, seq, d)
    y_sample = xs.reshape(dec_batch, dec_seq, d)
    return (y_prompt, y_sample) + tuple(jnp.stack(o) for o in outs)
```
